```python
import jax
import jax.numpy as jnp
from jax import lax
import numpy as np

D_MODEL = 1024
BATCH = 8
SEQ = 4096
DEPTH = 2

N_EVEN = (DEPTH + 1) // 2
N_ODD = DEPTH // 2
ROPE_THETA = 10000.0
RMS_EPS = 1e-6
NEG_INF = -1e30
FORCE_SCORE = 1e4

NSA_HEADS = 8
NSA_KV_GROUPS = 2
NSA_HPG = NSA_HEADS // NSA_KV_GROUPS
NSA_HEAD_DIM = 64
NSA_WIDTH = NSA_HEADS * NSA_HEAD_DIM
NSA_KV_WIDTH = NSA_KV_GROUPS * NSA_HEAD_DIM
CMP_BLOCK = 32
CMP_STRIDE = 16
CMP_HIDDEN = 2 * NSA_HEAD_DIM
SLC_BLOCK = 64
SLC_TOPK = 16
WINDOW = 512
NSA_QBLOCK = 64

CONV_WIDTH = D_MODEL - NSA_WIDTH
CONV_K = 3

IN_A_SIZES = (NSA_WIDTH, NSA_KV_WIDTH, NSA_KV_WIDTH, NSA_KV_WIDTH, NSA_KV_WIDTH, NSA_KV_WIDTH, NSA_KV_WIDTH,
              3 * NSA_HEADS, NSA_WIDTH, CONV_WIDTH, CONV_WIDTH, CONV_WIDTH, CONV_WIDTH)
IN_A_WIDTH = 2 * NSA_WIDTH + 6 * NSA_KV_WIDTH + 3 * NSA_HEADS + 4 * CONV_WIDTH

MLA_HEADS = 8
MLA_NOPE_DIM = 128
MLA_ROPE_DIM = 64
MLA_V_DIM = 128
MLA_Q_RANK = 256
MLA_KV_RANK = 256
MLA_WIDTH = MLA_HEADS * MLA_V_DIM
MLA_QBLOCK = 128
IN_C_SIZES = (MLA_Q_RANK, MLA_KV_RANK, MLA_ROPE_DIM, MLA_WIDTH)
IN_C_WIDTH = MLA_Q_RANK + MLA_KV_RANK + MLA_ROPE_DIM + MLA_WIDTH

kernel_name = 'hybrid_nsa_shortconv_mla'


def _split(t, sizes):
    return jnp.split(t, np.cumsum(sizes)[:-1].tolist(), axis=-1)


def rms_norm(x, g):
    xf = x.astype(jnp.float32)
    y = xf * lax.rsqrt(jnp.mean(xf * xf, axis=-1, keepdims=True) + RMS_EPS)
    return (y * g.astype(jnp.float32)).astype(x.dtype)


def rope(x, pos):
    half = x.shape[-1] // 2
    inv_freq = ROPE_THETA ** (-jnp.arange(half, dtype=jnp.float32) / half)
    ang = pos.astype(jnp.float32)[..., None] * inv_freq
    cos = jnp.cos(ang)[:, :, None, :]
    sin = jnp.sin(ang)[:, :, None, :]
    x1 = x[..., :half].astype(jnp.float32)
    x2 = x[..., half:].astype(jnp.float32)
    out = jnp.concatenate([x1 * cos - x2 * sin, x2 * cos + x1 * sin], axis=-1)
    return out.astype(x.dtype)


def masked_softmax(s, valid):
    s = jnp.where(valid, s.astype(jnp.float32), NEG_INF)
    return jnp.where(valid, jax.nn.softmax(s, axis=-1), 0.0)


def nsa_attention(q, k_cmp_raw, v_cmp_raw, k_slc, v_slc, k_win, v_win, gates, positions,
                  pe_k, pe_v, w_ck1, w_ck2, w_cv1, w_cv2):
    B, S = q.shape[:2]
    G, HPG, hd, QB = NSA_KV_GROUPS, NSA_HPG, NSA_HEAD_DIM, NSA_QBLOCK
    scale = hd ** -0.5
    q = rope(q, positions).reshape(B, S, G, HPG, hd)
    k_slc = rope(k_slc, positions)
    k_win = rope(k_win, positions)

    n_cmp = (S - CMP_BLOCK) // CMP_STRIDE + 1
    cmp_starts = jnp.arange(n_cmp) * CMP_STRIDE
    cmp_idx = cmp_starts[:, None] + jnp.arange(CMP_BLOCK)[None, :]
    cmp_ends = cmp_starts + CMP_BLOCK - 1

    def compress(raw, pe, w1, w2):
        blk = raw[:, cmp_idx] + pe[None, None, :, None, :]
        blk = blk.transpose(0, 1, 3, 2, 4).reshape(B, n_cmp, G, CMP_BLOCK * hd)
        return jax.nn.silu(blk @ w1) @ w2

    k_cmp = rope(compress(k_cmp_raw, pe_k, w_ck1, w_ck2), positions[:, cmp_ends])
    v_cmp = compress(v_cmp_raw, pe_v, w_cv1, w_cv2)

    n_slc = S // SLC_BLOCK
    n_top = min(SLC_TOPK, n_slc)
    slc_starts = jnp.arange(n_slc) * SLC_BLOCK
    overlap = ((cmp_starts[:, None] < slc_starts[None, :] + SLC_BLOCK) &
               (cmp_starts[:, None] + CMP_BLOCK > slc_starts[None, :])).astype(jnp.float32)
    k_blocks = k_slc.reshape(B, n_slc, SLC_BLOCK, G, hd).transpose(0, 3, 1, 2, 4)
    v_blocks = v_slc.reshape(B, n_slc, SLC_BLOCK, G, hd).transpose(0, 3, 1, 2, 4)

    k_pad = jnp.pad(k_win, ((0, 0), (WINDOW, 0), (0, 0), (0, 0)))
    v_pad = jnp.pad(v_win, ((0, 0), (WINDOW, 0), (0, 0), (0, 0)))

    gates = gates.reshape(B, S, G, HPG, 3)
    bi = jnp.arange(B)[:, None, None, None]
    gi = jnp.arange(G)[None, :, None, None]
    blk_id = jnp.arange(n_slc)[None, :]

    def block(qi):
        t0 = qi * QB
        tq = t0 + jnp.arange(QB)
        qb = lax.dynamic_slice_in_dim(q, t0, QB, axis=1)
        gb = lax.dynamic_slice_in_dim(gates, t0, QB, axis=1)

        s_c = jnp.einsum('bqghd,bcgd->bghqc', qb, k_cmp) * scale
        p_c = masked_softmax(s_c, cmp_ends[None, :] <= tq[:, None])
        o_c = jnp.einsum('bghqc,bcgd->bqghd', p_c.astype(v_cmp.dtype), v_cmp)

        imp = jnp.einsum('bghqc,cn->bgqn', p_c, overlap)
        blk_valid = slc_starts[None, :] <= tq[:, None]
        cur = (tq // SLC_BLOCK)[:, None]
        forced = (blk_id == 0) | (blk_id == cur) | (blk_id == cur - 1)
        imp = jnp.where(blk_valid, jnp.where(forced, FORCE_SCORE, imp), -1.0)
        sel = lax.top_k(imp, n_top)[1]
        ks = k_blocks[bi, gi, sel]
        vs = v_blocks[bi, gi, sel]
        tok = sel[..., None] * SLC_BLOCK + jnp.arange(SLC_BLOCK)
        valid_s = (tok <= tq[None, None, :, None, None]).reshape(B, G, 1, QB, n_top * SLC_BLOCK)
        s_s = jnp.einsum('bqghd,bgqnld->bghqnl', qb, ks) * scale
        p_s = masked_softmax(s_s.reshape(B, G, HPG, QB, n_top * SLC_BLOCK), valid_s)
        p_s = p_s.reshape(B, G, HPG, QB, n_top, SLC_BLOCK)
        o_s = jnp.einsum('bghqnl,bgqnld->bqghd', p_s.astype(vs.dtype), vs)

        kw = lax.dynamic_slice_in_dim(k_pad, t0, WINDOW + QB, axis=1)
        vw = lax.dynamic_slice_in_dim(v_pad, t0, WINDOW + QB, axis=1)
        kpos = t0 - WINDOW + jnp.arange(WINDOW + QB)
        valid_w = ((kpos[None, :] <= tq[:, None]) & (kpos[None, :] > tq[:, None] - WINDOW) &
                   (kpos[None, :] >= 0))
        s_w = jnp.einsum('bqghd,bkgd->bghqk', qb, kw) * scale
        p_w = masked_softmax(s_w, valid_w)
        o_w = jnp.einsum('bghqk,bkgd->bqghd', p_w.astype(vw.dtype), vw)

        o = gb[..., 0:1] * o_c + gb[..., 1:2] * o_s + gb[..., 2:3] * o_w
        return o.reshape(B, QB, G * HPG * hd)

    out = lax.map(block, jnp.arange(S // QB))
    return out.transpose(1, 0, 2, 3).reshape(B, S, NSA_WIDTH)


def short_conv(b_gate, c_gate, h, w):
    S = h.shape[1]
    u = c_gate * h
    up = jnp.pad(u, ((0, 0), (CONV_K - 1, 0), (0, 0)))
    y = sum(w[k] * up[:, k:k + S] for k in range(CONV_K))
    return b_gate * y


def mla_attention(c_q, c_kv, k_rope_raw, positions, q_norm, kv_norm, w_uq, w_ukv):
    B, S = c_q.shape[:2]
    H, QB = MLA_HEADS, MLA_QBLOCK
    q = (rms_norm(c_q, q_norm) @ w_uq).reshape(B, S, H, MLA_NOPE_DIM + MLA_ROPE_DIM)
    q = jnp.concatenate([q[..., :MLA_NOPE_DIM], rope(q[..., MLA_NOPE_DIM:], positions)], axis=-1)
    kv = (rms_norm(c_kv, kv_norm) @ w_ukv).reshape(B, S, H, MLA_NOPE_DIM + MLA_V_DIM)
    k_nope, v = kv[..., :MLA_NOPE_DIM], kv[..., MLA_NOPE_DIM:]
    k_pe = rope(k_rope_raw[:, :, None, :], positions)
    k = jnp.concatenate([k_nope, jnp.broadcast_to(k_pe, (B, S, H, MLA_ROPE_DIM))], axis=-1)
    scale = (MLA_NOPE_DIM + MLA_ROPE_DIM) ** -0.5
    kpos = jnp.arange(S)

    def block(qi):
        t0 = qi * QB
        qb = lax.dynamic_slice_in_dim(q, t0, QB, axis=1)
        s = jnp.einsum('bqhd,bkhd->bhqk', qb, k) * scale
        p = masked_softmax(s, kpos[None, :] <= (t0 + jnp.arange(QB))[:, None])
        return jnp.einsum('bhqk,bkhd->bqhd', p.astype(v.dtype), v).reshape(B, QB, MLA_WIDTH)

    out = lax.map(block, jnp.arange(S // QB))
    return out.transpose(1, 0, 2, 3).reshape(B, S, MLA_WIDTH)


def even_layer(x, positions, norm_g, w_in, pe_k, pe_v, w_ck1, w_ck2, w_cv1, w_cv2, conv_w, w_out):
    B, S = x.shape[:2]
    proj = rms_norm(x, norm_g) @ w_in
    (q, kc, vc, ksl, vsl, kw, vw, gate_logits, gate_a,
     cb, cc, ch, gate_b) = _split(proj, IN_A_SIZES)
    heads = lambda t, n: t.reshape(B, S, n, NSA_HEAD_DIM)
    o_a = nsa_attention(heads(q, NSA_HEADS), heads(kc, NSA_KV_GROUPS), heads(vc, NSA_KV_GROUPS),
                        heads(ksl, NSA_KV_GROUPS), heads(vsl, NSA_KV_GROUPS),
                        heads(kw, NSA_KV_GROUPS), heads(vw, NSA_KV_GROUPS),
                        jax.nn.sigmoid(gate_logits).reshape(B, S, NSA_HEADS, 3), positions,
                        pe_k, pe_v, w_ck1, w_ck2, w_cv1, w_cv2)
    o_b = short_conv(cb, cc, ch, conv_w)
    mixed = jnp.concatenate([jax.nn.silu(gate_a) * o_a, jax.nn.silu(gate_b) * o_b], axis=-1)
    return x + mixed @ w_out


def odd_layer(x, positions, norm_g, w_in, q_norm, kv_norm, w_uq, w_ukv, w_out):
    proj = rms_norm(x, norm_g) @ w_in
    c_q, c_kv, k_rope_raw, gate = _split(proj, IN_C_SIZES)
    o = mla_attention(c_q, c_kv, k_rope_raw, positions, q_norm, kv_norm, w_uq, w_ukv)
    return x + (jax.nn.silu(gate) * o) @ w_out


def _normal(key, shape, scale):
    return jax.random.normal(key, shape, jnp.float32) * scale


def setup_inputs(seed: int = 0) -> dict:
    key = jax.random.key(seed)
    ks = jax.random.split(key, 20)
    hd = NSA_HEAD_DIM
    x = jax.random.normal(ks[0], (BATCH, SEQ, D_MODEL), jnp.float32)
    offset = jax.random.randint(ks[1], (BATCH, 1), 0, 1024, dtype=jnp.int32)
    positions = (offset + jnp.arange(SEQ, dtype=jnp.int32)[None, :]).astype(jnp.int32)
    return {
        'x': x,
        'positions': positions,
        'a_norm': 1.0 + _normal(ks[2], (N_EVEN, D_MODEL), 0.02),
        'a_w_in': _normal(ks[3], (N_EVEN, D_MODEL, IN_A_WIDTH), D_MODEL ** -0.5),
        'a_pe_k': _normal(ks[4], (N_EVEN, CMP_BLOCK, hd), 0.02),
        'a_pe_v': _normal(ks[5], (N_EVEN, CMP_BLOCK, hd), 0.02),
        'a_w_ck1': _normal(ks[6], (N_EVEN, CMP_BLOCK * hd, CMP_HIDDEN), (CMP_BLOCK * hd) ** -0.5),
        'a_w_ck2': _normal(ks[7], (N_EVEN, CMP_HIDDEN, hd), CMP_HIDDEN ** -0.5),
        'a_w_cv1': _normal(ks[8], (N_EVEN, CMP_BLOCK * hd, CMP_HIDDEN), (CMP_BLOCK * hd) ** -0.5),
        'a_w_cv2': _normal(ks[9], (N_EVEN, CMP_HIDDEN, hd), CMP_HIDDEN ** -0.5),
        'a_conv_w': _normal(ks[10], (N_EVEN, CONV_K, CONV_WIDTH), CONV_K ** -0.5),
        'a_w_out': _normal(ks[11], (N_EVEN, NSA_WIDTH + CONV_WIDTH, D_MODEL), (NSA_WIDTH + CONV_WIDTH) ** -0.5),
        'c_norm': 1.0 + _normal(ks[12], (N_ODD, D_MODEL), 0.02),
        'c_w_in': _normal(ks[13], (N_ODD, D_MODEL, IN_C_WIDTH), D_MODEL ** -0.5),
        'c_q_norm': 1.0 + _normal(ks[14], (N_ODD, MLA_Q_RANK), 0.02),
        'c_kv_norm': 1.0 + _normal(ks[15], (N_ODD, MLA_KV_RANK), 0.02),
        'c_w_uq': _normal(ks[16], (N_ODD, MLA_Q_RANK, MLA_HEADS * (MLA_NOPE_DIM + MLA_ROPE_DIM)), MLA_Q_RANK ** -0.5),
        'c_w_ukv': _normal(ks[17], (N_ODD, MLA_KV_RANK, MLA_HEADS * (MLA_NOPE_DIM + MLA_V_DIM)), MLA_KV_RANK ** -0.5),
        'c_w_out': _normal(ks[18], (N_ODD, MLA_WIDTH, D_MODEL), MLA_WIDTH ** -0.5),
        'final_norm': 1.0 + _normal(ks[19], (D_MODEL,), 0.02),
    }


def reference(x, positions, a_norm, a_w_in, a_pe_k, a_pe_v, a_w_ck1, a_w_ck2, a_w_cv1, a_w_cv2,
              a_conv_w, a_w_out, c_norm, c_w_in, c_q_norm, c_kv_norm, c_w_uq, c_w_ukv, c_w_out,
              final_norm):
    for i in range(DEPTH):
        j = i // 2
        if i % 2 == 0:
            x = even_layer(x, positions, a_norm[j], a_w_in[j], a_pe_k[j], a_pe_v[j],
                           a_w_ck1[j], a_w_ck2[j], a_w_cv1[j], a_w_cv2[j], a_conv_w[j], a_w_out[j])
        else:
            x = odd_layer(x, positions, c_norm[j], c_w_in[j], c_q_norm[j], c_kv_norm[j],
                          c_w_uq[j], c_w_ukv[j], c_w_out[j])
    return rms_norm(x, final_norm)
```

```python
import functools

import jax
import jax.numpy as jnp
import numpy as np
from jax import lax
from jax.experimental import pallas as pl
from jax.experimental.pallas import tpu as pltpu

F32 = jnp.float32
BF16 = jnp.bfloat16

D_MODEL = 1024
ROPE_THETA = 10000.0
RMS_EPS = 1e-6
NEG_INF = -1e30
FORCE_SCORE = 1e4
SEL_PENALTY = -1e9

NSA_HEADS = 8
NSA_KV_GROUPS = 2
NSA_HPG = NSA_HEADS // NSA_KV_GROUPS
NSA_HEAD_DIM = 64
NSA_WIDTH = NSA_HEADS * NSA_HEAD_DIM
NSA_KV_WIDTH = NSA_KV_GROUPS * NSA_HEAD_DIM
CMP_BLOCK = 32
CMP_STRIDE = 16
CMP_HIDDEN = 2 * NSA_HEAD_DIM
SLC_BLOCK = 64
SLC_TOPK = 16
WINDOW = 512
CONV_WIDTH = D_MODEL - NSA_WIDTH
CONV_K = 3

MLA_HEADS = 8
MLA_NOPE_DIM = 128
MLA_ROPE_DIM = 64
MLA_V_DIM = 128
MLA_Q_RANK = 256
MLA_KV_RANK = 256
MLA_WIDTH = MLA_HEADS * MLA_V_DIM
MLA_QK_PAD = 256

LANES = 128
VMEM_LIMIT = 56 * 1024 * 1024

PROJ_ROWS = 512
NSA_TQ = 128
NSA_KC = 256
NSA_KW = 128
MLA_TQ = 256
MLA_KC = 512


def _dot(a, b):
    return jnp.dot(a, b, preferred_element_type=F32)


def _dot_nt(a, b):
    return lax.dot_general(a, b, (((1,), (1,)), ((), ())), preferred_element_type=F32)


def _silu(x):
    return x * jax.nn.sigmoid(x)


def _rms(x, g):
    return x * lax.rsqrt(jnp.mean(x * x, axis=-1, keepdims=True) + RMS_EPS) * g


def _rope_tables(pos_col, invf_row, sign_row):
    ang = pos_col.astype(F32) * invf_row
    return jnp.cos(ang), jnp.sin(ang) * sign_row


def _rope_slab(s, cos2, sin2, first_half):
    swapped = jnp.where(first_half, pltpu.roll(s, 96, 1), pltpu.roll(s, 32, 1))
    return s * cos2 + swapped * sin2


def _even_proj_kernel(x_ref, pos_ref, g_ref, wqk_ref, wg_ref, wga_ref, wcb_ref, wcc_ref, wch_ref,
                      wgb_ref, convw_ref, invf_ref, sgn_ref,
                      q_ref, kc_ref, vc_ref, ksel_ref, vsl_ref, kw_ref, vw_ref, gates_ref, sga_ref,
                      mixb_ref, carry_ref, *, tiles_per_seq, tm):
    i = pl.program_id(0)
    xn = _rms(x_ref[...], g_ref[...]).astype(BF16)
    cos2, sin2 = _rope_tables(pos_ref[...], invf_ref[...], sgn_ref[...])
    lane = lax.broadcasted_iota(jnp.int32, (tm, LANES), 1)
    first_half = (lane & 32) == 0
    low = lane < 64
    rope = functools.partial(_rope_slab, cos2=cos2, sin2=sin2, first_half=first_half)

    qf = _dot(xn, wqk_ref[:, 0:NSA_WIDTH])
    scale = NSA_HEAD_DIM ** -0.5
    for c in range(NSA_WIDTH // LANES):
        q_ref[:, c * LANES:(c + 1) * LANES] = (rope(qf[:, c * LANES:(c + 1) * LANES]) * scale).astype(BF16)

    seg = _dot(xn, wqk_ref[:, NSA_WIDTH:NSA_WIDTH + 6 * LANES])
    kc_ref[...] = seg[:, 0:LANES]
    vc_ref[...] = seg[:, LANES:2 * LANES]
    ksl = rope(seg[:, 2 * LANES:3 * LANES])
    vsl_ref[...] = seg[:, 3 * LANES:4 * LANES].astype(BF16)
    kw_ref[...] = rope(seg[:, 4 * LANES:5 * LANES]).astype(BF16)
    vw_ref[...] = seg[:, 5 * LANES:6 * LANES].astype(BF16)

    spos = (i % tiles_per_seq) * tm + lax.broadcasted_iota(jnp.int32, (tm, LANES), 0)
    blk = lax.shift_right_logical(spos, 6)
    ksel_ref[:, 0:LANES] = jnp.where(low, ksl, (lane - 64 == blk).astype(F32)).astype(BF16)
    ksel_ref[:, LANES:2 * LANES] = jnp.where(low, (lane == blk).astype(F32), ksl).astype(BF16)

    gates_ref[...] = jax.nn.sigmoid(_dot(xn, wg_ref[...]))
    sga_ref[...] = _silu(_dot(xn, wga_ref[...])).astype(BF16)

    u = _dot(xn, wcc_ref[...]) * _dot(xn, wch_ref[...])

    @pl.when(i % tiles_per_seq == 0)
    def _():
        carry_ref[...] = jnp.zeros_like(carry_ref)

    row = lax.broadcasted_iota(jnp.int32, (tm, CONV_WIDTH), 0)
    prev1 = carry_ref[7:8, :]
    prev2 = carry_ref[6:7, :]
    u1 = jnp.where(row == 0, prev1, pltpu.roll(u, 1, 0))
    u2 = jnp.where(row == 0, prev2, jnp.where(row == 1, prev1, pltpu.roll(u, 2, 0)))
    carry_ref[...] = u[tm - 8:tm, :]
    w = convw_ref[...]
    y = w[0:1, :] * u2 + w[1:2, :] * u1 + w[2:3, :] * u
    cb = _dot(xn, wcb_ref[...])
    mixb_ref[...] = (_silu(_dot(xn, wgb_ref[...])) * (cb * y)).astype(BF16)


def _even_proj(x2d, pos2d, norm_g, w_in, conv_w, invf, sgn, seq):
    t = x2d.shape[0]
    tm = min(PROJ_ROWS, seq)
    assert seq % tm == 0 and t % tm == 0
    sizes = (NSA_WIDTH,) + (NSA_KV_WIDTH,) * 6 + (3 * NSA_HEADS, NSA_WIDTH) + (CONV_WIDTH,) * 4
    offs = np.concatenate([[0], np.cumsum(sizes)])
    col = lambda k: w_in[:, offs[k]:offs[k + 1]]
    wqk = jnp.concatenate([col(k) for k in range(7)], axis=1).astype(BF16)
    wg = jnp.pad(col(7), ((0, 0), (0, LANES - 3 * NSA_HEADS))).astype(BF16)
    wga, wcb, wcc, wch, wgb = (col(k).astype(BF16) for k in range(8, 13))

    full = lambda a: pl.BlockSpec(a.shape, lambda i: (0,) * a.ndim)
    rows = lambda n: pl.BlockSpec((tm, n), lambda i: (i, 0))
    out_shapes = [
        jax.ShapeDtypeStruct((t, NSA_WIDTH), BF16),
        jax.ShapeDtypeStruct((t, NSA_KV_WIDTH), F32),
        jax.ShapeDtypeStruct((t, NSA_KV_WIDTH), F32),
        jax.ShapeDtypeStruct((t, 2 * LANES), BF16),
        jax.ShapeDtypeStruct((t, NSA_KV_WIDTH), BF16),
        jax.ShapeDtypeStruct((t, NSA_KV_WIDTH), BF16),
        jax.ShapeDtypeStruct((t, NSA_KV_WIDTH), BF16),
        jax.ShapeDtypeStruct((t, LANES), F32),
        jax.ShapeDtypeStruct((t, NSA_WIDTH), BF16),
        jax.ShapeDtypeStruct((t, CONV_WIDTH), BF16),
    ]
    ins = [x2d, pos2d, norm_g.reshape(1, -1), wqk, wg, wga, wcb, wcc, wch, wgb, conv_w, invf, sgn]
    in_specs = [rows(D_MODEL), rows(1)] + [full(a) for a in ins[2:]]
    return pl.pallas_call(
        functools.partial(_even_proj_kernel, tiles_per_seq=seq // tm, tm=tm),
        grid=(t // tm,),
        in_specs=in_specs,
        out_specs=[rows(s.shape[1]) for s in out_shapes],
        out_shape=out_shapes,
        scratch_shapes=[pltpu.VMEM((8, CONV_WIDTH), F32)],
        compiler_params=pltpu.CompilerParams(dimension_semantics=("arbitrary",),
                                             vmem_limit_bytes=VMEM_LIMIT),
        name="even_proj",
    )(*ins)


def _compress_kernel(kc_ref, vc_ref, pos_ref, pek_ref, pev_ref, w1k_ref, w2k_ref, w1v_ref, w2v_ref,
                     invf_ref, sgn_ref, kcmp_ref, vcmp_ref, *, nchunk):
    half = NSA_KV_GROUPS * CMP_HIDDEN

    def comp(raw_ref, pe_ref, w1_ref, w2_ref):
        raw = raw_ref[0]
        top = _dot((raw + pe_ref[0:1, :]).astype(BF16), w1_ref[:, 0:half])
        bot = _dot((raw + pe_ref[1:2, :]).astype(BF16), w1_ref[:, half:2 * half])
        h = _silu(top + pltpu.roll(bot, nchunk - 1, 0))
        return _dot(h.astype(BF16), w2_ref[...])

    cos2, sin2 = _rope_tables(pos_ref[0], invf_ref[...], sgn_ref[...])
    lane = lax.broadcasted_iota(jnp.int32, (nchunk, LANES), 1)
    kcmp = comp(kc_ref, pek_ref, w1k_ref, w2k_ref)
    kcmp_ref[0] = _rope_slab(kcmp, cos2, sin2, (lane & 32) == 0).astype(BF16)
    vcmp_ref[0] = comp(vc_ref, pev_ref, w1v_ref, w2v_ref).astype(BF16)


def _compress_weights(pe, w1, w2):
    eye = jnp.eye(NSA_KV_GROUPS, dtype=F32)
    half_rows = CMP_BLOCK // 2
    w1r = w1.reshape(CMP_BLOCK, NSA_HEAD_DIM, CMP_HIDDEN)
    big = lambda part: jnp.einsum('ldn,gh->lgdhn', part, eye).reshape(
        half_rows * NSA_KV_WIDTH, NSA_KV_GROUPS * CMP_HIDDEN)
    w1big = jnp.concatenate([big(w1r[:half_rows]), big(w1r[half_rows:])], axis=1).astype(BF16)
    w2big = jnp.einsum('nd,gh->gnhd', w2, eye).reshape(NSA_KV_GROUPS * CMP_HIDDEN, NSA_KV_WIDTH).astype(BF16)
    pe_row = lambda part: jnp.broadcast_to(part[:, None, :], (half_rows, NSA_KV_GROUPS, NSA_HEAD_DIM)).reshape(1, -1)
    pe_rows = jnp.concatenate([pe_row(pe[:half_rows]), pe_row(pe[half_rows:]),
                               jnp.zeros((6, half_rows * NSA_KV_WIDTH), F32)], axis=0)
    return pe_rows, w1big, w2big


def _compress(kc, vc, pos_cmp, pe_k, pe_v, w_ck1, w_ck2, w_cv1, w_cv2, invf, sgn):
    b, seq, _ = kc.shape
    nchunk = seq // CMP_STRIDE
    width = CMP_STRIDE * NSA_KV_WIDTH
    pek, w1k, w2k = _compress_weights(pe_k, w_ck1, w_ck2)
    pev, w1v, w2v = _compress_weights(pe_v, w_cv1, w_cv2)
    ins = [kc.reshape(b, nchunk, width), vc.reshape(b, nchunk, width), pos_cmp,
           pek, pev, w1k, w2k, w1v, w2v, invf, sgn]
    full = lambda a: pl.BlockSpec(a.shape, lambda i: (0,) * a.ndim)
    per_b = lambda n: pl.BlockSpec((1, nchunk, n), lambda i: (i, 0, 0))
    out_shape = [jax.ShapeDtypeStruct((b, nchunk, NSA_KV_WIDTH), BF16)] * 2
    return pl.pallas_call(
        functools.partial(_compress_kernel, nchunk=nchunk),
        grid=(b,),
        in_specs=[per_b(width), per_b(width), per_b(1)] + [full(a) for a in ins[3:]],
        out_specs=[per_b(NSA_KV_WIDTH)] * 2,
        out_shape=out_shape,
        compiler_params=pltpu.CompilerParams(dimension_semantics=("arbitrary",),
                                             vmem_limit_bytes=VMEM_LIMIT),
        name="compress",
    )(*ins)


def _nsa_kernel(q_ref, kcmp_ref, vcmp_ref, ksel_ref, vsl_ref, kw_ref, vw_ref, gates_ref, sga_ref,
                ov_ref, ex_ref, out_ref, v_scr, m_scr, l_scr, acc_scr, *, tq, n_cmp, n_top):
    t0 = pl.program_id(1) * tq
    rows = NSA_HPG * tq
    ncp = kcmp_ref.shape[1]
    lane = lax.broadcasted_iota(jnp.int32, (tq, LANES), 1)
    low = lane < 64

    def flash(qa, k_ref, v_ref, col0, lo, hi, kc, mask_fn):
        m_scr[...] = jnp.full(m_scr.shape, NEG_INF, F32)
        l_scr[...] = jnp.zeros(l_scr.shape, F32)
        acc_scr[...] = jnp.zeros(acc_scr.shape, F32)
        rq = t0 + (lax.broadcasted_iota(jnp.int32, (rows, kc), 0) & (tq - 1))
        colk = lax.broadcasted_iota(jnp.int32, (rows, kc), 1)

        def body(j, carry):
            start = pl.multiple_of(j * kc, kc)
            k = k_ref[0, pl.ds(start, kc), col0:col0 + LANES]
            s = jnp.where(mask_fn(colk + j * kc, rq), _dot_nt(qa, k), NEG_INF)
            m_old = m_scr[...]
            m_new = jnp.maximum(m_old, jnp.max(s, axis=1, keepdims=True))
            alpha = jnp.exp(m_old - m_new)
            p = jnp.exp(s - m_new)
            l_scr[...] = alpha * l_scr[...] + jnp.sum(p, axis=1, keepdims=True)
            acc_scr[...] = alpha * acc_scr[...] + _dot(p.astype(BF16), v_ref[0, pl.ds(start, kc), :])
            m_scr[...] = m_new
            return carry

        lax.fori_loop(lo, hi, body, 0)
        return acc_scr[...] / l_scr[...]

    def natural(o, g, pair):
        a = o[(2 * pair) * tq:(2 * pair + 1) * tq]
        b = o[(2 * pair + 1) * tq:(2 * pair + 2) * tq]
        if g == 0:
            return jnp.where(low, a, pltpu.roll(b, 64, 1))
        return jnp.where(low, pltpu.roll(a, 64, 1), b)

    gexp = _dot(gates_ref[0].astype(BF16), ex_ref[...])

    for g in range(NSA_KV_GROUPS):
        parts = []
        for pair in range(2):
            x = q_ref[0, :, (2 * g + pair) * LANES:(2 * g + pair + 1) * LANES].astype(F32)
            xr = pltpu.roll(x, 64, 1)
            if g == 0:
                parts += [jnp.where(low, x, 0.0), jnp.where(low, xr, 0.0)]
            else:
                parts += [jnp.where(low, 0.0, xr), jnp.where(low, 0.0, x)]
        qz = jnp.concatenate(parts, axis=0)
        qzb = qz.astype(BF16)

        s = _dot_nt(qzb, kcmp_ref[0])
        colc = lax.broadcasted_iota(jnp.int32, (rows, ncp), 1)
        rqc = t0 + (lax.broadcasted_iota(jnp.int32, (rows, ncp), 0) & (tq - 1))
        valid = (colc * CMP_STRIDE + (CMP_BLOCK - 1) <= rqc) & (colc < n_cmp)
        s = jnp.where(valid, s, NEG_INF)
        p = jnp.where(valid, jnp.exp(s - jnp.max(s, axis=1, keepdims=True)), 0.0)
        l = jnp.sum(p, axis=1, keepdims=True)
        pn = p * jnp.where(l > 0.0, 1.0 / l, 0.0)
        o_c = _dot(pn.astype(BF16), vcmp_ref[0])

        psum = pn[0:tq] + pn[tq:2 * tq] + pn[2 * tq:3 * tq] + pn[3 * tq:4 * tq]
        hi = psum.astype(BF16)
        lo = (psum - hi.astype(F32)).astype(BF16)
        imp_t = (_dot(hi, ov_ref[...]) + _dot(lo, ov_ref[...])).T[0:64, :]
        nb = lax.broadcasted_iota(jnp.int32, (64, tq), 0)
        tqv = t0 + lax.broadcasted_iota(jnp.int32, (64, tq), 1)
        cur = lax.shift_right_logical(tqv, 6)
        forced = (nb == 0) | (nb == cur) | (nb == cur - 1)
        v = jnp.where(nb * SLC_BLOCK <= tqv, jnp.where(forced, FORCE_SCORE, imp_t), -1.0)
        v_scr[...] = v
        rank = jnp.zeros((64, tq), jnp.int32)
        for mblk in range(64):
            vm = jnp.broadcast_to(v_scr[mblk:mblk + 1, :], (64, tq))
            rank = rank + jnp.where(nb > mblk, (vm >= v).astype(jnp.int32), (vm > v).astype(jnp.int32))
        pen = jnp.where(rank < n_top, 0.0, SEL_PENALTY)
        zero = jnp.zeros((64, tq), F32)
        pen_t = jnp.concatenate([zero, pen] if g == 0 else [pen, zero], axis=0).T
        qa = (qz + jnp.concatenate([pen_t] * NSA_HPG, axis=0)).astype(BF16)

        o_s = flash(qa, ksel_ref, vsl_ref, g * LANES, 0, (t0 + tq + NSA_KC - 1) // NSA_KC, NSA_KC,
                    lambda kpos, rq: kpos <= rq)

        o_w = flash(qzb, kw_ref, vw_ref, 0, jnp.maximum(t0 - WINDOW, 0) // NSA_KW, (t0 + tq) // NSA_KW,
                    NSA_KW, lambda kpos, rq: (kpos <= rq) & (kpos > rq - WINDOW))

        for pair in range(2):
            c0 = (2 * g + pair) * LANES
            merged = (gexp[:, c0:c0 + LANES] * natural(o_c, g, pair)
                      + gexp[:, NSA_WIDTH + c0:NSA_WIDTH + c0 + LANES] * natural(o_s, g, pair)
                      + gexp[:, 2 * NSA_WIDTH + c0:2 * NSA_WIDTH + c0 + LANES] * natural(o_w, g, pair))
            out_ref[0, :, c0:c0 + LANES] = (sga_ref[0, :, c0:c0 + LANES].astype(F32) * merged).astype(BF16)


def _nsa_constants(ncp, n_slc):
    cmp_starts = np.arange(ncp) * CMP_STRIDE
    slc_starts = np.arange(LANES) * SLC_BLOCK
    ov = ((cmp_starts[:, None] < slc_starts[None, :] + SLC_BLOCK)
          & (cmp_starts[:, None] + CMP_BLOCK > slc_starts[None, :])
          & (np.arange(LANES)[None, :] < n_slc))
    ex = np.zeros((LANES, 3 * NSA_WIDTH), np.float32)
    for h in range(NSA_HEADS):
        for br in range(3):
            ex[h * 3 + br, br * NSA_WIDTH + h * NSA_HEAD_DIM: br * NSA_WIDTH + (h + 1) * NSA_HEAD_DIM] = 1.0
    return jnp.asarray(ov.astype(np.float32), BF16), jnp.asarray(ex, BF16)


def _nsa_attention(q, kcmp, vcmp, ksel, vsl, kw, vw, gates, sga):
    b, seq, _ = q.shape
    tq = min(NSA_TQ, seq)
    n_slc = seq // SLC_BLOCK
    assert n_slc <= 64 and seq % NSA_KC == 0 and seq % tq == 0
    ncp = kcmp.shape[1]
    n_cmp = (seq - CMP_BLOCK) // CMP_STRIDE + 1
    ov, ex = _nsa_constants(ncp, n_slc)
    rows = NSA_HPG * tq
    tile = lambda n: pl.BlockSpec((1, tq, n), lambda bi, qi: (bi, qi, 0))
    per_b = lambda a: pl.BlockSpec((1,) + a.shape[1:], lambda bi, qi: (bi, 0, 0))
    full = lambda a: pl.BlockSpec(a.shape, lambda bi, qi: (0, 0))
    return pl.pallas_call(
        functools.partial(_nsa_kernel, tq=tq, n_cmp=n_cmp, n_top=min(SLC_TOPK, n_slc)),
        grid=(b, seq // tq),
        in_specs=[tile(NSA_WIDTH), per_b(kcmp), per_b(vcmp), per_b(ksel), per_b(vsl), per_b(kw), per_b(vw),
                  tile(LANES), tile(NSA_WIDTH), full(ov), full(ex)],
        out_specs=tile(NSA_WIDTH),
        out_shape=jax.ShapeDtypeStruct((b, seq, NSA_WIDTH), BF16),
        scratch_shapes=[pltpu.VMEM((64, tq), F32), pltpu.VMEM((rows, 1), F32), pltpu.VMEM((rows, 1), F32),
                        pltpu.VMEM((rows, LANES), F32)],
        compiler_params=pltpu.CompilerParams(dimension_semantics=("arbitrary", "arbitrary"),
                                             vmem_limit_bytes=VMEM_LIMIT),
        name="nsa_attn",
    )(q, kcmp, vcmp, ksel, vsl, kw, vw, gates, sga, ov, ex)


def _odd_proj_kernel(x_ref, mixa_ref, mixb_ref, pos_ref, wout_ref, g_ref, wc_ref, wgate_ref, qn_ref, kvn_ref,
                     wuq_ref, wuk_ref, wuv_ref, invf_ref, sgn_ref,
                     y_ref, q_ref, k_ref, v_ref, sg_ref, *, tm):
    y = (x_ref[...] + _dot(mixa_ref[...], wout_ref[0:NSA_WIDTH, :])
         + _dot(mixb_ref[...], wout_ref[NSA_WIDTH:, :]))
    y_ref[...] = y
    yn = _rms(y, g_ref[...]).astype(BF16)
    cos2, sin2 = _rope_tables(pos_ref[...], invf_ref[...], sgn_ref[...])
    lane = lax.broadcasted_iota(jnp.int32, (tm, LANES), 1)
    rope = functools.partial(_rope_slab, cos2=cos2, sin2=sin2, first_half=(lane & 32) == 0)

    seg = _dot(yn, wc_ref[...])
    k_pe = rope(seg[:, MLA_Q_RANK + MLA_KV_RANK:]).astype(BF16)
    cq = _rms(seg[:, 0:MLA_Q_RANK], qn_ref[...]).astype(BF16)
    ckv = _rms(seg[:, MLA_Q_RANK:MLA_Q_RANK + MLA_KV_RANK], kvn_ref[...]).astype(BF16)
    scale = (MLA_NOPE_DIM + MLA_ROPE_DIM) ** -0.5
    qf = _dot(cq, wuq_ref[...])
    kf = _dot(ckv, wuk_ref[...])
    for h in range(MLA_HEADS):
        c0 = h * MLA_QK_PAD
        q_ref[:, c0:c0 + LANES] = (qf[:, c0:c0 + LANES] * scale).astype(BF16)
        q_ref[:, c0 + LANES:c0 + 2 * LANES] = (rope(qf[:, c0 + LANES:c0 + 2 * LANES]) * scale).astype(BF16)
        k_ref[:, c0:c0 + LANES] = kf[:, h * LANES:(h + 1) * LANES].astype(BF16)
        k_ref[:, c0 + LANES:c0 + 2 * LANES] = k_pe
    v_ref[...] = _dot(ckv, wuv_ref[...]).astype(BF16)
    sg_ref[...] = _silu(_dot(yn, wgate_ref[...])).astype(BF16)


def _odd_proj(x2d, mixa, mixb, pos2d, a_w_out, c_norm, c_w_in, q_norm, kv_norm, w_uq, w_ukv, invf, sgn, seq):
    t = x2d.shape[0]
    tm = min(PROJ_ROWS, seq)
    r0, r1, r2 = MLA_Q_RANK, MLA_Q_RANK + MLA_KV_RANK, MLA_Q_RANK + MLA_KV_RANK + MLA_ROPE_DIM
    wc = jnp.pad(c_w_in[:, :r2], ((0, 0), (0, LANES - MLA_ROPE_DIM))).astype(BF16)
    wgate = c_w_in[:, r2:].astype(BF16)
    wuq = jnp.pad(w_uq.reshape(MLA_Q_RANK, MLA_HEADS, MLA_NOPE_DIM + MLA_ROPE_DIM),
                  ((0, 0), (0, 0), (0, MLA_QK_PAD - MLA_NOPE_DIM - MLA_ROPE_DIM))
                  ).reshape(MLA_Q_RANK, MLA_HEADS * MLA_QK_PAD).astype(BF16)
    wukv = w_ukv.reshape(MLA_KV_RANK, MLA_HEADS, MLA_NOPE_DIM + MLA_V_DIM)
    wuk = wukv[:, :, :MLA_NOPE_DIM].reshape(MLA_KV_RANK, -1).astype(BF16)
    wuv = wukv[:, :, MLA_NOPE_DIM:].reshape(MLA_KV_RANK, -1).astype(BF16)
    del r0, r1
    ins = [x2d, mixa, mixb, pos2d, a_w_out.astype(BF16), c_norm.reshape(1, -1), wc, wgate,
           q_norm.reshape(1, -1), kv_norm.reshape(1, -1), wuq, wuk, wuv, invf, sgn]
    full = lambda a: pl.BlockSpec(a.shape, lambda i: (0,) * a.ndim)
    rows = lambda n: pl.BlockSpec((tm, n), lambda i: (i, 0))
    out_shapes = [
        jax.ShapeDtypeStruct((t, D_MODEL), F32),
        jax.ShapeDtypeStruct((t, MLA_HEADS * MLA_QK_PAD), BF16),
        jax.ShapeDtypeStruct((t, MLA_HEADS * MLA_QK_PAD), BF16),
        jax.ShapeDtypeStruct((t, MLA_WIDTH), BF16),
        jax.ShapeDtypeStruct((t, MLA_WIDTH), BF16),
    ]
    return pl.pallas_call(
        functools.partial(_odd_proj_kernel, tm=tm),
        grid=(t // tm,),
        in_specs=[rows(D_MODEL), rows(NSA_WIDTH), rows(CONV_WIDTH), rows(1)] + [full(a) for a in ins[4:]],
        out_specs=[rows(s.shape[1]) for s in out_shapes],
        out_shape=out_shapes,
        compiler_params=pltpu.CompilerParams(dimension_semantics=("arbitrary",),
                                             vmem_limit_bytes=VMEM_LIMIT),
        name="odd_proj",
    )(*ins)


def _mla_kernel(q_ref, k_ref, v_ref, o_ref, m_scr, l_scr, acc_scr, *, tq, kc):
    t0 = pl.program_id(2) * tq
    q = q_ref[0]
    m_scr[...] = jnp.full(m_scr.shape, NEG_INF, F32)
    l_scr[...] = jnp.zeros(l_scr.shape, F32)
    acc_scr[...] = jnp.zeros(acc_scr.shape, F32)
    rq = t0 + lax.broadcasted_iota(jnp.int32, (tq, kc), 0)
    colk = lax.broadcasted_iota(jnp.int32, (tq, kc), 1)

    def body(j, carry):
        start = pl.multiple_of(j * kc, kc)
        s = jnp.where(colk + j * kc <= rq, _dot_nt(q, k_ref[0, pl.ds(start, kc), :]), NEG_INF)
        m_old = m_scr[...]
        m_new = jnp.maximum(m_old, jnp.max(s, axis=1, keepdims=True))
        alpha = jnp.exp(m_old - m_new)
        p = jnp.exp(s - m_new)
        l_scr[...] = alpha * l_scr[...] + jnp.sum(p, axis=1, keepdims=True)
        acc_scr[...] = alpha * acc_scr[...] + _dot(p.astype(BF16), v_ref[0, pl.ds(start, kc), :])
        m_scr[...] = m_new
        return carry

    lax.fori_loop(0, (t0 + tq + kc - 1) // kc, body, 0)
    o_ref[0] = (acc_scr[...] / l_scr[...]).astype(BF16)


def _mla_attention(q, k, v):
    b, seq, _ = q.shape
    tq = min(MLA_TQ, seq)
    kc = min(MLA_KC, seq)
    assert seq % tq == 0 and seq % kc == 0
    return pl.pallas_call(
        functools.partial(_mla_kernel, tq=tq, kc=kc),
        grid=(b, MLA_HEADS, seq // tq),
        in_specs=[pl.BlockSpec((1, tq, MLA_QK_PAD), lambda bi, h, qi: (bi, qi, h)),
                  pl.BlockSpec((1, seq, MLA_QK_PAD), lambda bi, h, qi: (bi, 0, h)),
                  pl.BlockSpec((1, seq, MLA_V_DIM), lambda bi, h, qi: (bi, 0, h))],
        out_specs=pl.BlockSpec((1, tq, MLA_V_DIM), lambda bi, h, qi: (bi, qi, h)),
        out_shape=jax.ShapeDtypeStruct((b, seq, MLA_WIDTH), BF16),
        scratch_shapes=[pltpu.VMEM((tq, 1), F32), pltpu.VMEM((tq, 1), F32), pltpu.VMEM((tq, MLA_V_DIM), F32)],
        compiler_params=pltpu.CompilerParams(dimension_semantics=("arbitrary", "arbitrary", "arbitrary"),
                                             vmem_limit_bytes=VMEM_LIMIT),
        name="mla_attn",
    )(q, k, v)


def _final_kernel(y_ref, o_ref, sg_ref, w_ref, g_ref, out_ref):
    gated = (sg_ref[...].astype(F32) * o_ref[...].astype(F32)).astype(BF16)
    out_ref[...] = _rms(y_ref[...] + _dot(gated, w_ref[...]), g_ref[...])


def _final(y, o, sg, w_out, final_norm, seq):
    t = y.shape[0]
    tm = min(PROJ_ROWS, seq)
    rows = lambda n: pl.BlockSpec((tm, n), lambda i: (i, 0))
    full = lambda a: pl.BlockSpec(a.shape, lambda i: (0,) * a.ndim)
    w = w_out.astype(BF16)
    g = final_norm.reshape(1, -1)
    return pl.pallas_call(
        _final_kernel,
        grid=(t // tm,),
        in_specs=[rows(D_MODEL), rows(MLA_WIDTH), rows(MLA_WIDTH), full(w), full(g)],
        out_specs=rows(D_MODEL),
        out_shape=jax.ShapeDtypeStruct((t, D_MODEL), F32),
        compiler_params=pltpu.CompilerParams(dimension_semantics=("arbitrary",),
                                             vmem_limit_bytes=VMEM_LIMIT),
        name="final",
    )(y, o, sg, w, g)


def _rope_constants():
    half = NSA_HEAD_DIM // 2
    inv_freq = ROPE_THETA ** (-jnp.arange(half, dtype=F32) / half)
    invf = jnp.tile(inv_freq, LANES // half).reshape(1, LANES)
    sgn = jnp.tile(jnp.concatenate([-jnp.ones((half,), F32), jnp.ones((half,), F32)]), LANES // (2 * half))
    return invf, sgn.reshape(1, LANES)


def kernel(x, positions, a_norm, a_w_in, a_pe_k, a_pe_v, a_w_ck1, a_w_ck2, a_w_cv1, a_w_cv2, a_conv_w, a_w_out, c_norm, c_w_in, c_q_norm, c_kv_norm, c_w_uq, c_w_ukv, c_w_out, final_norm):
    b, seq, d = x.shape
    assert d == D_MODEL and NSA_HEAD_DIM == MLA_ROPE_DIM
    assert a_norm.shape[0] == 1 and c_norm.shape[0] == 1
    t = b * seq
    invf, sgn = _rope_constants()
    x2d = x.reshape(t, d)
    pos2d = positions.reshape(t, 1)

    q, kc, vc, ksel, vsl, kw, vw, gates, sga, mixb = _even_proj(
        x2d, pos2d, a_norm[0], a_w_in[0], a_conv_w[0], invf, sgn, seq)
    nchunk = seq // CMP_STRIDE
    pos_cmp = jnp.pad(positions[:, CMP_BLOCK - 1::CMP_STRIDE], ((0, 0), (0, 1)))[:, :nchunk, None]
    kcmp, vcmp = _compress(kc.reshape(b, seq, -1), vc.reshape(b, seq, -1), pos_cmp, a_pe_k[0], a_pe_v[0],
                           a_w_ck1[0], a_w_ck2[0], a_w_cv1[0], a_w_cv2[0], invf, sgn)
    r3 = lambda a: a.reshape(b, seq, a.shape[-1])
    mixa = _nsa_attention(r3(q), kcmp, vcmp, r3(ksel), r3(vsl), r3(kw), r3(vw), r3(gates), r3(sga))

    y, mq, mk, mv, sg = _odd_proj(x2d, mixa.reshape(t, -1), mixb, pos2d, a_w_out[0], c_norm[0], c_w_in[0],
                                  c_q_norm[0], c_kv_norm[0], c_w_uq[0], c_w_ukv[0], invf, sgn, seq)
    o = _mla_attention(r3(mq), r3(mk), r3(mv))
    out = _final(y, o.reshape(t, -1), sg, c_w_out[0], final_norm, seq)
    return out.reshape(b, seq, d)
```

```python
import functools

import jax
import jax.numpy as jnp
import numpy as np
from jax import lax
from jax.experimental import pallas as pl
from jax.experimental.pallas import tpu as pltpu

F32 = jnp.float32
BF16 = jnp.bfloat16

D_MODEL = 1024
ROPE_THETA = 10000.0
RMS_EPS = 1e-6
NEG_INF = -1e30
FORCE_SCORE = 1e4
SEL_PENALTY = -1e9

NSA_HEADS = 8
NSA_KV_GROUPS = 2
NSA_HPG = NSA_HEADS // NSA_KV_GROUPS
NSA_HEAD_DIM = 64
NSA_WIDTH = NSA_HEADS * NSA_HEAD_DIM
NSA_KV_WIDTH = NSA_KV_GROUPS * NSA_HEAD_DIM
CMP_BLOCK = 32
CMP_STRIDE = 16
CMP_HIDDEN = 2 * NSA_HEAD_DIM
SLC_BLOCK = 64
SLC_TOPK = 16
WINDOW = 512
CONV_WIDTH = D_MODEL - NSA_WIDTH
CONV_K = 3

MLA_HEADS = 8
MLA_NOPE_DIM = 128
MLA_ROPE_DIM = 64
MLA_V_DIM = 128
MLA_Q_RANK = 256
MLA_KV_RANK = 256
MLA_WIDTH = MLA_HEADS * MLA_V_DIM
MLA_QK_PAD = 256

LANES = 128
VMEM_LIMIT = 56 * 1024 * 1024

PROJ_ROWS = 512
NSA_TQ = 256
MLA_TQ = 512
MLA_KC = 512
MLA_HEADS_PER_STEP = 4
LOG2_E = 1.4426950408889634


def _dot(a, b):
    return jnp.dot(a, b, preferred_element_type=F32)


def _dot_nt(a, b):
    return lax.dot_general(a, b, (((1,), (1,)), ((), ())), preferred_element_type=F32)


def _silu(x):
    return x * jax.nn.sigmoid(x)


def _rms(x, g):
    return x * lax.rsqrt(jnp.mean(x * x, axis=-1, keepdims=True) + RMS_EPS) * g


def _rope_tables(pos_col, invf_row, sign_row):
    ang = pos_col.astype(F32) * invf_row
    return jnp.cos(ang), jnp.sin(ang) * sign_row


def _rope_slab(s, cos2, sin2, first_half):
    swapped = jnp.where(first_half, pltpu.roll(s, 96, 1), pltpu.roll(s, 32, 1))
    return s * cos2 + swapped * sin2


def _even_proj_kernel(x_ref, pos_ref, g_ref, wqk_ref, wg_ref, wga_ref, wcb_ref, wcc_ref, wch_ref,
                      wgb_ref, convw_ref, invf_ref, sgn_ref,
                      q_ref, kc_ref, vc_ref, ksel_ref, vsl_ref, kw_ref, vw_ref, gates_ref, sga_ref,
                      mixb_ref, carry_ref, *, tiles_per_seq, tm):
    i = pl.program_id(0)
    xn = _rms(x_ref[...], g_ref[...]).astype(BF16)
    cos2, sin2 = _rope_tables(pos_ref[...], invf_ref[...], sgn_ref[...])
    lane = lax.broadcasted_iota(jnp.int32, (tm, LANES), 1)
    first_half = (lane & 32) == 0
    low = lane < 64
    rope = functools.partial(_rope_slab, cos2=cos2, sin2=sin2, first_half=first_half)

    qf = _dot(xn, wqk_ref[:, 0:NSA_WIDTH])
    scale = NSA_HEAD_DIM ** -0.5 * LOG2_E
    for c in range(NSA_WIDTH // LANES):
        q_ref[:, c * LANES:(c + 1) * LANES] = (rope(qf[:, c * LANES:(c + 1) * LANES]) * scale).astype(BF16)

    seg = _dot(xn, wqk_ref[:, NSA_WIDTH:NSA_WIDTH + 6 * LANES])
    kc_ref[...] = seg[:, 0:LANES]
    vc_ref[...] = seg[:, LANES:2 * LANES]
    ksl = rope(seg[:, 2 * LANES:3 * LANES])
    vsl_ref[...] = seg[:, 3 * LANES:4 * LANES].astype(BF16)
    kw_ref[...] = rope(seg[:, 4 * LANES:5 * LANES]).astype(BF16)
    vw_ref[...] = seg[:, 5 * LANES:6 * LANES].astype(BF16)

    spos = (i % tiles_per_seq) * tm + lax.broadcasted_iota(jnp.int32, (tm, LANES), 0)
    blk = lax.shift_right_logical(spos, 6)
    ksel_ref[:, 0:LANES] = jnp.where(low, ksl, (lane - 64 == blk).astype(F32)).astype(BF16)
    ksel_ref[:, LANES:2 * LANES] = jnp.where(low, (lane == blk).astype(F32), ksl).astype(BF16)

    gates_ref[...] = jax.nn.sigmoid(_dot(xn, wg_ref[...]))
    sga_ref[...] = _silu(_dot(xn, wga_ref[...])).astype(BF16)

    u = _dot(xn, wcc_ref[...]) * _dot(xn, wch_ref[...])

    @pl.when(i % tiles_per_seq == 0)
    def _():
        carry_ref[...] = jnp.zeros_like(carry_ref)

    row = lax.broadcasted_iota(jnp.int32, (tm, CONV_WIDTH), 0)
    prev1 = carry_ref[7:8, :]
    prev2 = carry_ref[6:7, :]
    u1 = jnp.where(row == 0, prev1, pltpu.roll(u, 1, 0))
    u2 = jnp.where(row == 0, prev2, jnp.where(row == 1, prev1, pltpu.roll(u, 2, 0)))
    carry_ref[...] = u[tm - 8:tm, :]
    w = convw_ref[...]
    y = w[0:1, :] * u2 + w[1:2, :] * u1 + w[2:3, :] * u
    cb = _dot(xn, wcb_ref[...])
    mixb_ref[...] = (_silu(_dot(xn, wgb_ref[...])) * (cb * y)).astype(BF16)


def _even_proj(x2d, pos2d, norm_g, w_in, conv_w, invf, sgn, seq):
    t = x2d.shape[0]
    tm = min(PROJ_ROWS, seq)
    assert seq % tm == 0 and t % tm == 0
    sizes = (NSA_WIDTH,) + (NSA_KV_WIDTH,) * 6 + (3 * NSA_HEADS, NSA_WIDTH) + (CONV_WIDTH,) * 4
    offs = np.concatenate([[0], np.cumsum(sizes)])
    col = lambda k: w_in[:, offs[k]:offs[k + 1]]
    wqk = jnp.concatenate([col(k) for k in range(7)], axis=1).astype(BF16)
    wg = jnp.pad(col(7), ((0, 0), (0, LANES - 3 * NSA_HEADS))).astype(BF16)
    wga, wcb, wcc, wch, wgb = (col(k).astype(BF16) for k in range(8, 13))

    full = lambda a: pl.BlockSpec(a.shape, lambda i: (0,) * a.ndim)
    rows = lambda n: pl.BlockSpec((tm, n), lambda i: (i, 0))
    out_shapes = [
        jax.ShapeDtypeStruct((t, NSA_WIDTH), BF16),
        jax.ShapeDtypeStruct((t, NSA_KV_WIDTH), F32),
        jax.ShapeDtypeStruct((t, NSA_KV_WIDTH), F32),
        jax.ShapeDtypeStruct((t, 2 * LANES), BF16),
        jax.ShapeDtypeStruct((t, NSA_KV_WIDTH), BF16),
        jax.ShapeDtypeStruct((t, NSA_KV_WIDTH), BF16),
        jax.ShapeDtypeStruct((t, NSA_KV_WIDTH), BF16),
        jax.ShapeDtypeStruct((t, LANES), F32),
        jax.ShapeDtypeStruct((t, NSA_WIDTH), BF16),
        jax.ShapeDtypeStruct((t, CONV_WIDTH), BF16),
    ]
    ins = [x2d, pos2d, norm_g.reshape(1, -1), wqk, wg, wga, wcb, wcc, wch, wgb, conv_w, invf, sgn]
    in_specs = [rows(D_MODEL), rows(1)] + [full(a) for a in ins[2:]]
    return pl.pallas_call(
        functools.partial(_even_proj_kernel, tiles_per_seq=seq // tm, tm=tm),
        grid=(t // tm,),
        in_specs=in_specs,
        out_specs=[rows(s.shape[1]) for s in out_shapes],
        out_shape=out_shapes,
        scratch_shapes=[pltpu.VMEM((8, CONV_WIDTH), F32)],
        compiler_params=pltpu.CompilerParams(dimension_semantics=("arbitrary",),
                                             vmem_limit_bytes=VMEM_LIMIT),
        name="even_proj",
    )(*ins)


def _compress_kernel(kc_ref, vc_ref, pos_ref, pek_ref, pev_ref, w1k_ref, w2k_ref, w1v_ref, w2v_ref,
                     invf_ref, sgn_ref, kcmp_ref, vcmp_ref, *, nchunk):
    half = NSA_KV_GROUPS * CMP_HIDDEN

    def comp(raw_ref, pe_ref, w1_ref, w2_ref):
        raw = raw_ref[0]
        top = _dot((raw + pe_ref[0:1, :]).astype(BF16), w1_ref[:, 0:half])
        bot = _dot((raw + pe_ref[1:2, :]).astype(BF16), w1_ref[:, half:2 * half])
        h = _silu(top + pltpu.roll(bot, nchunk - 1, 0))
        return _dot(h.astype(BF16), w2_ref[...])

    cos2, sin2 = _rope_tables(pos_ref[0], invf_ref[...], sgn_ref[...])
    lane = lax.broadcasted_iota(jnp.int32, (nchunk, LANES), 1)
    kcmp = comp(kc_ref, pek_ref, w1k_ref, w2k_ref)
    kcmp_ref[0] = _rope_slab(kcmp, cos2, sin2, (lane & 32) == 0).astype(BF16)
    vcmp_ref[0] = comp(vc_ref, pev_ref, w1v_ref, w2v_ref).astype(BF16)


def _compress_weights(pe, w1, w2):
    eye = jnp.eye(NSA_KV_GROUPS, dtype=F32)
    half_rows = CMP_BLOCK // 2
    w1r = w1.reshape(CMP_BLOCK, NSA_HEAD_DIM, CMP_HIDDEN)
    big = lambda part: jnp.einsum('ldn,gh->lgdhn', part, eye).reshape(
        half_rows * NSA_KV_WIDTH, NSA_KV_GROUPS * CMP_HIDDEN)
    w1big = jnp.concatenate([big(w1r[:half_rows]), big(w1r[half_rows:])], axis=1).astype(BF16)
    w2big = jnp.einsum('nd,gh->gnhd', w2, eye).reshape(NSA_KV_GROUPS * CMP_HIDDEN, NSA_KV_WIDTH).astype(BF16)
    pe_row = lambda part: jnp.broadcast_to(part[:, None, :], (half_rows, NSA_KV_GROUPS, NSA_HEAD_DIM)).reshape(1, -1)
    pe_rows = jnp.concatenate([pe_row(pe[:half_rows]), pe_row(pe[half_rows:]),
                               jnp.zeros((6, half_rows * NSA_KV_WIDTH), F32)], axis=0)
    return pe_rows, w1big, w2big


def _compress(kc, vc, pos_cmp, pe_k, pe_v, w_ck1, w_ck2, w_cv1, w_cv2, invf, sgn):
    b, seq, _ = kc.shape
    nchunk = seq // CMP_STRIDE
    width = CMP_STRIDE * NSA_KV_WIDTH
    pek, w1k, w2k = _compress_weights(pe_k, w_ck1, w_ck2)
    pev, w1v, w2v = _compress_weights(pe_v, w_cv1, w_cv2)
    ins = [kc.reshape(b, nchunk, width), vc.reshape(b, nchunk, width), pos_cmp,
           pek, pev, w1k, w2k, w1v, w2v, invf, sgn]
    full = lambda a: pl.BlockSpec(a.shape, lambda i: (0,) * a.ndim)
    per_b = lambda n: pl.BlockSpec((1, nchunk, n), lambda i: (i, 0, 0))
    out_shape = [jax.ShapeDtypeStruct((b, nchunk, NSA_KV_WIDTH), BF16)] * 2
    return pl.pallas_call(
        functools.partial(_compress_kernel, nchunk=nchunk),
        grid=(b,),
        in_specs=[per_b(width), per_b(width), per_b(1)] + [full(a) for a in ins[3:]],
        out_specs=[per_b(NSA_KV_WIDTH)] * 2,
        out_shape=out_shape,
        compiler_params=pltpu.CompilerParams(dimension_semantics=("arbitrary",),
                                             vmem_limit_bytes=VMEM_LIMIT),
        name="compress",
    )(*ins)


def _nsa_kernel(q_ref, kcmp_ref, vcmp_ref, ksel_ref, vsl_ref, kw_ref, vw_ref, gates_ref, sga_ref,
                ov_ref, ex_ref, out_ref, qz_scr, qa_scr, v_scr, m_scr, l_scr, acc_scr, mix_scr,
                *, tq, kc, n_cmp, n_top):
    t0 = pl.program_id(1) * tq
    rows = NSA_HPG * tq
    ncp = kcmp_ref.shape[1]
    nslab = kc // LANES
    lane = lax.broadcasted_iota(jnp.int32, (tq, LANES), 1)
    low = lane < 64
    diffq = (lax.broadcasted_iota(jnp.int32, (rows, LANES), 1)
             - (lax.broadcasted_iota(jnp.int32, (rows, LANES), 0) & (tq - 1)))

    def natural(o, g, pair):
        a = o[(2 * pair) * tq:(2 * pair + 1) * tq]
        b = o[(2 * pair + 1) * tq:(2 * pair + 2) * tq]
        if g == 0:
            return jnp.where(low, a, pltpu.roll(b, 64, 1))
        return jnp.where(low, pltpu.roll(a, 64, 1), b)

    gexp = _dot(gates_ref[0].astype(BF16), ex_ref[...])

    def add_branch(branch, g, o):
        for pair in range(2):
            c0 = (2 * g + pair) * LANES
            term = gexp[:, branch * NSA_WIDTH + c0:branch * NSA_WIDTH + c0 + LANES] * natural(o, g, pair)
            if branch == 0:
                mix_scr[:, c0:c0 + LANES] = term
            else:
                mix_scr[:, c0:c0 + LANES] += term

    def reset_flash():
        m_scr[...] = jnp.full(m_scr.shape, NEG_INF, F32)
        l_scr[...] = jnp.zeros(l_scr.shape, F32)
        acc_scr[...] = jnp.zeros(acc_scr.shape, F32)

    def flash_step(q_scr, k_ref, v_ref, start, mask_fn):
        for g in range(NSA_KV_GROUPS):
            col0 = g * LANES if k_ref is ksel_ref else 0
            s = _dot_nt(q_scr[g], k_ref[0, pl.ds(start, kc), col0:col0 + LANES])
            slabs = [s[:, c * LANES:(c + 1) * LANES] for c in range(nslab)]
            if mask_fn is not None:
                slabs = [jnp.where(mask_fn(c), sl, NEG_INF) for c, sl in enumerate(slabs)]
            _flash_update(slabs, v_ref[0, pl.ds(start, kc), :], m_scr, l_scr, acc_scr, pl.ds(g * rows, rows))

    def finish_flash(branch):
        for g in range(NSA_KV_GROUPS):
            r = pl.ds(g * rows, rows)
            add_branch(branch, g, acc_scr[r, :] / jnp.sum(l_scr[r, :], axis=1, keepdims=True))

    for g in range(NSA_KV_GROUPS):
        parts = []
        for pair in range(2):
            x = q_ref[0, :, (2 * g + pair) * LANES:(2 * g + pair + 1) * LANES].astype(F32)
            xr = pltpu.roll(x, 64, 1)
            if g == 0:
                parts += [jnp.where(low, x, 0.0), jnp.where(low, xr, 0.0)]
            else:
                parts += [jnp.where(low, 0.0, xr), jnp.where(low, 0.0, x)]
        qz = jnp.concatenate(parts, axis=0)
        qzb = qz.astype(BF16)
        qz_scr[g] = qzb

        s = _dot_nt(qzb, kcmp_ref[0])
        colc = lax.broadcasted_iota(jnp.int32, (rows, ncp), 1)
        rqc = t0 + (lax.broadcasted_iota(jnp.int32, (rows, ncp), 0) & (tq - 1))
        valid = (colc * CMP_STRIDE + (CMP_BLOCK - 1) <= rqc) & (colc < n_cmp)
        s = jnp.where(valid, s, NEG_INF)
        p = jnp.where(valid, jnp.exp2(s - jnp.max(s, axis=1, keepdims=True)), 0.0)
        l = jnp.sum(p, axis=1, keepdims=True)
        pn = p * jnp.where(l > 0.0, 1.0 / l, 0.0)
        add_branch(0, g, _dot(pn.astype(BF16), vcmp_ref[0]))

        psum = pn[0:tq] + pn[tq:2 * tq] + pn[2 * tq:3 * tq] + pn[3 * tq:4 * tq]
        hi = psum.astype(BF16)
        lo = (psum - hi.astype(F32)).astype(BF16)
        imp_t = (_dot(hi, ov_ref[...]) + _dot(lo, ov_ref[...])).T[0:64, :]
        nb = lax.broadcasted_iota(jnp.int32, (64, tq), 0)
        tqv = t0 + lax.broadcasted_iota(jnp.int32, (64, tq), 1)
        cur = lax.shift_right_logical(tqv, 6)
        forced = (nb == 0) | (nb == cur) | (nb == cur - 1)
        v = jnp.where(nb * SLC_BLOCK <= tqv, jnp.where(forced, FORCE_SCORE, imp_t), -1.0)
        v_scr[...] = v
        rank = jnp.zeros((64, tq), jnp.int32)
        for mblk in range(64):
            vm = jnp.broadcast_to(v_scr[mblk:mblk + 1, :], (64, tq))
            rank = rank + jnp.where(nb > mblk, (vm >= v).astype(jnp.int32), (vm > v).astype(jnp.int32))
        pen = jnp.where(rank < n_top, 0.0, SEL_PENALTY)
        zero = jnp.zeros((64, tq), F32)
        pen_t = jnp.concatenate([zero, pen] if g == 0 else [pen, zero], axis=0).T
        qa_scr[g] = (qz + jnp.concatenate([pen_t] * NSA_HPG, axis=0)).astype(BF16)

    causal = lambda c: diffq <= -c * LANES

    reset_flash()
    lax.fori_loop(0, t0 // kc, lambda j, cr: (
        flash_step(qa_scr, ksel_ref, vsl_ref, pl.multiple_of(j * kc, kc), None), cr)[1], 0)
    flash_step(qa_scr, ksel_ref, vsl_ref, pl.multiple_of(t0, kc), causal)
    finish_flash(1)

    reset_flash()

    @pl.when(t0 >= WINDOW)
    def _():
        flash_step(qz_scr, kw_ref, vw_ref, pl.multiple_of(t0 - WINDOW, kc), lambda c: diffq > -c * LANES)

    lax.fori_loop(jnp.maximum(t0 - WINDOW + kc, 0) // kc, t0 // kc, lambda j, cr: (
        flash_step(qz_scr, kw_ref, vw_ref, pl.multiple_of(j * kc, kc), None), cr)[1], 0)
    flash_step(qz_scr, kw_ref, vw_ref, pl.multiple_of(t0, kc), causal)
    finish_flash(2)

    out_ref[0] = (sga_ref[0].astype(F32) * mix_scr[...]).astype(BF16)


def _nsa_constants(ncp, n_slc):
    cmp_starts = np.arange(ncp) * CMP_STRIDE
    slc_starts = np.arange(LANES) * SLC_BLOCK
    ov = ((cmp_starts[:, None] < slc_starts[None, :] + SLC_BLOCK)
          & (cmp_starts[:, None] + CMP_BLOCK > slc_starts[None, :])
          & (np.arange(LANES)[None, :] < n_slc))
    ex = np.zeros((LANES, 3 * NSA_WIDTH), np.float32)
    for h in range(NSA_HEADS):
        for br in range(3):
            ex[h * 3 + br, br * NSA_WIDTH + h * NSA_HEAD_DIM: br * NSA_WIDTH + (h + 1) * NSA_HEAD_DIM] = 1.0
    return jnp.asarray(ov.astype(np.float32), BF16), jnp.asarray(ex, BF16)


def _nsa_attention(q, kcmp, vcmp, ksel, vsl, kw, vw, gates, sga):
    b, seq, _ = q.shape
    tq = min(NSA_TQ, seq)
    n_slc = seq // SLC_BLOCK
    assert n_slc <= 64 and seq % tq == 0 and WINDOW % tq == 0 and tq & (tq - 1) == 0
    ncp = kcmp.shape[1]
    n_cmp = (seq - CMP_BLOCK) // CMP_STRIDE + 1
    ov, ex = _nsa_constants(ncp, n_slc)
    rows = NSA_HPG * tq
    tile = lambda n: pl.BlockSpec((1, tq, n), lambda bi, qi: (bi, qi, 0))
    per_b = lambda a: pl.BlockSpec((1,) + a.shape[1:], lambda bi, qi: (bi, 0, 0))
    full = lambda a: pl.BlockSpec(a.shape, lambda bi, qi: (0, 0))
    return pl.pallas_call(
        functools.partial(_nsa_kernel, tq=tq, kc=tq, n_cmp=n_cmp, n_top=min(SLC_TOPK, n_slc)),
        grid=(b, seq // tq),
        in_specs=[tile(NSA_WIDTH), per_b(kcmp), per_b(vcmp), per_b(ksel), per_b(vsl), per_b(kw), per_b(vw),
                  tile(LANES), tile(NSA_WIDTH), full(ov), full(ex)],
        out_specs=tile(NSA_WIDTH),
        out_shape=jax.ShapeDtypeStruct((b, seq, NSA_WIDTH), BF16),
        scratch_shapes=[pltpu.VMEM((NSA_KV_GROUPS, rows, LANES), BF16),
                        pltpu.VMEM((NSA_KV_GROUPS, rows, LANES), BF16),
                        pltpu.VMEM((64, tq), F32),
                        pltpu.VMEM((NSA_KV_GROUPS * rows, LANES), F32),
                        pltpu.VMEM((NSA_KV_GROUPS * rows, LANES), F32),
                        pltpu.VMEM((NSA_KV_GROUPS * rows, LANES), F32),
                        pltpu.VMEM((tq, NSA_WIDTH), F32)],
        compiler_params=pltpu.CompilerParams(dimension_semantics=("arbitrary", "arbitrary"),
                                             vmem_limit_bytes=VMEM_LIMIT),
        name="nsa_attn",
    )(q, kcmp, vcmp, ksel, vsl, kw, vw, gates, sga, ov, ex)


def _odd_proj_kernel(x_ref, mixa_ref, mixb_ref, pos_ref, wout_ref, g_ref, wc_ref, wgate_ref, qn_ref, kvn_ref,
                     wuq_ref, wuk_ref, wuv_ref, invf_ref, sgn_ref,
                     y_ref, q_ref, k_ref, v_ref, sg_ref, *, tm):
    y = (x_ref[...] + _dot(mixa_ref[...], wout_ref[0:NSA_WIDTH, :])
         + _dot(mixb_ref[...], wout_ref[NSA_WIDTH:, :]))
    y_ref[...] = y
    yn = _rms(y, g_ref[...]).astype(BF16)
    cos2, sin2 = _rope_tables(pos_ref[...], invf_ref[...], sgn_ref[...])
    lane = lax.broadcasted_iota(jnp.int32, (tm, LANES), 1)
    rope = functools.partial(_rope_slab, cos2=cos2, sin2=sin2, first_half=(lane & 32) == 0)

    seg = _dot(yn, wc_ref[...])
    k_pe = rope(seg[:, MLA_Q_RANK + MLA_KV_RANK:]).astype(BF16)
    cq = _rms(seg[:, 0:MLA_Q_RANK], qn_ref[...]).astype(BF16)
    ckv = _rms(seg[:, MLA_Q_RANK:MLA_Q_RANK + MLA_KV_RANK], kvn_ref[...]).astype(BF16)
    scale = (MLA_NOPE_DIM + MLA_ROPE_DIM) ** -0.5 * LOG2_E
    qf = _dot(cq, wuq_ref[...])
    kf = _dot(ckv, wuk_ref[...])
    for h in range(MLA_HEADS):
        c0 = h * MLA_QK_PAD
        q_ref[:, c0:c0 + LANES] = (qf[:, c0:c0 + LANES] * scale).astype(BF16)
        q_ref[:, c0 + LANES:c0 + 2 * LANES] = (rope(qf[:, c0 + LANES:c0 + 2 * LANES]) * scale).astype(BF16)
        k_ref[:, c0:c0 + LANES] = kf[:, h * LANES:(h + 1) * LANES].astype(BF16)
        k_ref[:, c0 + LANES:c0 + 2 * LANES] = k_pe
    v_ref[...] = _dot(ckv, wuv_ref[...]).astype(BF16)
    sg_ref[...] = _silu(_dot(yn, wgate_ref[...])).astype(BF16)


def _odd_proj(x2d, mixa, mixb, pos2d, a_w_out, c_norm, c_w_in, q_norm, kv_norm, w_uq, w_ukv, invf, sgn, seq):
    t = x2d.shape[0]
    tm = min(PROJ_ROWS, seq)
    r0, r1, r2 = MLA_Q_RANK, MLA_Q_RANK + MLA_KV_RANK, MLA_Q_RANK + MLA_KV_RANK + MLA_ROPE_DIM
    wc = jnp.pad(c_w_in[:, :r2], ((0, 0), (0, LANES - MLA_ROPE_DIM))).astype(BF16)
    wgate = c_w_in[:, r2:].astype(BF16)
    wuq = jnp.pad(w_uq.reshape(MLA_Q_RANK, MLA_HEADS, MLA_NOPE_DIM + MLA_ROPE_DIM),
                  ((0, 0), (0, 0), (0, MLA_QK_PAD - MLA_NOPE_DIM - MLA_ROPE_DIM))
                  ).reshape(MLA_Q_RANK, MLA_HEADS * MLA_QK_PAD).astype(BF16)
    wukv = w_ukv.reshape(MLA_KV_RANK, MLA_HEADS, MLA_NOPE_DIM + MLA_V_DIM)
    wuk = wukv[:, :, :MLA_NOPE_DIM].reshape(MLA_KV_RANK, -1).astype(BF16)
    wuv = wukv[:, :, MLA_NOPE_DIM:].reshape(MLA_KV_RANK, -1).astype(BF16)
    del r0, r1
    ins = [x2d, mixa, mixb, pos2d, a_w_out.astype(BF16), c_norm.reshape(1, -1), wc, wgate,
           q_norm.reshape(1, -1), kv_norm.reshape(1, -1), wuq, wuk, wuv, invf, sgn]
    full = lambda a: pl.BlockSpec(a.shape, lambda i: (0,) * a.ndim)
    rows = lambda n: pl.BlockSpec((tm, n), lambda i: (i, 0))
    out_shapes = [
        jax.ShapeDtypeStruct((t, D_MODEL), F32),
        jax.ShapeDtypeStruct((t, MLA_HEADS * MLA_QK_PAD), BF16),
        jax.ShapeDtypeStruct((t, MLA_HEADS * MLA_QK_PAD), BF16),
        jax.ShapeDtypeStruct((t, MLA_WIDTH), BF16),
        jax.ShapeDtypeStruct((t, MLA_WIDTH), BF16),
    ]
    return pl.pallas_call(
        functools.partial(_odd_proj_kernel, tm=tm),
        grid=(t // tm,),
        in_specs=[rows(D_MODEL), rows(NSA_WIDTH), rows(CONV_WIDTH), rows(1)] + [full(a) for a in ins[4:]],
        out_specs=[rows(s.shape[1]) for s in out_shapes],
        out_shape=out_shapes,
        compiler_params=pltpu.CompilerParams(dimension_semantics=("arbitrary",),
                                             vmem_limit_bytes=VMEM_LIMIT),
        name="odd_proj",
    )(*ins)


def _flash_update(s_slabs, v, m_ref, l_ref, acc_ref, rows):
    mx = functools.reduce(jnp.maximum, s_slabs)
    m_old = m_ref[rows, :]
    m_new = jnp.maximum(m_old, jnp.max(mx, axis=1, keepdims=True))
    alpha = jnp.exp2(m_old - m_new)
    ps = [jnp.exp2(sl - m_new) for sl in s_slabs]
    l_ref[rows, :] = alpha * l_ref[rows, :] + functools.reduce(jnp.add, ps)
    p = jnp.concatenate([x.astype(BF16) for x in ps], axis=1)
    acc_ref[rows, :] = alpha * acc_ref[rows, :] + _dot(p, v)
    m_ref[rows, :] = m_new


def _mla_kernel(q_ref, k_ref, v_ref, o_ref, m_scr, l_scr, acc_scr, *, tq, kc, hp):
    t0 = pl.program_id(2) * tq
    m_scr[...] = jnp.full(m_scr.shape, NEG_INF, F32)
    l_scr[...] = jnp.zeros(l_scr.shape, F32)
    acc_scr[...] = jnp.zeros(acc_scr.shape, F32)
    nslab = kc // LANES
    diff = lax.broadcasted_iota(jnp.int32, (tq, LANES), 1) - lax.broadcasted_iota(jnp.int32, (tq, LANES), 0)

    def step(j, masked):
        start = pl.multiple_of(j * kc, kc)
        for h in range(hp):
            qk = slice(h * MLA_QK_PAD, (h + 1) * MLA_QK_PAD)
            s = _dot_nt(q_ref[0, :, qk], k_ref[0, pl.ds(start, kc), qk])
            slabs = [s[:, c * LANES:(c + 1) * LANES] for c in range(nslab)]
            if masked:
                bound = t0 - j * kc
                slabs = [jnp.where(diff <= bound - c * LANES, sl, NEG_INF) for c, sl in enumerate(slabs)]
            v = v_ref[0, pl.ds(start, kc), h * MLA_V_DIM:(h + 1) * MLA_V_DIM]
            _flash_update(slabs, v, m_scr, l_scr, acc_scr, pl.ds(h * tq, tq))

    n_full = t0 // kc
    lax.fori_loop(0, n_full, lambda j, c: (step(j, False), c)[1], 0)
    lax.fori_loop(n_full, (t0 + tq + kc - 1) // kc, lambda j, c: (step(j, True), c)[1], 0)
    for h in range(hp):
        rows = pl.ds(h * tq, tq)
        o_ref[0, :, h * MLA_V_DIM:(h + 1) * MLA_V_DIM] = (
            acc_scr[rows, :] / jnp.sum(l_scr[rows, :], axis=1, keepdims=True)).astype(BF16)


def _mla_attention(q, k, v):
    b, seq, _ = q.shape
    tq = min(MLA_TQ, seq)
    kc = min(MLA_KC, seq)
    hp = MLA_HEADS_PER_STEP
    assert seq % tq == 0 and seq % kc == 0 and MLA_HEADS % hp == 0
    return pl.pallas_call(
        functools.partial(_mla_kernel, tq=tq, kc=kc, hp=hp),
        grid=(b, MLA_HEADS // hp, seq // tq),
        in_specs=[pl.BlockSpec((1, tq, hp * MLA_QK_PAD), lambda bi, h, qi: (bi, qi, h)),
                  pl.BlockSpec((1, seq, hp * MLA_QK_PAD), lambda bi, h, qi: (bi, 0, h)),
                  pl.BlockSpec((1, seq, hp * MLA_V_DIM), lambda bi, h, qi: (bi, 0, h))],
        out_specs=pl.BlockSpec((1, tq, hp * MLA_V_DIM), lambda bi, h, qi: (bi, qi, h)),
        out_shape=jax.ShapeDtypeStruct((b, seq, MLA_WIDTH), BF16),
        scratch_shapes=[pltpu.VMEM((hp * tq, LANES), F32), pltpu.VMEM((hp * tq, LANES), F32),
                        pltpu.VMEM((hp * tq, MLA_V_DIM), F32)],
        compiler_params=pltpu.CompilerParams(dimension_semantics=("arbitrary", "arbitrary", "arbitrary"),
                                             vmem_limit_bytes=VMEM_LIMIT),
        name="mla_attn",
    )(q, k, v)


def _final_kernel(y_ref, o_ref, sg_ref, w_ref, g_ref, out_ref):
    gated = (sg_ref[...].astype(F32) * o_ref[...].astype(F32)).astype(BF16)
    out_ref[...] = _rms(y_ref[...] + _dot(gated, w_ref[...]), g_ref[...])


def _final(y, o, sg, w_out, final_norm, seq):
    t = y.shape[0]
    tm = min(PROJ_ROWS, seq)
    rows = lambda n: pl.BlockSpec((tm, n), lambda i: (i, 0))
    full = lambda a: pl.BlockSpec(a.shape, lambda i: (0,) * a.ndim)
    w = w_out.astype(BF16)
    g = final_norm.reshape(1, -1)
    return pl.pallas_call(
        _final_kernel,
        grid=(t // tm,),
        in_specs=[rows(D_MODEL), rows(MLA_WIDTH), rows(MLA_WIDTH), full(w), full(g)],
        out_specs=rows(D_MODEL),
        out_shape=jax.ShapeDtypeStruct((t, D_MODEL), F32),
        compiler_params=pltpu.CompilerParams(dimension_semantics=("arbitrary",),
                                             vmem_limit_bytes=VMEM_LIMIT),
        name="final",
    )(y, o, sg, w, g)


def _rope_constants():
    half = NSA_HEAD_DIM // 2
    inv_freq = ROPE_THETA ** (-jnp.arange(half, dtype=F32) / half)
    invf = jnp.tile(inv_freq, LANES // half).reshape(1, LANES)
    sgn = jnp.tile(jnp.concatenate([-jnp.ones((half,), F32), jnp.ones((half,), F32)]), LANES // (2 * half))
    return invf, sgn.reshape(1, LANES)


def kernel(x, positions, a_norm, a_w_in, a_pe_k, a_pe_v, a_w_ck1, a_w_ck2, a_w_cv1, a_w_cv2, a_conv_w, a_w_out, c_norm, c_w_in, c_q_norm, c_kv_norm, c_w_uq, c_w_ukv, c_w_out, final_norm):
    b, seq, d = x.shape
    assert d == D_MODEL and NSA_HEAD_DIM == MLA_ROPE_DIM
    assert a_norm.shape[0] == 1 and c_norm.shape[0] == 1
    t = b * seq
    invf, sgn = _rope_constants()
    x2d = x.reshape(t, d)
    pos2d = positions.reshape(t, 1)

    q, kc, vc, ksel, vsl, kw, vw, gates, sga, mixb = _even_proj(
        x2d, pos2d, a_norm[0], a_w_in[0], a_conv_w[0], invf, sgn, seq)
    nchunk = seq // CMP_STRIDE
    pos_cmp = jnp.pad(positions[:, CMP_BLOCK - 1::CMP_STRIDE], ((0, 0), (0, 1)))[:, :nchunk, None]
    kcmp, vcmp = _compress(kc.reshape(b, seq, -1), vc.reshape(b, seq, -1), pos_cmp, a_pe_k[0], a_pe_v[0],
                           a_w_ck1[0], a_w_ck2[0], a_w_cv1[0], a_w_cv2[0], invf, sgn)
    r3 = lambda a: a.reshape(b, seq, a.shape[-1])
    mixa = _nsa_attention(r3(q), kcmp, vcmp, r3(ksel), r3(vsl), r3(kw), r3(vw), r3(gates), r3(sga))

    y, mq, mk, mv, sg = _odd_proj(x2d, mixa.reshape(t, -1), mixb, pos2d, a_w_out[0], c_norm[0], c_w_in[0],
                                  c_q_norm[0], c_kv_norm[0], c_w_uq[0], c_w_ukv[0], invf, sgn, seq)
    o = _mla_attention(r3(mq), r3(mk), r3(mv))
    out = _final(y, o.reshape(t, -1), sg, c_w_out[0], final_norm, seq)
    return out.reshape(b, seq, d)
```

```python
import functools

import jax
import jax.numpy as jnp
import numpy as np
from jax import lax
from jax.experimental import pallas as pl
from jax.experimental.pallas import tpu as pltpu

F32 = jnp.float32
BF16 = jnp.bfloat16

D_MODEL = 1024
ROPE_THETA = 10000.0
RMS_EPS = 1e-6
NEG_INF = -1e30
FORCE_SCORE = 1e4
SEL_PENALTY = -1e9

NSA_HEADS = 8
NSA_KV_GROUPS = 2
NSA_HPG = NSA_HEADS // NSA_KV_GROUPS
NSA_HEAD_DIM = 64
NSA_WIDTH = NSA_HEADS * NSA_HEAD_DIM
NSA_KV_WIDTH = NSA_KV_GROUPS * NSA_HEAD_DIM
CMP_BLOCK = 32
CMP_STRIDE = 16
CMP_HIDDEN = 2 * NSA_HEAD_DIM
SLC_BLOCK = 64
SLC_TOPK = 16
WINDOW = 512
CONV_WIDTH = D_MODEL - NSA_WIDTH
CONV_K = 3

MLA_HEADS = 8
MLA_NOPE_DIM = 128
MLA_ROPE_DIM = 64
MLA_V_DIM = 128
MLA_Q_RANK = 256
MLA_KV_RANK = 256
MLA_WIDTH = MLA_HEADS * MLA_V_DIM
MLA_QK_PAD = 256

LANES = 128
VMEM_LIMIT = 56 * 1024 * 1024

PROJ_ROWS = 512
NSA_TQ = 256
MLA_TQ = 512
MLA_KC = 512
MLA_HEADS_PER_STEP = 4
LOG2_E = 1.4426950408889634


def _dot(a, b):
    return jnp.dot(a, b, preferred_element_type=F32)


def _dot_nt(a, b):
    return lax.dot_general(a, b, (((1,), (1,)), ((), ())), preferred_element_type=F32)


def _silu(x):
    return x * jax.nn.sigmoid(x)


def _rms(x, g):
    return x * lax.rsqrt(jnp.mean(x * x, axis=-1, keepdims=True) + RMS_EPS) * g


def _rope_tables(pos_col, invf_row, sign_row):
    ang = pos_col.astype(F32) * invf_row
    return jnp.cos(ang), jnp.sin(ang) * sign_row


def _rope_slab(s, cos2, sin2, first_half):
    swapped = jnp.where(first_half, pltpu.roll(s, 96, 1), pltpu.roll(s, 32, 1))
    return s * cos2 + swapped * sin2


def _even_proj_kernel(x_ref, pos_ref, g_ref, wqk_ref, wg_ref, wga_ref, wcb_ref, wcc_ref, wch_ref,
                      wgb_ref, convw_ref, invf_ref, sgn_ref,
                      q_ref, kc_ref, vc_ref, ksel_ref, vsl_ref, kw_ref, vw_ref, gates_ref, sga_ref,
                      mixb_ref, carry_ref, *, tiles_per_seq, tm):
    i = pl.program_id(0)
    xn = _rms(x_ref[...], g_ref[...]).astype(BF16)
    cos2, sin2 = _rope_tables(pos_ref[...], invf_ref[...], sgn_ref[...])
    lane = lax.broadcasted_iota(jnp.int32, (tm, LANES), 1)
    first_half = (lane & 32) == 0
    low = lane < 64
    rope = functools.partial(_rope_slab, cos2=cos2, sin2=sin2, first_half=first_half)

    qf = _dot(xn, wqk_ref[:, 0:NSA_WIDTH])
    scale = NSA_HEAD_DIM ** -0.5 * LOG2_E
    for c in range(NSA_WIDTH // LANES):
        q_ref[:, c * LANES:(c + 1) * LANES] = (rope(qf[:, c * LANES:(c + 1) * LANES]) * scale).astype(BF16)

    seg = _dot(xn, wqk_ref[:, NSA_WIDTH:NSA_WIDTH + 6 * LANES])
    kc_ref[...] = seg[:, 0:LANES]
    vc_ref[...] = seg[:, LANES:2 * LANES]
    ksl = rope(seg[:, 2 * LANES:3 * LANES])
    vsl_ref[...] = seg[:, 3 * LANES:4 * LANES].astype(BF16)
    kw_ref[...] = rope(seg[:, 4 * LANES:5 * LANES]).astype(BF16)
    vw_ref[...] = seg[:, 5 * LANES:6 * LANES].astype(BF16)

    spos = (i % tiles_per_seq) * tm + lax.broadcasted_iota(jnp.int32, (tm, LANES), 0)
    blk = lax.shift_right_logical(spos, 6)
    ksel_ref[:, 0:LANES] = jnp.where(low, ksl, (lane - 64 == blk).astype(F32)).astype(BF16)
    ksel_ref[:, LANES:2 * LANES] = jnp.where(low, (lane == blk).astype(F32), ksl).astype(BF16)

    gates_ref[...] = jax.nn.sigmoid(_dot(xn, wg_ref[...]))
    sga_ref[...] = _silu(_dot(xn, wga_ref[...])).astype(BF16)

    u = _dot(xn, wcc_ref[...]) * _dot(xn, wch_ref[...])

    @pl.when(i % tiles_per_seq == 0)
    def _():
        carry_ref[...] = jnp.zeros_like(carry_ref)

    row = lax.broadcasted_iota(jnp.int32, (tm, CONV_WIDTH), 0)
    prev1 = carry_ref[7:8, :]
    prev2 = carry_ref[6:7, :]
    u1 = jnp.where(row == 0, prev1, pltpu.roll(u, 1, 0))
    u2 = jnp.where(row == 0, prev2, jnp.where(row == 1, prev1, pltpu.roll(u, 2, 0)))
    carry_ref[...] = u[tm - 8:tm, :]
    w = convw_ref[...]
    y = w[0:1, :] * u2 + w[1:2, :] * u1 + w[2:3, :] * u
    cb = _dot(xn, wcb_ref[...])
    mixb_ref[...] = (_silu(_dot(xn, wgb_ref[...])) * (cb * y)).astype(BF16)


def _even_proj(x2d, pos2d, norm_g, w_in, conv_w, invf, sgn, seq):
    t = x2d.shape[0]
    tm = min(PROJ_ROWS, seq)
    assert seq % tm == 0 and t % tm == 0
    sizes = (NSA_WIDTH,) + (NSA_KV_WIDTH,) * 6 + (3 * NSA_HEADS, NSA_WIDTH) + (CONV_WIDTH,) * 4
    offs = np.concatenate([[0], np.cumsum(sizes)])
    col = lambda k: w_in[:, offs[k]:offs[k + 1]]
    wqk = jnp.concatenate([col(k) for k in range(7)], axis=1).astype(BF16)
    wg = jnp.pad(col(7), ((0, 0), (0, LANES - 3 * NSA_HEADS))).astype(BF16)
    wga, wcb, wcc, wch, wgb = (col(k).astype(BF16) for k in range(8, 13))

    full = lambda a: pl.BlockSpec(a.shape, lambda i: (0,) * a.ndim)
    rows = lambda n: pl.BlockSpec((tm, n), lambda i: (i, 0))
    out_shapes = [
        jax.ShapeDtypeStruct((t, NSA_WIDTH), BF16),
        jax.ShapeDtypeStruct((t, NSA_KV_WIDTH), F32),
        jax.ShapeDtypeStruct((t, NSA_KV_WIDTH), F32),
        jax.ShapeDtypeStruct((t, 2 * LANES), BF16),
        jax.ShapeDtypeStruct((t, NSA_KV_WIDTH), BF16),
        jax.ShapeDtypeStruct((t, NSA_KV_WIDTH), BF16),
        jax.ShapeDtypeStruct((t, NSA_KV_WIDTH), BF16),
        jax.ShapeDtypeStruct((t, LANES), F32),
        jax.ShapeDtypeStruct((t, NSA_WIDTH), BF16),
        jax.ShapeDtypeStruct((t, CONV_WIDTH), BF16),
    ]
    ins = [x2d, pos2d, norm_g.reshape(1, -1), wqk, wg, wga, wcb, wcc, wch, wgb, conv_w, invf, sgn]
    in_specs = [rows(D_MODEL), rows(1)] + [full(a) for a in ins[2:]]
    return pl.pallas_call(
        functools.partial(_even_proj_kernel, tiles_per_seq=seq // tm, tm=tm),
        grid=(t // tm,),
        in_specs=in_specs,
        out_specs=[rows(s.shape[1]) for s in out_shapes],
        out_shape=out_shapes,
        scratch_shapes=[pltpu.VMEM((8, CONV_WIDTH), F32)],
        compiler_params=pltpu.CompilerParams(dimension_semantics=("arbitrary",),
                                             vmem_limit_bytes=VMEM_LIMIT),
        name="even_proj",
    )(*ins)


def _compress_kernel(kc_ref, vc_ref, pos_ref, pek_ref, pev_ref, w1k_ref, w2k_ref, w1v_ref, w2v_ref,
                     invf_ref, sgn_ref, kcmp_ref, vcmp_ref, *, nchunk):
    half = NSA_KV_GROUPS * CMP_HIDDEN

    def comp(raw_ref, pe_ref, w1_ref, w2_ref):
        raw = raw_ref[0]
        top = _dot((raw + pe_ref[0:1, :]).astype(BF16), w1_ref[:, 0:half])
        bot = _dot((raw + pe_ref[1:2, :]).astype(BF16), w1_ref[:, half:2 * half])
        h = _silu(top + pltpu.roll(bot, nchunk - 1, 0))
        return _dot(h.astype(BF16), w2_ref[...])

    cos2, sin2 = _rope_tables(pos_ref[0], invf_ref[...], sgn_ref[...])
    lane = lax.broadcasted_iota(jnp.int32, (nchunk, LANES), 1)
    kcmp = comp(kc_ref, pek_ref, w1k_ref, w2k_ref)
    kcmp_ref[0] = _rope_slab(kcmp, cos2, sin2, (lane & 32) == 0).astype(BF16)
    vcmp_ref[0] = comp(vc_ref, pev_ref, w1v_ref, w2v_ref).astype(BF16)


def _compress_weights(pe, w1, w2):
    eye = jnp.eye(NSA_KV_GROUPS, dtype=F32)
    half_rows = CMP_BLOCK // 2
    w1r = w1.reshape(CMP_BLOCK, NSA_HEAD_DIM, CMP_HIDDEN)
    big = lambda part: jnp.einsum('ldn,gh->lgdhn', part, eye).reshape(
        half_rows * NSA_KV_WIDTH, NSA_KV_GROUPS * CMP_HIDDEN)
    w1big = jnp.concatenate([big(w1r[:half_rows]), big(w1r[half_rows:])], axis=1).astype(BF16)
    w2big = jnp.einsum('nd,gh->gnhd', w2, eye).reshape(NSA_KV_GROUPS * CMP_HIDDEN, NSA_KV_WIDTH).astype(BF16)
    pe_row = lambda part: jnp.broadcast_to(part[:, None, :], (half_rows, NSA_KV_GROUPS, NSA_HEAD_DIM)).reshape(1, -1)
    pe_rows = jnp.concatenate([pe_row(pe[:half_rows]), pe_row(pe[half_rows:]),
                               jnp.zeros((6, half_rows * NSA_KV_WIDTH), F32)], axis=0)
    return pe_rows, w1big, w2big


def _compress(kc, vc, pos_cmp, pe_k, pe_v, w_ck1, w_ck2, w_cv1, w_cv2, invf, sgn):
    b, seq, _ = kc.shape
    nchunk = seq // CMP_STRIDE
    width = CMP_STRIDE * NSA_KV_WIDTH
    pek, w1k, w2k = _compress_weights(pe_k, w_ck1, w_ck2)
    pev, w1v, w2v = _compress_weights(pe_v, w_cv1, w_cv2)
    ins = [kc.reshape(b, nchunk, width), vc.reshape(b, nchunk, width), pos_cmp,
           pek, pev, w1k, w2k, w1v, w2v, invf, sgn]
    full = lambda a: pl.BlockSpec(a.shape, lambda i: (0,) * a.ndim)
    per_b = lambda n: pl.BlockSpec((1, nchunk, n), lambda i: (i, 0, 0))
    out_shape = [jax.ShapeDtypeStruct((b, nchunk, NSA_KV_WIDTH), BF16)] * 2
    return pl.pallas_call(
        functools.partial(_compress_kernel, nchunk=nchunk),
        grid=(b,),
        in_specs=[per_b(width), per_b(width), per_b(1)] + [full(a) for a in ins[3:]],
        out_specs=[per_b(NSA_KV_WIDTH)] * 2,
        out_shape=out_shape,
        compiler_params=pltpu.CompilerParams(dimension_semantics=("arbitrary",),
                                             vmem_limit_bytes=VMEM_LIMIT),
        name="compress",
    )(*ins)


def _flash_update_t(s_t, v_t, m_ref, acc_ref, cols):
    m_old = m_ref[:, cols]
    m_new = jnp.maximum(m_old, jnp.max(s_t, axis=0, keepdims=True))
    p = jnp.exp2(s_t - m_new)
    acc_ref[:, cols] = jnp.exp2(m_old - m_new) * acc_ref[:, cols] + _dot(v_t, p.astype(BF16))
    m_ref[:, cols] = m_new


def _nsa_kernel(q_ref, kcmp_ref, vcmp_t_ref, ksel_ref, vsl_t_ref, kw_ref, vw_t_ref, gates_ref, sga_ref,
                ov_t_ref, out_ref, qz_scr, qa_scr, v_scr, m_scr, acc_scr, mix_scr,
                *, tq, kc, n_cmp, n_top):
    t0 = pl.program_id(1) * tq
    n = NSA_HPG * tq
    ncp = kcmp_ref.shape[1]
    hd = NSA_HEAD_DIM
    q_t = q_ref[0].astype(F32).T
    gates_t = gates_ref[0].T
    kq = lax.broadcasted_iota(jnp.int32, (kc, tq), 0) - lax.broadcasted_iota(jnp.int32, (kc, tq), 1)
    causal = kq <= 0
    newer = kq > 0

    def add_branch(branch, g, o_t):
        for h in range(NSA_HPG):
            head = NSA_HPG * g + h
            gate = gates_t[head * 3 + branch:head * 3 + branch + 1, :]
            term = gate * o_t[:, h * tq:(h + 1) * tq]
            r = pl.ds(head * hd, hd)
            if branch == 0:
                mix_scr[r, :] = term
            else:
                mix_scr[r, :] += term

    def reset_flash():
        m_scr[...] = jnp.full(m_scr.shape, NEG_INF, F32)
        acc_scr[...] = jnp.zeros(acc_scr.shape, F32)

    def flash_step(q_scr, k_ref, v_t_ref, j, mask):
        start = pl.multiple_of(j * kc, kc)
        chains = []
        for g in range(NSA_KV_GROUPS):
            col0 = g * LANES if k_ref is ksel_ref else 0
            k = k_ref[0, pl.ds(start, kc), col0:col0 + LANES]
            for h in range(NSA_HPG):
                chains.append((g, g * n + h * tq, _dot(k, q_scr[g, :, h * tq:(h + 1) * tq])))
        for g, c0, s_t in chains:
            if mask is not None:
                s_t = jnp.where(mask, s_t, NEG_INF)
            _flash_update_t(s_t, v_t_ref[0, j, g], m_scr, acc_scr, pl.ds(c0, tq))

    def finish_flash(branch):
        for g in range(NSA_KV_GROUPS):
            c = pl.ds(g * n, n)
            ones_row = hd * (1 - g)
            add_branch(branch, g, acc_scr[g * hd:(g + 1) * hd, c] / acc_scr[ones_row:ones_row + 1, c])

    zeros_q = jnp.zeros((hd, n), F32)
    for g in range(NSA_KV_GROUPS):
        qg = jnp.concatenate([q_t[(NSA_HPG * g + h) * hd:(NSA_HPG * g + h + 1) * hd, :] for h in range(NSA_HPG)],
                             axis=1)
        qz = jnp.concatenate([qg, zeros_q] if g == 0 else [zeros_q, qg], axis=0).astype(BF16)
        qz_scr[g] = qz

        s_t = _dot(kcmp_ref[0], qz)
        cq = (lax.broadcasted_iota(jnp.int32, (ncp, tq), 0) * CMP_STRIDE
              - lax.broadcasted_iota(jnp.int32, (ncp, tq), 1))
        valid1 = (cq <= t0 - (CMP_BLOCK - 1)) & (lax.broadcasted_iota(jnp.int32, (ncp, tq), 0) < n_cmp)
        valid = jnp.concatenate([valid1] * NSA_HPG, axis=1)
        s_t = jnp.where(valid, s_t, NEG_INF)
        p = jnp.where(valid, jnp.exp2(s_t - jnp.max(s_t, axis=0, keepdims=True)), 0.0)
        l = jnp.sum(p, axis=0, keepdims=True)
        pn = p * jnp.where(l > 0.0, 1.0 / l, 0.0)
        o_t = _dot(vcmp_t_ref[0], pn.astype(BF16))
        add_branch(0, g, o_t[g * hd:(g + 1) * hd, :])

        psum = functools.reduce(jnp.add, [pn[:, h * tq:(h + 1) * tq] for h in range(NSA_HPG)])
        hi = psum.astype(BF16)
        lo = (psum - hi.astype(F32)).astype(BF16)
        imp_t = (_dot(ov_t_ref[...], hi) + _dot(ov_t_ref[...], lo))[0:64, :]
        nb = lax.broadcasted_iota(jnp.int32, (64, tq), 0)
        tqv = t0 + lax.broadcasted_iota(jnp.int32, (64, tq), 1)
        cur = lax.shift_right_logical(tqv, 6)
        forced = (nb == 0) | (nb == cur) | (nb == cur - 1)
        v_scr[...] = jnp.where(nb * SLC_BLOCK <= tqv, jnp.where(forced, FORCE_SCORE, imp_t), -1.0)
        sub = lax.broadcasted_iota(jnp.int32, (8, tq), 0)
        vch = [v_scr[r * 8:(r + 1) * 8, :] for r in range(8)]
        rank = [jnp.zeros((8, tq), jnp.int32) for _ in range(8)]
        for mblk in range(64):
            vm = jnp.broadcast_to(v_scr[mblk:mblk + 1, :], (8, tq))
            for r in range(8):
                if r * 8 > mblk:
                    before = vm >= vch[r]
                elif r * 8 + 7 < mblk:
                    before = vm > vch[r]
                else:
                    before = (vm > vch[r]) | ((vm == vch[r]) & (sub > mblk - r * 8))
                rank[r] = rank[r] + jnp.where(before, 1, 0)
        pen = jnp.concatenate([jnp.where(rk < n_top, 0.0, SEL_PENALTY) for rk in rank], axis=0)
        pen4 = jnp.concatenate([pen] * NSA_HPG, axis=1)
        qa_scr[g] = jnp.concatenate([qg, pen4] if g == 0 else [pen4, qg], axis=0).astype(BF16)

    jd = t0 // kc

    reset_flash()
    lax.fori_loop(0, jd, lambda j, cr: (flash_step(qa_scr, ksel_ref, vsl_t_ref, j, None), cr)[1], 0)
    flash_step(qa_scr, ksel_ref, vsl_t_ref, jd, causal)
    finish_flash(1)

    reset_flash()
    wch = WINDOW // kc

    @pl.when(jd >= wch)
    def _():
        flash_step(qz_scr, kw_ref, vw_t_ref, jd - wch, newer)

    lax.fori_loop(jnp.maximum(jd - wch + 1, 0), jd, lambda j, cr: (
        flash_step(qz_scr, kw_ref, vw_t_ref, j, None), cr)[1], 0)
    flash_step(qz_scr, kw_ref, vw_t_ref, jd, causal)
    finish_flash(2)

    out_ref[0] = (sga_ref[0].astype(F32) * mix_scr[...].T).astype(BF16)


def _overlap_t(ncp, n_slc):
    cmp_starts = np.arange(ncp) * CMP_STRIDE
    slc_starts = np.arange(LANES) * SLC_BLOCK
    ov = ((cmp_starts[None, :] < slc_starts[:, None] + SLC_BLOCK)
          & (cmp_starts[None, :] + CMP_BLOCK > slc_starts[:, None])
          & (np.arange(LANES)[:, None] < n_slc))
    return jnp.asarray(ov.astype(np.float32), BF16)


def _values_t(v, kc):
    b, seq, d = v.shape
    v_t = v.reshape(b, seq // kc, 1, kc, d).transpose(0, 1, 2, 4, 3)
    row_group = (jnp.arange(d) // NSA_HEAD_DIM)[None, :, None]
    mine = row_group == jnp.arange(NSA_KV_GROUPS)[:, None, None]
    return jnp.where(mine, v_t, jnp.ones((), v.dtype))


def _nsa_attention(q, kcmp, vcmp, ksel, vsl, kw, vw, gates, sga):
    b, seq, _ = q.shape
    tq = min(NSA_TQ, seq)
    n_slc = seq // SLC_BLOCK
    assert n_slc <= 64 and seq % tq == 0 and WINDOW % tq == 0 and tq & (tq - 1) == 0
    ncp = kcmp.shape[1]
    n_cmp = (seq - CMP_BLOCK) // CMP_STRIDE + 1
    ov_t = _overlap_t(ncp, n_slc)
    vcmp_t = vcmp.transpose(0, 2, 1)
    vsl_t = _values_t(vsl, tq)
    vw_t = _values_t(vw, tq)
    n = NSA_HPG * tq
    tile = lambda w: pl.BlockSpec((1, tq, w), lambda bi, qi: (bi, qi, 0))
    per_b = lambda a: pl.BlockSpec((1,) + a.shape[1:], lambda bi, qi: (bi,) + (0,) * (a.ndim - 1))
    full = lambda a: pl.BlockSpec(a.shape, lambda bi, qi: (0, 0))
    return pl.pallas_call(
        functools.partial(_nsa_kernel, tq=tq, kc=tq, n_cmp=n_cmp, n_top=min(SLC_TOPK, n_slc)),
        grid=(b, seq // tq),
        in_specs=[tile(NSA_WIDTH), per_b(kcmp), per_b(vcmp_t), per_b(ksel), per_b(vsl_t), per_b(kw), per_b(vw_t),
                  tile(LANES), tile(NSA_WIDTH), full(ov_t)],
        out_specs=tile(NSA_WIDTH),
        out_shape=jax.ShapeDtypeStruct((b, seq, NSA_WIDTH), BF16),
        scratch_shapes=[pltpu.VMEM((NSA_KV_GROUPS, LANES, n), BF16),
                        pltpu.VMEM((NSA_KV_GROUPS, LANES, n), BF16),
                        pltpu.VMEM((64, tq), F32),
                        pltpu.VMEM((1, NSA_KV_GROUPS * n), F32),
                        pltpu.VMEM((LANES, NSA_KV_GROUPS * n), F32),
                        pltpu.VMEM((NSA_WIDTH, tq), F32)],
        compiler_params=pltpu.CompilerParams(dimension_semantics=("arbitrary", "arbitrary"),
                                             vmem_limit_bytes=VMEM_LIMIT),
        name="nsa_attn",
    )(q, kcmp, vcmp_t, ksel, vsl_t, kw, vw_t, gates, sga, ov_t)


def _odd_proj_kernel(x_ref, mixa_ref, mixb_ref, pos_ref, wout_ref, g_ref, wc_ref, wgate_ref, qn_ref, kvn_ref,
                     wuq_ref, wuk_ref, wuv_ref, invf_ref, sgn_ref,
                     y_ref, q_ref, k_ref, v_ref, sg_ref, *, tm):
    y = (x_ref[...] + _dot(mixa_ref[...], wout_ref[0:NSA_WIDTH, :])
         + _dot(mixb_ref[...], wout_ref[NSA_WIDTH:, :]))
    y_ref[...] = y
    yn = _rms(y, g_ref[...]).astype(BF16)
    cos2, sin2 = _rope_tables(pos_ref[...], invf_ref[...], sgn_ref[...])
    lane = lax.broadcasted_iota(jnp.int32, (tm, LANES), 1)
    rope = functools.partial(_rope_slab, cos2=cos2, sin2=sin2, first_half=(lane & 32) == 0)

    seg = _dot(yn, wc_ref[...])
    k_pe = rope(seg[:, MLA_Q_RANK + MLA_KV_RANK:]).astype(BF16)
    cq = _rms(seg[:, 0:MLA_Q_RANK], qn_ref[...]).astype(BF16)
    ckv = _rms(seg[:, MLA_Q_RANK:MLA_Q_RANK + MLA_KV_RANK], kvn_ref[...]).astype(BF16)
    scale = (MLA_NOPE_DIM + MLA_ROPE_DIM) ** -0.5 * LOG2_E
    qf = _dot(cq, wuq_ref[...])
    kf = _dot(ckv, wuk_ref[...])
    for h in range(MLA_HEADS):
        c0 = h * MLA_QK_PAD
        q_ref[:, c0:c0 + LANES] = (qf[:, c0:c0 + LANES] * scale).astype(BF16)
        q_ref[:, c0 + LANES:c0 + 2 * LANES] = (rope(qf[:, c0 + LANES:c0 + 2 * LANES]) * scale).astype(BF16)
        k_ref[:, c0:c0 + LANES] = kf[:, h * LANES:(h + 1) * LANES].astype(BF16)
        k_ref[:, c0 + LANES:c0 + 2 * LANES] = k_pe
    v_ref[...] = _dot(ckv, wuv_ref[...]).astype(BF16)
    sg_ref[...] = _silu(_dot(yn, wgate_ref[...])).astype(BF16)


def _odd_proj(x2d, mixa, mixb, pos2d, a_w_out, c_norm, c_w_in, q_norm, kv_norm, w_uq, w_ukv, invf, sgn, seq):
    t = x2d.shape[0]
    tm = min(PROJ_ROWS, seq)
    r0, r1, r2 = MLA_Q_RANK, MLA_Q_RANK + MLA_KV_RANK, MLA_Q_RANK + MLA_KV_RANK + MLA_ROPE_DIM
    wc = jnp.pad(c_w_in[:, :r2], ((0, 0), (0, LANES - MLA_ROPE_DIM))).astype(BF16)
    wgate = c_w_in[:, r2:].astype(BF16)
    wuq = jnp.pad(w_uq.reshape(MLA_Q_RANK, MLA_HEADS, MLA_NOPE_DIM + MLA_ROPE_DIM),
                  ((0, 0), (0, 0), (0, MLA_QK_PAD - MLA_NOPE_DIM - MLA_ROPE_DIM))
                  ).reshape(MLA_Q_RANK, MLA_HEADS * MLA_QK_PAD).astype(BF16)
    wukv = w_ukv.reshape(MLA_KV_RANK, MLA_HEADS, MLA_NOPE_DIM + MLA_V_DIM)
    wuk = wukv[:, :, :MLA_NOPE_DIM].reshape(MLA_KV_RANK, -1).astype(BF16)
    wuv = wukv[:, :, MLA_NOPE_DIM:].reshape(MLA_KV_RANK, -1).astype(BF16)
    del r0, r1
    ins = [x2d, mixa, mixb, pos2d, a_w_out.astype(BF16), c_norm.reshape(1, -1), wc, wgate,
           q_norm.reshape(1, -1), kv_norm.reshape(1, -1), wuq, wuk, wuv, invf, sgn]
    full = lambda a: pl.BlockSpec(a.shape, lambda i: (0,) * a.ndim)
    rows = lambda n: pl.BlockSpec((tm, n), lambda i: (i, 0))
    out_shapes = [
        jax.ShapeDtypeStruct((t, D_MODEL), F32),
        jax.ShapeDtypeStruct((t, MLA_HEADS * MLA_QK_PAD), BF16),
        jax.ShapeDtypeStruct((t, MLA_HEADS * MLA_QK_PAD), BF16),
        jax.ShapeDtypeStruct((t, MLA_WIDTH), BF16),
        jax.ShapeDtypeStruct((t, MLA_WIDTH), BF16),
    ]
    return pl.pallas_call(
        functools.partial(_odd_proj_kernel, tm=tm),
        grid=(t // tm,),
        in_specs=[rows(D_MODEL), rows(NSA_WIDTH), rows(CONV_WIDTH), rows(1)] + [full(a) for a in ins[4:]],
        out_specs=[rows(s.shape[1]) for s in out_shapes],
        out_shape=out_shapes,
        compiler_params=pltpu.CompilerParams(dimension_semantics=("arbitrary",),
                                             vmem_limit_bytes=VMEM_LIMIT),
        name="odd_proj",
    )(*ins)


def _flash_update(s_slabs, v, m_ref, l_ref, acc_ref, rows):
    mx = functools.reduce(jnp.maximum, s_slabs)
    m_old = m_ref[rows, :]
    m_new = jnp.maximum(m_old, jnp.max(mx, axis=1, keepdims=True))
    alpha = jnp.exp2(m_old - m_new)
    ps = [jnp.exp2(sl - m_new) for sl in s_slabs]
    l_ref[rows, :] = alpha * l_ref[rows, :] + functools.reduce(jnp.add, ps)
    p = jnp.concatenate([x.astype(BF16) for x in ps], axis=1)
    acc_ref[rows, :] = alpha * acc_ref[rows, :] + _dot(p, v)
    m_ref[rows, :] = m_new


def _mla_kernel(q_ref, k_ref, v_ref, o_ref, m_scr, l_scr, acc_scr, *, tq, kc, hp):
    t0 = pl.program_id(2) * tq
    m_scr[...] = jnp.full(m_scr.shape, NEG_INF, F32)
    l_scr[...] = jnp.zeros(l_scr.shape, F32)
    acc_scr[...] = jnp.zeros(acc_scr.shape, F32)
    nslab = kc // LANES
    diff = lax.broadcasted_iota(jnp.int32, (tq, LANES), 1) - lax.broadcasted_iota(jnp.int32, (tq, LANES), 0)

    def step(j, masked):
        start = pl.multiple_of(j * kc, kc)
        scores = []
        for h in range(hp):
            qk = slice(h * MLA_QK_PAD, (h + 1) * MLA_QK_PAD)
            scores.append(_dot_nt(q_ref[0, :, qk], k_ref[0, pl.ds(start, kc), qk]))
        for h, s in enumerate(scores):
            slabs = [s[:, c * LANES:(c + 1) * LANES] for c in range(nslab)]
            if masked:
                bound = t0 - j * kc
                slabs = [jnp.where(diff <= bound - c * LANES, sl, NEG_INF) for c, sl in enumerate(slabs)]
            v = v_ref[0, pl.ds(start, kc), h * MLA_V_DIM:(h + 1) * MLA_V_DIM]
            _flash_update(slabs, v, m_scr, l_scr, acc_scr, pl.ds(h * tq, tq))

    n_full = t0 // kc
    lax.fori_loop(0, n_full, lambda j, c: (step(j, False), c)[1], 0)
    lax.fori_loop(n_full, (t0 + tq + kc - 1) // kc, lambda j, c: (step(j, True), c)[1], 0)
    for h in range(hp):
        rows = pl.ds(h * tq, tq)
        o_ref[0, :, h * MLA_V_DIM:(h + 1) * MLA_V_DIM] = (
            acc_scr[rows, :] / jnp.sum(l_scr[rows, :], axis=1, keepdims=True)).astype(BF16)


def _mla_attention(q, k, v):
    b, seq, _ = q.shape
    tq = min(MLA_TQ, seq)
    kc = min(MLA_KC, seq)
    hp = MLA_HEADS_PER_STEP
    assert seq % tq == 0 and seq % kc == 0 and MLA_HEADS % hp == 0
    return pl.pallas_call(
        functools.partial(_mla_kernel, tq=tq, kc=kc, hp=hp),
        grid=(b, MLA_HEADS // hp, seq // tq),
        in_specs=[pl.BlockSpec((1, tq, hp * MLA_QK_PAD), lambda bi, h, qi: (bi, qi, h)),
                  pl.BlockSpec((1, seq, hp * MLA_QK_PAD), lambda bi, h, qi: (bi, 0, h)),
                  pl.BlockSpec((1, seq, hp * MLA_V_DIM), lambda bi, h, qi: (bi, 0, h))],
        out_specs=pl.BlockSpec((1, tq, hp * MLA_V_DIM), lambda bi, h, qi: (bi, qi, h)),
        out_shape=jax.ShapeDtypeStruct((b, seq, MLA_WIDTH), BF16),
        scratch_shapes=[pltpu.VMEM((hp * tq, LANES), F32), pltpu.VMEM((hp * tq, LANES), F32),
                        pltpu.VMEM((hp * tq, MLA_V_DIM), F32)],
        compiler_params=pltpu.CompilerParams(dimension_semantics=("arbitrary", "arbitrary", "arbitrary"),
                                             vmem_limit_bytes=VMEM_LIMIT),
        name="mla_attn",
    )(q, k, v)


def _final_kernel(y_ref, o_ref, sg_ref, w_ref, g_ref, out_ref):
    gated = (sg_ref[...].astype(F32) * o_ref[...].astype(F32)).astype(BF16)
    out_ref[...] = _rms(y_ref[...] + _dot(gated, w_ref[...]), g_ref[...])


def _final(y, o, sg, w_out, final_norm, seq):
    t = y.shape[0]
    tm = min(PROJ_ROWS, seq)
    rows = lambda n: pl.BlockSpec((tm, n), lambda i: (i, 0))
    full = lambda a: pl.BlockSpec(a.shape, lambda i: (0,) * a.ndim)
    w = w_out.astype(BF16)
    g = final_norm.reshape(1, -1)
    return pl.pallas_call(
        _final_kernel,
        grid=(t // tm,),
        in_specs=[rows(D_MODEL), rows(MLA_WIDTH), rows(MLA_WIDTH), full(w), full(g)],
        out_specs=rows(D_MODEL),
        out_shape=jax.ShapeDtypeStruct((t, D_MODEL), F32),
        compiler_params=pltpu.CompilerParams(dimension_semantics=("arbitrary",),
                                             vmem_limit_bytes=VMEM_LIMIT),
        name="final",
    )(y, o, sg, w, g)


def _rope_constants():
    half = NSA_HEAD_DIM // 2
    inv_freq = ROPE_THETA ** (-jnp.arange(half, dtype=F32) / half)
    invf = jnp.tile(inv_freq, LANES // half).reshape(1, LANES)
    sgn = jnp.tile(jnp.concatenate([-jnp.ones((half,), F32), jnp.ones((half,), F32)]), LANES // (2 * half))
    return invf, sgn.reshape(1, LANES)


def kernel(x, positions, a_norm, a_w_in, a_pe_k, a_pe_v, a_w_ck1, a_w_ck2, a_w_cv1, a_w_cv2, a_conv_w, a_w_out, c_norm, c_w_in, c_q_norm, c_kv_norm, c_w_uq, c_w_ukv, c_w_out, final_norm):
    b, seq, d = x.shape
    assert d == D_MODEL and NSA_HEAD_DIM == MLA_ROPE_DIM
    assert a_norm.shape[0] == 1 and c_norm.shape[0] == 1
    t = b * seq
    invf, sgn = _rope_constants()
    x2d = x.reshape(t, d)
    pos2d = positions.reshape(t, 1)

    q, kc, vc, ksel, vsl, kw, vw, gates, sga, mixb = _even_proj(
        x2d, pos2d, a_norm[0], a_w_in[0], a_conv_w[0], invf, sgn, seq)
    nchunk = seq // CMP_STRIDE
    pos_cmp = jnp.pad(positions[:, CMP_BLOCK - 1::CMP_STRIDE], ((0, 0), (0, 1)))[:, :nchunk, None]
    kcmp, vcmp = _compress(kc.reshape(b, seq, -1), vc.reshape(b, seq, -1), pos_cmp, a_pe_k[0], a_pe_v[0],
                           a_w_ck1[0], a_w_ck2[0], a_w_cv1[0], a_w_cv2[0], invf, sgn)
    r3 = lambda a: a.reshape(b, seq, a.shape[-1])
    mixa = _nsa_attention(r3(q), kcmp, vcmp, r3(ksel), r3(vsl), r3(kw), r3(vw), r3(gates), r3(sga))

    y, mq, mk, mv, sg = _odd_proj(x2d, mixa.reshape(t, -1), mixb, pos2d, a_w_out[0], c_norm[0], c_w_in[0],
                                  c_q_norm[0], c_kv_norm[0], c_w_uq[0], c_w_ukv[0], invf, sgn, seq)
    o = _mla_attention(r3(mq), r3(mk), r3(mv))
    out = _final(y, o.reshape(t, -1), sg, c_w_out[0], final_norm, seq)
    return out.reshape(b, seq, d)
```

```python
import functools

import jax
import jax.numpy as jnp
import numpy as np
from jax import lax
from jax.experimental import pallas as pl
from jax.experimental.pallas import tpu as pltpu

F32 = jnp.float32
BF16 = jnp.bfloat16

D_MODEL = 1024
ROPE_THETA = 10000.0
RMS_EPS = 1e-6
NEG_INF = -1e30
FORCE_SCORE = 1e4
SEL_PENALTY = -1e9

NSA_HEADS = 8
NSA_KV_GROUPS = 2
NSA_HPG = NSA_HEADS // NSA_KV_GROUPS
NSA_HEAD_DIM = 64
NSA_WIDTH = NSA_HEADS * NSA_HEAD_DIM
NSA_KV_WIDTH = NSA_KV_GROUPS * NSA_HEAD_DIM
CMP_BLOCK = 32
CMP_STRIDE = 16
CMP_HIDDEN = 2 * NSA_HEAD_DIM
SLC_BLOCK = 64
SLC_TOPK = 16
WINDOW = 512
CONV_WIDTH = D_MODEL - NSA_WIDTH
CONV_K = 3

MLA_HEADS = 8
MLA_NOPE_DIM = 128
MLA_ROPE_DIM = 64
MLA_V_DIM = 128
MLA_Q_RANK = 256
MLA_KV_RANK = 256
MLA_WIDTH = MLA_HEADS * MLA_V_DIM
MLA_QK_PAD = 256

LANES = 128
VMEM_LIMIT = 56 * 1024 * 1024

PROJ_ROWS = 512
NSA_TQ = 256
MLA_TQ = 512
MLA_KC = 512
MLA_HEADS_PER_STEP = 4
LOG2_E = 1.4426950408889634


def _dot(a, b):
    return jnp.dot(a, b, preferred_element_type=F32)


def _dot_nt(a, b):
    return lax.dot_general(a, b, (((1,), (1,)), ((), ())), preferred_element_type=F32)


def _silu(x):
    return x * jax.nn.sigmoid(x)


def _rms(x, g):
    return x * lax.rsqrt(jnp.mean(x * x, axis=-1, keepdims=True) + RMS_EPS) * g


def _rope_tables(pos_col, invf_row, sign_row):
    ang = pos_col.astype(F32) * invf_row
    return jnp.cos(ang), jnp.sin(ang) * sign_row


def _rope_slab(s, cos2, sin2, first_half):
    swapped = jnp.where(first_half, pltpu.roll(s, 96, 1), pltpu.roll(s, 32, 1))
    return s * cos2 + swapped * sin2


def _even_proj_kernel(x_ref, pos_ref, g_ref, wqk_ref, wg_ref, wga_ref, wcb_ref, wcc_ref, wch_ref,
                      wgb_ref, convw_ref, invf_ref, sgn_ref,
                      q_ref, kc_ref, vc_ref, ksel_ref, vsl_ref, kw_ref, vw_ref, gates_ref, sga_ref,
                      mixb_ref, carry_ref, *, tiles_per_seq, tm):
    i = pl.program_id(0)
    xn = _rms(x_ref[...], g_ref[...]).astype(BF16)
    cos2, sin2 = _rope_tables(pos_ref[...], invf_ref[...], sgn_ref[...])
    lane = lax.broadcasted_iota(jnp.int32, (tm, LANES), 1)
    first_half = (lane & 32) == 0
    low = lane < 64
    rope = functools.partial(_rope_slab, cos2=cos2, sin2=sin2, first_half=first_half)

    qf = _dot(xn, wqk_ref[:, 0:NSA_WIDTH])
    scale = NSA_HEAD_DIM ** -0.5 * LOG2_E
    for c in range(NSA_WIDTH // LANES):
        q_ref[:, c * LANES:(c + 1) * LANES] = (rope(qf[:, c * LANES:(c + 1) * LANES]) * scale).astype(BF16)

    seg = _dot(xn, wqk_ref[:, NSA_WIDTH:NSA_WIDTH + 6 * LANES])
    kc_ref[...] = seg[:, 0:LANES]
    vc_ref[...] = seg[:, LANES:2 * LANES]
    ksl = rope(seg[:, 2 * LANES:3 * LANES])
    vsl_ref[...] = seg[:, 3 * LANES:4 * LANES].astype(BF16)
    kw_ref[...] = rope(seg[:, 4 * LANES:5 * LANES]).astype(BF16)
    vw_ref[...] = seg[:, 5 * LANES:6 * LANES].astype(BF16)

    spos = (i % tiles_per_seq) * tm + lax.broadcasted_iota(jnp.int32, (tm, LANES), 0)
    blk = lax.shift_right_logical(spos, 6)
    ksel_ref[:, 0:LANES] = jnp.where(low, ksl, (lane - 64 == blk).astype(F32)).astype(BF16)
    ksel_ref[:, LANES:2 * LANES] = jnp.where(low, (lane == blk).astype(F32), ksl).astype(BF16)

    gates_ref[...] = jax.nn.sigmoid(_dot(xn, wg_ref[...]))
    sga_ref[...] = _silu(_dot(xn, wga_ref[...])).astype(BF16)

    u = _dot(xn, wcc_ref[...]) * _dot(xn, wch_ref[...])

    @pl.when(i % tiles_per_seq == 0)
    def _():
        carry_ref[...] = jnp.zeros_like(carry_ref)

    row = lax.broadcasted_iota(jnp.int32, (tm, CONV_WIDTH), 0)
    prev1 = carry_ref[7:8, :]
    prev2 = carry_ref[6:7, :]
    u1 = jnp.where(row == 0, prev1, pltpu.roll(u, 1, 0))
    u2 = jnp.where(row == 0, prev2, jnp.where(row == 1, prev1, pltpu.roll(u, 2, 0)))
    carry_ref[...] = u[tm - 8:tm, :]
    w = convw_ref[...]
    y = w[0:1, :] * u2 + w[1:2, :] * u1 + w[2:3, :] * u
    cb = _dot(xn, wcb_ref[...])
    mixb_ref[...] = (_silu(_dot(xn, wgb_ref[...])) * (cb * y)).astype(BF16)


def _even_proj(x2d, pos2d, norm_g, w_in, conv_w, invf, sgn, seq):
    t = x2d.shape[0]
    tm = min(PROJ_ROWS, seq)
    assert seq % tm == 0 and t % tm == 0
    sizes = (NSA_WIDTH,) + (NSA_KV_WIDTH,) * 6 + (3 * NSA_HEADS, NSA_WIDTH) + (CONV_WIDTH,) * 4
    offs = np.concatenate([[0], np.cumsum(sizes)])
    col = lambda k: w_in[:, offs[k]:offs[k + 1]]
    wqk = jnp.concatenate([col(k) for k in range(7)], axis=1).astype(BF16)
    wg = jnp.pad(col(7), ((0, 0), (0, LANES - 3 * NSA_HEADS))).astype(BF16)
    wga, wcb, wcc, wch, wgb = (col(k).astype(BF16) for k in range(8, 13))

    full = lambda a: pl.BlockSpec(a.shape, lambda i: (0,) * a.ndim)
    rows = lambda n: pl.BlockSpec((tm, n), lambda i: (i, 0))
    out_shapes = [
        jax.ShapeDtypeStruct((t, NSA_WIDTH), BF16),
        jax.ShapeDtypeStruct((t, NSA_KV_WIDTH), F32),
        jax.ShapeDtypeStruct((t, NSA_KV_WIDTH), F32),
        jax.ShapeDtypeStruct((t, 2 * LANES), BF16),
        jax.ShapeDtypeStruct((t, NSA_KV_WIDTH), BF16),
        jax.ShapeDtypeStruct((t, NSA_KV_WIDTH), BF16),
        jax.ShapeDtypeStruct((t, NSA_KV_WIDTH), BF16),
        jax.ShapeDtypeStruct((t, LANES), F32),
        jax.ShapeDtypeStruct((t, NSA_WIDTH), BF16),
        jax.ShapeDtypeStruct((t, CONV_WIDTH), BF16),
    ]
    ins = [x2d, pos2d, norm_g.reshape(1, -1), wqk, wg, wga, wcb, wcc, wch, wgb, conv_w, invf, sgn]
    in_specs = [rows(D_MODEL), rows(1)] + [full(a) for a in ins[2:]]
    return pl.pallas_call(
        functools.partial(_even_proj_kernel, tiles_per_seq=seq // tm, tm=tm),
        grid=(t // tm,),
        in_specs=in_specs,
        out_specs=[rows(s.shape[1]) for s in out_shapes],
        out_shape=out_shapes,
        scratch_shapes=[pltpu.VMEM((8, CONV_WIDTH), F32)],
        compiler_params=pltpu.CompilerParams(dimension_semantics=("arbitrary",),
                                             vmem_limit_bytes=VMEM_LIMIT),
        name="even_proj",
    )(*ins)


def _compress_kernel(kc_ref, vc_ref, pos_ref, pek_ref, pev_ref, w1k_ref, w2k_ref, w1v_ref, w2v_ref,
                     invf_ref, sgn_ref, kcmp_ref, vcmp_ref, *, nchunk):
    half = NSA_KV_GROUPS * CMP_HIDDEN

    def comp(raw_ref, pe_ref, w1_ref, w2_ref):
        raw = raw_ref[0]
        top = _dot((raw + pe_ref[0:1, :]).astype(BF16), w1_ref[:, 0:half])
        bot = _dot((raw + pe_ref[1:2, :]).astype(BF16), w1_ref[:, half:2 * half])
        h = _silu(top + pltpu.roll(bot, nchunk - 1, 0))
        return _dot(h.astype(BF16), w2_ref[...])

    cos2, sin2 = _rope_tables(pos_ref[0], invf_ref[...], sgn_ref[...])
    lane = lax.broadcasted_iota(jnp.int32, (nchunk, LANES), 1)
    kcmp = comp(kc_ref, pek_ref, w1k_ref, w2k_ref)
    kcmp_ref[0] = _rope_slab(kcmp, cos2, sin2, (lane & 32) == 0).astype(BF16)
    vcmp_ref[0] = comp(vc_ref, pev_ref, w1v_ref, w2v_ref).astype(BF16)


def _compress_weights(pe, w1, w2):
    eye = jnp.eye(NSA_KV_GROUPS, dtype=F32)
    half_rows = CMP_BLOCK // 2
    w1r = w1.reshape(CMP_BLOCK, NSA_HEAD_DIM, CMP_HIDDEN)
    big = lambda part: jnp.einsum('ldn,gh->lgdhn', part, eye).reshape(
        half_rows * NSA_KV_WIDTH, NSA_KV_GROUPS * CMP_HIDDEN)
    w1big = jnp.concatenate([big(w1r[:half_rows]), big(w1r[half_rows:])], axis=1).astype(BF16)
    w2big = jnp.einsum('nd,gh->gnhd', w2, eye).reshape(NSA_KV_GROUPS * CMP_HIDDEN, NSA_KV_WIDTH).astype(BF16)
    pe_row = lambda part: jnp.broadcast_to(part[:, None, :], (half_rows, NSA_KV_GROUPS, NSA_HEAD_DIM)).reshape(1, -1)
    pe_rows = jnp.concatenate([pe_row(pe[:half_rows]), pe_row(pe[half_rows:]),
                               jnp.zeros((6, half_rows * NSA_KV_WIDTH), F32)], axis=0)
    return pe_rows, w1big, w2big


def _compress(kc, vc, pos_cmp, pe_k, pe_v, w_ck1, w_ck2, w_cv1, w_cv2, invf, sgn):
    b, seq, _ = kc.shape
    nchunk = seq // CMP_STRIDE
    width = CMP_STRIDE * NSA_KV_WIDTH
    pek, w1k, w2k = _compress_weights(pe_k, w_ck1, w_ck2)
    pev, w1v, w2v = _compress_weights(pe_v, w_cv1, w_cv2)
    ins = [kc.reshape(b, nchunk, width), vc.reshape(b, nchunk, width), pos_cmp,
           pek, pev, w1k, w2k, w1v, w2v, invf, sgn]
    full = lambda a: pl.BlockSpec(a.shape, lambda i: (0,) * a.ndim)
    per_b = lambda n: pl.BlockSpec((1, nchunk, n), lambda i: (i, 0, 0))
    out_shape = [jax.ShapeDtypeStruct((b, nchunk, NSA_KV_WIDTH), BF16)] * 2
    return pl.pallas_call(
        functools.partial(_compress_kernel, nchunk=nchunk),
        grid=(b,),
        in_specs=[per_b(width), per_b(width), per_b(1)] + [full(a) for a in ins[3:]],
        out_specs=[per_b(NSA_KV_WIDTH)] * 2,
        out_shape=out_shape,
        compiler_params=pltpu.CompilerParams(dimension_semantics=("arbitrary",),
                                             vmem_limit_bytes=VMEM_LIMIT),
        name="compress",
    )(*ins)


def _flash_update_t(s_t, v_t, m_ref, acc_ref, cols):
    m_old = m_ref[:, cols]
    m_new = jnp.maximum(m_old, jnp.max(s_t, axis=0, keepdims=True))
    p = jnp.exp2(s_t - m_new)
    acc_ref[:, cols] = jnp.exp2(m_old - m_new) * acc_ref[:, cols] + _dot(v_t, p.astype(BF16))
    m_ref[:, cols] = m_new


def _nsa_kernel(q_ref, kcmp_ref, vcmp_t_ref, ksel_ref, vsl_t_ref, kw_ref, vw_t_ref, gates_ref, sga_ref,
                ov_t_ref, out_ref, qz_scr, qa_scr, v_scr, rank_scr, m_scr, acc_scr, mix_scr,
                *, tq, kc, n_cmp, n_top):
    t0 = pl.program_id(1) * tq
    n = NSA_HPG * tq
    ncp = kcmp_ref.shape[1]
    hd = NSA_HEAD_DIM
    q_t = q_ref[0].astype(F32).T
    gates_t = gates_ref[0].T
    kq = lax.broadcasted_iota(jnp.int32, (kc, tq), 0) - lax.broadcasted_iota(jnp.int32, (kc, tq), 1)
    causal = kq <= 0
    newer = kq > 0

    def add_branch(branch, g, o_t):
        for h in range(NSA_HPG):
            head = NSA_HPG * g + h
            gate = gates_t[head * 3 + branch:head * 3 + branch + 1, :]
            term = gate * o_t[:, h * tq:(h + 1) * tq]
            r = pl.ds(head * hd, hd)
            if branch == 0:
                mix_scr[r, :] = term
            else:
                mix_scr[r, :] += term

    def reset_flash():
        m_scr[...] = jnp.full(m_scr.shape, NEG_INF, F32)
        acc_scr[...] = jnp.zeros(acc_scr.shape, F32)

    def flash_step(q_scr, k_ref, v_t_ref, chunks):
        chains = []
        for j, mask in chunks:
            start = pl.multiple_of(j * kc, kc)
            for g in range(NSA_KV_GROUPS):
                col0 = g * LANES if k_ref is ksel_ref else 0
                k = k_ref[0, pl.ds(start, kc), col0:col0 + LANES]
                for h in range(NSA_HPG):
                    s_t = _dot(k, q_scr[g, :, h * tq:(h + 1) * tq])
                    chains.append((j, mask, g, g * n + h * tq, s_t))
        for j, mask, g, c0, s_t in chains:
            if mask is not None:
                s_t = jnp.where(mask, s_t, NEG_INF)
            _flash_update_t(s_t, v_t_ref[0, j, g], m_scr, acc_scr, pl.ds(c0, tq))

    def finish_flash(branch):
        for g in range(NSA_KV_GROUPS):
            c = pl.ds(g * n, n)
            ones_row = hd * (1 - g)
            add_branch(branch, g, acc_scr[g * hd:(g + 1) * hd, c] / acc_scr[ones_row:ones_row + 1, c])

    zeros_q = jnp.zeros((hd, n), F32)
    for g in range(NSA_KV_GROUPS):
        qg = jnp.concatenate([q_t[(NSA_HPG * g + h) * hd:(NSA_HPG * g + h + 1) * hd, :] for h in range(NSA_HPG)],
                             axis=1)
        qz = jnp.concatenate([qg, zeros_q] if g == 0 else [zeros_q, qg], axis=0).astype(BF16)
        qz_scr[g] = qz

        s_t = _dot(kcmp_ref[0], qz)
        cq = (lax.broadcasted_iota(jnp.int32, (ncp, tq), 0) * CMP_STRIDE
              - lax.broadcasted_iota(jnp.int32, (ncp, tq), 1))
        valid1 = (cq <= t0 - (CMP_BLOCK - 1)) & (lax.broadcasted_iota(jnp.int32, (ncp, tq), 0) < n_cmp)
        valid = jnp.concatenate([valid1] * NSA_HPG, axis=1)
        s_t = jnp.where(valid, s_t, NEG_INF)
        p = jnp.where(valid, jnp.exp2(s_t - jnp.max(s_t, axis=0, keepdims=True)), 0.0)
        l = jnp.sum(p, axis=0, keepdims=True)
        pn = p * jnp.where(l > 0.0, 1.0 / l, 0.0)
        o_t = _dot(vcmp_t_ref[0], pn.astype(BF16))
        add_branch(0, g, o_t[g * hd:(g + 1) * hd, :])

        psum = functools.reduce(jnp.add, [pn[:, h * tq:(h + 1) * tq] for h in range(NSA_HPG)])
        hi = psum.astype(BF16)
        lo = (psum - hi.astype(F32)).astype(BF16)
        imp_t = (_dot(ov_t_ref[...], hi) + _dot(ov_t_ref[...], lo))[0:64, :]
        nb = lax.broadcasted_iota(jnp.int32, (64, tq), 0)
        tqv = t0 + lax.broadcasted_iota(jnp.int32, (64, tq), 1)
        cur = lax.shift_right_logical(tqv, 6)
        forced = (nb == 0) | (nb == cur) | (nb == cur - 1)
        v_scr[...] = jnp.where(nb * SLC_BLOCK <= tqv, jnp.where(forced, FORCE_SCORE, imp_t), -1.0)
        sub = lax.broadcasted_iota(jnp.int32, (8, tq), 0)
        vch = [v_scr[r * 8:(r + 1) * 8, :] for r in range(8)]
        rank_scr[...] = jnp.zeros(rank_scr.shape, jnp.int32)
        for mg in range(8):
            @pl.when(mg * 8 * SLC_BLOCK < t0 + tq)
            def _():
                count = [jnp.zeros((8, tq), jnp.int32) for _ in range(8)]
                for mblk in range(mg * 8, mg * 8 + 8):
                    vm = jnp.broadcast_to(v_scr[mblk:mblk + 1, :], (8, tq))
                    for r in range(8):
                        if r > mg:
                            before = vm >= vch[r]
                        elif r < mg:
                            before = vm > vch[r]
                        else:
                            before = (vm > vch[r]) | ((vm == vch[r]) & (sub > mblk - r * 8))
                        count[r] = count[r] + jnp.where(before, 1, 0)
                for r in range(8):
                    rank_scr[r * 8:(r + 1) * 8, :] += count[r]
        pen = jnp.where(rank_scr[...] < n_top, 0.0, SEL_PENALTY)
        pen4 = jnp.concatenate([pen] * NSA_HPG, axis=1)
        qa_scr[g] = jnp.concatenate([qg, pen4] if g == 0 else [pen4, qg], axis=0).astype(BF16)

    jd = t0 // kc

    reset_flash()
    sel = functools.partial(flash_step, qa_scr, ksel_ref, vsl_t_ref)
    lax.fori_loop(0, jd // 2, lambda i, cr: (sel([(2 * i, None), (2 * i + 1, None)]), cr)[1], 0)

    @pl.when(jd % 2 == 0)
    def _():
        sel([(jd, causal)])

    @pl.when(jd % 2 == 1)
    def _():
        sel([(jd - 1, None), (jd, causal)])

    finish_flash(1)

    reset_flash()
    win = functools.partial(flash_step, qz_scr, kw_ref, vw_t_ref)
    assert WINDOW == 2 * kc

    @pl.when(jd == 0)
    def _():
        win([(jd, causal)])

    @pl.when(jd == 1)
    def _():
        win([(jd - 1, None), (jd, causal)])

    @pl.when(jd >= 2)
    def _():
        win([(jd - 2, newer), (jd - 1, None), (jd, causal)])

    finish_flash(2)

    out_ref[0] = (sga_ref[0].astype(F32) * mix_scr[...].T).astype(BF16)


def _overlap_t(ncp, n_slc):
    cmp_starts = np.arange(ncp) * CMP_STRIDE
    slc_starts = np.arange(LANES) * SLC_BLOCK
    ov = ((cmp_starts[None, :] < slc_starts[:, None] + SLC_BLOCK)
          & (cmp_starts[None, :] + CMP_BLOCK > slc_starts[:, None])
          & (np.arange(LANES)[:, None] < n_slc))
    return jnp.asarray(ov.astype(np.float32), BF16)


def _values_t(v, kc):
    b, seq, d = v.shape
    v_t = v.reshape(b, seq // kc, 1, kc, d).transpose(0, 1, 2, 4, 3)
    row_group = (jnp.arange(d) // NSA_HEAD_DIM)[None, :, None]
    mine = row_group == jnp.arange(NSA_KV_GROUPS)[:, None, None]
    return jnp.where(mine, v_t, jnp.ones((), v.dtype))


def _nsa_attention(q, kcmp, vcmp, ksel, vsl, kw, vw, gates, sga):
    b, seq, _ = q.shape
    tq = min(NSA_TQ, seq)
    n_slc = seq // SLC_BLOCK
    assert n_slc <= 64 and seq % tq == 0 and WINDOW % tq == 0 and tq & (tq - 1) == 0
    ncp = kcmp.shape[1]
    n_cmp = (seq - CMP_BLOCK) // CMP_STRIDE + 1
    ov_t = _overlap_t(ncp, n_slc)
    vcmp_t = vcmp.transpose(0, 2, 1)
    vsl_t = _values_t(vsl, tq)
    vw_t = _values_t(vw, tq)
    n = NSA_HPG * tq
    tile = lambda w: pl.BlockSpec((1, tq, w), lambda bi, qi: (bi, qi, 0))
    per_b = lambda a: pl.BlockSpec((1,) + a.shape[1:], lambda bi, qi: (bi,) + (0,) * (a.ndim - 1))
    full = lambda a: pl.BlockSpec(a.shape, lambda bi, qi: (0, 0))
    return pl.pallas_call(
        functools.partial(_nsa_kernel, tq=tq, kc=tq, n_cmp=n_cmp, n_top=min(SLC_TOPK, n_slc)),
        grid=(b, seq // tq),
        in_specs=[tile(NSA_WIDTH), per_b(kcmp), per_b(vcmp_t), per_b(ksel), per_b(vsl_t), per_b(kw), per_b(vw_t),
                  tile(LANES), tile(NSA_WIDTH), full(ov_t)],
        out_specs=tile(NSA_WIDTH),
        out_shape=jax.ShapeDtypeStruct((b, seq, NSA_WIDTH), BF16),
        scratch_shapes=[pltpu.VMEM((NSA_KV_GROUPS, LANES, n), BF16),
                        pltpu.VMEM((NSA_KV_GROUPS, LANES, n), BF16),
                        pltpu.VMEM((64, tq), F32),
                        pltpu.VMEM((64, tq), jnp.int32),
                        pltpu.VMEM((1, NSA_KV_GROUPS * n), F32),
                        pltpu.VMEM((LANES, NSA_KV_GROUPS * n), F32),
                        pltpu.VMEM((NSA_WIDTH, tq), F32)],
        compiler_params=pltpu.CompilerParams(dimension_semantics=("arbitrary", "arbitrary"),
                                             vmem_limit_bytes=VMEM_LIMIT),
        name="nsa_attn",
    )(q, kcmp, vcmp_t, ksel, vsl_t, kw, vw_t, gates, sga, ov_t)


def _odd_proj_kernel(x_ref, mixa_ref, mixb_ref, pos_ref, wout_ref, g_ref, wc_ref, wgate_ref, qn_ref, kvn_ref,
                     wuq_ref, wuk_ref, wuv_ref, invf_ref, sgn_ref,
                     y_ref, q_ref, k_ref, v_ref, sg_ref, *, tm):
    y = (x_ref[...] + _dot(mixa_ref[...], wout_ref[0:NSA_WIDTH, :])
         + _dot(mixb_ref[...], wout_ref[NSA_WIDTH:, :]))
    y_ref[...] = y
    yn = _rms(y, g_ref[...]).astype(BF16)
    cos2, sin2 = _rope_tables(pos_ref[...], invf_ref[...], sgn_ref[...])
    lane = lax.broadcasted_iota(jnp.int32, (tm, LANES), 1)
    rope = functools.partial(_rope_slab, cos2=cos2, sin2=sin2, first_half=(lane & 32) == 0)

    seg = _dot(yn, wc_ref[...])
    k_pe = rope(seg[:, MLA_Q_RANK + MLA_KV_RANK:]).astype(BF16)
    cq = _rms(seg[:, 0:MLA_Q_RANK], qn_ref[...]).astype(BF16)
    ckv = _rms(seg[:, MLA_Q_RANK:MLA_Q_RANK + MLA_KV_RANK], kvn_ref[...]).astype(BF16)
    scale = (MLA_NOPE_DIM + MLA_ROPE_DIM) ** -0.5 * LOG2_E
    qf = _dot(cq, wuq_ref[...])
    kf = _dot(ckv, wuk_ref[...])
    for h in range(MLA_HEADS):
        c0 = h * MLA_QK_PAD
        q_ref[:, c0:c0 + LANES] = (qf[:, c0:c0 + LANES] * scale).astype(BF16)
        q_ref[:, c0 + LANES:c0 + 2 * LANES] = (rope(qf[:, c0 + LANES:c0 + 2 * LANES]) * scale).astype(BF16)
        k_ref[:, c0:c0 + LANES] = kf[:, h * LANES:(h + 1) * LANES].astype(BF16)
        k_ref[:, c0 + LANES:c0 + 2 * LANES] = k_pe
    v_ref[...] = _dot(ckv, wuv_ref[...]).astype(BF16)
    sg_ref[...] = _silu(_dot(yn, wgate_ref[...])).astype(BF16)


def _odd_proj(x2d, mixa, mixb, pos2d, a_w_out, c_norm, c_w_in, q_norm, kv_norm, w_uq, w_ukv, invf, sgn, seq):
    t = x2d.shape[0]
    tm = min(PROJ_ROWS, seq)
    r0, r1, r2 = MLA_Q_RANK, MLA_Q_RANK + MLA_KV_RANK, MLA_Q_RANK + MLA_KV_RANK + MLA_ROPE_DIM
    wc = jnp.pad(c_w_in[:, :r2], ((0, 0), (0, LANES - MLA_ROPE_DIM))).astype(BF16)
    wgate = c_w_in[:, r2:].astype(BF16)
    wuq = jnp.pad(w_uq.reshape(MLA_Q_RANK, MLA_HEADS, MLA_NOPE_DIM + MLA_ROPE_DIM),
                  ((0, 0), (0, 0), (0, MLA_QK_PAD - MLA_NOPE_DIM - MLA_ROPE_DIM))
                  ).reshape(MLA_Q_RANK, MLA_HEADS * MLA_QK_PAD).astype(BF16)
    wukv = w_ukv.reshape(MLA_KV_RANK, MLA_HEADS, MLA_NOPE_DIM + MLA_V_DIM)
    wuk = wukv[:, :, :MLA_NOPE_DIM].reshape(MLA_KV_RANK, -1).astype(BF16)
    wuv = wukv[:, :, MLA_NOPE_DIM:].reshape(MLA_KV_RANK, -1).astype(BF16)
    del r0, r1
    ins = [x2d, mixa, mixb, pos2d, a_w_out.astype(BF16), c_norm.reshape(1, -1), wc, wgate,
           q_norm.reshape(1, -1), kv_norm.reshape(1, -1), wuq, wuk, wuv, invf, sgn]
    full = lambda a: pl.BlockSpec(a.shape, lambda i: (0,) * a.ndim)
    rows = lambda n: pl.BlockSpec((tm, n), lambda i: (i, 0))
    out_shapes = [
        jax.ShapeDtypeStruct((t, D_MODEL), F32),
        jax.ShapeDtypeStruct((t, MLA_HEADS * MLA_QK_PAD), BF16),
        jax.ShapeDtypeStruct((t, MLA_HEADS * MLA_QK_PAD), BF16),
        jax.ShapeDtypeStruct((t, MLA_WIDTH), BF16),
        jax.ShapeDtypeStruct((t, MLA_WIDTH), BF16),
    ]
    return pl.pallas_call(
        functools.partial(_odd_proj_kernel, tm=tm),
        grid=(t // tm,),
        in_specs=[rows(D_MODEL), rows(NSA_WIDTH), rows(CONV_WIDTH), rows(1)] + [full(a) for a in ins[4:]],
        out_specs=[rows(s.shape[1]) for s in out_shapes],
        out_shape=out_shapes,
        compiler_params=pltpu.CompilerParams(dimension_semantics=("arbitrary",),
                                             vmem_limit_bytes=VMEM_LIMIT),
        name="odd_proj",
    )(*ins)


def _flash_update(s_slabs, v, m_ref, l_ref, acc_ref, rows):
    mx = functools.reduce(jnp.maximum, s_slabs)
    m_old = m_ref[rows, :]
    m_new = jnp.maximum(m_old, jnp.max(mx, axis=1, keepdims=True))
    alpha = jnp.exp2(m_old - m_new)
    ps = [jnp.exp2(sl - m_new) for sl in s_slabs]
    l_ref[rows, :] = alpha * l_ref[rows, :] + functools.reduce(jnp.add, ps)
    p = jnp.concatenate([x.astype(BF16) for x in ps], axis=1)
    acc_ref[rows, :] = alpha * acc_ref[rows, :] + _dot(p, v)
    m_ref[rows, :] = m_new


def _mla_kernel(q_ref, k_ref, v_ref, o_ref, m_scr, l_scr, acc_scr, *, tq, kc, hp):
    t0 = pl.program_id(2) * tq
    m_scr[...] = jnp.full(m_scr.shape, NEG_INF, F32)
    l_scr[...] = jnp.zeros(l_scr.shape, F32)
    acc_scr[...] = jnp.zeros(acc_scr.shape, F32)
    nslab = kc // LANES
    diff = lax.broadcasted_iota(jnp.int32, (tq, LANES), 1) - lax.broadcasted_iota(jnp.int32, (tq, LANES), 0)

    def step(j, masked):
        start = pl.multiple_of(j * kc, kc)
        scores = []
        for h in range(hp):
            qk = slice(h * MLA_QK_PAD, (h + 1) * MLA_QK_PAD)
            scores.append(_dot_nt(q_ref[0, :, qk], k_ref[0, pl.ds(start, kc), qk]))
        for h, s in enumerate(scores):
            slabs = [s[:, c * LANES:(c + 1) * LANES] for c in range(nslab)]
            if masked:
                bound = t0 - j * kc
                slabs = [jnp.where(diff <= bound - c * LANES, sl, NEG_INF) for c, sl in enumerate(slabs)]
            v = v_ref[0, pl.ds(start, kc), h * MLA_V_DIM:(h + 1) * MLA_V_DIM]
            _flash_update(slabs, v, m_scr, l_scr, acc_scr, pl.ds(h * tq, tq))

    n_full = t0 // kc
    lax.fori_loop(0, n_full, lambda j, c: (step(j, False), c)[1], 0)
    lax.fori_loop(n_full, (t0 + tq + kc - 1) // kc, lambda j, c: (step(j, True), c)[1], 0)
    for h in range(hp):
        rows = pl.ds(h * tq, tq)
        o_ref[0, :, h * MLA_V_DIM:(h + 1) * MLA_V_DIM] = (
            acc_scr[rows, :] / jnp.sum(l_scr[rows, :], axis=1, keepdims=True)).astype(BF16)


def _mla_attention(q, k, v):
    b, seq, _ = q.shape
    tq = min(MLA_TQ, seq)
    kc = min(MLA_KC, seq)
    hp = MLA_HEADS_PER_STEP
    assert seq % tq == 0 and seq % kc == 0 and MLA_HEADS % hp == 0
    return pl.pallas_call(
        functools.partial(_mla_kernel, tq=tq, kc=kc, hp=hp),
        grid=(b, MLA_HEADS // hp, seq // tq),
        in_specs=[pl.BlockSpec((1, tq, hp * MLA_QK_PAD), lambda bi, h, qi: (bi, qi, h)),
                  pl.BlockSpec((1, seq, hp * MLA_QK_PAD), lambda bi, h, qi: (bi, 0, h)),
                  pl.BlockSpec((1, seq, hp * MLA_V_DIM), lambda bi, h, qi: (bi, 0, h))],
        out_specs=pl.BlockSpec((1, tq, hp * MLA_V_DIM), lambda bi, h, qi: (bi, qi, h)),
        out_shape=jax.ShapeDtypeStruct((b, seq, MLA_WIDTH), BF16),
        scratch_shapes=[pltpu.VMEM((hp * tq, LANES), F32), pltpu.VMEM((hp * tq, LANES), F32),
                        pltpu.VMEM((hp * tq, MLA_V_DIM), F32)],
        compiler_params=pltpu.CompilerParams(dimension_semantics=("arbitrary", "arbitrary", "arbitrary"),
                                             vmem_limit_bytes=VMEM_LIMIT),
        name="mla_attn",
    )(q, k, v)


def _final_kernel(y_ref, o_ref, sg_ref, w_ref, g_ref, out_ref):
    gated = (sg_ref[...].astype(F32) * o_ref[...].astype(F32)).astype(BF16)
    out_ref[...] = _rms(y_ref[...] + _dot(gated, w_ref[...]), g_ref[...])


def _final(y, o, sg, w_out, final_norm, seq):
    t = y.shape[0]
    tm = min(PROJ_ROWS, seq)
    rows = lambda n: pl.BlockSpec((tm, n), lambda i: (i, 0))
    full = lambda a: pl.BlockSpec(a.shape, lambda i: (0,) * a.ndim)
    w = w_out.astype(BF16)
    g = final_norm.reshape(1, -1)
    return pl.pallas_call(
        _final_kernel,
        grid=(t // tm,),
        in_specs=[rows(D_MODEL), rows(MLA_WIDTH), rows(MLA_WIDTH), full(w), full(g)],
        out_specs=rows(D_MODEL),
        out_shape=jax.ShapeDtypeStruct((t, D_MODEL), F32),
        compiler_params=pltpu.CompilerParams(dimension_semantics=("arbitrary",),
                                             vmem_limit_bytes=VMEM_LIMIT),
        name="final",
    )(y, o, sg, w, g)


def _rope_constants():
    half = NSA_HEAD_DIM // 2
    inv_freq = ROPE_THETA ** (-jnp.arange(half, dtype=F32) / half)
    invf = jnp.tile(inv_freq, LANES // half).reshape(1, LANES)
    sgn = jnp.tile(jnp.concatenate([-jnp.ones((half,), F32), jnp.ones((half,), F32)]), LANES // (2 * half))
    return invf, sgn.reshape(1, LANES)


def kernel(x, positions, a_norm, a_w_in, a_pe_k, a_pe_v, a_w_ck1, a_w_ck2, a_w_cv1, a_w_cv2, a_conv_w, a_w_out, c_norm, c_w_in, c_q_norm, c_kv_norm, c_w_uq, c_w_ukv, c_w_out, final_norm):
    b, seq, d = x.shape
    assert d == D_MODEL and NSA_HEAD_DIM == MLA_ROPE_DIM
    assert a_norm.shape[0] == 1 and c_norm.shape[0] == 1
    t = b * seq
    invf, sgn = _rope_constants()
    x2d = x.reshape(t, d)
    pos2d = positions.reshape(t, 1)

    q, kc, vc, ksel, vsl, kw, vw, gates, sga, mixb = _even_proj(
        x2d, pos2d, a_norm[0], a_w_in[0], a_conv_w[0], invf, sgn, seq)
    nchunk = seq // CMP_STRIDE
    pos_cmp = jnp.pad(positions[:, CMP_BLOCK - 1::CMP_STRIDE], ((0, 0), (0, 1)))[:, :nchunk, None]
    kcmp, vcmp = _compress(kc.reshape(b, seq, -1), vc.reshape(b, seq, -1), pos_cmp, a_pe_k[0], a_pe_v[0],
                           a_w_ck1[0], a_w_ck2[0], a_w_cv1[0], a_w_cv2[0], invf, sgn)
    r3 = lambda a: a.reshape(b, seq, a.shape[-1])
    mixa = _nsa_attention(r3(q), kcmp, vcmp, r3(ksel), r3(vsl), r3(kw), r3(vw), r3(gates), r3(sga))

    y, mq, mk, mv, sg = _odd_proj(x2d, mixa.reshape(t, -1), mixb, pos2d, a_w_out[0], c_norm[0], c_w_in[0],
                                  c_q_norm[0], c_kv_norm[0], c_w_uq[0], c_w_ukv[0], invf, sgn, seq)
    o = _mla_attention(r3(mq), r3(mk), r3(mv))
    out = _final(y, o.reshape(t, -1), sg, c_w_out[0], final_norm, seq)
    return out.reshape(b, seq, d)
```

```python
import functools

import jax
import jax.numpy as jnp
import numpy as np
from jax import lax
from jax.experimental import pallas as pl
from jax.experimental.pallas import tpu as pltpu

F32 = jnp.float32
BF16 = jnp.bfloat16

D_MODEL = 1024
ROPE_THETA = 10000.0
RMS_EPS = 1e-6
NEG_INF = -1e30
FORCE_SCORE = 1e4
SEL_PENALTY = -1e9

NSA_HEADS = 8
NSA_KV_GROUPS = 2
NSA_HPG = NSA_HEADS // NSA_KV_GROUPS
NSA_HEAD_DIM = 64
NSA_WIDTH = NSA_HEADS * NSA_HEAD_DIM
NSA_KV_WIDTH = NSA_KV_GROUPS * NSA_HEAD_DIM
CMP_BLOCK = 32
CMP_STRIDE = 16
CMP_HIDDEN = 2 * NSA_HEAD_DIM
SLC_BLOCK = 64
SLC_TOPK = 16
WINDOW = 512
CONV_WIDTH = D_MODEL - NSA_WIDTH
CONV_K = 3

MLA_HEADS = 8
MLA_NOPE_DIM = 128
MLA_ROPE_DIM = 64
MLA_V_DIM = 128
MLA_Q_RANK = 256
MLA_KV_RANK = 256
MLA_WIDTH = MLA_HEADS * MLA_V_DIM
MLA_QK_PAD = 256

LANES = 128
VMEM_LIMIT = 56 * 1024 * 1024

PROJ_ROWS = 512
NSA_TQ = 256
MLA_TQ = 512
MLA_HEADS_PER_STEP = 4
MLA_ONES_ROWS = 16
LOG2_E = 1.4426950408889634


def _dot(a, b):
    return jnp.dot(a, b, preferred_element_type=F32)


def _dot_nt(a, b):
    return lax.dot_general(a, b, (((1,), (1,)), ((), ())), preferred_element_type=F32)


def _silu(x):
    return x * jax.nn.sigmoid(x)


def _rms(x, g):
    return x * lax.rsqrt(jnp.mean(x * x, axis=-1, keepdims=True) + RMS_EPS) * g


def _rope_tables(pos_col, invf_row, sign_row):
    ang = pos_col.astype(F32) * invf_row
    return jnp.cos(ang), jnp.sin(ang) * sign_row


def _rope_slab(s, cos2, sin2, first_half):
    swapped = jnp.where(first_half, pltpu.roll(s, 96, 1), pltpu.roll(s, 32, 1))
    return s * cos2 + swapped * sin2


def _rope_tables_t(invf_col, pos_row):
    ang = invf_col * pos_row.astype(F32)
    return jnp.cos(ang), jnp.sin(ang)


def _flash_update_t(s_t, v_t, m_ref, acc_ref, cols):
    m_old = m_ref[:, cols]
    m_new = jnp.maximum(m_old, jnp.max(s_t, axis=0, keepdims=True))
    p = jnp.exp2(s_t - m_new)
    acc_ref[:, cols] = jnp.exp2(m_old - m_new) * acc_ref[:, cols] + _dot(v_t, p.astype(BF16))
    m_ref[:, cols] = m_new


def _even_proj_kernel(x_ref, posr_ref, g_ref, wqt_ref, wkv_ref, wg_ref, wga_ref, wcb_ref, wcc_ref, wch_ref,
                      wgb_ref, convw_ref, invfc_ref,
                      qt_ref, kc_ref, vc_ref, ksel_ref, vslt_ref, kw_ref, vwt_ref, gates_ref, sga_ref,
                      mixb_ref, carry_ref, *, tiles_per_seq, tm, kc):
    i = pl.program_id(0)
    xn = _rms(x_ref[...], g_ref[...]).astype(BF16)
    cos_t, sin_t = _rope_tables_t(invfc_ref[...], posr_ref[...])
    half = NSA_HEAD_DIM // 2
    lane = lax.broadcasted_iota(jnp.int32, (tm, LANES), 1)
    low = lane < 64

    def rope_t(x_t):
        out = []
        for r in range(0, x_t.shape[0], 2 * half):
            x1, x2 = x_t[r:r + half], x_t[r + half:r + 2 * half]
            out += [x1 * cos_t - x2 * sin_t, x2 * cos_t + x1 * sin_t]
        return jnp.concatenate(out, axis=0)

    def store_values_t(ref, v):
        v_t = v.T
        mine0 = lax.broadcasted_iota(jnp.int32, (LANES, kc), 0) < NSA_HEAD_DIM
        for c in range(tm // kc):
            blk = v_t[:, c * kc:(c + 1) * kc]
            ref[c, 0] = jnp.where(mine0, blk, 1.0).astype(BF16)
            ref[c, 1] = jnp.where(mine0, 1.0, blk).astype(BF16)

    scale = NSA_HEAD_DIM ** -0.5 * LOG2_E
    qt_ref[...] = (rope_t(_dot_nt(wqt_ref[...], xn)) * scale).astype(BF16)

    seg = _dot(xn, wkv_ref[...])
    kc_ref[...] = seg[:, 0:LANES]
    vc_ref[...] = seg[:, LANES:2 * LANES]
    ksl = rope_t(seg[:, 2 * LANES:3 * LANES].T).T
    store_values_t(vslt_ref, seg[:, 3 * LANES:4 * LANES])
    kw_ref[...] = rope_t(seg[:, 4 * LANES:5 * LANES].T).T.astype(BF16)
    store_values_t(vwt_ref, seg[:, 5 * LANES:6 * LANES])

    spos = (i % tiles_per_seq) * tm + lax.broadcasted_iota(jnp.int32, (tm, LANES), 0)
    blk = lax.shift_right_logical(spos, 6)
    ksel_ref[:, 0:LANES] = jnp.where(low, ksl, (lane - 64 == blk).astype(F32)).astype(BF16)
    ksel_ref[:, LANES:2 * LANES] = jnp.where(low, (lane == blk).astype(F32), ksl).astype(BF16)

    gates_ref[...] = jax.nn.sigmoid(_dot(xn, wg_ref[...]))
    sga_ref[...] = _silu(_dot(xn, wga_ref[...])).astype(BF16)

    u = _dot(xn, wcc_ref[...]) * _dot(xn, wch_ref[...])

    @pl.when(i % tiles_per_seq == 0)
    def _():
        carry_ref[...] = jnp.zeros_like(carry_ref)

    row = lax.broadcasted_iota(jnp.int32, (tm, CONV_WIDTH), 0)
    prev1 = carry_ref[7:8, :]
    prev2 = carry_ref[6:7, :]
    u1 = jnp.where(row == 0, prev1, pltpu.roll(u, 1, 0))
    u2 = jnp.where(row == 0, prev2, jnp.where(row == 1, prev1, pltpu.roll(u, 2, 0)))
    carry_ref[...] = u[tm - 8:tm, :]
    w = convw_ref[...]
    y = w[0:1, :] * u2 + w[1:2, :] * u1 + w[2:3, :] * u
    cb = _dot(xn, wcb_ref[...])
    mixb_ref[...] = (_silu(_dot(xn, wgb_ref[...])) * (cb * y)).astype(BF16)


def _even_proj(x2d, pos_row, norm_g, w_in, conv_w, invf_col, seq):
    t = x2d.shape[0]
    tm = min(PROJ_ROWS, seq)
    kc = min(NSA_TQ, seq)
    assert seq % tm == 0 and t % tm == 0 and tm % kc == 0
    sizes = (NSA_WIDTH,) + (NSA_KV_WIDTH,) * 6 + (3 * NSA_HEADS, NSA_WIDTH) + (CONV_WIDTH,) * 4
    offs = np.concatenate([[0], np.cumsum(sizes)])
    col = lambda k: w_in[:, offs[k]:offs[k + 1]]
    wqt = col(0).T.astype(BF16)
    wkv = jnp.concatenate([col(k) for k in range(1, 7)], axis=1).astype(BF16)
    wg = jnp.pad(col(7), ((0, 0), (0, LANES - 3 * NSA_HEADS))).astype(BF16)
    wga, wcb, wcc, wch, wgb = (col(k).astype(BF16) for k in range(8, 13))

    full = lambda a: pl.BlockSpec(a.shape, lambda i: (0,) * a.ndim)
    rows = lambda n: pl.BlockSpec((tm, n), lambda i: (i, 0))
    cols = lambda n: pl.BlockSpec((n, tm), lambda i: (0, i))
    values_t = jax.ShapeDtypeStruct((t // kc, NSA_KV_GROUPS, LANES, kc), BF16)
    values_t_spec = pl.BlockSpec((tm // kc, NSA_KV_GROUPS, LANES, kc), lambda i: (i, 0, 0, 0))
    out_shapes = [
        jax.ShapeDtypeStruct((NSA_WIDTH, t), BF16),
        jax.ShapeDtypeStruct((t, NSA_KV_WIDTH), F32),
        jax.ShapeDtypeStruct((t, NSA_KV_WIDTH), F32),
        jax.ShapeDtypeStruct((t, 2 * LANES), BF16),
        values_t,
        jax.ShapeDtypeStruct((t, NSA_KV_WIDTH), BF16),
        values_t,
        jax.ShapeDtypeStruct((t, LANES), F32),
        jax.ShapeDtypeStruct((t, NSA_WIDTH), BF16),
        jax.ShapeDtypeStruct((t, CONV_WIDTH), BF16),
    ]
    out_specs = [cols(NSA_WIDTH), rows(NSA_KV_WIDTH), rows(NSA_KV_WIDTH), rows(2 * LANES), values_t_spec,
                 rows(NSA_KV_WIDTH), values_t_spec, rows(LANES), rows(NSA_WIDTH), rows(CONV_WIDTH)]
    ins = [x2d, pos_row, norm_g.reshape(1, -1), wqt, wkv, wg, wga, wcb, wcc, wch, wgb, conv_w, invf_col]
    in_specs = [rows(D_MODEL), cols(1)] + [full(a) for a in ins[2:]]
    return pl.pallas_call(
        functools.partial(_even_proj_kernel, tiles_per_seq=seq // tm, tm=tm, kc=kc),
        grid=(t // tm,),
        in_specs=in_specs,
        out_specs=out_specs,
        out_shape=out_shapes,
        scratch_shapes=[pltpu.VMEM((8, CONV_WIDTH), F32)],
        compiler_params=pltpu.CompilerParams(dimension_semantics=("arbitrary",),
                                             vmem_limit_bytes=VMEM_LIMIT),
        name="even_proj",
    )(*ins)


def _compress_kernel(kc_ref, vc_ref, pos_ref, pek_ref, pev_ref, w1k_ref, w2k_ref, w1v_ref, w2vt_ref,
                     invf_ref, sgn_ref, kcmp_ref, vcmpt_ref, *, nchunk):
    half = NSA_KV_GROUPS * CMP_HIDDEN

    def hidden(raw_ref, pe_ref, w1_ref):
        raw = raw_ref[0]
        top = _dot((raw + pe_ref[0:1, :]).astype(BF16), w1_ref[:, 0:half])
        bot = _dot((raw + pe_ref[1:2, :]).astype(BF16), w1_ref[:, half:2 * half])
        return _silu(top + pltpu.roll(bot, nchunk - 1, 0)).astype(BF16)

    cos2, sin2 = _rope_tables(pos_ref[0], invf_ref[...], sgn_ref[...])
    lane = lax.broadcasted_iota(jnp.int32, (nchunk, LANES), 1)
    kcmp = _dot(hidden(kc_ref, pek_ref, w1k_ref), w2k_ref[...])
    kcmp_ref[0] = _rope_slab(kcmp, cos2, sin2, (lane & 32) == 0).astype(BF16)
    vcmpt_ref[0] = _dot_nt(w2vt_ref[...], hidden(vc_ref, pev_ref, w1v_ref)).astype(BF16)


def _compress_weights(pe, w1, w2):
    eye = jnp.eye(NSA_KV_GROUPS, dtype=F32)
    half_rows = CMP_BLOCK // 2
    w1r = w1.reshape(CMP_BLOCK, NSA_HEAD_DIM, CMP_HIDDEN)
    big = lambda part: jnp.einsum('ldn,gh->lgdhn', part, eye).reshape(
        half_rows * NSA_KV_WIDTH, NSA_KV_GROUPS * CMP_HIDDEN)
    w1big = jnp.concatenate([big(w1r[:half_rows]), big(w1r[half_rows:])], axis=1).astype(BF16)
    w2big = jnp.einsum('nd,gh->gnhd', w2, eye).reshape(NSA_KV_GROUPS * CMP_HIDDEN, NSA_KV_WIDTH).astype(BF16)
    pe_row = lambda part: jnp.broadcast_to(part[:, None, :], (half_rows, NSA_KV_GROUPS, NSA_HEAD_DIM)).reshape(1, -1)
    pe_rows = jnp.concatenate([pe_row(pe[:half_rows]), pe_row(pe[half_rows:]),
                               jnp.zeros((6, half_rows * NSA_KV_WIDTH), F32)], axis=0)
    return pe_rows, w1big, w2big


def _compress(kc, vc, pos_cmp, pe_k, pe_v, w_ck1, w_ck2, w_cv1, w_cv2, invf, sgn):
    b, seq, _ = kc.shape
    nchunk = seq // CMP_STRIDE
    width = CMP_STRIDE * NSA_KV_WIDTH
    pek, w1k, w2k = _compress_weights(pe_k, w_ck1, w_ck2)
    pev, w1v, w2v = _compress_weights(pe_v, w_cv1, w_cv2)
    ins = [kc.reshape(b, nchunk, width), vc.reshape(b, nchunk, width), pos_cmp,
           pek, pev, w1k, w2k, w1v, w2v.T, invf, sgn]
    full = lambda a: pl.BlockSpec(a.shape, lambda i: (0,) * a.ndim)
    per_b = lambda n: pl.BlockSpec((1, nchunk, n), lambda i: (i, 0, 0))
    out_shape = [jax.ShapeDtypeStruct((b, nchunk, NSA_KV_WIDTH), BF16),
                 jax.ShapeDtypeStruct((b, NSA_KV_WIDTH, nchunk), BF16)]
    return pl.pallas_call(
        functools.partial(_compress_kernel, nchunk=nchunk),
        grid=(b,),
        in_specs=[per_b(width), per_b(width), per_b(1)] + [full(a) for a in ins[3:]],
        out_specs=[per_b(NSA_KV_WIDTH), pl.BlockSpec((1, NSA_KV_WIDTH, nchunk), lambda i: (i, 0, 0))],
        out_shape=out_shape,
        compiler_params=pltpu.CompilerParams(dimension_semantics=("arbitrary",),
                                             vmem_limit_bytes=VMEM_LIMIT),
        name="compress",
    )(*ins)


def _nsa_kernel(qt_ref, kcmp_ref, vcmp_t_ref, ksel_ref, vsl_t_ref, kw_ref, vw_t_ref, gates_ref, sga_ref,
                ov_t_ref, out_ref, qz_scr, qa_scr, v_scr, rank_scr, m_scr, acc_scr, mix_scr,
                *, tq, kc, n_cmp, n_top):
    t0 = pl.program_id(1) * tq
    n = NSA_HPG * tq
    ncp = kcmp_ref.shape[1]
    hd = NSA_HEAD_DIM
    q_t = qt_ref[...]
    gates_t = gates_ref[0].T
    kq = lax.broadcasted_iota(jnp.int32, (kc, tq), 0) - lax.broadcasted_iota(jnp.int32, (kc, tq), 1)
    causal = kq <= 0
    newer = kq > 0

    def add_branch(branch, g, o_t):
        for h in range(NSA_HPG):
            head = NSA_HPG * g + h
            gate = gates_t[head * 3 + branch:head * 3 + branch + 1, :]
            term = gate * o_t[:, h * tq:(h + 1) * tq]
            r = pl.ds(head * hd, hd)
            if branch == 0:
                mix_scr[r, :] = term
            else:
                mix_scr[r, :] += term

    def reset_flash():
        m_scr[...] = jnp.full(m_scr.shape, NEG_INF, F32)
        acc_scr[...] = jnp.zeros(acc_scr.shape, F32)

    def flash_step(q_scr, k_ref, v_t_ref, chunks):
        chains = []
        for j, mask in chunks:
            start = pl.multiple_of(j * kc, kc)
            for g in range(NSA_KV_GROUPS):
                col0 = g * LANES if k_ref is ksel_ref else 0
                k = k_ref[0, pl.ds(start, kc), col0:col0 + LANES]
                for h in range(NSA_HPG):
                    s_t = _dot(k, q_scr[g, :, h * tq:(h + 1) * tq])
                    chains.append((j, mask, g, g * n + h * tq, s_t))
        for j, mask, g, c0, s_t in chains:
            if mask is not None:
                s_t = jnp.where(mask, s_t, NEG_INF)
            _flash_update_t(s_t, v_t_ref[0, j, g], m_scr, acc_scr, pl.ds(c0, tq))

    def finish_flash(branch):
        for g in range(NSA_KV_GROUPS):
            c = pl.ds(g * n, n)
            ones_row = hd * (1 - g)
            add_branch(branch, g, acc_scr[g * hd:(g + 1) * hd, c] / acc_scr[ones_row:ones_row + 1, c])

    zeros_q = jnp.zeros((hd, n), BF16)
    for g in range(NSA_KV_GROUPS):
        qg = jnp.concatenate([q_t[(NSA_HPG * g + h) * hd:(NSA_HPG * g + h + 1) * hd, :] for h in range(NSA_HPG)],
                             axis=1)
        qz = jnp.concatenate([qg, zeros_q] if g == 0 else [zeros_q, qg], axis=0)
        qz_scr[g] = qz

        s_t = _dot(kcmp_ref[0], qz)
        cq = (lax.broadcasted_iota(jnp.int32, (ncp, tq), 0) * CMP_STRIDE
              - lax.broadcasted_iota(jnp.int32, (ncp, tq), 1))
        valid1 = (cq <= t0 - (CMP_BLOCK - 1)) & (lax.broadcasted_iota(jnp.int32, (ncp, tq), 0) < n_cmp)
        valid = jnp.concatenate([valid1] * NSA_HPG, axis=1)
        s_t = jnp.where(valid, s_t, NEG_INF)
        p = jnp.where(valid, jnp.exp2(s_t - jnp.max(s_t, axis=0, keepdims=True)), 0.0)
        l = jnp.sum(p, axis=0, keepdims=True)
        pn = p * jnp.where(l > 0.0, 1.0 / l, 0.0)
        o_t = _dot(vcmp_t_ref[0], pn.astype(BF16))
        add_branch(0, g, o_t[g * hd:(g + 1) * hd, :])

        psum = functools.reduce(jnp.add, [pn[:, h * tq:(h + 1) * tq] for h in range(NSA_HPG)])
        hi = psum.astype(BF16)
        lo = (psum - hi.astype(F32)).astype(BF16)
        imp_t = (_dot(ov_t_ref[...], hi) + _dot(ov_t_ref[...], lo))[0:64, :]
        nb = lax.broadcasted_iota(jnp.int32, (64, tq), 0)
        tqv = t0 + lax.broadcasted_iota(jnp.int32, (64, tq), 1)
        cur = lax.shift_right_logical(tqv, 6)
        forced = (nb == 0) | (nb == cur) | (nb == cur - 1)
        v_scr[...] = jnp.where(nb * SLC_BLOCK <= tqv, jnp.where(forced, FORCE_SCORE, imp_t), -1.0)
        sub = lax.broadcasted_iota(jnp.int32, (8, tq), 0)
        vch = [v_scr[r * 8:(r + 1) * 8, :] for r in range(8)]
        rank_scr[...] = jnp.zeros(rank_scr.shape, jnp.int32)
        for mg in range(8):
            @pl.when(mg * 8 * SLC_BLOCK < t0 + tq)
            def _():
                count = [jnp.zeros((8, tq), jnp.int32) for _ in range(8)]
                for mblk in range(mg * 8, mg * 8 + 8):
                    vm = jnp.broadcast_to(v_scr[mblk:mblk + 1, :], (8, tq))
                    for r in range(8):
                        if r > mg:
                            before = vm >= vch[r]
                        elif r < mg:
                            before = vm > vch[r]
                        else:
                            before = (vm > vch[r]) | ((vm == vch[r]) & (sub > mblk - r * 8))
                        count[r] = count[r] + jnp.where(before, 1, 0)
                for r in range(8):
                    rank_scr[r * 8:(r + 1) * 8, :] += count[r]
        pen = jnp.where(rank_scr[...] < n_top, 0.0, SEL_PENALTY)
        pen4 = jnp.concatenate([pen.astype(BF16)] * NSA_HPG, axis=1)
        qa_scr[g] = jnp.concatenate([qg, pen4] if g == 0 else [pen4, qg], axis=0)

    jd = t0 // kc

    reset_flash()
    sel = functools.partial(flash_step, qa_scr, ksel_ref, vsl_t_ref)
    lax.fori_loop(0, jd // 2, lambda i, cr: (sel([(2 * i, None), (2 * i + 1, None)]), cr)[1], 0)

    @pl.when(jd % 2 == 0)
    def _():
        sel([(jd, causal)])

    @pl.when(jd % 2 == 1)
    def _():
        sel([(jd - 1, None), (jd, causal)])

    finish_flash(1)

    reset_flash()
    win = functools.partial(flash_step, qz_scr, kw_ref, vw_t_ref)
    assert WINDOW == 2 * kc

    @pl.when(jd == 0)
    def _():
        win([(jd, causal)])

    @pl.when(jd == 1)
    def _():
        win([(jd - 1, None), (jd, causal)])

    @pl.when(jd >= 2)
    def _():
        win([(jd - 2, newer), (jd - 1, None), (jd, causal)])

    finish_flash(2)

    out_ref[0] = (sga_ref[0].astype(F32) * mix_scr[...].T).astype(BF16)


def _overlap_t(ncp, n_slc):
    cmp_starts = np.arange(ncp) * CMP_STRIDE
    slc_starts = np.arange(LANES) * SLC_BLOCK
    ov = ((cmp_starts[None, :] < slc_starts[:, None] + SLC_BLOCK)
          & (cmp_starts[None, :] + CMP_BLOCK > slc_starts[:, None])
          & (np.arange(LANES)[:, None] < n_slc))
    return jnp.asarray(ov.astype(np.float32), BF16)


def _nsa_attention(q_t, kcmp, vcmp_t, ksel, vsl_t, kw, vw_t, gates, sga):
    b, seq, _ = ksel.shape
    tq = min(NSA_TQ, seq)
    nq = seq // tq
    n_slc = seq // SLC_BLOCK
    assert n_slc <= 64 and seq % tq == 0 and WINDOW % tq == 0 and tq & (tq - 1) == 0
    ncp = kcmp.shape[1]
    n_cmp = (seq - CMP_BLOCK) // CMP_STRIDE + 1
    ov_t = _overlap_t(ncp, n_slc)
    vsl_t = vsl_t.reshape((b, nq) + vsl_t.shape[1:])
    vw_t = vw_t.reshape((b, nq) + vw_t.shape[1:])
    n = NSA_HPG * tq
    tile = lambda w: pl.BlockSpec((1, tq, w), lambda bi, qi: (bi, qi, 0))
    per_b = lambda a: pl.BlockSpec((1,) + a.shape[1:], lambda bi, qi: (bi,) + (0,) * (a.ndim - 1))
    full = lambda a: pl.BlockSpec(a.shape, lambda bi, qi: (0, 0))
    return pl.pallas_call(
        functools.partial(_nsa_kernel, tq=tq, kc=tq, n_cmp=n_cmp, n_top=min(SLC_TOPK, n_slc)),
        grid=(b, nq),
        in_specs=[pl.BlockSpec((NSA_WIDTH, tq), lambda bi, qi: (0, bi * nq + qi)),
                  per_b(kcmp), per_b(vcmp_t), per_b(ksel), per_b(vsl_t), per_b(kw), per_b(vw_t),
                  tile(LANES), tile(NSA_WIDTH), full(ov_t)],
        out_specs=tile(NSA_WIDTH),
        out_shape=jax.ShapeDtypeStruct((b, seq, NSA_WIDTH), BF16),
        scratch_shapes=[pltpu.VMEM((NSA_KV_GROUPS, LANES, n), BF16),
                        pltpu.VMEM((NSA_KV_GROUPS, LANES, n), BF16),
                        pltpu.VMEM((64, tq), F32),
                        pltpu.VMEM((64, tq), jnp.int32),
                        pltpu.VMEM((1, NSA_KV_GROUPS * n), F32),
                        pltpu.VMEM((LANES, NSA_KV_GROUPS * n), F32),
                        pltpu.VMEM((NSA_WIDTH, tq), F32)],
        compiler_params=pltpu.CompilerParams(dimension_semantics=("arbitrary", "arbitrary"),
                                             vmem_limit_bytes=VMEM_LIMIT),
        name="nsa_attn",
    )(q_t, kcmp, vcmp_t, ksel, vsl_t, kw, vw_t, gates, sga, ov_t)


def _odd_proj_kernel(x_ref, mixa_ref, mixb_ref, posr_ref, wout_ref, g_ref, wc_ref, wgate_ref, qn_ref, kvn_ref,
                     wuqt_ref, wuk_ref, wuvt_ref, invfc_ref,
                     y_ref, qt_ref, k_ref, vt_ref, sg_ref, *, tm):
    y = (x_ref[...] + _dot(mixa_ref[...], wout_ref[0:NSA_WIDTH, :])
         + _dot(mixb_ref[...], wout_ref[NSA_WIDTH:, :]))
    y_ref[...] = y
    yn = _rms(y, g_ref[...]).astype(BF16)
    cos_t, sin_t = _rope_tables_t(invfc_ref[...], posr_ref[...])
    half = MLA_ROPE_DIM // 2

    def rope_t(x1, x2):
        return x1 * cos_t - x2 * sin_t, x2 * cos_t + x1 * sin_t

    seg = _dot(yn, wc_ref[...])
    kr_t = seg[:, MLA_Q_RANK + MLA_KV_RANK:].T
    kp1, kp2 = rope_t(kr_t[0:half], kr_t[half:2 * half])
    k_pe = jnp.concatenate([kp1, kp2, kr_t[2 * half:]], axis=0).T.astype(BF16)
    cq = _rms(seg[:, 0:MLA_Q_RANK], qn_ref[...]).astype(BF16)
    ckv = _rms(seg[:, MLA_Q_RANK:MLA_Q_RANK + MLA_KV_RANK], kvn_ref[...]).astype(BF16)
    scale = (MLA_NOPE_DIM + MLA_ROPE_DIM) ** -0.5 * LOG2_E
    qt = _dot_nt(wuqt_ref[...], cq)
    kf = _dot(ckv, wuk_ref[...])
    for h in range(MLA_HEADS):
        c0 = h * MLA_QK_PAD
        r0 = c0 + MLA_NOPE_DIM
        qt_ref[c0:r0, :] = (qt[c0:r0] * scale).astype(BF16)
        q1, q2 = rope_t(qt[r0:r0 + half], qt[r0 + half:r0 + 2 * half])
        qt_ref[r0:r0 + half, :] = (q1 * scale).astype(BF16)
        qt_ref[r0 + half:r0 + 2 * half, :] = (q2 * scale).astype(BF16)
        qt_ref[r0 + 2 * half:c0 + MLA_QK_PAD, :] = jnp.zeros((MLA_QK_PAD - MLA_NOPE_DIM - 2 * half, tm), BF16)
        k_ref[:, c0:c0 + LANES] = kf[:, h * LANES:(h + 1) * LANES].astype(BF16)
        k_ref[:, c0 + LANES:c0 + 2 * LANES] = k_pe
    v_t = _dot_nt(wuvt_ref[...], ckv)
    for h in range(MLA_HEADS):
        vt_ref[0, h, 0, 0:MLA_V_DIM, :] = v_t[h * MLA_V_DIM:(h + 1) * MLA_V_DIM].astype(BF16)
        vt_ref[0, h, 0, MLA_V_DIM:, :] = jnp.ones((MLA_ONES_ROWS, tm), BF16)
    sg_ref[...] = _silu(_dot(yn, wgate_ref[...])).astype(BF16)


def _odd_proj(x2d, mixa, mixb, pos_row, a_w_out, c_norm, c_w_in, q_norm, kv_norm, w_uq, w_ukv, invf_col, seq):
    t = x2d.shape[0]
    tm = min(PROJ_ROWS, seq)
    r2 = MLA_Q_RANK + MLA_KV_RANK + MLA_ROPE_DIM
    wc = jnp.pad(c_w_in[:, :r2], ((0, 0), (0, LANES - MLA_ROPE_DIM))).astype(BF16)
    wgate = c_w_in[:, r2:].astype(BF16)
    wuqt = jnp.pad(w_uq.reshape(MLA_Q_RANK, MLA_HEADS, MLA_NOPE_DIM + MLA_ROPE_DIM),
                   ((0, 0), (0, 0), (0, MLA_QK_PAD - MLA_NOPE_DIM - MLA_ROPE_DIM))
                   ).reshape(MLA_Q_RANK, MLA_HEADS * MLA_QK_PAD).T.astype(BF16)
    wukv = w_ukv.reshape(MLA_KV_RANK, MLA_HEADS, MLA_NOPE_DIM + MLA_V_DIM)
    wuk = wukv[:, :, :MLA_NOPE_DIM].reshape(MLA_KV_RANK, -1).astype(BF16)
    wuvt = wukv[:, :, MLA_NOPE_DIM:].reshape(MLA_KV_RANK, -1).T.astype(BF16)
    nq = seq // tm
    assert tm == min(MLA_TQ, seq)
    vrows = MLA_V_DIM + MLA_ONES_ROWS
    ins = [x2d, mixa, mixb, pos_row, a_w_out.astype(BF16), c_norm.reshape(1, -1), wc, wgate,
           q_norm.reshape(1, -1), kv_norm.reshape(1, -1), wuqt, wuk, wuvt, invf_col]
    full = lambda a: pl.BlockSpec(a.shape, lambda i: (0,) * a.ndim)
    rows = lambda n: pl.BlockSpec((tm, n), lambda i: (i, 0))
    cols = lambda n: pl.BlockSpec((n, tm), lambda i: (0, i))
    out_shapes = [
        jax.ShapeDtypeStruct((t, D_MODEL), F32),
        jax.ShapeDtypeStruct((MLA_HEADS * MLA_QK_PAD, t), BF16),
        jax.ShapeDtypeStruct((t, MLA_HEADS * MLA_QK_PAD), BF16),
        jax.ShapeDtypeStruct((t // seq, MLA_HEADS, nq, vrows, tm), BF16),
        jax.ShapeDtypeStruct((t, MLA_WIDTH), BF16),
    ]
    vt_spec = pl.BlockSpec((1, MLA_HEADS, 1, vrows, tm), lambda i: (i // nq, 0, i % nq, 0, 0))
    out_specs = [rows(D_MODEL), cols(MLA_HEADS * MLA_QK_PAD), rows(MLA_HEADS * MLA_QK_PAD), vt_spec,
                 rows(MLA_WIDTH)]
    return pl.pallas_call(
        functools.partial(_odd_proj_kernel, tm=tm),
        grid=(t // tm,),
        in_specs=[rows(D_MODEL), rows(NSA_WIDTH), rows(CONV_WIDTH), cols(1)] + [full(a) for a in ins[4:]],
        out_specs=out_specs,
        out_shape=out_shapes,
        compiler_params=pltpu.CompilerParams(dimension_semantics=("arbitrary",),
                                             vmem_limit_bytes=VMEM_LIMIT),
        name="odd_proj",
    )(*ins)


def _mla_kernel(qt_ref, k_ref, v_t_ref, o_ref, m_scr, acc_scr, *, tq, kc, hp):
    jd = pl.program_id(2)
    m_scr[...] = jnp.full(m_scr.shape, NEG_INF, F32)
    acc_scr[...] = jnp.zeros(acc_scr.shape, F32)
    causal = (lax.broadcasted_iota(jnp.int32, (kc, tq), 0) <= lax.broadcasted_iota(jnp.int32, (kc, tq), 1))

    def step(j, masked):
        start = pl.multiple_of(j * kc, kc)
        scores = []
        for h in range(hp):
            qk = slice(h * MLA_QK_PAD, (h + 1) * MLA_QK_PAD)
            scores.append(_dot(k_ref[0, pl.ds(start, kc), qk], qt_ref[qk, :]))
        for h, s_t in enumerate(scores):
            if masked:
                s_t = jnp.where(causal, s_t, NEG_INF)
            _flash_update_t(s_t, v_t_ref[0, h, j], m_scr, acc_scr, pl.ds(h * tq, tq))

    lax.fori_loop(0, jd, lambda j, c: (step(j, False), c)[1], 0)
    step(jd, True)
    for h in range(hp):
        c = pl.ds(h * tq, tq)
        o_t = acc_scr[0:MLA_V_DIM, c] / acc_scr[MLA_V_DIM:MLA_V_DIM + 1, c]
        o_ref[0, :, h * MLA_V_DIM:(h + 1) * MLA_V_DIM] = o_t.T.astype(BF16)


def _mla_attention(q_t, k, v_t):
    b, seq, _ = k.shape
    tq = min(MLA_TQ, seq)
    hp = MLA_HEADS_PER_STEP
    nq = seq // tq
    assert seq % tq == 0 and MLA_HEADS % hp == 0
    vrows = MLA_V_DIM + MLA_ONES_ROWS
    return pl.pallas_call(
        functools.partial(_mla_kernel, tq=tq, kc=tq, hp=hp),
        grid=(b, MLA_HEADS // hp, nq),
        in_specs=[pl.BlockSpec((hp * MLA_QK_PAD, tq), lambda bi, h, qi: (h, bi * nq + qi)),
                  pl.BlockSpec((1, seq, hp * MLA_QK_PAD), lambda bi, h, qi: (bi, 0, h)),
                  pl.BlockSpec((1, hp, nq, vrows, tq), lambda bi, h, qi: (bi, h, 0, 0, 0))],
        out_specs=pl.BlockSpec((1, tq, hp * MLA_V_DIM), lambda bi, h, qi: (bi, qi, h)),
        out_shape=jax.ShapeDtypeStruct((b, seq, MLA_WIDTH), BF16),
        scratch_shapes=[pltpu.VMEM((1, hp * tq), F32), pltpu.VMEM((vrows, hp * tq), F32)],
        compiler_params=pltpu.CompilerParams(dimension_semantics=("arbitrary", "arbitrary", "arbitrary"),
                                             vmem_limit_bytes=VMEM_LIMIT),
        name="mla_attn",
    )(q_t, k, v_t)


def _final_kernel(y_ref, o_ref, sg_ref, w_ref, g_ref, out_ref):
    gated = (sg_ref[...].astype(F32) * o_ref[...].astype(F32)).astype(BF16)
    out_ref[...] = _rms(y_ref[...] + _dot(gated, w_ref[...]), g_ref[...])


def _final(y, o, sg, w_out, final_norm, seq):
    t = y.shape[0]
    tm = min(PROJ_ROWS, seq)
    rows = lambda n: pl.BlockSpec((tm, n), lambda i: (i, 0))
    full = lambda a: pl.BlockSpec(a.shape, lambda i: (0,) * a.ndim)
    w = w_out.astype(BF16)
    g = final_norm.reshape(1, -1)
    return pl.pallas_call(
        _final_kernel,
        grid=(t // tm,),
        in_specs=[rows(D_MODEL), rows(MLA_WIDTH), rows(MLA_WIDTH), full(w), full(g)],
        out_specs=rows(D_MODEL),
        out_shape=jax.ShapeDtypeStruct((t, D_MODEL), F32),
        compiler_params=pltpu.CompilerParams(dimension_semantics=("arbitrary",),
                                             vmem_limit_bytes=VMEM_LIMIT),
        name="final",
    )(y, o, sg, w, g)


def _rope_constants():
    half = NSA_HEAD_DIM // 2
    inv_freq = ROPE_THETA ** (-jnp.arange(half, dtype=F32) / half)
    invf = jnp.tile(inv_freq, LANES // half).reshape(1, LANES)
    sgn = jnp.tile(jnp.concatenate([-jnp.ones((half,), F32), jnp.ones((half,), F32)]), LANES // (2 * half))
    return invf, sgn.reshape(1, LANES), inv_freq.reshape(half, 1)


def kernel(x, positions, a_norm, a_w_in, a_pe_k, a_pe_v, a_w_ck1, a_w_ck2, a_w_cv1, a_w_cv2, a_conv_w, a_w_out, c_norm, c_w_in, c_q_norm, c_kv_norm, c_w_uq, c_w_ukv, c_w_out, final_norm):
    b, seq, d = x.shape
    assert d == D_MODEL and NSA_HEAD_DIM == MLA_ROPE_DIM
    assert a_norm.shape[0] == 1 and c_norm.shape[0] == 1
    t = b * seq
    invf, sgn, invf_col = _rope_constants()
    x2d = x.reshape(t, d)
    pos_row = positions.reshape(1, t)

    q_t, kc, vc, ksel, vsl_t, kw, vw_t, gates, sga, mixb = _even_proj(
        x2d, pos_row, a_norm[0], a_w_in[0], a_conv_w[0], invf_col, seq)
    nchunk = seq // CMP_STRIDE
    pos_cmp = jnp.pad(positions[:, CMP_BLOCK - 1::CMP_STRIDE], ((0, 0), (0, 1)))[:, :nchunk, None]
    kcmp, vcmp_t = _compress(kc.reshape(b, seq, -1), vc.reshape(b, seq, -1), pos_cmp, a_pe_k[0], a_pe_v[0],
                             a_w_ck1[0], a_w_ck2[0], a_w_cv1[0], a_w_cv2[0], invf, sgn)
    r3 = lambda a: a.reshape(b, seq, a.shape[-1])
    mixa = _nsa_attention(q_t, kcmp, vcmp_t, r3(ksel), vsl_t, r3(kw), vw_t, r3(gates), r3(sga))

    y, mq_t, mk, mv_t, sg = _odd_proj(x2d, mixa.reshape(t, -1), mixb, pos_row, a_w_out[0], c_norm[0],
                                      c_w_in[0], c_q_norm[0], c_kv_norm[0], c_w_uq[0], c_w_ukv[0], invf_col, seq)
    o = _mla_attention(mq_t, r3(mk), mv_t)
    out = _final(y, o.reshape(t, -1), sg, c_w_out[0], final_norm, seq)
    return out.reshape(b, seq, d)
```

```python
import functools

import jax
import jax.numpy as jnp
import numpy as np
from jax import lax
from jax.experimental import pallas as pl
from jax.experimental.pallas import tpu as pltpu

F32 = jnp.float32
BF16 = jnp.bfloat16

D_MODEL = 1024
ROPE_THETA = 10000.0
RMS_EPS = 1e-6
NEG_INF = -1e30
FORCE_SCORE = 1e4
SEL_PENALTY = -1e9

NSA_HEADS = 8
NSA_KV_GROUPS = 2
NSA_HPG = NSA_HEADS // NSA_KV_GROUPS
NSA_HEAD_DIM = 64
NSA_WIDTH = NSA_HEADS * NSA_HEAD_DIM
NSA_KV_WIDTH = NSA_KV_GROUPS * NSA_HEAD_DIM
CMP_BLOCK = 32
CMP_STRIDE = 16
CMP_HIDDEN = 2 * NSA_HEAD_DIM
SLC_BLOCK = 64
SLC_TOPK = 16
WINDOW = 512
CONV_WIDTH = D_MODEL - NSA_WIDTH
CONV_K = 3

MLA_HEADS = 8
MLA_NOPE_DIM = 128
MLA_ROPE_DIM = 64
MLA_V_DIM = 128
MLA_Q_RANK = 256
MLA_KV_RANK = 256
MLA_WIDTH = MLA_HEADS * MLA_V_DIM
MLA_QK_PAD = 256

LANES = 128
VMEM_LIMIT = 56 * 1024 * 1024

PROJ_ROWS = 512
NSA_TQ = 256
MLA_TQ = 512
MLA_HEADS_PER_STEP = 4
MLA_ONES_ROWS = 16
NSA_CMP_ROWS = NSA_KV_WIDTH + 64 + 16
LOG2_E = 1.4426950408889634


def _dot(a, b):
    return jnp.dot(a, b, preferred_element_type=F32)


def _dot_nt(a, b):
    return lax.dot_general(a, b, (((1,), (1,)), ((), ())), preferred_element_type=F32)


def _silu(x):
    return x * jax.nn.sigmoid(x)


def _rms(x, g):
    return x * lax.rsqrt(jnp.mean(x * x, axis=-1, keepdims=True) + RMS_EPS) * g


def _rope_tables(pos_col, invf_row, sign_row):
    ang = pos_col.astype(F32) * invf_row
    return jnp.cos(ang), jnp.sin(ang) * sign_row


def _rope_slab(s, cos2, sin2, first_half):
    swapped = jnp.where(first_half, pltpu.roll(s, 96, 1), pltpu.roll(s, 32, 1))
    return s * cos2 + swapped * sin2


def _rope_tables_t(invf_col, pos_row):
    ang = invf_col * pos_row.astype(F32)
    return jnp.cos(ang), jnp.sin(ang)


def _flash_update_t(s_t, v_t, m_ref, acc_ref, cols):
    m_old = m_ref[:, cols]
    m_new = jnp.maximum(m_old, jnp.max(s_t, axis=0, keepdims=True))
    p = jnp.exp2(s_t - m_new)
    rows = v_t.shape[0]
    acc_ref[0:rows, cols] = jnp.exp2(m_old - m_new) * acc_ref[0:rows, cols] + _dot(v_t, p.astype(BF16))
    m_ref[:, cols] = m_new


def _even_proj_kernel(x_ref, posr_ref, g_ref, wqt_ref, wkv_ref, wg_ref, wga_ref, wcb_ref, wcc_ref, wch_ref,
                      wgb_ref, convw_ref, invfc_ref,
                      qt_ref, kc_ref, vc_ref, ksel_ref, vslt_ref, kw_ref, vwt_ref, gates_ref, sga_ref,
                      mixb_ref, carry_ref, *, tiles_per_seq, tm, kc):
    i = pl.program_id(0)
    xn = _rms(x_ref[...], g_ref[...]).astype(BF16)
    cos_t, sin_t = _rope_tables_t(invfc_ref[...], posr_ref[...])
    half = NSA_HEAD_DIM // 2
    lane = lax.broadcasted_iota(jnp.int32, (tm, LANES), 1)
    low = lane < 64

    def rope_t(x_t):
        out = []
        for r in range(0, x_t.shape[0], 2 * half):
            x1, x2 = x_t[r:r + half], x_t[r + half:r + 2 * half]
            out += [x1 * cos_t - x2 * sin_t, x2 * cos_t + x1 * sin_t]
        return jnp.concatenate(out, axis=0)

    def store_values_t(ref, v):
        v_t = v.T
        mine0 = lax.broadcasted_iota(jnp.int32, (LANES, kc), 0) < NSA_HEAD_DIM
        for c in range(tm // kc):
            blk = v_t[:, c * kc:(c + 1) * kc]
            ref[c, 0] = jnp.where(mine0, blk, 1.0).astype(BF16)
            ref[c, 1] = jnp.where(mine0, 1.0, blk).astype(BF16)

    scale = NSA_HEAD_DIM ** -0.5 * LOG2_E
    qt_ref[...] = (rope_t(_dot_nt(wqt_ref[...], xn)) * scale).astype(BF16)

    seg = _dot(xn, wkv_ref[...])
    kc_ref[...] = seg[:, 0:LANES]
    vc_ref[...] = seg[:, LANES:2 * LANES]
    ksl = rope_t(seg[:, 2 * LANES:3 * LANES].T).T
    store_values_t(vslt_ref, seg[:, 3 * LANES:4 * LANES])
    kw_ref[...] = rope_t(seg[:, 4 * LANES:5 * LANES].T).T.astype(BF16)
    store_values_t(vwt_ref, seg[:, 5 * LANES:6 * LANES])

    spos = (i % tiles_per_seq) * tm + lax.broadcasted_iota(jnp.int32, (tm, LANES), 0)
    blk = lax.shift_right_logical(spos, 6)
    ksel_ref[:, 0:LANES] = jnp.where(low, ksl, (lane - 64 == blk).astype(F32)).astype(BF16)
    ksel_ref[:, LANES:2 * LANES] = jnp.where(low, (lane == blk).astype(F32), ksl).astype(BF16)

    gates_ref[...] = jax.nn.sigmoid(_dot(xn, wg_ref[...]))
    sga_ref[...] = _silu(_dot(xn, wga_ref[...])).astype(BF16)

    u = _dot(xn, wcc_ref[...]) * _dot(xn, wch_ref[...])

    @pl.when(i % tiles_per_seq == 0)
    def _():
        carry_ref[...] = jnp.zeros_like(carry_ref)

    row = lax.broadcasted_iota(jnp.int32, (tm, CONV_WIDTH), 0)
    prev1 = carry_ref[7:8, :]
    prev2 = carry_ref[6:7, :]
    u1 = jnp.where(row == 0, prev1, pltpu.roll(u, 1, 0))
    u2 = jnp.where(row == 0, prev2, jnp.where(row == 1, prev1, pltpu.roll(u, 2, 0)))
    carry_ref[...] = u[tm - 8:tm, :]
    w = convw_ref[...]
    y = w[0:1, :] * u2 + w[1:2, :] * u1 + w[2:3, :] * u
    cb = _dot(xn, wcb_ref[...])
    mixb_ref[...] = (_silu(_dot(xn, wgb_ref[...])) * (cb * y)).astype(BF16)


def _even_proj(x2d, pos_row, norm_g, w_in, conv_w, invf_col, seq):
    t = x2d.shape[0]
    tm = min(PROJ_ROWS, seq)
    kc = min(NSA_TQ, seq)
    assert seq % tm == 0 and t % tm == 0 and tm % kc == 0
    sizes = (NSA_WIDTH,) + (NSA_KV_WIDTH,) * 6 + (3 * NSA_HEADS, NSA_WIDTH) + (CONV_WIDTH,) * 4
    offs = np.concatenate([[0], np.cumsum(sizes)])
    col = lambda k: w_in[:, offs[k]:offs[k + 1]]
    wqt = col(0).T.astype(BF16)
    wkv = jnp.concatenate([col(k) for k in range(1, 7)], axis=1).astype(BF16)
    wg = jnp.pad(col(7), ((0, 0), (0, LANES - 3 * NSA_HEADS))).astype(BF16)
    wga, wcb, wcc, wch, wgb = (col(k).astype(BF16) for k in range(8, 13))

    full = lambda a: pl.BlockSpec(a.shape, lambda i: (0,) * a.ndim)
    rows = lambda n: pl.BlockSpec((tm, n), lambda i: (i, 0))
    cols = lambda n: pl.BlockSpec((n, tm), lambda i: (0, i))
    values_t = jax.ShapeDtypeStruct((t // kc, NSA_KV_GROUPS, LANES, kc), BF16)
    values_t_spec = pl.BlockSpec((tm // kc, NSA_KV_GROUPS, LANES, kc), lambda i: (i, 0, 0, 0))
    out_shapes = [
        jax.ShapeDtypeStruct((NSA_WIDTH, t), BF16),
        jax.ShapeDtypeStruct((t, NSA_KV_WIDTH), F32),
        jax.ShapeDtypeStruct((t, NSA_KV_WIDTH), F32),
        jax.ShapeDtypeStruct((t, 2 * LANES), BF16),
        values_t,
        jax.ShapeDtypeStruct((t, NSA_KV_WIDTH), BF16),
        values_t,
        jax.ShapeDtypeStruct((t, LANES), F32),
        jax.ShapeDtypeStruct((t, NSA_WIDTH), BF16),
        jax.ShapeDtypeStruct((t, CONV_WIDTH), BF16),
    ]
    out_specs = [cols(NSA_WIDTH), rows(NSA_KV_WIDTH), rows(NSA_KV_WIDTH), rows(2 * LANES), values_t_spec,
                 rows(NSA_KV_WIDTH), values_t_spec, rows(LANES), rows(NSA_WIDTH), rows(CONV_WIDTH)]
    ins = [x2d, pos_row, norm_g.reshape(1, -1), wqt, wkv, wg, wga, wcb, wcc, wch, wgb, conv_w, invf_col]
    in_specs = [rows(D_MODEL), cols(1)] + [full(a) for a in ins[2:]]
    return pl.pallas_call(
        functools.partial(_even_proj_kernel, tiles_per_seq=seq // tm, tm=tm, kc=kc),
        grid=(t // tm,),
        in_specs=in_specs,
        out_specs=out_specs,
        out_shape=out_shapes,
        scratch_shapes=[pltpu.VMEM((8, CONV_WIDTH), F32)],
        compiler_params=pltpu.CompilerParams(dimension_semantics=("arbitrary",),
                                             vmem_limit_bytes=VMEM_LIMIT),
        name="even_proj",
    )(*ins)


def _compress_kernel(kc_ref, vc_ref, pos_ref, pek_ref, pev_ref, w1k_ref, w2k_ref, w1v_ref, w2vt_ref,
                     invf_ref, sgn_ref, ov_ref, kcmp_ref, vcmpt_ref, *, nchunk):
    half = NSA_KV_GROUPS * CMP_HIDDEN
    rows16 = CMP_BLOCK // 2

    def hidden(raw_ref, pe_ref, w1_ref):
        acc = jnp.zeros((nchunk, 2 * half), F32)
        for l in range(rows16):
            x = raw_ref[0, pl.ds(l, nchunk, stride=CMP_STRIDE), :]
            lhs = jnp.concatenate([(x + pe_ref[l:l + 1, :]).astype(BF16),
                                   (x + pe_ref[rows16 + l:rows16 + l + 1, :]).astype(BF16)], axis=1)
            acc = acc + _dot(lhs, w1_ref[l])
        return _silu(acc[:, 0:half] + pltpu.roll(acc[:, half:], nchunk - 1, 0)).astype(BF16)

    cos2, sin2 = _rope_tables(pos_ref[0], invf_ref[...], sgn_ref[...])
    lane = lax.broadcasted_iota(jnp.int32, (nchunk, LANES), 1)
    kcmp = _dot(hidden(kc_ref, pek_ref, w1k_ref), w2k_ref[...])
    kcmp_ref[0] = _rope_slab(kcmp, cos2, sin2, (lane & 32) == 0).astype(BF16)
    vcmpt_ref[0, 0:NSA_KV_WIDTH, :] = _dot_nt(w2vt_ref[...], hidden(vc_ref, pev_ref, w1v_ref)).astype(BF16)
    vcmpt_ref[0, NSA_KV_WIDTH:NSA_KV_WIDTH + 64, :] = ov_ref[...]
    vcmpt_ref[0, NSA_KV_WIDTH + 64:, :] = jnp.ones((NSA_CMP_ROWS - NSA_KV_WIDTH - 64, nchunk), BF16)


def _overlap_t(ncp, n_slc):
    cmp_starts = np.arange(ncp) * CMP_STRIDE
    slc_starts = np.arange(64) * SLC_BLOCK
    ov = ((cmp_starts[None, :] < slc_starts[:, None] + SLC_BLOCK)
          & (cmp_starts[None, :] + CMP_BLOCK > slc_starts[:, None])
          & (np.arange(64)[:, None] < n_slc))
    return jnp.asarray(ov.astype(np.float32), BF16)


def _compress_weights(pe, w1, w2):
    eye = jnp.eye(NSA_KV_GROUPS, dtype=F32)
    rows16 = CMP_BLOCK // 2
    w1r = w1.reshape(CMP_BLOCK, NSA_HEAD_DIM, CMP_HIDDEN)
    big = lambda part: jnp.einsum('ldn,gh->lgdhn', part, eye).reshape(
        rows16, NSA_KV_WIDTH, NSA_KV_GROUPS * CMP_HIDDEN)
    top, bot = big(w1r[:rows16]), big(w1r[rows16:])
    zero = jnp.zeros_like(top)
    w1big = jnp.concatenate([jnp.concatenate([top, zero], axis=2),
                             jnp.concatenate([zero, bot], axis=2)], axis=1).astype(BF16)
    w2big = jnp.einsum('nd,gh->gnhd', w2, eye).reshape(NSA_KV_GROUPS * CMP_HIDDEN, NSA_KV_WIDTH).astype(BF16)
    pe_rows = jnp.tile(pe, (1, NSA_KV_GROUPS))
    return pe_rows, w1big, w2big


def _compress(kc, vc, pos_cmp, pe_k, pe_v, w_ck1, w_ck2, w_cv1, w_cv2, invf, sgn):
    b, seq, _ = kc.shape
    nchunk = seq // CMP_STRIDE
    pek, w1k, w2k = _compress_weights(pe_k, w_ck1, w_ck2)
    pev, w1v, w2v = _compress_weights(pe_v, w_cv1, w_cv2)
    ins = [kc, vc, pos_cmp,
           pek, pev, w1k, w2k, w1v, w2v.T, invf, sgn, _overlap_t(nchunk, seq // SLC_BLOCK)]
    full = lambda a: pl.BlockSpec(a.shape, lambda i: (0,) * a.ndim)
    per_b = lambda n: pl.BlockSpec((1, nchunk, n), lambda i: (i, 0, 0))
    out_shape = [jax.ShapeDtypeStruct((b, nchunk, NSA_KV_WIDTH), BF16),
                 jax.ShapeDtypeStruct((b, NSA_CMP_ROWS, nchunk), BF16)]
    return pl.pallas_call(
        functools.partial(_compress_kernel, nchunk=nchunk),
        grid=(b,),
        in_specs=[pl.BlockSpec((1, seq, NSA_KV_WIDTH), lambda i: (i, 0, 0))] * 2 + [per_b(1)]
        + [full(a) for a in ins[3:]],
        out_specs=[per_b(NSA_KV_WIDTH), pl.BlockSpec((1, NSA_CMP_ROWS, nchunk), lambda i: (i, 0, 0))],
        out_shape=out_shape,
        compiler_params=pltpu.CompilerParams(dimension_semantics=("arbitrary",),
                                             vmem_limit_bytes=VMEM_LIMIT),
        name="compress",
    )(*ins)


def _nsa_kernel(qt_ref, kcmp_ref, vcmp_t_ref, ksel_ref, vsl_t_ref, kw_ref, vw_t_ref, gates_ref, sga_ref,
                out_ref, qz_scr, qa_scr, v_scr, rank_scr, m_scr, acc_scr, mix_scr,
                *, tq, kc, n_cmp, n_top):
    t0 = pl.program_id(1) * tq
    n = NSA_HPG * tq
    ncp = kcmp_ref.shape[1]
    hd = NSA_HEAD_DIM
    q_t = qt_ref[...]
    gates_t = gates_ref[0].T
    kq = lax.broadcasted_iota(jnp.int32, (kc, tq), 0) - lax.broadcasted_iota(jnp.int32, (kc, tq), 1)
    causal = kq <= 0
    newer = kq > 0

    def add_branch(branch, g, o_t):
        for h in range(NSA_HPG):
            head = NSA_HPG * g + h
            gate = gates_t[head * 3 + branch:head * 3 + branch + 1, :]
            term = gate * o_t[:, h * tq:(h + 1) * tq]
            r = pl.ds(head * hd, hd)
            if branch == 0:
                mix_scr[r, :] = term
            else:
                mix_scr[r, :] += term

    def reset_flash():
        m_scr[...] = jnp.full(m_scr.shape, NEG_INF, F32)
        acc_scr[...] = jnp.zeros(acc_scr.shape, F32)

    def flash_step(q_scr, k_ref, v_t_ref, chunks):
        chains = []
        for j, mask in chunks:
            start = pl.multiple_of(j * kc, kc)
            for g in range(NSA_KV_GROUPS):
                col0 = g * LANES if k_ref is ksel_ref else 0
                k = k_ref[0, pl.ds(start, kc), col0:col0 + LANES]
                for h in range(NSA_HPG):
                    s_t = _dot(k, q_scr[g, :, h * tq:(h + 1) * tq])
                    chains.append((j, mask, g, g * n + h * tq, s_t))
        for j, mask, g, c0, s_t in chains:
            if mask is not None:
                s_t = jnp.where(mask, s_t, NEG_INF)
            _flash_update_t(s_t, v_t_ref[0, j, g], m_scr, acc_scr, pl.ds(c0, tq))

    def finish_flash(branch):
        for g in range(NSA_KV_GROUPS):
            c = pl.ds(g * n, n)
            ones_row = hd * (1 - g)
            add_branch(branch, g, acc_scr[g * hd:(g + 1) * hd, c] / acc_scr[ones_row:ones_row + 1, c])

    zeros_q = jnp.zeros((hd, n), BF16)
    qgs = []
    for g in range(NSA_KV_GROUPS):
        qg = jnp.concatenate([q_t[(NSA_HPG * g + h) * hd:(NSA_HPG * g + h + 1) * hd, :] for h in range(NSA_HPG)],
                             axis=1)
        qgs.append(qg)
        qz_scr[g] = jnp.concatenate([qg, zeros_q] if g == 0 else [zeros_q, qg], axis=0)

    reset_flash()
    cq = (lax.broadcasted_iota(jnp.int32, (ncp, tq), 0) * CMP_STRIDE
          - lax.broadcasted_iota(jnp.int32, (ncp, tq), 1))
    cmp_valid = (cq <= t0 - (CMP_BLOCK - 1)) & (lax.broadcasted_iota(jnp.int32, (ncp, tq), 0) < n_cmp)
    chains = [(g * n + h * tq, _dot(kcmp_ref[0], qz_scr[g, :, h * tq:(h + 1) * tq]))
              for g in range(NSA_KV_GROUPS) for h in range(NSA_HPG)]
    for c0, s_t in chains:
        _flash_update_t(jnp.where(cmp_valid, s_t, NEG_INF), vcmp_t_ref[0], m_scr, acc_scr, pl.ds(c0, tq))
    seen = (t0 + lax.broadcasted_iota(jnp.int32, (1, tq), 1) >= CMP_BLOCK - 1).astype(F32)
    seen4 = jnp.concatenate([seen] * NSA_HPG, axis=1)
    imp_rows = slice(NSA_KV_WIDTH, NSA_KV_WIDTH + 64)
    den_row = slice(NSA_KV_WIDTH + 64, NSA_KV_WIDTH + 65)

    for g in range(NSA_KV_GROUPS):
        qg = qgs[g]
        c = pl.ds(g * n, n)
        inv_l = seen4 / acc_scr[den_row, c]
        add_branch(0, g, acc_scr[g * hd:(g + 1) * hd, c] * inv_l)
        pooled = acc_scr[imp_rows, c] * inv_l
        imp_t = functools.reduce(jnp.add, [pooled[:, h * tq:(h + 1) * tq] for h in range(NSA_HPG)])
        nb = lax.broadcasted_iota(jnp.int32, (64, tq), 0)
        tqv = t0 + lax.broadcasted_iota(jnp.int32, (64, tq), 1)
        cur = lax.shift_right_logical(tqv, 6)
        forced = (nb == 0) | (nb == cur) | (nb == cur - 1)
        v_scr[...] = jnp.where(nb * SLC_BLOCK <= tqv, jnp.where(forced, FORCE_SCORE, imp_t), -1.0)
        sub = lax.broadcasted_iota(jnp.int32, (8, tq), 0)
        vch = [v_scr[r * 8:(r + 1) * 8, :] for r in range(8)]
        rank_scr[...] = jnp.zeros(rank_scr.shape, jnp.int32)
        for mg in range(8):
            @pl.when(mg * 8 * SLC_BLOCK < t0 + tq)
            def _():
                count = [jnp.zeros((8, tq), jnp.int32) for _ in range(8)]
                for mblk in range(mg * 8, mg * 8 + 8):
                    vm = jnp.broadcast_to(v_scr[mblk:mblk + 1, :], (8, tq))
                    for r in range(8):
                        if r > mg:
                            before = vm >= vch[r]
                        elif r < mg:
                            before = vm > vch[r]
                        else:
                            before = (vm > vch[r]) | ((vm == vch[r]) & (sub > mblk - r * 8))
                        count[r] = count[r] + jnp.where(before, 1, 0)
                for r in range(8):
                    rank_scr[r * 8:(r + 1) * 8, :] += count[r]
        pen = jnp.where(rank_scr[...] < n_top, 0.0, SEL_PENALTY)
        pen4 = jnp.concatenate([pen.astype(BF16)] * NSA_HPG, axis=1)
        qa_scr[g] = jnp.concatenate([qg, pen4] if g == 0 else [pen4, qg], axis=0)

    jd = t0 // kc

    reset_flash()
    sel = functools.partial(flash_step, qa_scr, ksel_ref, vsl_t_ref)
    lax.fori_loop(0, jd // 2, lambda i, cr: (sel([(2 * i, None), (2 * i + 1, None)]), cr)[1], 0)

    @pl.when(jd % 2 == 0)
    def _():
        sel([(jd, causal)])

    @pl.when(jd % 2 == 1)
    def _():
        sel([(jd - 1, None), (jd, causal)])

    finish_flash(1)

    reset_flash()
    win = functools.partial(flash_step, qz_scr, kw_ref, vw_t_ref)
    assert WINDOW == 2 * kc

    @pl.when(jd == 0)
    def _():
        win([(jd, causal)])

    @pl.when(jd == 1)
    def _():
        win([(jd - 1, None), (jd, causal)])

    @pl.when(jd >= 2)
    def _():
        win([(jd - 2, newer), (jd - 1, None), (jd, causal)])

    finish_flash(2)

    out_ref[0] = (sga_ref[0].astype(F32) * mix_scr[...].T).astype(BF16)


def _nsa_attention(q_t, kcmp, vcmp_t, ksel, vsl_t, kw, vw_t, gates, sga):
    b, seq, _ = ksel.shape
    tq = min(NSA_TQ, seq)
    nq = seq // tq
    n_slc = seq // SLC_BLOCK
    assert n_slc <= 64 and seq % tq == 0 and WINDOW % tq == 0 and tq & (tq - 1) == 0
    ncp = kcmp.shape[1]
    n_cmp = (seq - CMP_BLOCK) // CMP_STRIDE + 1
    vsl_t = vsl_t.reshape((b, nq) + vsl_t.shape[1:])
    vw_t = vw_t.reshape((b, nq) + vw_t.shape[1:])
    n = NSA_HPG * tq
    tile = lambda w: pl.BlockSpec((1, tq, w), lambda bi, qi: (bi, qi, 0))
    per_b = lambda a: pl.BlockSpec((1,) + a.shape[1:], lambda bi, qi: (bi,) + (0,) * (a.ndim - 1))
    return pl.pallas_call(
        functools.partial(_nsa_kernel, tq=tq, kc=tq, n_cmp=n_cmp, n_top=min(SLC_TOPK, n_slc)),
        grid=(b, nq),
        in_specs=[pl.BlockSpec((NSA_WIDTH, tq), lambda bi, qi: (0, bi * nq + qi)),
                  per_b(kcmp), per_b(vcmp_t), per_b(ksel), per_b(vsl_t), per_b(kw), per_b(vw_t),
                  tile(LANES), tile(NSA_WIDTH)],
        out_specs=tile(NSA_WIDTH),
        out_shape=jax.ShapeDtypeStruct((b, seq, NSA_WIDTH), BF16),
        scratch_shapes=[pltpu.VMEM((NSA_KV_GROUPS, LANES, n), BF16),
                        pltpu.VMEM((NSA_KV_GROUPS, LANES, n), BF16),
                        pltpu.VMEM((64, tq), F32),
                        pltpu.VMEM((64, tq), jnp.int32),
                        pltpu.VMEM((1, NSA_KV_GROUPS * n), F32),
                        pltpu.VMEM((NSA_CMP_ROWS, NSA_KV_GROUPS * n), F32),
                        pltpu.VMEM((NSA_WIDTH, tq), F32)],
        compiler_params=pltpu.CompilerParams(dimension_semantics=("arbitrary", "arbitrary"),
                                             vmem_limit_bytes=VMEM_LIMIT),
        name="nsa_attn",
    )(q_t, kcmp, vcmp_t, ksel, vsl_t, kw, vw_t, gates, sga)


def _odd_proj_kernel(x_ref, mixa_ref, mixb_ref, posr_ref, wout_ref, g_ref, wc_ref, wgate_ref, qn_ref, kvn_ref,
                     wuqt_ref, wuk_ref, wuvt_ref, invfc_ref,
                     y_ref, qt_ref, k_ref, vt_ref, sg_ref, *, tm):
    y = (x_ref[...] + _dot(mixa_ref[...], wout_ref[0:NSA_WIDTH, :])
         + _dot(mixb_ref[...], wout_ref[NSA_WIDTH:, :]))
    y_ref[...] = y
    yn = _rms(y, g_ref[...]).astype(BF16)
    cos_t, sin_t = _rope_tables_t(invfc_ref[...], posr_ref[...])
    half = MLA_ROPE_DIM // 2

    def rope_t(x1, x2):
        return x1 * cos_t - x2 * sin_t, x2 * cos_t + x1 * sin_t

    seg = _dot(yn, wc_ref[...])
    kr_t = seg[:, MLA_Q_RANK + MLA_KV_RANK:].T
    kp1, kp2 = rope_t(kr_t[0:half], kr_t[half:2 * half])
    k_pe = jnp.concatenate([kp1, kp2, kr_t[2 * half:]], axis=0).T.astype(BF16)
    cq = _rms(seg[:, 0:MLA_Q_RANK], qn_ref[...]).astype(BF16)
    ckv = _rms(seg[:, MLA_Q_RANK:MLA_Q_RANK + MLA_KV_RANK], kvn_ref[...]).astype(BF16)
    scale = (MLA_NOPE_DIM + MLA_ROPE_DIM) ** -0.5 * LOG2_E
    qt = _dot_nt(wuqt_ref[...], cq)
    kf = _dot(ckv, wuk_ref[...])
    for h in range(MLA_HEADS):
        c0 = h * MLA_QK_PAD
        r0 = c0 + MLA_NOPE_DIM
        qt_ref[c0:r0, :] = (qt[c0:r0] * scale).astype(BF16)
        q1, q2 = rope_t(qt[r0:r0 + half], qt[r0 + half:r0 + 2 * half])
        qt_ref[r0:r0 + half, :] = (q1 * scale).astype(BF16)
        qt_ref[r0 + half:r0 + 2 * half, :] = (q2 * scale).astype(BF16)
        qt_ref[r0 + 2 * half:c0 + MLA_QK_PAD, :] = jnp.zeros((MLA_QK_PAD - MLA_NOPE_DIM - 2 * half, tm), BF16)
        k_ref[:, c0:c0 + LANES] = kf[:, h * LANES:(h + 1) * LANES].astype(BF16)
        k_ref[:, c0 + LANES:c0 + 2 * LANES] = k_pe
    v_t = _dot_nt(wuvt_ref[...], ckv)
    for h in range(MLA_HEADS):
        vt_ref[0, h, 0, 0:MLA_V_DIM, :] = v_t[h * MLA_V_DIM:(h + 1) * MLA_V_DIM].astype(BF16)
        vt_ref[0, h, 0, MLA_V_DIM:, :] = jnp.ones((MLA_ONES_ROWS, tm), BF16)
    sg_ref[...] = _silu(_dot(yn, wgate_ref[...])).astype(BF16)


def _odd_proj(x2d, mixa, mixb, pos_row, a_w_out, c_norm, c_w_in, q_norm, kv_norm, w_uq, w_ukv, invf_col, seq):
    t = x2d.shape[0]
    tm = min(PROJ_ROWS, seq)
    r2 = MLA_Q_RANK + MLA_KV_RANK + MLA_ROPE_DIM
    wc = jnp.pad(c_w_in[:, :r2], ((0, 0), (0, LANES - MLA_ROPE_DIM))).astype(BF16)
    wgate = c_w_in[:, r2:].astype(BF16)
    wuqt = jnp.pad(w_uq.reshape(MLA_Q_RANK, MLA_HEADS, MLA_NOPE_DIM + MLA_ROPE_DIM),
                   ((0, 0), (0, 0), (0, MLA_QK_PAD - MLA_NOPE_DIM - MLA_ROPE_DIM))
                   ).reshape(MLA_Q_RANK, MLA_HEADS * MLA_QK_PAD).T.astype(BF16)
    wukv = w_ukv.reshape(MLA_KV_RANK, MLA_HEADS, MLA_NOPE_DIM + MLA_V_DIM)
    wuk = wukv[:, :, :MLA_NOPE_DIM].reshape(MLA_KV_RANK, -1).astype(BF16)
    wuvt = wukv[:, :, MLA_NOPE_DIM:].reshape(MLA_KV_RANK, -1).T.astype(BF16)
    nq = seq // tm
    assert tm == min(MLA_TQ, seq)
    vrows = MLA_V_DIM + MLA_ONES_ROWS
    ins = [x2d, mixa, mixb, pos_row, a_w_out.astype(BF16), c_norm.reshape(1, -1), wc, wgate,
           q_norm.reshape(1, -1), kv_norm.reshape(1, -1), wuqt, wuk, wuvt, invf_col]
    full = lambda a: pl.BlockSpec(a.shape, lambda i: (0,) * a.ndim)
    rows = lambda n: pl.BlockSpec((tm, n), lambda i: (i, 0))
    cols = lambda n: pl.BlockSpec((n, tm), lambda i: (0, i))
    out_shapes = [
        jax.ShapeDtypeStruct((t, D_MODEL), F32),
        jax.ShapeDtypeStruct((MLA_HEADS * MLA_QK_PAD, t), BF16),
        jax.ShapeDtypeStruct((t, MLA_HEADS * MLA_QK_PAD), BF16),
        jax.ShapeDtypeStruct((t // seq, MLA_HEADS, nq, vrows, tm), BF16),
        jax.ShapeDtypeStruct((t, MLA_WIDTH), BF16),
    ]
    vt_spec = pl.BlockSpec((1, MLA_HEADS, 1, vrows, tm), lambda i: (i // nq, 0, i % nq, 0, 0))
    out_specs = [rows(D_MODEL), cols(MLA_HEADS * MLA_QK_PAD), rows(MLA_HEADS * MLA_QK_PAD), vt_spec,
                 rows(MLA_WIDTH)]
    return pl.pallas_call(
        functools.partial(_odd_proj_kernel, tm=tm),
        grid=(t // tm,),
        in_specs=[rows(D_MODEL), rows(NSA_WIDTH), rows(CONV_WIDTH), cols(1)] + [full(a) for a in ins[4:]],
        out_specs=out_specs,
        out_shape=out_shapes,
        compiler_params=pltpu.CompilerParams(dimension_semantics=("arbitrary",),
                                             vmem_limit_bytes=VMEM_LIMIT),
        name="odd_proj",
    )(*ins)


def _mla_kernel(qt_ref, k_ref, v_t_ref, o_ref, m_scr, acc_scr, *, tq, kc, hp):
    jd = pl.program_id(2)
    m_scr[...] = jnp.full(m_scr.shape, NEG_INF, F32)
    acc_scr[...] = jnp.zeros(acc_scr.shape, F32)
    causal = (lax.broadcasted_iota(jnp.int32, (kc, tq), 0) <= lax.broadcasted_iota(jnp.int32, (kc, tq), 1))

    def step(j):
        start = pl.multiple_of(j * kc, kc)
        scores = []
        for h in range(hp):
            qk = slice(h * MLA_QK_PAD, (h + 1) * MLA_QK_PAD)
            scores.append(_dot(k_ref[0, pl.ds(start, kc), qk], qt_ref[qk, :]))
        for h, s_t in enumerate(scores):
            _flash_update_t(s_t, v_t_ref[0, h, j], m_scr, acc_scr, pl.ds(h * tq, tq))

    lax.fori_loop(0, jd, lambda j, c: (step(j), c)[1], 0)

    hq = tq // 2
    start = pl.multiple_of(jd * kc, kc)
    tri = causal[0:hq, 0:hq]
    keep_hi = jnp.concatenate([jnp.ones((hq, hq), jnp.bool_), tri], axis=0)
    scores = []
    for h in range(hp):
        qk = slice(h * MLA_QK_PAD, (h + 1) * MLA_QK_PAD)
        scores.append((_dot(k_ref[0, pl.ds(start, hq), qk], qt_ref[qk, 0:hq]),
                       _dot(k_ref[0, pl.ds(start, kc), qk], qt_ref[qk, hq:tq])))
    for h, (s_lo, s_hi) in enumerate(scores):
        v_t = v_t_ref[0, h, jd]
        _flash_update_t(jnp.where(tri, s_lo, NEG_INF), v_t[:, 0:hq], m_scr, acc_scr, pl.ds(h * tq, hq))
        _flash_update_t(jnp.where(keep_hi, s_hi, NEG_INF), v_t, m_scr, acc_scr, pl.ds(h * tq + hq, hq))

    for h in range(hp):
        c = pl.ds(h * tq, tq)
        o_t = acc_scr[0:MLA_V_DIM, c] / acc_scr[MLA_V_DIM:MLA_V_DIM + 1, c]
        o_ref[0, :, h * MLA_V_DIM:(h + 1) * MLA_V_DIM] = o_t.T.astype(BF16)


def _mla_attention(q_t, k, v_t):
    b, seq, _ = k.shape
    tq = min(MLA_TQ, seq)
    hp = MLA_HEADS_PER_STEP
    nq = seq // tq
    assert seq % tq == 0 and MLA_HEADS % hp == 0
    vrows = MLA_V_DIM + MLA_ONES_ROWS
    return pl.pallas_call(
        functools.partial(_mla_kernel, tq=tq, kc=tq, hp=hp),
        grid=(b, MLA_HEADS // hp, nq),
        in_specs=[pl.BlockSpec((hp * MLA_QK_PAD, tq), lambda bi, h, qi: (h, bi * nq + qi)),
                  pl.BlockSpec((1, seq, hp * MLA_QK_PAD), lambda bi, h, qi: (bi, 0, h)),
                  pl.BlockSpec((1, hp, nq, vrows, tq), lambda bi, h, qi: (bi, h, 0, 0, 0))],
        out_specs=pl.BlockSpec((1, tq, hp * MLA_V_DIM), lambda bi, h, qi: (bi, qi, h)),
        out_shape=jax.ShapeDtypeStruct((b, seq, MLA_WIDTH), BF16),
        scratch_shapes=[pltpu.VMEM((1, hp * tq), F32), pltpu.VMEM((vrows, hp * tq), F32)],
        compiler_params=pltpu.CompilerParams(dimension_semantics=("arbitrary", "arbitrary", "arbitrary"),
                                             vmem_limit_bytes=VMEM_LIMIT),
        name="mla_attn",
    )(q_t, k, v_t)


def _final_kernel(y_ref, o_ref, sg_ref, w_ref, g_ref, out_ref):
    gated = (sg_ref[...].astype(F32) * o_ref[...].astype(F32)).astype(BF16)
    out_ref[...] = _rms(y_ref[...] + _dot(gated, w_ref[...]), g_ref[...])


def _final(y, o, sg, w_out, final_norm, seq):
    t = y.shape[0]
    tm = min(PROJ_ROWS, seq)
    rows = lambda n: pl.BlockSpec((tm, n), lambda i: (i, 0))
    full = lambda a: pl.BlockSpec(a.shape, lambda i: (0,) * a.ndim)
    w = w_out.astype(BF16)
    g = final_norm.reshape(1, -1)
    return pl.pallas_call(
        _final_kernel,
        grid=(t // tm,),
        in_specs=[rows(D_MODEL), rows(MLA_WIDTH), rows(MLA_WIDTH), full(w), full(g)],
        out_specs=rows(D_MODEL),
        out_shape=jax.ShapeDtypeStruct((t, D_MODEL), F32),
        compiler_params=pltpu.CompilerParams(dimension_semantics=("arbitrary",),
                                             vmem_limit_bytes=VMEM_LIMIT),
        name="final",
    )(y, o, sg, w, g)


def _rope_constants():
    half = NSA_HEAD_DIM // 2
    inv_freq = ROPE_THETA ** (-jnp.arange(half, dtype=F32) / half)
    invf = jnp.tile(inv_freq, LANES // half).reshape(1, LANES)
    sgn = jnp.tile(jnp.concatenate([-jnp.ones((half,), F32), jnp.ones((half,), F32)]), LANES // (2 * half))
    return invf, sgn.reshape(1, LANES), inv_freq.reshape(half, 1)


def kernel(x, positions, a_norm, a_w_in, a_pe_k, a_pe_v, a_w_ck1, a_w_ck2, a_w_cv1, a_w_cv2, a_conv_w, a_w_out, c_norm, c_w_in, c_q_norm, c_kv_norm, c_w_uq, c_w_ukv, c_w_out, final_norm):
    b, seq, d = x.shape
    assert d == D_MODEL and NSA_HEAD_DIM == MLA_ROPE_DIM
    assert a_norm.shape[0] == 1 and c_norm.shape[0] == 1
    t = b * seq
    invf, sgn, invf_col = _rope_constants()
    x2d = x.reshape(t, d)
    pos_row = positions.reshape(1, t)

    q_t, kc, vc, ksel, vsl_t, kw, vw_t, gates, sga, mixb = _even_proj(
        x2d, pos_row, a_norm[0], a_w_in[0], a_conv_w[0], invf_col, seq)
    nchunk = seq // CMP_STRIDE
    pos_cmp = jnp.pad(positions[:, CMP_BLOCK - 1::CMP_STRIDE], ((0, 0), (0, 1)))[:, :nchunk, None]
    kcmp, vcmp_t = _compress(kc.reshape(b, seq, -1), vc.reshape(b, seq, -1), pos_cmp, a_pe_k[0], a_pe_v[0],
                             a_w_ck1[0], a_w_ck2[0], a_w_cv1[0], a_w_cv2[0], invf, sgn)
    r3 = lambda a: a.reshape(b, seq, a.shape[-1])
    mixa = _nsa_attention(q_t, kcmp, vcmp_t, r3(ksel), vsl_t, r3(kw), vw_t, r3(gates), r3(sga))

    y, mq_t, mk, mv_t, sg = _odd_proj(x2d, mixa.reshape(t, -1), mixb, pos_row, a_w_out[0], c_norm[0],
                                      c_w_in[0], c_q_norm[0], c_kv_norm[0], c_w_uq[0], c_w_ukv[0], invf_col, seq)
    o = _mla_attention(mq_t, r3(mk), mv_t)
    out = _final(y, o.reshape(t, -1), sg, c_w_out[0], final_norm, seq)
    return out.reshape(b, seq, d)
```

```python
import functools

import jax
import jax.numpy as jnp
import numpy as np
from jax import lax
from jax.experimental import pallas as pl
from jax.experimental.pallas import tpu as pltpu

F32 = jnp.float32
BF16 = jnp.bfloat16

D_MODEL = 1024
ROPE_THETA = 10000.0
RMS_EPS = 1e-6
NEG_INF = -1e30
FORCE_SCORE = 1e4
SEL_PENALTY = NEG_INF

NSA_HEADS = 8
NSA_KV_GROUPS = 2
NSA_HPG = NSA_HEADS // NSA_KV_GROUPS
NSA_HEAD_DIM = 64
NSA_WIDTH = NSA_HEADS * NSA_HEAD_DIM
NSA_KV_WIDTH = NSA_KV_GROUPS * NSA_HEAD_DIM
CMP_BLOCK = 32
CMP_STRIDE = 16
CMP_HIDDEN = 2 * NSA_HEAD_DIM
SLC_BLOCK = 64
SLC_TOPK = 16
WINDOW = 512
CONV_WIDTH = D_MODEL - NSA_WIDTH
CONV_K = 3

MLA_HEADS = 8
MLA_NOPE_DIM = 128
MLA_ROPE_DIM = 64
MLA_V_DIM = 128
MLA_Q_RANK = 256
MLA_KV_RANK = 256
MLA_WIDTH = MLA_HEADS * MLA_V_DIM
MLA_QK_PAD = 256

LANES = 128
VMEM_LIMIT = 56 * 1024 * 1024

PROJ_ROWS = 512
NSA_TQ = 256
MLA_TQ = 512
MLA_HEADS_PER_STEP = 4
MLA_ONES_ROWS = 16
NSA_ONES_ROWS = 16
NSA_CMP_ROWS = NSA_HEAD_DIM + 64 + NSA_ONES_ROWS
LOG2_E = 1.4426950408889634


def _dot(a, b):
    return jnp.dot(a, b, preferred_element_type=F32)


def _dot_nt(a, b):
    return lax.dot_general(a, b, (((1,), (1,)), ((), ())), preferred_element_type=F32)


def _silu(x):
    return x * jax.nn.sigmoid(x)


def _rms(x, g):
    return x * lax.rsqrt(jnp.mean(x * x, axis=-1, keepdims=True) + RMS_EPS) * g


def _rope_tables(pos_col, invf_row, sign_row):
    ang = pos_col.astype(F32) * invf_row
    return jnp.cos(ang), jnp.sin(ang) * sign_row


def _rope_slab(s, cos2, sin2, first_half):
    swapped = jnp.where(first_half, pltpu.roll(s, 96, 1), pltpu.roll(s, 32, 1))
    return s * cos2 + swapped * sin2


def _rope_tables_t(invf_col, pos_row):
    ang = invf_col * pos_row.astype(F32)
    return jnp.cos(ang), jnp.sin(ang)


def _flash_update_t(s_t, v_t, m_ref, acc_ref, cols):
    m_old = m_ref[:, cols]
    m_new = jnp.maximum(m_old, jnp.max(s_t, axis=0, keepdims=True))
    p = jnp.exp2(s_t - m_new)
    rows = v_t.shape[0]
    acc_ref[0:rows, cols] = jnp.exp2(m_old - m_new) * acc_ref[0:rows, cols] + _dot(v_t, p.astype(BF16))
    m_ref[:, cols] = m_new


def _even_proj_kernel(x_ref, posr_ref, g_ref, wqt_ref, wkv_ref, wg_ref, wga_ref, wcb_ref, wcc_ref, wch_ref,
                      wgb_ref, convw_ref, invfc_ref,
                      qt_ref, kc_ref, vc_ref, ksel_ref, vslt_ref, kw_ref, vwt_ref, gates_ref, sga_ref,
                      mixb_ref, carry_ref, *, tiles_per_seq, tm, kc):
    i = pl.program_id(0)
    xn = _rms(x_ref[...], g_ref[...]).astype(BF16)
    cos_t, sin_t = _rope_tables_t(invfc_ref[...], posr_ref[...])
    half = NSA_HEAD_DIM // 2
    lane = lax.broadcasted_iota(jnp.int32, (tm, LANES), 1)
    low = lane < 64

    def rope_t(x_t):
        out = []
        for r in range(0, x_t.shape[0], 2 * half):
            x1, x2 = x_t[r:r + half], x_t[r + half:r + 2 * half]
            out += [x1 * cos_t - x2 * sin_t, x2 * cos_t + x1 * sin_t]
        return jnp.concatenate(out, axis=0)

    def store_values_t(ref, v):
        v_t = v.T.astype(BF16)
        ones = jnp.ones((NSA_ONES_ROWS, kc), BF16)
        for c in range(tm // kc):
            for g in range(NSA_KV_GROUPS):
                ref[c, g, 0:NSA_HEAD_DIM, :] = v_t[g * NSA_HEAD_DIM:(g + 1) * NSA_HEAD_DIM, c * kc:(c + 1) * kc]
                ref[c, g, NSA_HEAD_DIM:, :] = ones

    scale = NSA_HEAD_DIM ** -0.5 * LOG2_E
    qt_ref[...] = (rope_t(_dot_nt(wqt_ref[...], xn)) * scale).astype(BF16)

    seg = _dot(xn, wkv_ref[...])
    kc_ref[...] = seg[:, 0:LANES]
    vc_ref[...] = seg[:, LANES:2 * LANES]
    ksl = rope_t(seg[:, 2 * LANES:3 * LANES].T).T
    store_values_t(vslt_ref, seg[:, 3 * LANES:4 * LANES])
    kw_ref[...] = rope_t(seg[:, 4 * LANES:5 * LANES].T).T.astype(BF16)
    store_values_t(vwt_ref, seg[:, 5 * LANES:6 * LANES])

    spos = (i % tiles_per_seq) * tm + lax.broadcasted_iota(jnp.int32, (tm, LANES), 0)
    blk = lax.shift_right_logical(spos, 6)
    ksel_ref[:, 0:LANES] = jnp.where(low, ksl, (lane - 64 == blk).astype(F32)).astype(BF16)
    ksel_ref[:, LANES:2 * LANES] = jnp.where(low, (lane == blk).astype(F32), ksl).astype(BF16)

    gates_ref[...] = jax.nn.sigmoid(_dot(xn, wg_ref[...]))
    sga_ref[...] = _silu(_dot(xn, wga_ref[...])).astype(BF16)

    u = _dot(xn, wcc_ref[...]) * _dot(xn, wch_ref[...])

    @pl.when(i % tiles_per_seq == 0)
    def _():
        carry_ref[...] = jnp.zeros_like(carry_ref)

    row = lax.broadcasted_iota(jnp.int32, (tm, CONV_WIDTH), 0)
    prev1 = carry_ref[7:8, :]
    prev2 = carry_ref[6:7, :]
    u1 = jnp.where(row == 0, prev1, pltpu.roll(u, 1, 0))
    u2 = jnp.where(row == 0, prev2, jnp.where(row == 1, prev1, pltpu.roll(u, 2, 0)))
    carry_ref[...] = u[tm - 8:tm, :]
    w = convw_ref[...]
    y = w[0:1, :] * u2 + w[1:2, :] * u1 + w[2:3, :] * u
    cb = _dot(xn, wcb_ref[...])
    mixb_ref[...] = (_silu(_dot(xn, wgb_ref[...])) * (cb * y)).astype(BF16)


def _even_proj(x2d, pos_row, norm_g, w_in, conv_w, invf_col, seq):
    t = x2d.shape[0]
    tm = min(PROJ_ROWS, seq)
    kc = min(NSA_TQ, seq)
    assert seq % tm == 0 and t % tm == 0 and tm % kc == 0
    sizes = (NSA_WIDTH,) + (NSA_KV_WIDTH,) * 6 + (3 * NSA_HEADS, NSA_WIDTH) + (CONV_WIDTH,) * 4
    offs = np.concatenate([[0], np.cumsum(sizes)])
    col = lambda k: w_in[:, offs[k]:offs[k + 1]]
    wqt = col(0).T.astype(BF16)
    wkv = jnp.concatenate([col(k) for k in range(1, 7)], axis=1).astype(BF16)
    wg = jnp.pad(col(7), ((0, 0), (0, LANES - 3 * NSA_HEADS))).astype(BF16)
    wga, wcb, wcc, wch, wgb = (col(k).astype(BF16) for k in range(8, 13))

    full = lambda a: pl.BlockSpec(a.shape, lambda i: (0,) * a.ndim)
    rows = lambda n: pl.BlockSpec((tm, n), lambda i: (i, 0))
    cols = lambda n: pl.BlockSpec((n, tm), lambda i: (0, i))
    vrows = NSA_HEAD_DIM + NSA_ONES_ROWS
    values_t = jax.ShapeDtypeStruct((t // kc, NSA_KV_GROUPS, vrows, kc), BF16)
    values_t_spec = pl.BlockSpec((tm // kc, NSA_KV_GROUPS, vrows, kc), lambda i: (i, 0, 0, 0))
    out_shapes = [
        jax.ShapeDtypeStruct((NSA_WIDTH, t), BF16),
        jax.ShapeDtypeStruct((t, NSA_KV_WIDTH), F32),
        jax.ShapeDtypeStruct((t, NSA_KV_WIDTH), F32),
        jax.ShapeDtypeStruct((t, 2 * LANES), BF16),
        values_t,
        jax.ShapeDtypeStruct((t, NSA_KV_WIDTH), BF16),
        values_t,
        jax.ShapeDtypeStruct((t, LANES), F32),
        jax.ShapeDtypeStruct((t, NSA_WIDTH), BF16),
        jax.ShapeDtypeStruct((t, CONV_WIDTH), BF16),
    ]
    out_specs = [cols(NSA_WIDTH), rows(NSA_KV_WIDTH), rows(NSA_KV_WIDTH), rows(2 * LANES), values_t_spec,
                 rows(NSA_KV_WIDTH), values_t_spec, rows(LANES), rows(NSA_WIDTH), rows(CONV_WIDTH)]
    ins = [x2d, pos_row, norm_g.reshape(1, -1), wqt, wkv, wg, wga, wcb, wcc, wch, wgb, conv_w, invf_col]
    in_specs = [rows(D_MODEL), cols(1)] + [full(a) for a in ins[2:]]
    return pl.pallas_call(
        functools.partial(_even_proj_kernel, tiles_per_seq=seq // tm, tm=tm, kc=kc),
        grid=(t // tm,),
        in_specs=in_specs,
        out_specs=out_specs,
        out_shape=out_shapes,
        scratch_shapes=[pltpu.VMEM((8, CONV_WIDTH), F32)],
        compiler_params=pltpu.CompilerParams(dimension_semantics=("arbitrary",),
                                             vmem_limit_bytes=VMEM_LIMIT),
        name="even_proj",
    )(*ins)


def _compress_kernel(kc_ref, vc_ref, pos_ref, pek_ref, pev_ref, w1k_ref, w2k_ref, w1v_ref, w2vt_ref,
                     invf_ref, sgn_ref, ov_ref, kcmp_ref, vcmpt_ref, *, nchunk):
    half = NSA_KV_GROUPS * CMP_HIDDEN
    rows16 = CMP_BLOCK // 2

    def hidden(raw_ref, pe_ref, w1_ref):
        acc = jnp.zeros((nchunk, 2 * half), F32)
        for l in range(rows16):
            x = raw_ref[0, pl.ds(l, nchunk, stride=CMP_STRIDE), :]
            lhs = jnp.concatenate([(x + pe_ref[l:l + 1, :]).astype(BF16),
                                   (x + pe_ref[rows16 + l:rows16 + l + 1, :]).astype(BF16)], axis=1)
            acc = acc + _dot(lhs, w1_ref[l])
        return _silu(acc[:, 0:half] + pltpu.roll(acc[:, half:], nchunk - 1, 0)).astype(BF16)

    cos2, sin2 = _rope_tables(pos_ref[0], invf_ref[...], sgn_ref[...])
    lane = lax.broadcasted_iota(jnp.int32, (nchunk, LANES), 1)
    kcmp = _dot(hidden(kc_ref, pek_ref, w1k_ref), w2k_ref[...])
    kcmp_ref[0] = _rope_slab(kcmp, cos2, sin2, (lane & 32) == 0).astype(BF16)
    vcmp_t = _dot_nt(w2vt_ref[...], hidden(vc_ref, pev_ref, w1v_ref)).astype(BF16)
    for g in range(NSA_KV_GROUPS):
        vcmpt_ref[0, g, 0:NSA_HEAD_DIM, :] = vcmp_t[g * NSA_HEAD_DIM:(g + 1) * NSA_HEAD_DIM]
        vcmpt_ref[0, g, NSA_HEAD_DIM:NSA_HEAD_DIM + 64, :] = ov_ref[...]
        vcmpt_ref[0, g, NSA_HEAD_DIM + 64:, :] = jnp.ones((NSA_ONES_ROWS, nchunk), BF16)


def _overlap_t(ncp, n_slc):
    cmp_starts = np.arange(ncp) * CMP_STRIDE
    slc_starts = np.arange(64) * SLC_BLOCK
    ov = ((cmp_starts[None, :] < slc_starts[:, None] + SLC_BLOCK)
          & (cmp_starts[None, :] + CMP_BLOCK > slc_starts[:, None])
          & (np.arange(64)[:, None] < n_slc))
    return jnp.asarray(ov.astype(np.float32), BF16)


def _compress_weights(pe, w1, w2):
    eye = jnp.eye(NSA_KV_GROUPS, dtype=F32)
    rows16 = CMP_BLOCK // 2
    w1r = w1.reshape(CMP_BLOCK, NSA_HEAD_DIM, CMP_HIDDEN)
    big = lambda part: jnp.einsum('ldn,gh->lgdhn', part, eye).reshape(
        rows16, NSA_KV_WIDTH, NSA_KV_GROUPS * CMP_HIDDEN)
    top, bot = big(w1r[:rows16]), big(w1r[rows16:])
    zero = jnp.zeros_like(top)
    w1big = jnp.concatenate([jnp.concatenate([top, zero], axis=2),
                             jnp.concatenate([zero, bot], axis=2)], axis=1).astype(BF16)
    w2big = jnp.einsum('nd,gh->gnhd', w2, eye).reshape(NSA_KV_GROUPS * CMP_HIDDEN, NSA_KV_WIDTH).astype(BF16)
    pe_rows = jnp.tile(pe, (1, NSA_KV_GROUPS))
    return pe_rows, w1big, w2big


def _compress(kc, vc, pos_cmp, pe_k, pe_v, w_ck1, w_ck2, w_cv1, w_cv2, invf, sgn):
    b, seq, _ = kc.shape
    nchunk = seq // CMP_STRIDE
    pek, w1k, w2k = _compress_weights(pe_k, w_ck1, w_ck2)
    pev, w1v, w2v = _compress_weights(pe_v, w_cv1, w_cv2)
    ins = [kc, vc, pos_cmp,
           pek, pev, w1k, w2k, w1v, w2v.T, invf, sgn, _overlap_t(nchunk, seq // SLC_BLOCK)]
    full = lambda a: pl.BlockSpec(a.shape, lambda i: (0,) * a.ndim)
    per_b = lambda n: pl.BlockSpec((1, nchunk, n), lambda i: (i, 0, 0))
    out_shape = [jax.ShapeDtypeStruct((b, nchunk, NSA_KV_WIDTH), BF16),
                 jax.ShapeDtypeStruct((b, NSA_KV_GROUPS, NSA_CMP_ROWS, nchunk), BF16)]
    return pl.pallas_call(
        functools.partial(_compress_kernel, nchunk=nchunk),
        grid=(b,),
        in_specs=[pl.BlockSpec((1, seq, NSA_KV_WIDTH), lambda i: (i, 0, 0))] * 2 + [per_b(1)]
        + [full(a) for a in ins[3:]],
        out_specs=[per_b(NSA_KV_WIDTH),
                   pl.BlockSpec((1, NSA_KV_GROUPS, NSA_CMP_ROWS, nchunk), lambda i: (i, 0, 0, 0))],
        out_shape=out_shape,
        compiler_params=pltpu.CompilerParams(dimension_semantics=("arbitrary",),
                                             vmem_limit_bytes=VMEM_LIMIT),
        name="compress",
    )(*ins)


def _nsa_kernel(qt_ref, kcmp_ref, vcmp_t_ref, ksel_ref, vsl_t_ref, kw_ref, vw_t_ref, gates_ref, sga_ref,
                out_ref, qz_scr, qa_scr, v_scr, rank_scr, m_scr, acc_scr, mix_scr,
                *, tq, kc, n_cmp, n_top):
    t0 = pl.program_id(1) * tq
    n = NSA_HPG * tq
    ncp = kcmp_ref.shape[1]
    hd = NSA_HEAD_DIM
    q_t = qt_ref[...]
    gates_t = gates_ref[0].T
    kq = lax.broadcasted_iota(jnp.int32, (kc, tq), 0) - lax.broadcasted_iota(jnp.int32, (kc, tq), 1)
    causal = kq <= 0
    newer = kq > 0

    def add_branch(branch, g, o_t):
        for h in range(NSA_HPG):
            head = NSA_HPG * g + h
            gate = gates_t[head * 3 + branch:head * 3 + branch + 1, :]
            term = gate * o_t[:, h * tq:(h + 1) * tq]
            r = pl.ds(head * hd, hd)
            if branch == 0:
                mix_scr[r, :] = term
            else:
                mix_scr[r, :] += term

    def reset_flash():
        m_scr[...] = jnp.full(m_scr.shape, NEG_INF, F32)
        acc_scr[...] = jnp.zeros(acc_scr.shape, F32)

    def flash_step(q_scr, k_ref, v_t_ref, chunks):
        chains = []
        for j, mask in chunks:
            start = pl.multiple_of(j * kc, kc)
            for g in range(NSA_KV_GROUPS):
                col0 = g * LANES if k_ref is ksel_ref else 0
                k = k_ref[0, pl.ds(start, kc), col0:col0 + LANES]
                for h in range(NSA_HPG):
                    s_t = _dot(k, q_scr[g, :, h * tq:(h + 1) * tq])
                    chains.append((j, mask, g, g * n + h * tq, s_t))
        for j, mask, g, c0, s_t in chains:
            if mask is not None:
                s_t = jnp.where(mask, s_t, NEG_INF)
            _flash_update_t(s_t, v_t_ref[0, j, g], m_scr, acc_scr, pl.ds(c0, tq))

    def finish_flash(branch):
        for g in range(NSA_KV_GROUPS):
            c = pl.ds(g * n, n)
            add_branch(branch, g, acc_scr[0:hd, c] / acc_scr[hd:hd + 1, c])

    zeros_q = jnp.zeros((hd, n), BF16)
    qgs = []
    for g in range(NSA_KV_GROUPS):
        qg = jnp.concatenate([q_t[(NSA_HPG * g + h) * hd:(NSA_HPG * g + h + 1) * hd, :] for h in range(NSA_HPG)],
                             axis=1)
        qgs.append(qg)
        qz_scr[g] = jnp.concatenate([qg, zeros_q] if g == 0 else [zeros_q, qg], axis=0)

    reset_flash()
    cq = (lax.broadcasted_iota(jnp.int32, (ncp, tq), 0) * CMP_STRIDE
          - lax.broadcasted_iota(jnp.int32, (ncp, tq), 1))
    cmp_valid = (cq <= t0 - (CMP_BLOCK - 1)) & (lax.broadcasted_iota(jnp.int32, (ncp, tq), 0) < n_cmp)
    chains = [(g * n + h * tq, _dot(kcmp_ref[0], qz_scr[g, :, h * tq:(h + 1) * tq]))
              for g in range(NSA_KV_GROUPS) for h in range(NSA_HPG)]
    for c0, s_t in chains:
        _flash_update_t(jnp.where(cmp_valid, s_t, NEG_INF), vcmp_t_ref[0, c0 // n], m_scr, acc_scr, pl.ds(c0, tq))
    seen = (t0 + lax.broadcasted_iota(jnp.int32, (1, tq), 1) >= CMP_BLOCK - 1).astype(F32)
    seen4 = jnp.concatenate([seen] * NSA_HPG, axis=1)
    imp_rows = slice(hd, hd + 64)
    den_row = slice(hd + 64, hd + 65)

    for g in range(NSA_KV_GROUPS):
        qg = qgs[g]
        c = pl.ds(g * n, n)
        inv_l = seen4 / acc_scr[den_row, c]
        add_branch(0, g, acc_scr[0:hd, c] * inv_l)
        pooled = acc_scr[imp_rows, c] * inv_l
        imp_t = functools.reduce(jnp.add, [pooled[:, h * tq:(h + 1) * tq] for h in range(NSA_HPG)])
        nb = lax.broadcasted_iota(jnp.int32, (64, tq), 0)
        tqv = t0 + lax.broadcasted_iota(jnp.int32, (64, tq), 1)
        cur = lax.shift_right_logical(tqv, 6)
        forced = (nb == 0) | (nb == cur) | (nb == cur - 1)
        v_scr[...] = jnp.where(nb * SLC_BLOCK <= tqv, jnp.where(forced, FORCE_SCORE, imp_t), -1.0)
        sub = lax.broadcasted_iota(jnp.int32, (8, tq), 0)
        vch = [v_scr[r * 8:(r + 1) * 8, :] for r in range(8)]
        rank_scr[...] = jnp.zeros(rank_scr.shape, jnp.int32)
        for mg in range(8):
            @pl.when(mg * 8 * SLC_BLOCK < t0 + tq)
            def _():
                count = [jnp.zeros((8, tq), jnp.int32) for _ in range(8)]
                for mblk in range(mg * 8, mg * 8 + 8):
                    vm = jnp.broadcast_to(v_scr[mblk:mblk + 1, :], (8, tq))
                    for r in range(8):
                        if r > mg:
                            before = vm >= vch[r]
                        elif r < mg:
                            before = vm > vch[r]
                        else:
                            before = (vm > vch[r]) | ((vm == vch[r]) & (sub > mblk - r * 8))
                        count[r] = count[r] + jnp.where(before, 1, 0)
                for r in range(8):
                    rank_scr[r * 8:(r + 1) * 8, :] += count[r]
        pen = jnp.where(rank_scr[...] < n_top, 0.0, SEL_PENALTY)
        pen4 = jnp.concatenate([pen.astype(BF16)] * NSA_HPG, axis=1)
        qa_scr[g] = jnp.concatenate([qg, pen4] if g == 0 else [pen4, qg], axis=0)

    jd = t0 // kc

    reset_flash()
    sel = functools.partial(flash_step, qa_scr, ksel_ref, vsl_t_ref)
    lax.fori_loop(0, jd // 2, lambda i, cr: (sel([(2 * i, None), (2 * i + 1, None)]), cr)[1], 0)

    @pl.when(jd % 2 == 0)
    def _():
        sel([(jd, causal)])

    @pl.when(jd % 2 == 1)
    def _():
        sel([(jd - 1, None), (jd, causal)])

    finish_flash(1)

    reset_flash()
    win = functools.partial(flash_step, qz_scr, kw_ref, vw_t_ref)
    assert WINDOW == 2 * kc

    @pl.when(jd == 0)
    def _():
        win([(jd, causal)])

    @pl.when(jd == 1)
    def _():
        win([(jd - 1, None), (jd, causal)])

    @pl.when(jd >= 2)
    def _():
        win([(jd - 2, newer), (jd - 1, None), (jd, causal)])

    finish_flash(2)

    out_ref[0] = (sga_ref[0].astype(F32) * mix_scr[...].T).astype(BF16)


def _nsa_attention(q_t, kcmp, vcmp_t, ksel, vsl_t, kw, vw_t, gates, sga):
    b, seq, _ = ksel.shape
    tq = min(NSA_TQ, seq)
    nq = seq // tq
    n_slc = seq // SLC_BLOCK
    assert n_slc <= 64 and seq % tq == 0 and WINDOW % tq == 0 and tq & (tq - 1) == 0
    ncp = kcmp.shape[1]
    n_cmp = (seq - CMP_BLOCK) // CMP_STRIDE + 1
    vsl_t = vsl_t.reshape((b, nq) + vsl_t.shape[1:])
    vw_t = vw_t.reshape((b, nq) + vw_t.shape[1:])
    n = NSA_HPG * tq
    tile = lambda w: pl.BlockSpec((1, tq, w), lambda bi, qi: (bi, qi, 0))
    per_b = lambda a: pl.BlockSpec((1,) + a.shape[1:], lambda bi, qi: (bi,) + (0,) * (a.ndim - 1))
    return pl.pallas_call(
        functools.partial(_nsa_kernel, tq=tq, kc=tq, n_cmp=n_cmp, n_top=min(SLC_TOPK, n_slc)),
        grid=(b, nq),
        in_specs=[pl.BlockSpec((NSA_WIDTH, tq), lambda bi, qi: (0, bi * nq + qi)),
                  per_b(kcmp), per_b(vcmp_t), per_b(ksel), per_b(vsl_t), per_b(kw), per_b(vw_t),
                  tile(LANES), tile(NSA_WIDTH)],
        out_specs=tile(NSA_WIDTH),
        out_shape=jax.ShapeDtypeStruct((b, seq, NSA_WIDTH), BF16),
        scratch_shapes=[pltpu.VMEM((NSA_KV_GROUPS, LANES, n), BF16),
                        pltpu.VMEM((NSA_KV_GROUPS, LANES, n), BF16),
                        pltpu.VMEM((64, tq), F32),
                        pltpu.VMEM((64, tq), jnp.int32),
                        pltpu.VMEM((1, NSA_KV_GROUPS * n), F32),
                        pltpu.VMEM((NSA_CMP_ROWS, NSA_KV_GROUPS * n), F32),
                        pltpu.VMEM((NSA_WIDTH, tq), F32)],
        compiler_params=pltpu.CompilerParams(dimension_semantics=("arbitrary", "arbitrary"),
                                             vmem_limit_bytes=VMEM_LIMIT),
        name="nsa_attn",
    )(q_t, kcmp, vcmp_t, ksel, vsl_t, kw, vw_t, gates, sga)


def _odd_proj_kernel(x_ref, mixa_ref, mixb_ref, posr_ref, wout_ref, g_ref, wc_ref, wgate_ref, qn_ref, kvn_ref,
                     wuqt_ref, wuk_ref, wuvt_ref, invfc_ref,
                     y_ref, qt_ref, kn_ref, kpe_ref, vt_ref, sg_ref, *, tm):
    y = (x_ref[...] + _dot(mixa_ref[...], wout_ref[0:NSA_WIDTH, :])
         + _dot(mixb_ref[...], wout_ref[NSA_WIDTH:, :]))
    y_ref[...] = y
    yn = _rms(y, g_ref[...]).astype(BF16)
    cos_t, sin_t = _rope_tables_t(invfc_ref[...], posr_ref[...])
    half = MLA_ROPE_DIM // 2

    def rope_t(x1, x2):
        return x1 * cos_t - x2 * sin_t, x2 * cos_t + x1 * sin_t

    seg = _dot(yn, wc_ref[...])
    kr_t = seg[:, MLA_Q_RANK + MLA_KV_RANK:].T
    kp1, kp2 = rope_t(kr_t[0:half], kr_t[half:2 * half])
    kpe_ref[...] = jnp.concatenate([kp1, kp2, kr_t[2 * half:]], axis=0).T.astype(BF16)
    cq = _rms(seg[:, 0:MLA_Q_RANK], qn_ref[...]).astype(BF16)
    ckv = _rms(seg[:, MLA_Q_RANK:MLA_Q_RANK + MLA_KV_RANK], kvn_ref[...]).astype(BF16)
    scale = (MLA_NOPE_DIM + MLA_ROPE_DIM) ** -0.5 * LOG2_E
    qt = _dot_nt(wuqt_ref[...], cq)
    kn_ref[...] = _dot(ckv, wuk_ref[...]).astype(BF16)
    for h in range(MLA_HEADS):
        c0 = h * MLA_QK_PAD
        r0 = c0 + MLA_NOPE_DIM
        qt_ref[c0:r0, :] = (qt[c0:r0] * scale).astype(BF16)
        q1, q2 = rope_t(qt[r0:r0 + half], qt[r0 + half:r0 + 2 * half])
        qt_ref[r0:r0 + half, :] = (q1 * scale).astype(BF16)
        qt_ref[r0 + half:r0 + 2 * half, :] = (q2 * scale).astype(BF16)
        qt_ref[r0 + 2 * half:c0 + MLA_QK_PAD, :] = jnp.zeros((MLA_QK_PAD - MLA_NOPE_DIM - 2 * half, tm), BF16)
    v_t = _dot_nt(wuvt_ref[...], ckv)
    for h in range(MLA_HEADS):
        vt_ref[0, h, 0, 0:MLA_V_DIM, :] = v_t[h * MLA_V_DIM:(h + 1) * MLA_V_DIM].astype(BF16)
        vt_ref[0, h, 0, MLA_V_DIM:, :] = jnp.ones((MLA_ONES_ROWS, tm), BF16)
    sg_ref[...] = _silu(_dot(yn, wgate_ref[...])).astype(BF16)


def _odd_proj(x2d, mixa, mixb, pos_row, a_w_out, c_norm, c_w_in, q_norm, kv_norm, w_uq, w_ukv, invf_col, seq):
    t = x2d.shape[0]
    tm = min(PROJ_ROWS, seq)
    r2 = MLA_Q_RANK + MLA_KV_RANK + MLA_ROPE_DIM
    wc = jnp.pad(c_w_in[:, :r2], ((0, 0), (0, LANES - MLA_ROPE_DIM))).astype(BF16)
    wgate = c_w_in[:, r2:].astype(BF16)
    wuqt = jnp.pad(w_uq.reshape(MLA_Q_RANK, MLA_HEADS, MLA_NOPE_DIM + MLA_ROPE_DIM),
                   ((0, 0), (0, 0), (0, MLA_QK_PAD - MLA_NOPE_DIM - MLA_ROPE_DIM))
                   ).reshape(MLA_Q_RANK, MLA_HEADS * MLA_QK_PAD).T.astype(BF16)
    wukv = w_ukv.reshape(MLA_KV_RANK, MLA_HEADS, MLA_NOPE_DIM + MLA_V_DIM)
    wuk = wukv[:, :, :MLA_NOPE_DIM].reshape(MLA_KV_RANK, -1).astype(BF16)
    wuvt = wukv[:, :, MLA_NOPE_DIM:].reshape(MLA_KV_RANK, -1).T.astype(BF16)
    nq = seq // tm
    assert tm == min(MLA_TQ, seq)
    vrows = MLA_V_DIM + MLA_ONES_ROWS
    ins = [x2d, mixa, mixb, pos_row, a_w_out.astype(BF16), c_norm.reshape(1, -1), wc, wgate,
           q_norm.reshape(1, -1), kv_norm.reshape(1, -1), wuqt, wuk, wuvt, invf_col]
    full = lambda a: pl.BlockSpec(a.shape, lambda i: (0,) * a.ndim)
    rows = lambda n: pl.BlockSpec((tm, n), lambda i: (i, 0))
    cols = lambda n: pl.BlockSpec((n, tm), lambda i: (0, i))
    out_shapes = [
        jax.ShapeDtypeStruct((t, D_MODEL), F32),
        jax.ShapeDtypeStruct((MLA_HEADS * MLA_QK_PAD, t), BF16),
        jax.ShapeDtypeStruct((t, MLA_HEADS * MLA_NOPE_DIM), BF16),
        jax.ShapeDtypeStruct((t, LANES), BF16),
        jax.ShapeDtypeStruct((t // seq, MLA_HEADS, nq, vrows, tm), BF16),
        jax.ShapeDtypeStruct((t, MLA_WIDTH), BF16),
    ]
    vt_spec = pl.BlockSpec((1, MLA_HEADS, 1, vrows, tm), lambda i: (i // nq, 0, i % nq, 0, 0))
    out_specs = [rows(D_MODEL), cols(MLA_HEADS * MLA_QK_PAD), rows(MLA_HEADS * MLA_NOPE_DIM), rows(LANES), vt_spec,
                 rows(MLA_WIDTH)]
    return pl.pallas_call(
        functools.partial(_odd_proj_kernel, tm=tm),
        grid=(t // tm,),
        in_specs=[rows(D_MODEL), rows(NSA_WIDTH), rows(CONV_WIDTH), cols(1)] + [full(a) for a in ins[4:]],
        out_specs=out_specs,
        out_shape=out_shapes,
        compiler_params=pltpu.CompilerParams(dimension_semantics=("arbitrary",),
                                             vmem_limit_bytes=VMEM_LIMIT),
        name="odd_proj",
    )(*ins)


def _mla_kernel(qt_ref, kn_ref, kpe_ref, v_t_ref, o_ref, m_scr, acc_scr, *, tq, kc, hp):
    jd = pl.program_id(2)
    m_scr[...] = jnp.full(m_scr.shape, NEG_INF, F32)
    acc_scr[...] = jnp.zeros(acc_scr.shape, F32)
    causal = (lax.broadcasted_iota(jnp.int32, (kc, tq), 0) <= lax.broadcasted_iota(jnp.int32, (kc, tq), 1))

    def keys(h, start, size):
        return jnp.concatenate([kn_ref[0, pl.ds(start, size), h * MLA_NOPE_DIM:(h + 1) * MLA_NOPE_DIM],
                                kpe_ref[0, pl.ds(start, size), :]], axis=1)

    def step(j):
        start = pl.multiple_of(j * kc, kc)
        scores = []
        for h in range(hp):
            qk = slice(h * MLA_QK_PAD, (h + 1) * MLA_QK_PAD)
            scores.append(_dot(keys(h, start, kc), qt_ref[qk, :]))
        for h, s_t in enumerate(scores):
            _flash_update_t(s_t, v_t_ref[0, h, j], m_scr, acc_scr, pl.ds(h * tq, tq))

    lax.fori_loop(0, jd, lambda j, c: (step(j), c)[1], 0)

    hq = tq // 2
    start = pl.multiple_of(jd * kc, kc)
    tri = causal[0:hq, 0:hq]
    keep_hi = jnp.concatenate([jnp.ones((hq, hq), jnp.bool_), tri], axis=0)
    scores = []
    for h in range(hp):
        qk = slice(h * MLA_QK_PAD, (h + 1) * MLA_QK_PAD)
        scores.append((_dot(keys(h, start, hq), qt_ref[qk, 0:hq]),
                       _dot(keys(h, start, kc), qt_ref[qk, hq:tq])))
    for h, (s_lo, s_hi) in enumerate(scores):
        v_t = v_t_ref[0, h, jd]
        _flash_update_t(jnp.where(tri, s_lo, NEG_INF), v_t[:, 0:hq], m_scr, acc_scr, pl.ds(h * tq, hq))
        _flash_update_t(jnp.where(keep_hi, s_hi, NEG_INF), v_t, m_scr, acc_scr, pl.ds(h * tq + hq, hq))

    for h in range(hp):
        c = pl.ds(h * tq, tq)
        o_t = acc_scr[0:MLA_V_DIM, c] / acc_scr[MLA_V_DIM:MLA_V_DIM + 1, c]
        o_ref[0, :, h * MLA_V_DIM:(h + 1) * MLA_V_DIM] = o_t.T.astype(BF16)


def _mla_attention(q_t, k_nope, k_pe, v_t):
    b, seq, _ = k_pe.shape
    tq = min(MLA_TQ, seq)
    hp = MLA_HEADS_PER_STEP
    nq = seq // tq
    assert seq % tq == 0 and MLA_HEADS % hp == 0
    vrows = MLA_V_DIM + MLA_ONES_ROWS
    return pl.pallas_call(
        functools.partial(_mla_kernel, tq=tq, kc=tq, hp=hp),
        grid=(b, MLA_HEADS // hp, nq),
        in_specs=[pl.BlockSpec((hp * MLA_QK_PAD, tq), lambda bi, h, qi: (h, bi * nq + qi)),
                  pl.BlockSpec((1, seq, hp * MLA_NOPE_DIM), lambda bi, h, qi: (bi, 0, h)),
                  pl.BlockSpec((1, seq, LANES), lambda bi, h, qi: (bi, 0, 0)),
                  pl.BlockSpec((1, hp, nq, vrows, tq), lambda bi, h, qi: (bi, h, 0, 0, 0))],
        out_specs=pl.BlockSpec((1, tq, hp * MLA_V_DIM), lambda bi, h, qi: (bi, qi, h)),
        out_shape=jax.ShapeDtypeStruct((b, seq, MLA_WIDTH), BF16),
        scratch_shapes=[pltpu.VMEM((1, hp * tq), F32), pltpu.VMEM((vrows, hp * tq), F32)],
        compiler_params=pltpu.CompilerParams(dimension_semantics=("arbitrary", "arbitrary", "arbitrary"),
                                             vmem_limit_bytes=VMEM_LIMIT),
        name="mla_attn",
    )(q_t, k_nope, k_pe, v_t)


def _final_kernel(y_ref, o_ref, sg_ref, w_ref, g_ref, out_ref):
    gated = (sg_ref[...].astype(F32) * o_ref[...].astype(F32)).astype(BF16)
    out_ref[...] = _rms(y_ref[...] + _dot(gated, w_ref[...]), g_ref[...])


def _final(y, o, sg, w_out, final_norm, seq):
    t = y.shape[0]
    tm = min(PROJ_ROWS, seq)
    rows = lambda n: pl.BlockSpec((tm, n), lambda i: (i, 0))
    full = lambda a: pl.BlockSpec(a.shape, lambda i: (0,) * a.ndim)
    w = w_out.astype(BF16)
    g = final_norm.reshape(1, -1)
    return pl.pallas_call(
        _final_kernel,
        grid=(t // tm,),
        in_specs=[rows(D_MODEL), rows(MLA_WIDTH), rows(MLA_WIDTH), full(w), full(g)],
        out_specs=rows(D_MODEL),
        out_shape=jax.ShapeDtypeStruct((t, D_MODEL), F32),
        compiler_params=pltpu.CompilerParams(dimension_semantics=("arbitrary",),
                                             vmem_limit_bytes=VMEM_LIMIT),
        name="final",
    )(y, o, sg, w, g)


def _rope_constants():
    half = NSA_HEAD_DIM // 2
    inv_freq = ROPE_THETA ** (-jnp.arange(half, dtype=F32) / half)
    invf = jnp.tile(inv_freq, LANES // half).reshape(1, LANES)
    sgn = jnp.tile(jnp.concatenate([-jnp.ones((half,), F32), jnp.ones((half,), F32)]), LANES // (2 * half))
    return invf, sgn.reshape(1, LANES), inv_freq.reshape(half, 1)


def kernel(x, positions, a_norm, a_w_in, a_pe_k, a_pe_v, a_w_ck1, a_w_ck2, a_w_cv1, a_w_cv2, a_conv_w, a_w_out, c_norm, c_w_in, c_q_norm, c_kv_norm, c_w_uq, c_w_ukv, c_w_out, final_norm):
    b, seq, d = x.shape
    assert d == D_MODEL and NSA_HEAD_DIM == MLA_ROPE_DIM
    assert a_norm.shape[0] == 1 and c_norm.shape[0] == 1
    t = b * seq
    invf, sgn, invf_col = _rope_constants()
    x2d = x.reshape(t, d)
    pos_row = positions.reshape(1, t)

    q_t, kc, vc, ksel, vsl_t, kw, vw_t, gates, sga, mixb = _even_proj(
        x2d, pos_row, a_norm[0], a_w_in[0], a_conv_w[0], invf_col, seq)
    nchunk = seq // CMP_STRIDE
    pos_cmp = jnp.pad(positions[:, CMP_BLOCK - 1::CMP_STRIDE], ((0, 0), (0, 1)))[:, :nchunk, None]
    kcmp, vcmp_t = _compress(kc.reshape(b, seq, -1), vc.reshape(b, seq, -1), pos_cmp, a_pe_k[0], a_pe_v[0],
                             a_w_ck1[0], a_w_ck2[0], a_w_cv1[0], a_w_cv2[0], invf, sgn)
    r3 = lambda a: a.reshape(b, seq, a.shape[-1])
    mixa = _nsa_attention(q_t, kcmp, vcmp_t, r3(ksel), vsl_t, r3(kw), vw_t, r3(gates), r3(sga))

    y, mq_t, mkn, mkpe, mv_t, sg = _odd_proj(x2d, mixa.reshape(t, -1), mixb, pos_row, a_w_out[0], c_norm[0],
                                      c_w_in[0], c_q_norm[0], c_kv_norm[0], c_w_uq[0], c_w_ukv[0], invf_col, seq)
    o = _mla_attention(mq_t, r3(mkn), r3(mkpe), mv_t)
    out = _final(y, o.reshape(t, -1), sg, c_w_out[0], final_norm, seq)
    return out.reshape(b, seq, d)
```

```python
import functools

import jax
import jax.numpy as jnp
import numpy as np
from jax import lax
from jax.experimental import pallas as pl
from jax.experimental.pallas import tpu as pltpu

F32 = jnp.float32
BF16 = jnp.bfloat16

D_MODEL = 1024
ROPE_THETA = 10000.0
RMS_EPS = 1e-6
NEG_INF = -1e30
FORCE_SCORE = 1e4
SEL_PENALTY = NEG_INF

NSA_HEADS = 8
NSA_KV_GROUPS = 2
NSA_HPG = NSA_HEADS // NSA_KV_GROUPS
NSA_HEAD_DIM = 64
NSA_WIDTH = NSA_HEADS * NSA_HEAD_DIM
NSA_KV_WIDTH = NSA_KV_GROUPS * NSA_HEAD_DIM
CMP_BLOCK = 32
CMP_STRIDE = 16
CMP_HIDDEN = 2 * NSA_HEAD_DIM
SLC_BLOCK = 64
SLC_TOPK = 16
WINDOW = 512
CONV_WIDTH = D_MODEL - NSA_WIDTH
CONV_K = 3

MLA_HEADS = 8
MLA_NOPE_DIM = 128
MLA_ROPE_DIM = 64
MLA_V_DIM = 128
MLA_Q_RANK = 256
MLA_KV_RANK = 256
MLA_WIDTH = MLA_HEADS * MLA_V_DIM
MLA_QK_PAD = 256

LANES = 128
VMEM_LIMIT = 56 * 1024 * 1024

PROJ_ROWS = 1024
NSA_TQ = 256
MLA_TQ = 512
MLA_HEADS_PER_STEP = 4
MLA_ONES_ROWS = 16
NSA_ONES_ROWS = 16
NSA_CMP_ROWS = NSA_HEAD_DIM + 64 + NSA_ONES_ROWS
LOG2_E = 1.4426950408889634


def _dot(a, b):
    return jnp.dot(a, b, preferred_element_type=F32)


def _dot_nt(a, b):
    return lax.dot_general(a, b, (((1,), (1,)), ((), ())), preferred_element_type=F32)


def _silu(x):
    return x * jax.nn.sigmoid(x)


def _rms(x, g):
    return x * lax.rsqrt(jnp.mean(x * x, axis=-1, keepdims=True) + RMS_EPS) * g


def _rope_tables(pos_col, invf_row, sign_row):
    ang = pos_col.astype(F32) * invf_row
    return jnp.cos(ang), jnp.sin(ang) * sign_row


def _rope_slab(s, cos2, sin2, first_half):
    swapped = jnp.where(first_half, pltpu.roll(s, 96, 1), pltpu.roll(s, 32, 1))
    return s * cos2 + swapped * sin2


def _rope_tables_t(invf_col, pos_row):
    ang = invf_col * pos_row.astype(F32)
    return jnp.cos(ang), jnp.sin(ang)


def _flash_update_t(s_t, v_t, m_ref, acc_ref, cols):
    m_old = m_ref[:, cols]
    m_new = jnp.maximum(m_old, jnp.max(s_t, axis=0, keepdims=True))
    p = jnp.exp2(s_t - m_new)
    rows = v_t.shape[0]
    acc_ref[0:rows, cols] = jnp.exp2(m_old - m_new) * acc_ref[0:rows, cols] + _dot(v_t, p.astype(BF16))
    m_ref[:, cols] = m_new


def _even_proj_kernel(x_ref, posr_ref, g_ref, wqt_ref, wkv_ref, wg_ref, wga_ref, wcb_ref, wcc_ref, wch_ref,
                      wgb_ref, convw_ref, invfc_ref,
                      qt_ref, kc_ref, vc_ref, ksel_ref, vslt_ref, kw_ref, vwt_ref, gates_ref, sga_ref,
                      mixb_ref, carry_ref, *, tiles_per_seq, tm, kc):
    i = pl.program_id(0)
    xn = _rms(x_ref[...], g_ref[...]).astype(BF16)
    cos_t, sin_t = _rope_tables_t(invfc_ref[...], posr_ref[...])
    half = NSA_HEAD_DIM // 2
    lane = lax.broadcasted_iota(jnp.int32, (tm, LANES), 1)
    low = lane < 64

    def rope_t(x_t):
        out = []
        for r in range(0, x_t.shape[0], 2 * half):
            x1, x2 = x_t[r:r + half], x_t[r + half:r + 2 * half]
            out += [x1 * cos_t - x2 * sin_t, x2 * cos_t + x1 * sin_t]
        return jnp.concatenate(out, axis=0)

    def store_values_t(ref, v):
        v_t = v.T.astype(BF16)
        ones = jnp.ones((NSA_ONES_ROWS, kc), BF16)
        for c in range(tm // kc):
            for g in range(NSA_KV_GROUPS):
                ref[c, g, 0:NSA_HEAD_DIM, :] = v_t[g * NSA_HEAD_DIM:(g + 1) * NSA_HEAD_DIM, c * kc:(c + 1) * kc]
                ref[c, g, NSA_HEAD_DIM:, :] = ones

    scale = NSA_HEAD_DIM ** -0.5 * LOG2_E
    qt_ref[...] = (rope_t(_dot_nt(wqt_ref[...], xn)) * scale).astype(BF16)

    seg = _dot(xn, wkv_ref[...])
    kc_ref[...] = seg[:, 0:LANES]
    vc_ref[...] = seg[:, LANES:2 * LANES]
    ksl = rope_t(seg[:, 2 * LANES:3 * LANES].T).T
    store_values_t(vslt_ref, seg[:, 3 * LANES:4 * LANES])
    kw_ref[...] = rope_t(seg[:, 4 * LANES:5 * LANES].T).T.astype(BF16)
    store_values_t(vwt_ref, seg[:, 5 * LANES:6 * LANES])

    spos = (i % tiles_per_seq) * tm + lax.broadcasted_iota(jnp.int32, (tm, LANES), 0)
    blk = lax.shift_right_logical(spos, 6)
    ksel_ref[:, 0:LANES] = jnp.where(low, ksl, (lane - 64 == blk).astype(F32)).astype(BF16)
    ksel_ref[:, LANES:2 * LANES] = jnp.where(low, (lane == blk).astype(F32), ksl).astype(BF16)

    gates_ref[...] = jax.nn.sigmoid(_dot(xn, wg_ref[...]))
    sga_ref[...] = _silu(_dot(xn, wga_ref[...])).astype(BF16)

    u = _dot(xn, wcc_ref[...]) * _dot(xn, wch_ref[...])

    @pl.when(i % tiles_per_seq == 0)
    def _():
        carry_ref[...] = jnp.zeros_like(carry_ref)

    row = lax.broadcasted_iota(jnp.int32, (tm, CONV_WIDTH), 0)
    prev1 = carry_ref[7:8, :]
    prev2 = carry_ref[6:7, :]
    u1 = jnp.where(row == 0, prev1, pltpu.roll(u, 1, 0))
    u2 = jnp.where(row == 0, prev2, jnp.where(row == 1, prev1, pltpu.roll(u, 2, 0)))
    carry_ref[...] = u[tm - 8:tm, :]
    w = convw_ref[...]
    y = w[0:1, :] * u2 + w[1:2, :] * u1 + w[2:3, :] * u
    cb = _dot(xn, wcb_ref[...])
    mixb_ref[...] = (_silu(_dot(xn, wgb_ref[...])) * (cb * y)).astype(BF16)


def _even_proj(x2d, pos_row, norm_g, w_in, conv_w, invf_col, seq):
    t = x2d.shape[0]
    tm = min(PROJ_ROWS, seq)
    kc = min(NSA_TQ, seq)
    assert seq % tm == 0 and t % tm == 0 and tm % kc == 0
    sizes = (NSA_WIDTH,) + (NSA_KV_WIDTH,) * 6 + (3 * NSA_HEADS, NSA_WIDTH) + (CONV_WIDTH,) * 4
    offs = np.concatenate([[0], np.cumsum(sizes)])
    col = lambda k: w_in[:, offs[k]:offs[k + 1]]
    wqt = col(0).T.astype(BF16)
    wkv = jnp.concatenate([col(k) for k in range(1, 7)], axis=1).astype(BF16)
    wg = jnp.pad(col(7), ((0, 0), (0, LANES - 3 * NSA_HEADS))).astype(BF16)
    wga, wcb, wcc, wch, wgb = (col(k).astype(BF16) for k in range(8, 13))

    full = lambda a: pl.BlockSpec(a.shape, lambda i: (0,) * a.ndim)
    rows = lambda n: pl.BlockSpec((tm, n), lambda i: (i, 0))
    cols = lambda n: pl.BlockSpec((n, tm), lambda i: (0, i))
    vrows = NSA_HEAD_DIM + NSA_ONES_ROWS
    values_t = jax.ShapeDtypeStruct((t // kc, NSA_KV_GROUPS, vrows, kc), BF16)
    values_t_spec = pl.BlockSpec((tm // kc, NSA_KV_GROUPS, vrows, kc), lambda i: (i, 0, 0, 0))
    out_shapes = [
        jax.ShapeDtypeStruct((NSA_WIDTH, t), BF16),
        jax.ShapeDtypeStruct((t, NSA_KV_WIDTH), F32),
        jax.ShapeDtypeStruct((t, NSA_KV_WIDTH), F32),
        jax.ShapeDtypeStruct((t, 2 * LANES), BF16),
        values_t,
        jax.ShapeDtypeStruct((t, NSA_KV_WIDTH), BF16),
        values_t,
        jax.ShapeDtypeStruct((t, LANES), F32),
        jax.ShapeDtypeStruct((t, NSA_WIDTH), BF16),
        jax.ShapeDtypeStruct((t, CONV_WIDTH), BF16),
    ]
    out_specs = [cols(NSA_WIDTH), rows(NSA_KV_WIDTH), rows(NSA_KV_WIDTH), rows(2 * LANES), values_t_spec,
                 rows(NSA_KV_WIDTH), values_t_spec, rows(LANES), rows(NSA_WIDTH), rows(CONV_WIDTH)]
    ins = [x2d, pos_row, norm_g.reshape(1, -1), wqt, wkv, wg, wga, wcb, wcc, wch, wgb, conv_w, invf_col]
    in_specs = [rows(D_MODEL), cols(1)] + [full(a) for a in ins[2:]]
    return pl.pallas_call(
        functools.partial(_even_proj_kernel, tiles_per_seq=seq // tm, tm=tm, kc=kc),
        grid=(t // tm,),
        in_specs=in_specs,
        out_specs=out_specs,
        out_shape=out_shapes,
        scratch_shapes=[pltpu.VMEM((8, CONV_WIDTH), F32)],
        compiler_params=pltpu.CompilerParams(dimension_semantics=("arbitrary",),
                                             vmem_limit_bytes=VMEM_LIMIT),
        name="even_proj",
    )(*ins)


def _compress_kernel(kc_ref, vc_ref, pos_ref, pek_ref, pev_ref, w1k_ref, w2k_ref, w1v_ref, w2vt_ref,
                     invf_ref, sgn_ref, ov_ref, kcmp_ref, vcmpt_ref, *, nchunk):
    half = NSA_KV_GROUPS * CMP_HIDDEN
    rows16 = CMP_BLOCK // 2

    def hidden(raw_ref, pe_ref, w1_ref):
        acc = jnp.zeros((nchunk, 2 * half), F32)
        for l in range(rows16):
            x = raw_ref[0, pl.ds(l, nchunk, stride=CMP_STRIDE), :]
            lhs = jnp.concatenate([(x + pe_ref[l:l + 1, :]).astype(BF16),
                                   (x + pe_ref[rows16 + l:rows16 + l + 1, :]).astype(BF16)], axis=1)
            acc = acc + _dot(lhs, w1_ref[l])
        return _silu(acc[:, 0:half] + pltpu.roll(acc[:, half:], nchunk - 1, 0)).astype(BF16)

    cos2, sin2 = _rope_tables(pos_ref[0], invf_ref[...], sgn_ref[...])
    lane = lax.broadcasted_iota(jnp.int32, (nchunk, LANES), 1)
    kcmp = _dot(hidden(kc_ref, pek_ref, w1k_ref), w2k_ref[...])
    kcmp_ref[0] = _rope_slab(kcmp, cos2, sin2, (lane & 32) == 0).astype(BF16)
    vcmp_t = _dot_nt(w2vt_ref[...], hidden(vc_ref, pev_ref, w1v_ref)).astype(BF16)
    for g in range(NSA_KV_GROUPS):
        vcmpt_ref[0, g, 0:NSA_HEAD_DIM, :] = vcmp_t[g * NSA_HEAD_DIM:(g + 1) * NSA_HEAD_DIM]
        vcmpt_ref[0, g, NSA_HEAD_DIM:NSA_HEAD_DIM + 64, :] = ov_ref[...]
        vcmpt_ref[0, g, NSA_HEAD_DIM + 64:, :] = jnp.ones((NSA_ONES_ROWS, nchunk), BF16)


def _overlap_t(ncp, n_slc):
    cmp_starts = np.arange(ncp) * CMP_STRIDE
    slc_starts = np.arange(64) * SLC_BLOCK
    ov = ((cmp_starts[None, :] < slc_starts[:, None] + SLC_BLOCK)
          & (cmp_starts[None, :] + CMP_BLOCK > slc_starts[:, None])
          & (np.arange(64)[:, None] < n_slc))
    return jnp.asarray(ov.astype(np.float32), BF16)


def _compress_weights(pe, w1, w2):
    eye = jnp.eye(NSA_KV_GROUPS, dtype=F32)
    rows16 = CMP_BLOCK // 2
    w1r = w1.reshape(CMP_BLOCK, NSA_HEAD_DIM, CMP_HIDDEN)
    big = lambda part: jnp.einsum('ldn,gh->lgdhn', part, eye).reshape(
        rows16, NSA_KV_WIDTH, NSA_KV_GROUPS * CMP_HIDDEN)
    top, bot = big(w1r[:rows16]), big(w1r[rows16:])
    zero = jnp.zeros_like(top)
    w1big = jnp.concatenate([jnp.concatenate([top, zero], axis=2),
                             jnp.concatenate([zero, bot], axis=2)], axis=1).astype(BF16)
    w2big = jnp.einsum('nd,gh->gnhd', w2, eye).reshape(NSA_KV_GROUPS * CMP_HIDDEN, NSA_KV_WIDTH).astype(BF16)
    pe_rows = jnp.tile(pe, (1, NSA_KV_GROUPS))
    return pe_rows, w1big, w2big


def _compress(kc, vc, pos_cmp, pe_k, pe_v, w_ck1, w_ck2, w_cv1, w_cv2, invf, sgn):
    b, seq, _ = kc.shape
    nchunk = seq // CMP_STRIDE
    pek, w1k, w2k = _compress_weights(pe_k, w_ck1, w_ck2)
    pev, w1v, w2v = _compress_weights(pe_v, w_cv1, w_cv2)
    ins = [kc, vc, pos_cmp,
           pek, pev, w1k, w2k, w1v, w2v.T, invf, sgn, _overlap_t(nchunk, seq // SLC_BLOCK)]
    full = lambda a: pl.BlockSpec(a.shape, lambda i: (0,) * a.ndim)
    per_b = lambda n: pl.BlockSpec((1, nchunk, n), lambda i: (i, 0, 0))
    out_shape = [jax.ShapeDtypeStruct((b, nchunk, NSA_KV_WIDTH), BF16),
                 jax.ShapeDtypeStruct((b, NSA_KV_GROUPS, NSA_CMP_ROWS, nchunk), BF16)]
    return pl.pallas_call(
        functools.partial(_compress_kernel, nchunk=nchunk),
        grid=(b,),
        in_specs=[pl.BlockSpec((1, seq, NSA_KV_WIDTH), lambda i: (i, 0, 0))] * 2 + [per_b(1)]
        + [full(a) for a in ins[3:]],
        out_specs=[per_b(NSA_KV_WIDTH),
                   pl.BlockSpec((1, NSA_KV_GROUPS, NSA_CMP_ROWS, nchunk), lambda i: (i, 0, 0, 0))],
        out_shape=out_shape,
        compiler_params=pltpu.CompilerParams(dimension_semantics=("arbitrary",),
                                             vmem_limit_bytes=VMEM_LIMIT),
        name="compress",
    )(*ins)


def _nsa_kernel(qt_ref, kcmp_ref, vcmp_t_ref, ksel_ref, vsl_t_ref, kw_ref, vw_t_ref, gates_ref, sga_ref,
                out_ref, qz_scr, qa_scr, v_scr, rank_scr, m_scr, acc_scr, mix_scr,
                *, tq, kc, n_cmp, n_top):
    t0 = pl.program_id(1) * tq
    n = NSA_HPG * tq
    ncp = kcmp_ref.shape[1]
    hd = NSA_HEAD_DIM
    q_t = qt_ref[...]
    gates_t = gates_ref[0].T
    kq = lax.broadcasted_iota(jnp.int32, (kc, tq), 0) - lax.broadcasted_iota(jnp.int32, (kc, tq), 1)
    causal = kq <= 0
    newer = kq > 0

    def add_branch(branch, g, o_t):
        for h in range(NSA_HPG):
            head = NSA_HPG * g + h
            gate = gates_t[head * 3 + branch:head * 3 + branch + 1, :]
            term = gate * o_t[:, h * tq:(h + 1) * tq]
            r = pl.ds(head * hd, hd)
            if branch == 0:
                mix_scr[r, :] = term
            else:
                mix_scr[r, :] += term

    def reset_flash():
        m_scr[...] = jnp.full(m_scr.shape, NEG_INF, F32)
        acc_scr[...] = jnp.zeros(acc_scr.shape, F32)

    def flash_step(q_scr, k_ref, v_t_ref, chunks):
        chains = []
        for j, mask in chunks:
            start = pl.multiple_of(j * kc, kc)
            for g in range(NSA_KV_GROUPS):
                col0 = g * LANES if k_ref is ksel_ref else 0
                k = k_ref[0, pl.ds(start, kc), col0:col0 + LANES]
                for h in range(NSA_HPG):
                    s_t = _dot(k, q_scr[g, :, h * tq:(h + 1) * tq])
                    chains.append((j, mask, g, g * n + h * tq, s_t))
        for j, mask, g, c0, s_t in chains:
            if mask is not None:
                s_t = jnp.where(mask, s_t, NEG_INF)
            _flash_update_t(s_t, v_t_ref[0, j, g], m_scr, acc_scr, pl.ds(c0, tq))

    def finish_flash(branch):
        for g in range(NSA_KV_GROUPS):
            c = pl.ds(g * n, n)
            add_branch(branch, g, acc_scr[0:hd, c] / acc_scr[hd:hd + 1, c])

    zeros_q = jnp.zeros((hd, n), BF16)
    qgs = []
    for g in range(NSA_KV_GROUPS):
        qg = jnp.concatenate([q_t[(NSA_HPG * g + h) * hd:(NSA_HPG * g + h + 1) * hd, :] for h in range(NSA_HPG)],
                             axis=1)
        qgs.append(qg)
        qz_scr[g] = jnp.concatenate([qg, zeros_q] if g == 0 else [zeros_q, qg], axis=0)

    reset_flash()
    cq = (lax.broadcasted_iota(jnp.int32, (ncp, tq), 0) * CMP_STRIDE
          - lax.broadcasted_iota(jnp.int32, (ncp, tq), 1))
    cmp_valid = (cq <= t0 - (CMP_BLOCK - 1)) & (lax.broadcasted_iota(jnp.int32, (ncp, tq), 0) < n_cmp)
    chains = [(g * n + h * tq, _dot(kcmp_ref[0], qz_scr[g, :, h * tq:(h + 1) * tq]))
              for g in range(NSA_KV_GROUPS) for h in range(NSA_HPG)]
    for c0, s_t in chains:
        _flash_update_t(jnp.where(cmp_valid, s_t, NEG_INF), vcmp_t_ref[0, c0 // n], m_scr, acc_scr, pl.ds(c0, tq))
    seen = (t0 + lax.broadcasted_iota(jnp.int32, (1, tq), 1) >= CMP_BLOCK - 1).astype(F32)
    seen4 = jnp.concatenate([seen] * NSA_HPG, axis=1)
    imp_rows = slice(hd, hd + 64)
    den_row = slice(hd + 64, hd + 65)

    for g in range(NSA_KV_GROUPS):
        qg = qgs[g]
        c = pl.ds(g * n, n)
        inv_l = seen4 / acc_scr[den_row, c]
        add_branch(0, g, acc_scr[0:hd, c] * inv_l)
        pooled = acc_scr[imp_rows, c] * inv_l
        imp_t = functools.reduce(jnp.add, [pooled[:, h * tq:(h + 1) * tq] for h in range(NSA_HPG)])
        nb = lax.broadcasted_iota(jnp.int32, (64, tq), 0)
        tqv = t0 + lax.broadcasted_iota(jnp.int32, (64, tq), 1)
        cur = lax.shift_right_logical(tqv, 6)
        forced = (nb == 0) | (nb == cur) | (nb == cur - 1)
        v_scr[...] = jnp.where(nb * SLC_BLOCK <= tqv, jnp.where(forced, FORCE_SCORE, imp_t), -1.0)
        sub = lax.broadcasted_iota(jnp.int32, (8, tq), 0)
        vch = [v_scr[r * 8:(r + 1) * 8, :] for r in range(8)]
        rank_scr[...] = jnp.zeros(rank_scr.shape, jnp.int32)
        for mg in range(8):
            @pl.when(mg * 8 * SLC_BLOCK < t0 + tq)
            def _():
                count = [jnp.zeros((8, tq), jnp.int32) for _ in range(8)]
                for mblk in range(mg * 8, mg * 8 + 8):
                    vm = jnp.broadcast_to(v_scr[mblk:mblk + 1, :], (8, tq))
                    for r in range(8):
                        if r > mg:
                            before = vm >= vch[r]
                        elif r < mg:
                            before = vm > vch[r]
                        else:
                            before = (vm > vch[r]) | ((vm == vch[r]) & (sub > mblk - r * 8))
                        count[r] = count[r] + jnp.where(before, 1, 0)
                for r in range(8):
                    rank_scr[r * 8:(r + 1) * 8, :] += count[r]
        pen = jnp.where(rank_scr[...] < n_top, 0.0, SEL_PENALTY)
        pen4 = jnp.concatenate([pen.astype(BF16)] * NSA_HPG, axis=1)
        qa_scr[g] = jnp.concatenate([qg, pen4] if g == 0 else [pen4, qg], axis=0)

    jd = t0 // kc

    reset_flash()
    sel = functools.partial(flash_step, qa_scr, ksel_ref, vsl_t_ref)
    lax.fori_loop(0, jd // 2, lambda i, cr: (sel([(2 * i, None), (2 * i + 1, None)]), cr)[1], 0)

    @pl.when(jd % 2 == 0)
    def _():
        sel([(jd, causal)])

    @pl.when(jd % 2 == 1)
    def _():
        sel([(jd - 1, None), (jd, causal)])

    finish_flash(1)

    reset_flash()
    win = functools.partial(flash_step, qz_scr, kw_ref, vw_t_ref)
    assert WINDOW == 2 * kc

    @pl.when(jd == 0)
    def _():
        win([(jd, causal)])

    @pl.when(jd == 1)
    def _():
        win([(jd - 1, None), (jd, causal)])

    @pl.when(jd >= 2)
    def _():
        win([(jd - 2, newer), (jd - 1, None), (jd, causal)])

    finish_flash(2)

    out_ref[0] = (sga_ref[0].astype(F32) * mix_scr[...].T).astype(BF16)


def _nsa_attention(q_t, kcmp, vcmp_t, ksel, vsl_t, kw, vw_t, gates, sga):
    b, seq, _ = ksel.shape
    tq = min(NSA_TQ, seq)
    nq = seq // tq
    n_slc = seq // SLC_BLOCK
    assert n_slc <= 64 and seq % tq == 0 and WINDOW % tq == 0 and tq & (tq - 1) == 0
    ncp = kcmp.shape[1]
    n_cmp = (seq - CMP_BLOCK) // CMP_STRIDE + 1
    vsl_t = vsl_t.reshape((b, nq) + vsl_t.shape[1:])
    vw_t = vw_t.reshape((b, nq) + vw_t.shape[1:])
    n = NSA_HPG * tq
    tile = lambda w: pl.BlockSpec((1, tq, w), lambda bi, qi: (bi, qi, 0))
    per_b = lambda a: pl.BlockSpec((1,) + a.shape[1:], lambda bi, qi: (bi,) + (0,) * (a.ndim - 1))
    return pl.pallas_call(
        functools.partial(_nsa_kernel, tq=tq, kc=tq, n_cmp=n_cmp, n_top=min(SLC_TOPK, n_slc)),
        grid=(b, nq),
        in_specs=[pl.BlockSpec((NSA_WIDTH, tq), lambda bi, qi: (0, bi * nq + qi)),
                  per_b(kcmp), per_b(vcmp_t), per_b(ksel), per_b(vsl_t), per_b(kw), per_b(vw_t),
                  tile(LANES), tile(NSA_WIDTH)],
        out_specs=tile(NSA_WIDTH),
        out_shape=jax.ShapeDtypeStruct((b, seq, NSA_WIDTH), BF16),
        scratch_shapes=[pltpu.VMEM((NSA_KV_GROUPS, LANES, n), BF16),
                        pltpu.VMEM((NSA_KV_GROUPS, LANES, n), BF16),
                        pltpu.VMEM((64, tq), F32),
                        pltpu.VMEM((64, tq), jnp.int32),
                        pltpu.VMEM((1, NSA_KV_GROUPS * n), F32),
                        pltpu.VMEM((NSA_CMP_ROWS, NSA_KV_GROUPS * n), F32),
                        pltpu.VMEM((NSA_WIDTH, tq), F32)],
        compiler_params=pltpu.CompilerParams(dimension_semantics=("arbitrary", "arbitrary"),
                                             vmem_limit_bytes=VMEM_LIMIT),
        name="nsa_attn",
    )(q_t, kcmp, vcmp_t, ksel, vsl_t, kw, vw_t, gates, sga)


def _odd_proj_kernel(x_ref, mixa_ref, mixb_ref, posr_ref, wout_ref, g_ref, wc_ref, wgate_ref, qn_ref, kvn_ref,
                     wuqt_ref, wuk_ref, wuvt_ref, invfc_ref,
                     y_ref, qt_ref, kn_ref, kpe_ref, vt_ref, sg_ref, *, tm):
    y = (x_ref[...] + _dot(mixa_ref[...], wout_ref[0:NSA_WIDTH, :])
         + _dot(mixb_ref[...], wout_ref[NSA_WIDTH:, :]))
    y_ref[...] = y
    yn = _rms(y, g_ref[...]).astype(BF16)
    cos_t, sin_t = _rope_tables_t(invfc_ref[...], posr_ref[...])
    half = MLA_ROPE_DIM // 2

    def rope_t(x1, x2):
        return x1 * cos_t - x2 * sin_t, x2 * cos_t + x1 * sin_t

    seg = _dot(yn, wc_ref[...])
    kr_t = seg[:, MLA_Q_RANK + MLA_KV_RANK:].T
    kp1, kp2 = rope_t(kr_t[0:half], kr_t[half:2 * half])
    kpe_ref[...] = jnp.concatenate([kp1, kp2, kr_t[2 * half:]], axis=0).T.astype(BF16)
    cq = _rms(seg[:, 0:MLA_Q_RANK], qn_ref[...]).astype(BF16)
    ckv = _rms(seg[:, MLA_Q_RANK:MLA_Q_RANK + MLA_KV_RANK], kvn_ref[...]).astype(BF16)
    scale = (MLA_NOPE_DIM + MLA_ROPE_DIM) ** -0.5 * LOG2_E
    qt = _dot_nt(wuqt_ref[...], cq)
    kn_ref[...] = _dot(ckv, wuk_ref[...]).astype(BF16)
    for h in range(MLA_HEADS):
        c0 = h * MLA_QK_PAD
        r0 = c0 + MLA_NOPE_DIM
        qt_ref[c0:r0, :] = (qt[c0:r0] * scale).astype(BF16)
        q1, q2 = rope_t(qt[r0:r0 + half], qt[r0 + half:r0 + 2 * half])
        qt_ref[r0:r0 + half, :] = (q1 * scale).astype(BF16)
        qt_ref[r0 + half:r0 + 2 * half, :] = (q2 * scale).astype(BF16)
        qt_ref[r0 + 2 * half:c0 + MLA_QK_PAD, :] = jnp.zeros((MLA_QK_PAD - MLA_NOPE_DIM - 2 * half, tm), BF16)
    v_t = _dot_nt(wuvt_ref[...], ckv)
    kc = vt_ref.shape[-1]
    for h in range(MLA_HEADS):
        for c in range(tm // kc):
            vt_ref[0, h, c, 0:MLA_V_DIM, :] = v_t[h * MLA_V_DIM:(h + 1) * MLA_V_DIM, c * kc:(c + 1) * kc].astype(BF16)
            vt_ref[0, h, c, MLA_V_DIM:, :] = jnp.ones((MLA_ONES_ROWS, kc), BF16)
    sg_ref[...] = _silu(_dot(yn, wgate_ref[...])).astype(BF16)


def _odd_proj(x2d, mixa, mixb, pos_row, a_w_out, c_norm, c_w_in, q_norm, kv_norm, w_uq, w_ukv, invf_col, seq):
    t = x2d.shape[0]
    tm = min(PROJ_ROWS, seq)
    r2 = MLA_Q_RANK + MLA_KV_RANK + MLA_ROPE_DIM
    wc = jnp.pad(c_w_in[:, :r2], ((0, 0), (0, LANES - MLA_ROPE_DIM))).astype(BF16)
    wgate = c_w_in[:, r2:].astype(BF16)
    wuqt = jnp.pad(w_uq.reshape(MLA_Q_RANK, MLA_HEADS, MLA_NOPE_DIM + MLA_ROPE_DIM),
                   ((0, 0), (0, 0), (0, MLA_QK_PAD - MLA_NOPE_DIM - MLA_ROPE_DIM))
                   ).reshape(MLA_Q_RANK, MLA_HEADS * MLA_QK_PAD).T.astype(BF16)
    wukv = w_ukv.reshape(MLA_KV_RANK, MLA_HEADS, MLA_NOPE_DIM + MLA_V_DIM)
    wuk = wukv[:, :, :MLA_NOPE_DIM].reshape(MLA_KV_RANK, -1).astype(BF16)
    wuvt = wukv[:, :, MLA_NOPE_DIM:].reshape(MLA_KV_RANK, -1).T.astype(BF16)
    kc = min(MLA_TQ, seq)
    nq = seq // kc
    cps = tm // kc
    assert tm % kc == 0 and seq % tm == 0
    vrows = MLA_V_DIM + MLA_ONES_ROWS
    ins = [x2d, mixa, mixb, pos_row, a_w_out.astype(BF16), c_norm.reshape(1, -1), wc, wgate,
           q_norm.reshape(1, -1), kv_norm.reshape(1, -1), wuqt, wuk, wuvt, invf_col]
    full = lambda a: pl.BlockSpec(a.shape, lambda i: (0,) * a.ndim)
    rows = lambda n: pl.BlockSpec((tm, n), lambda i: (i, 0))
    cols = lambda n: pl.BlockSpec((n, tm), lambda i: (0, i))
    out_shapes = [
        jax.ShapeDtypeStruct((t, D_MODEL), F32),
        jax.ShapeDtypeStruct((MLA_HEADS * MLA_QK_PAD, t), BF16),
        jax.ShapeDtypeStruct((t, MLA_HEADS * MLA_NOPE_DIM), BF16),
        jax.ShapeDtypeStruct((t, LANES), BF16),
        jax.ShapeDtypeStruct((t // seq, MLA_HEADS, nq, vrows, kc), BF16),
        jax.ShapeDtypeStruct((t, MLA_WIDTH), BF16),
    ]
    steps = seq // tm
    vt_spec = pl.BlockSpec((1, MLA_HEADS, cps, vrows, kc), lambda i: (i // steps, 0, i % steps, 0, 0))
    out_specs = [rows(D_MODEL), cols(MLA_HEADS * MLA_QK_PAD), rows(MLA_HEADS * MLA_NOPE_DIM), rows(LANES), vt_spec,
                 rows(MLA_WIDTH)]
    return pl.pallas_call(
        functools.partial(_odd_proj_kernel, tm=tm),
        grid=(t // tm,),
        in_specs=[rows(D_MODEL), rows(NSA_WIDTH), rows(CONV_WIDTH), cols(1)] + [full(a) for a in ins[4:]],
        out_specs=out_specs,
        out_shape=out_shapes,
        compiler_params=pltpu.CompilerParams(dimension_semantics=("arbitrary",),
                                             vmem_limit_bytes=VMEM_LIMIT),
        name="odd_proj",
    )(*ins)


def _mla_kernel(qt_ref, kn_ref, kpe_ref, v_t_ref, o_ref, m_scr, acc_scr, *, tq, kc, hp):
    jd = pl.program_id(2)
    m_scr[...] = jnp.full(m_scr.shape, NEG_INF, F32)
    acc_scr[...] = jnp.zeros(acc_scr.shape, F32)
    causal = (lax.broadcasted_iota(jnp.int32, (kc, tq), 0) <= lax.broadcasted_iota(jnp.int32, (kc, tq), 1))

    def keys(h, start, size):
        return jnp.concatenate([kn_ref[0, pl.ds(start, size), h * MLA_NOPE_DIM:(h + 1) * MLA_NOPE_DIM],
                                kpe_ref[0, pl.ds(start, size), :]], axis=1)

    def step(j):
        start = pl.multiple_of(j * kc, kc)
        scores = []
        for h in range(hp):
            qk = slice(h * MLA_QK_PAD, (h + 1) * MLA_QK_PAD)
            scores.append(_dot(keys(h, start, kc), qt_ref[qk, :]))
        for h, s_t in enumerate(scores):
            _flash_update_t(s_t, v_t_ref[0, h, j], m_scr, acc_scr, pl.ds(h * tq, tq))

    lax.fori_loop(0, jd, lambda j, c: (step(j), c)[1], 0)

    hq = tq // 2
    start = pl.multiple_of(jd * kc, kc)
    tri = causal[0:hq, 0:hq]
    keep_hi = jnp.concatenate([jnp.ones((hq, hq), jnp.bool_), tri], axis=0)
    scores = []
    for h in range(hp):
        qk = slice(h * MLA_QK_PAD, (h + 1) * MLA_QK_PAD)
        scores.append((_dot(keys(h, start, hq), qt_ref[qk, 0:hq]),
                       _dot(keys(h, start, kc), qt_ref[qk, hq:tq])))
    for h, (s_lo, s_hi) in enumerate(scores):
        v_t = v_t_ref[0, h, jd]
        _flash_update_t(jnp.where(tri, s_lo, NEG_INF), v_t[:, 0:hq], m_scr, acc_scr, pl.ds(h * tq, hq))
        _flash_update_t(jnp.where(keep_hi, s_hi, NEG_INF), v_t, m_scr, acc_scr, pl.ds(h * tq + hq, hq))

    for h in range(hp):
        c = pl.ds(h * tq, tq)
        o_t = acc_scr[0:MLA_V_DIM, c] / acc_scr[MLA_V_DIM:MLA_V_DIM + 1, c]
        o_ref[0, :, h * MLA_V_DIM:(h + 1) * MLA_V_DIM] = o_t.T.astype(BF16)


def _mla_attention(q_t, k_nope, k_pe, v_t):
    b, seq, _ = k_pe.shape
    tq = min(MLA_TQ, seq)
    hp = MLA_HEADS_PER_STEP
    nq = seq // tq
    assert seq % tq == 0 and MLA_HEADS % hp == 0
    vrows = MLA_V_DIM + MLA_ONES_ROWS
    return pl.pallas_call(
        functools.partial(_mla_kernel, tq=tq, kc=tq, hp=hp),
        grid=(b, MLA_HEADS // hp, nq),
        in_specs=[pl.BlockSpec((hp * MLA_QK_PAD, tq), lambda bi, h, qi: (h, bi * nq + qi)),
                  pl.BlockSpec((1, seq, hp * MLA_NOPE_DIM), lambda bi, h, qi: (bi, 0, h)),
                  pl.BlockSpec((1, seq, LANES), lambda bi, h, qi: (bi, 0, 0)),
                  pl.BlockSpec((1, hp, nq, vrows, tq), lambda bi, h, qi: (bi, h, 0, 0, 0))],
        out_specs=pl.BlockSpec((1, tq, hp * MLA_V_DIM), lambda bi, h, qi: (bi, qi, h)),
        out_shape=jax.ShapeDtypeStruct((b, seq, MLA_WIDTH), BF16),
        scratch_shapes=[pltpu.VMEM((1, hp * tq), F32), pltpu.VMEM((vrows, hp * tq), F32)],
        compiler_params=pltpu.CompilerParams(dimension_semantics=("arbitrary", "arbitrary", "arbitrary"),
                                             vmem_limit_bytes=VMEM_LIMIT),
        name="mla_attn",
    )(q_t, k_nope, k_pe, v_t)


def _final_kernel(y_ref, o_ref, sg_ref, w_ref, g_ref, out_ref):
    gated = (sg_ref[...].astype(F32) * o_ref[...].astype(F32)).astype(BF16)
    out_ref[...] = _rms(y_ref[...] + _dot(gated, w_ref[...]), g_ref[...])


def _final(y, o, sg, w_out, final_norm, seq):
    t = y.shape[0]
    tm = min(PROJ_ROWS, seq)
    rows = lambda n: pl.BlockSpec((tm, n), lambda i: (i, 0))
    full = lambda a: pl.BlockSpec(a.shape, lambda i: (0,) * a.ndim)
    w = w_out.astype(BF16)
    g = final_norm.reshape(1, -1)
    return pl.pallas_call(
        _final_kernel,
        grid=(t // tm,),
        in_specs=[rows(D_MODEL), rows(MLA_WIDTH), rows(MLA_WIDTH), full(w), full(g)],
        out_specs=rows(D_MODEL),
        out_shape=jax.ShapeDtypeStruct((t, D_MODEL), F32),
        compiler_params=pltpu.CompilerParams(dimension_semantics=("arbitrary",),
                                             vmem_limit_bytes=VMEM_LIMIT),
        name="final",
    )(y, o, sg, w, g)


def _rope_constants():
    half = NSA_HEAD_DIM // 2
    inv_freq = ROPE_THETA ** (-jnp.arange(half, dtype=F32) / half)
    invf = jnp.tile(inv_freq, LANES // half).reshape(1, LANES)
    sgn = jnp.tile(jnp.concatenate([-jnp.ones((half,), F32), jnp.ones((half,), F32)]), LANES // (2 * half))
    return invf, sgn.reshape(1, LANES), inv_freq.reshape(half, 1)


def kernel(x, positions, a_norm, a_w_in, a_pe_k, a_pe_v, a_w_ck1, a_w_ck2, a_w_cv1, a_w_cv2, a_conv_w, a_w_out, c_norm, c_w_in, c_q_norm, c_kv_norm, c_w_uq, c_w_ukv, c_w_out, final_norm):
    b, seq, d = x.shape
    assert d == D_MODEL and NSA_HEAD_DIM == MLA_ROPE_DIM
    assert a_norm.shape[0] == 1 and c_norm.shape[0] == 1
    t = b * seq
    invf, sgn, invf_col = _rope_constants()
    x2d = x.reshape(t, d)
    pos_row = positions.reshape(1, t)

    q_t, kc, vc, ksel, vsl_t, kw, vw_t, gates, sga, mixb = _even_proj(
        x2d, pos_row, a_norm[0], a_w_in[0], a_conv_w[0], invf_col, seq)
    nchunk = seq // CMP_STRIDE
    pos_cmp = jnp.pad(positions[:, CMP_BLOCK - 1::CMP_STRIDE], ((0, 0), (0, 1)))[:, :nchunk, None]
    kcmp, vcmp_t = _compress(kc.reshape(b, seq, -1), vc.reshape(b, seq, -1), pos_cmp, a_pe_k[0], a_pe_v[0],
                             a_w_ck1[0], a_w_ck2[0], a_w_cv1[0], a_w_cv2[0], invf, sgn)
    r3 = lambda a: a.reshape(b, seq, a.shape[-1])
    mixa = _nsa_attention(q_t, kcmp, vcmp_t, r3(ksel), vsl_t, r3(kw), vw_t, r3(gates), r3(sga))

    y, mq_t, mkn, mkpe, mv_t, sg = _odd_proj(x2d, mixa.reshape(t, -1), mixb, pos_row, a_w_out[0], c_norm[0],
                                      c_w_in[0], c_q_norm[0], c_kv_norm[0], c_w_uq[0], c_w_ukv[0], invf_col, seq)
    o = _mla_attention(mq_t, r3(mkn), r3(mkpe), mv_t)
    out = _final(y, o.reshape(t, -1), sg, c_w_out[0], final_norm, seq)
    return out.reshape(b, seq, d)
```

```python
import functools

import jax
import jax.numpy as jnp
import numpy as np
from jax import lax
from jax.experimental import pallas as pl
from jax.experimental.pallas import tpu as pltpu

F32 = jnp.float32
BF16 = jnp.bfloat16

D_MODEL = 1024
ROPE_THETA = 10000.0
RMS_EPS = 1e-6
NEG_INF = -1e30
FORCE_SCORE = 1e4
SEL_PENALTY = NEG_INF

NSA_HEADS = 8
NSA_KV_GROUPS = 2
NSA_HPG = NSA_HEADS // NSA_KV_GROUPS
NSA_HEAD_DIM = 64
NSA_WIDTH = NSA_HEADS * NSA_HEAD_DIM
NSA_KV_WIDTH = NSA_KV_GROUPS * NSA_HEAD_DIM
CMP_BLOCK = 32
CMP_STRIDE = 16
CMP_HIDDEN = 2 * NSA_HEAD_DIM
SLC_BLOCK = 64
SLC_TOPK = 16
WINDOW = 512
CONV_WIDTH = D_MODEL - NSA_WIDTH
CONV_K = 3

MLA_HEADS = 8
MLA_NOPE_DIM = 128
MLA_ROPE_DIM = 64
MLA_V_DIM = 128
MLA_Q_RANK = 256
MLA_KV_RANK = 256
MLA_WIDTH = MLA_HEADS * MLA_V_DIM
MLA_QK_PAD = 256

LANES = 128
VMEM_LIMIT = 56 * 1024 * 1024

PROJ_ROWS = 1024
NSA_TQ = 256
MLA_TQ = 512
MLA_HEADS_PER_STEP = 8
MLA_ONES_ROWS = 16
NSA_ONES_ROWS = 16
NSA_CMP_ROWS = NSA_HEAD_DIM + 64 + NSA_ONES_ROWS
LOG2_E = 1.4426950408889634


def _dot(a, b):
    return jnp.dot(a, b, preferred_element_type=F32)


def _dot_nt(a, b):
    return lax.dot_general(a, b, (((1,), (1,)), ((), ())), preferred_element_type=F32)


def _silu(x):
    return x * jax.nn.sigmoid(x)


def _rms(x, g):
    return x * lax.rsqrt(jnp.mean(x * x, axis=-1, keepdims=True) + RMS_EPS) * g


def _rope_tables(pos_col, invf_row, sign_row):
    ang = pos_col.astype(F32) * invf_row
    return jnp.cos(ang), jnp.sin(ang) * sign_row


def _rope_slab(s, cos2, sin2, first_half):
    swapped = jnp.where(first_half, pltpu.roll(s, 96, 1), pltpu.roll(s, 32, 1))
    return s * cos2 + swapped * sin2


def _rope_tables_t(invf_col, pos_row):
    ang = invf_col * pos_row.astype(F32)
    return jnp.cos(ang), jnp.sin(ang)


def _flash_update_t(s_t, v_t, m_ref, acc_ref, cols):
    m_old = m_ref[:, cols]
    m_new = jnp.maximum(m_old, jnp.max(s_t, axis=0, keepdims=True))
    p = jnp.exp2(s_t - m_new)
    rows = v_t.shape[0]
    acc_ref[0:rows, cols] = jnp.exp2(m_old - m_new) * acc_ref[0:rows, cols] + _dot(v_t, p.astype(BF16))
    m_ref[:, cols] = m_new


def _even_proj_kernel(x_ref, posr_ref, g_ref, wqt_ref, wkv_ref, wg_ref, wga_ref, wcb_ref, wcc_ref, wch_ref,
                      wgb_ref, convw_ref, invfc_ref,
                      qt_ref, kc_ref, vc_ref, ksel_ref, vslt_ref, kw_ref, vwt_ref, gates_ref, sga_ref,
                      mixb_ref, carry_ref, *, tiles_per_seq, tm, kc):
    i = pl.program_id(0)
    xn = _rms(x_ref[...], g_ref[...]).astype(BF16)
    cos_t, sin_t = _rope_tables_t(invfc_ref[...], posr_ref[...])
    half = NSA_HEAD_DIM // 2
    lane = lax.broadcasted_iota(jnp.int32, (tm, LANES), 1)
    low = lane < 64

    def rope_t(x_t):
        out = []
        for r in range(0, x_t.shape[0], 2 * half):
            x1, x2 = x_t[r:r + half], x_t[r + half:r + 2 * half]
            out += [x1 * cos_t - x2 * sin_t, x2 * cos_t + x1 * sin_t]
        return jnp.concatenate(out, axis=0)

    def store_values_t(ref, v):
        v_t = v.T.astype(BF16)
        ones = jnp.ones((NSA_ONES_ROWS, kc), BF16)
        for c in range(tm // kc):
            for g in range(NSA_KV_GROUPS):
                ref[c, g, 0:NSA_HEAD_DIM, :] = v_t[g * NSA_HEAD_DIM:(g + 1) * NSA_HEAD_DIM, c * kc:(c + 1) * kc]
                ref[c, g, NSA_HEAD_DIM:, :] = ones

    scale = NSA_HEAD_DIM ** -0.5 * LOG2_E
    qt_ref[...] = (rope_t(_dot_nt(wqt_ref[...], xn)) * scale).astype(BF16)

    seg = _dot(xn, wkv_ref[...])
    kc_ref[...] = seg[:, 0:LANES]
    vc_ref[...] = seg[:, LANES:2 * LANES]
    ksl = rope_t(seg[:, 2 * LANES:3 * LANES].T).T
    store_values_t(vslt_ref, seg[:, 3 * LANES:4 * LANES])
    kw_ref[...] = rope_t(seg[:, 4 * LANES:5 * LANES].T).T.astype(BF16)
    store_values_t(vwt_ref, seg[:, 5 * LANES:6 * LANES])

    spos = (i % tiles_per_seq) * tm + lax.broadcasted_iota(jnp.int32, (tm, LANES), 0)
    blk = lax.shift_right_logical(spos, 6)
    ksel_ref[:, 0:LANES] = jnp.where(low, ksl, (lane - 64 == blk).astype(F32)).astype(BF16)
    ksel_ref[:, LANES:2 * LANES] = jnp.where(low, (lane == blk).astype(F32), ksl).astype(BF16)

    gates_ref[...] = jax.nn.sigmoid(_dot(xn, wg_ref[...]))
    sga_ref[...] = _silu(_dot(xn, wga_ref[...])).astype(BF16)

    u = _dot(xn, wcc_ref[...]) * _dot(xn, wch_ref[...])

    @pl.when(i % tiles_per_seq == 0)
    def _():
        carry_ref[...] = jnp.zeros_like(carry_ref)

    row = lax.broadcasted_iota(jnp.int32, (tm, CONV_WIDTH), 0)
    prev1 = carry_ref[7:8, :]
    prev2 = carry_ref[6:7, :]
    u1 = jnp.where(row == 0, prev1, pltpu.roll(u, 1, 0))
    u2 = jnp.where(row == 0, prev2, jnp.where(row == 1, prev1, pltpu.roll(u, 2, 0)))
    carry_ref[...] = u[tm - 8:tm, :]
    w = convw_ref[...]
    y = w[0:1, :] * u2 + w[1:2, :] * u1 + w[2:3, :] * u
    cb = _dot(xn, wcb_ref[...])
    mixb_ref[...] = (_silu(_dot(xn, wgb_ref[...])) * (cb * y)).astype(BF16)


def _even_proj(x2d, pos_row, norm_g, w_in, conv_w, invf_col, seq):
    t = x2d.shape[0]
    tm = min(PROJ_ROWS, seq)
    kc = min(NSA_TQ, seq)
    assert seq % tm == 0 and t % tm == 0 and tm % kc == 0
    sizes = (NSA_WIDTH,) + (NSA_KV_WIDTH,) * 6 + (3 * NSA_HEADS, NSA_WIDTH) + (CONV_WIDTH,) * 4
    offs = np.concatenate([[0], np.cumsum(sizes)])
    col = lambda k: w_in[:, offs[k]:offs[k + 1]]
    wqt = col(0).T.astype(BF16)
    wkv = jnp.concatenate([col(k) for k in range(1, 7)], axis=1).astype(BF16)
    wg = jnp.pad(col(7), ((0, 0), (0, LANES - 3 * NSA_HEADS))).astype(BF16)
    wga, wcb, wcc, wch, wgb = (col(k).astype(BF16) for k in range(8, 13))

    full = lambda a: pl.BlockSpec(a.shape, lambda i: (0,) * a.ndim)
    rows = lambda n: pl.BlockSpec((tm, n), lambda i: (i, 0))
    cols = lambda n: pl.BlockSpec((n, tm), lambda i: (0, i))
    vrows = NSA_HEAD_DIM + NSA_ONES_ROWS
    values_t = jax.ShapeDtypeStruct((t // kc, NSA_KV_GROUPS, vrows, kc), BF16)
    values_t_spec = pl.BlockSpec((tm // kc, NSA_KV_GROUPS, vrows, kc), lambda i: (i, 0, 0, 0))
    out_shapes = [
        jax.ShapeDtypeStruct((NSA_WIDTH, t), BF16),
        jax.ShapeDtypeStruct((t, NSA_KV_WIDTH), F32),
        jax.ShapeDtypeStruct((t, NSA_KV_WIDTH), F32),
        jax.ShapeDtypeStruct((t, 2 * LANES), BF16),
        values_t,
        jax.ShapeDtypeStruct((t, NSA_KV_WIDTH), BF16),
        values_t,
        jax.ShapeDtypeStruct((t, LANES), F32),
        jax.ShapeDtypeStruct((t, NSA_WIDTH), BF16),
        jax.ShapeDtypeStruct((t, CONV_WIDTH), BF16),
    ]
    out_specs = [cols(NSA_WIDTH), rows(NSA_KV_WIDTH), rows(NSA_KV_WIDTH), rows(2 * LANES), values_t_spec,
                 rows(NSA_KV_WIDTH), values_t_spec, rows(LANES), rows(NSA_WIDTH), rows(CONV_WIDTH)]
    ins = [x2d, pos_row, norm_g.reshape(1, -1), wqt, wkv, wg, wga, wcb, wcc, wch, wgb, conv_w, invf_col]
    in_specs = [rows(D_MODEL), cols(1)] + [full(a) for a in ins[2:]]
    return pl.pallas_call(
        functools.partial(_even_proj_kernel, tiles_per_seq=seq // tm, tm=tm, kc=kc),
        grid=(t // tm,),
        in_specs=in_specs,
        out_specs=out_specs,
        out_shape=out_shapes,
        scratch_shapes=[pltpu.VMEM((8, CONV_WIDTH), F32)],
        compiler_params=pltpu.CompilerParams(dimension_semantics=("arbitrary",),
                                             vmem_limit_bytes=VMEM_LIMIT),
        name="even_proj",
    )(*ins)


def _compress_kernel(kc_ref, vc_ref, pos_ref, pek_ref, pev_ref, w1k_ref, w2k_ref, w1v_ref, w2vt_ref,
                     invf_ref, sgn_ref, ov_ref, kcmp_ref, vcmpt_ref, *, nchunk):
    half = NSA_KV_GROUPS * CMP_HIDDEN
    rows16 = CMP_BLOCK // 2

    def hidden(raw_ref, pe_ref, w1_ref):
        acc = jnp.zeros((nchunk, 2 * half), F32)
        for l in range(rows16):
            x = raw_ref[0, pl.ds(l, nchunk, stride=CMP_STRIDE), :]
            lhs = jnp.concatenate([(x + pe_ref[l:l + 1, :]).astype(BF16),
                                   (x + pe_ref[rows16 + l:rows16 + l + 1, :]).astype(BF16)], axis=1)
            acc = acc + _dot(lhs, w1_ref[l])
        return _silu(acc[:, 0:half] + pltpu.roll(acc[:, half:], nchunk - 1, 0)).astype(BF16)

    cos2, sin2 = _rope_tables(pos_ref[0], invf_ref[...], sgn_ref[...])
    lane = lax.broadcasted_iota(jnp.int32, (nchunk, LANES), 1)
    kcmp = _dot(hidden(kc_ref, pek_ref, w1k_ref), w2k_ref[...])
    kcmp_ref[0] = _rope_slab(kcmp, cos2, sin2, (lane & 32) == 0).astype(BF16)
    vcmp_t = _dot_nt(w2vt_ref[...], hidden(vc_ref, pev_ref, w1v_ref)).astype(BF16)
    for g in range(NSA_KV_GROUPS):
        vcmpt_ref[0, g, 0:NSA_HEAD_DIM, :] = vcmp_t[g * NSA_HEAD_DIM:(g + 1) * NSA_HEAD_DIM]
        vcmpt_ref[0, g, NSA_HEAD_DIM:NSA_HEAD_DIM + 64, :] = ov_ref[...]
        vcmpt_ref[0, g, NSA_HEAD_DIM + 64:, :] = jnp.ones((NSA_ONES_ROWS, nchunk), BF16)


def _overlap_t(ncp, n_slc):
    cmp_starts = np.arange(ncp) * CMP_STRIDE
    slc_starts = np.arange(64) * SLC_BLOCK
    ov = ((cmp_starts[None, :] < slc_starts[:, None] + SLC_BLOCK)
          & (cmp_starts[None, :] + CMP_BLOCK > slc_starts[:, None])
          & (np.arange(64)[:, None] < n_slc))
    return jnp.asarray(ov.astype(np.float32), BF16)


def _compress_weights(pe, w1, w2):
    eye = jnp.eye(NSA_KV_GROUPS, dtype=F32)
    rows16 = CMP_BLOCK // 2
    w1r = w1.reshape(CMP_BLOCK, NSA_HEAD_DIM, CMP_HIDDEN)
    big = lambda part: jnp.einsum('ldn,gh->lgdhn', part, eye).reshape(
        rows16, NSA_KV_WIDTH, NSA_KV_GROUPS * CMP_HIDDEN)
    top, bot = big(w1r[:rows16]), big(w1r[rows16:])
    zero = jnp.zeros_like(top)
    w1big = jnp.concatenate([jnp.concatenate([top, zero], axis=2),
                             jnp.concatenate([zero, bot], axis=2)], axis=1).astype(BF16)
    w2big = jnp.einsum('nd,gh->gnhd', w2, eye).reshape(NSA_KV_GROUPS * CMP_HIDDEN, NSA_KV_WIDTH).astype(BF16)
    pe_rows = jnp.tile(pe, (1, NSA_KV_GROUPS))
    return pe_rows, w1big, w2big


def _compress(kc, vc, pos_cmp, pe_k, pe_v, w_ck1, w_ck2, w_cv1, w_cv2, invf, sgn):
    b, seq, _ = kc.shape
    nchunk = seq // CMP_STRIDE
    pek, w1k, w2k = _compress_weights(pe_k, w_ck1, w_ck2)
    pev, w1v, w2v = _compress_weights(pe_v, w_cv1, w_cv2)
    ins = [kc, vc, pos_cmp,
           pek, pev, w1k, w2k, w1v, w2v.T, invf, sgn, _overlap_t(nchunk, seq // SLC_BLOCK)]
    full = lambda a: pl.BlockSpec(a.shape, lambda i: (0,) * a.ndim)
    per_b = lambda n: pl.BlockSpec((1, nchunk, n), lambda i: (i, 0, 0))
    out_shape = [jax.ShapeDtypeStruct((b, nchunk, NSA_KV_WIDTH), BF16),
                 jax.ShapeDtypeStruct((b, NSA_KV_GROUPS, NSA_CMP_ROWS, nchunk), BF16)]
    return pl.pallas_call(
        functools.partial(_compress_kernel, nchunk=nchunk),
        grid=(b,),
        in_specs=[pl.BlockSpec((1, seq, NSA_KV_WIDTH), lambda i: (i, 0, 0))] * 2 + [per_b(1)]
        + [full(a) for a in ins[3:]],
        out_specs=[per_b(NSA_KV_WIDTH),
                   pl.BlockSpec((1, NSA_KV_GROUPS, NSA_CMP_ROWS, nchunk), lambda i: (i, 0, 0, 0))],
        out_shape=out_shape,
        compiler_params=pltpu.CompilerParams(dimension_semantics=("arbitrary",),
                                             vmem_limit_bytes=VMEM_LIMIT),
        name="compress",
    )(*ins)


def _nsa_kernel(qt_ref, kcmp_ref, vcmp_t_ref, ksel_ref, vsl_t_ref, kw_ref, vw_t_ref, gates_ref, sga_ref,
                out_ref, qz_scr, qa_scr, v_scr, rank_scr, m_scr, acc_scr, mix_scr,
                *, tq, kc, n_cmp, n_top):
    t0 = pl.program_id(1) * tq
    n = NSA_HPG * tq
    ncp = kcmp_ref.shape[1]
    hd = NSA_HEAD_DIM
    q_t = qt_ref[...]
    gates_t = gates_ref[0].T
    kq = lax.broadcasted_iota(jnp.int32, (kc, tq), 0) - lax.broadcasted_iota(jnp.int32, (kc, tq), 1)
    causal = kq <= 0
    newer = kq > 0

    def add_branch(branch, g, o_t):
        for h in range(NSA_HPG):
            head = NSA_HPG * g + h
            gate = gates_t[head * 3 + branch:head * 3 + branch + 1, :]
            term = gate * o_t[:, h * tq:(h + 1) * tq]
            r = pl.ds(head * hd, hd)
            if branch == 0:
                mix_scr[r, :] = term
            else:
                mix_scr[r, :] += term

    def reset_flash():
        m_scr[...] = jnp.full(m_scr.shape, NEG_INF, F32)
        acc_scr[...] = jnp.zeros(acc_scr.shape, F32)

    def flash_step(q_scr, k_ref, v_t_ref, chunks):
        chains = []
        for j, mask in chunks:
            start = pl.multiple_of(j * kc, kc)
            for g in range(NSA_KV_GROUPS):
                col0 = g * LANES if k_ref is ksel_ref else 0
                k = k_ref[0, pl.ds(start, kc), col0:col0 + LANES]
                for h in range(NSA_HPG):
                    s_t = _dot(k, q_scr[g, :, h * tq:(h + 1) * tq])
                    chains.append((j, mask, g, g * n + h * tq, s_t))
        for j, mask, g, c0, s_t in chains:
            if mask is not None:
                s_t = jnp.where(mask, s_t, NEG_INF)
            _flash_update_t(s_t, v_t_ref[0, j, g], m_scr, acc_scr, pl.ds(c0, tq))

    def finish_flash(branch):
        for g in range(NSA_KV_GROUPS):
            c = pl.ds(g * n, n)
            add_branch(branch, g, acc_scr[0:hd, c] / acc_scr[hd:hd + 1, c])

    zeros_q = jnp.zeros((hd, n), BF16)
    qgs = []
    for g in range(NSA_KV_GROUPS):
        qg = jnp.concatenate([q_t[(NSA_HPG * g + h) * hd:(NSA_HPG * g + h + 1) * hd, :] for h in range(NSA_HPG)],
                             axis=1)
        qgs.append(qg)
        qz_scr[g] = jnp.concatenate([qg, zeros_q] if g == 0 else [zeros_q, qg], axis=0)

    reset_flash()
    cq = (lax.broadcasted_iota(jnp.int32, (ncp, tq), 0) * CMP_STRIDE
          - lax.broadcasted_iota(jnp.int32, (ncp, tq), 1))
    cmp_valid = (cq <= t0 - (CMP_BLOCK - 1)) & (lax.broadcasted_iota(jnp.int32, (ncp, tq), 0) < n_cmp)
    chains = [(g * n + h * tq, _dot(kcmp_ref[0], qz_scr[g, :, h * tq:(h + 1) * tq]))
              for g in range(NSA_KV_GROUPS) for h in range(NSA_HPG)]
    for c0, s_t in chains:
        _flash_update_t(jnp.where(cmp_valid, s_t, NEG_INF), vcmp_t_ref[0, c0 // n], m_scr, acc_scr, pl.ds(c0, tq))
    seen = (t0 + lax.broadcasted_iota(jnp.int32, (1, tq), 1) >= CMP_BLOCK - 1).astype(F32)
    seen4 = jnp.concatenate([seen] * NSA_HPG, axis=1)
    imp_rows = slice(hd, hd + 64)
    den_row = slice(hd + 64, hd + 65)

    for g in range(NSA_KV_GROUPS):
        qg = qgs[g]
        c = pl.ds(g * n, n)
        inv_l = seen4 / acc_scr[den_row, c]
        add_branch(0, g, acc_scr[0:hd, c] * inv_l)
        pooled = acc_scr[imp_rows, c] * inv_l
        imp_t = functools.reduce(jnp.add, [pooled[:, h * tq:(h + 1) * tq] for h in range(NSA_HPG)])
        nb = lax.broadcasted_iota(jnp.int32, (64, tq), 0)
        tqv = t0 + lax.broadcasted_iota(jnp.int32, (64, tq), 1)
        cur = lax.shift_right_logical(tqv, 6)
        forced = (nb == 0) | (nb == cur) | (nb == cur - 1)
        v_scr[...] = jnp.where(nb * SLC_BLOCK <= tqv, jnp.where(forced, FORCE_SCORE, imp_t), -1.0)
        sub = lax.broadcasted_iota(jnp.int32, (8, tq), 0)
        vch = [v_scr[r * 8:(r + 1) * 8, :] for r in range(8)]
        rank_scr[...] = jnp.zeros(rank_scr.shape, jnp.int32)
        for mg in range(8):
            @pl.when(mg * 8 * SLC_BLOCK < t0 + tq)
            def _():
                count = [jnp.zeros((8, tq), jnp.int32) for _ in range(8)]
                for mblk in range(mg * 8, mg * 8 + 8):
                    vm = jnp.broadcast_to(v_scr[mblk:mblk + 1, :], (8, tq))
                    for r in range(8):
                        if r > mg:
                            before = vm >= vch[r]
                        elif r < mg:
                            before = vm > vch[r]
                        else:
                            before = (vm > vch[r]) | ((vm == vch[r]) & (sub > mblk - r * 8))
                        count[r] = count[r] + jnp.where(before, 1, 0)
                for r in range(8):
                    rank_scr[r * 8:(r + 1) * 8, :] += count[r]
        pen = jnp.where(rank_scr[...] < n_top, 0.0, SEL_PENALTY)
        pen4 = jnp.concatenate([pen.astype(BF16)] * NSA_HPG, axis=1)
        qa_scr[g] = jnp.concatenate([qg, pen4] if g == 0 else [pen4, qg], axis=0)

    jd = t0 // kc

    reset_flash()
    sel = functools.partial(flash_step, qa_scr, ksel_ref, vsl_t_ref)
    lax.fori_loop(0, jd // 2, lambda i, cr: (sel([(2 * i, None), (2 * i + 1, None)]), cr)[1], 0)

    @pl.when(jd % 2 == 0)
    def _():
        sel([(jd, causal)])

    @pl.when(jd % 2 == 1)
    def _():
        sel([(jd - 1, None), (jd, causal)])

    finish_flash(1)

    reset_flash()
    win = functools.partial(flash_step, qz_scr, kw_ref, vw_t_ref)
    assert WINDOW == 2 * kc

    @pl.when(jd == 0)
    def _():
        win([(jd, causal)])

    @pl.when(jd == 1)
    def _():
        win([(jd - 1, None), (jd, causal)])

    @pl.when(jd >= 2)
    def _():
        win([(jd - 2, newer), (jd - 1, None), (jd, causal)])

    finish_flash(2)

    out_ref[0] = (sga_ref[0].astype(F32) * mix_scr[...].T).astype(BF16)


def _nsa_attention(q_t, kcmp, vcmp_t, ksel, vsl_t, kw, vw_t, gates, sga):
    b, seq, _ = ksel.shape
    tq = min(NSA_TQ, seq)
    nq = seq // tq
    n_slc = seq // SLC_BLOCK
    assert n_slc <= 64 and seq % tq == 0 and WINDOW % tq == 0 and tq & (tq - 1) == 0
    ncp = kcmp.shape[1]
    n_cmp = (seq - CMP_BLOCK) // CMP_STRIDE + 1
    vsl_t = vsl_t.reshape((b, nq) + vsl_t.shape[1:])
    vw_t = vw_t.reshape((b, nq) + vw_t.shape[1:])
    n = NSA_HPG * tq
    tile = lambda w: pl.BlockSpec((1, tq, w), lambda bi, qi: (bi, qi, 0))
    per_b = lambda a: pl.BlockSpec((1,) + a.shape[1:], lambda bi, qi: (bi,) + (0,) * (a.ndim - 1))
    return pl.pallas_call(
        functools.partial(_nsa_kernel, tq=tq, kc=tq, n_cmp=n_cmp, n_top=min(SLC_TOPK, n_slc)),
        grid=(b, nq),
        in_specs=[pl.BlockSpec((NSA_WIDTH, tq), lambda bi, qi: (0, bi * nq + qi)),
                  per_b(kcmp), per_b(vcmp_t), per_b(ksel), per_b(vsl_t), per_b(kw), per_b(vw_t),
                  tile(LANES), tile(NSA_WIDTH)],
        out_specs=tile(NSA_WIDTH),
        out_shape=jax.ShapeDtypeStruct((b, seq, NSA_WIDTH), BF16),
        scratch_shapes=[pltpu.VMEM((NSA_KV_GROUPS, LANES, n), BF16),
                        pltpu.VMEM((NSA_KV_GROUPS, LANES, n), BF16),
                        pltpu.VMEM((64, tq), F32),
                        pltpu.VMEM((64, tq), jnp.int32),
                        pltpu.VMEM((1, NSA_KV_GROUPS * n), F32),
                        pltpu.VMEM((NSA_CMP_ROWS, NSA_KV_GROUPS * n), F32),
                        pltpu.VMEM((NSA_WIDTH, tq), F32)],
        compiler_params=pltpu.CompilerParams(dimension_semantics=("arbitrary", "arbitrary"),
                                             vmem_limit_bytes=VMEM_LIMIT),
        name="nsa_attn",
    )(q_t, kcmp, vcmp_t, ksel, vsl_t, kw, vw_t, gates, sga)


def _odd_proj_kernel(x_ref, mixa_ref, mixb_ref, posr_ref, wout_ref, g_ref, wc_ref, wgate_ref, qn_ref, kvn_ref,
                     wuqt_ref, wuk_ref, wuvt_ref, invfc_ref,
                     y_ref, qt_ref, kn_ref, kpe_ref, vt_ref, sg_ref, *, tm):
    y = (x_ref[...] + _dot(mixa_ref[...], wout_ref[0:NSA_WIDTH, :])
         + _dot(mixb_ref[...], wout_ref[NSA_WIDTH:, :]))
    y_ref[...] = y
    yn = _rms(y, g_ref[...]).astype(BF16)
    cos_t, sin_t = _rope_tables_t(invfc_ref[...], posr_ref[...])
    half = MLA_ROPE_DIM // 2

    def rope_t(x1, x2):
        return x1 * cos_t - x2 * sin_t, x2 * cos_t + x1 * sin_t

    seg = _dot(yn, wc_ref[...])
    kr_t = seg[:, MLA_Q_RANK + MLA_KV_RANK:].T
    kp1, kp2 = rope_t(kr_t[0:half], kr_t[half:2 * half])
    kpe_ref[...] = jnp.concatenate([kp1, kp2, kr_t[2 * half:]], axis=0).T.astype(BF16)
    cq = _rms(seg[:, 0:MLA_Q_RANK], qn_ref[...]).astype(BF16)
    ckv = _rms(seg[:, MLA_Q_RANK:MLA_Q_RANK + MLA_KV_RANK], kvn_ref[...]).astype(BF16)
    scale = (MLA_NOPE_DIM + MLA_ROPE_DIM) ** -0.5 * LOG2_E
    qt = _dot_nt(wuqt_ref[...], cq)
    kn_ref[...] = _dot(ckv, wuk_ref[...]).astype(BF16)
    for h in range(MLA_HEADS):
        c0 = h * MLA_QK_PAD
        r0 = c0 + MLA_NOPE_DIM
        qt_ref[c0:r0, :] = (qt[c0:r0] * scale).astype(BF16)
        q1, q2 = rope_t(qt[r0:r0 + half], qt[r0 + half:r0 + 2 * half])
        qt_ref[r0:r0 + half, :] = (q1 * scale).astype(BF16)
        qt_ref[r0 + half:r0 + 2 * half, :] = (q2 * scale).astype(BF16)
        qt_ref[r0 + 2 * half:c0 + MLA_QK_PAD, :] = jnp.zeros((MLA_QK_PAD - MLA_NOPE_DIM - 2 * half, tm), BF16)
    v_t = _dot_nt(wuvt_ref[...], ckv)
    kc = vt_ref.shape[-1]
    for h in range(MLA_HEADS):
        for c in range(tm // kc):
            vt_ref[0, h, c, 0:MLA_V_DIM, :] = v_t[h * MLA_V_DIM:(h + 1) * MLA_V_DIM, c * kc:(c + 1) * kc].astype(BF16)
            vt_ref[0, h, c, MLA_V_DIM:, :] = jnp.ones((MLA_ONES_ROWS, kc), BF16)
    sg_ref[...] = _silu(_dot(yn, wgate_ref[...])).astype(BF16)


def _odd_proj(x2d, mixa, mixb, pos_row, a_w_out, c_norm, c_w_in, q_norm, kv_norm, w_uq, w_ukv, invf_col, seq):
    t = x2d.shape[0]
    tm = min(PROJ_ROWS, seq)
    r2 = MLA_Q_RANK + MLA_KV_RANK + MLA_ROPE_DIM
    wc = jnp.pad(c_w_in[:, :r2], ((0, 0), (0, LANES - MLA_ROPE_DIM))).astype(BF16)
    wgate = c_w_in[:, r2:].astype(BF16)
    wuqt = jnp.pad(w_uq.reshape(MLA_Q_RANK, MLA_HEADS, MLA_NOPE_DIM + MLA_ROPE_DIM),
                   ((0, 0), (0, 0), (0, MLA_QK_PAD - MLA_NOPE_DIM - MLA_ROPE_DIM))
                   ).reshape(MLA_Q_RANK, MLA_HEADS * MLA_QK_PAD).T.astype(BF16)
    wukv = w_ukv.reshape(MLA_KV_RANK, MLA_HEADS, MLA_NOPE_DIM + MLA_V_DIM)
    wuk = wukv[:, :, :MLA_NOPE_DIM].reshape(MLA_KV_RANK, -1).astype(BF16)
    wuvt = wukv[:, :, MLA_NOPE_DIM:].reshape(MLA_KV_RANK, -1).T.astype(BF16)
    kc = min(MLA_TQ, seq)
    nq = seq // kc
    cps = tm // kc
    assert tm % kc == 0 and seq % tm == 0
    vrows = MLA_V_DIM + MLA_ONES_ROWS
    ins = [x2d, mixa, mixb, pos_row, a_w_out.astype(BF16), c_norm.reshape(1, -1), wc, wgate,
           q_norm.reshape(1, -1), kv_norm.reshape(1, -1), wuqt, wuk, wuvt, invf_col]
    full = lambda a: pl.BlockSpec(a.shape, lambda i: (0,) * a.ndim)
    rows = lambda n: pl.BlockSpec((tm, n), lambda i: (i, 0))
    cols = lambda n: pl.BlockSpec((n, tm), lambda i: (0, i))
    out_shapes = [
        jax.ShapeDtypeStruct((t, D_MODEL), F32),
        jax.ShapeDtypeStruct((MLA_HEADS * MLA_QK_PAD, t), BF16),
        jax.ShapeDtypeStruct((t, MLA_HEADS * MLA_NOPE_DIM), BF16),
        jax.ShapeDtypeStruct((t, LANES), BF16),
        jax.ShapeDtypeStruct((t // seq, MLA_HEADS, nq, vrows, kc), BF16),
        jax.ShapeDtypeStruct((t, MLA_WIDTH), BF16),
    ]
    steps = seq // tm
    vt_spec = pl.BlockSpec((1, MLA_HEADS, cps, vrows, kc), lambda i: (i // steps, 0, i % steps, 0, 0))
    out_specs = [rows(D_MODEL), cols(MLA_HEADS * MLA_QK_PAD), rows(MLA_HEADS * MLA_NOPE_DIM), rows(LANES), vt_spec,
                 rows(MLA_WIDTH)]
    return pl.pallas_call(
        functools.partial(_odd_proj_kernel, tm=tm),
        grid=(t // tm,),
        in_specs=[rows(D_MODEL), rows(NSA_WIDTH), rows(CONV_WIDTH), cols(1)] + [full(a) for a in ins[4:]],
        out_specs=out_specs,
        out_shape=out_shapes,
        compiler_params=pltpu.CompilerParams(dimension_semantics=("arbitrary",),
                                             vmem_limit_bytes=VMEM_LIMIT),
        name="odd_proj",
    )(*ins)


def _mla_kernel(qt_ref, kn_ref, kpe_ref, v_t_ref, o_ref, m_scr, acc_scr, *, tq, kc, hp):
    jd = pl.program_id(2)
    m_scr[...] = jnp.full(m_scr.shape, NEG_INF, F32)
    acc_scr[...] = jnp.zeros(acc_scr.shape, F32)
    causal = (lax.broadcasted_iota(jnp.int32, (kc, tq), 0) <= lax.broadcasted_iota(jnp.int32, (kc, tq), 1))

    def keys(h, start, size):
        return jnp.concatenate([kn_ref[0, pl.ds(start, size), h * MLA_NOPE_DIM:(h + 1) * MLA_NOPE_DIM],
                                kpe_ref[0, pl.ds(start, size), :]], axis=1)

    def step(j):
        start = pl.multiple_of(j * kc, kc)
        scores = []
        for h in range(hp):
            qk = slice(h * MLA_QK_PAD, (h + 1) * MLA_QK_PAD)
            scores.append(_dot(keys(h, start, kc), qt_ref[qk, :]))
        for h, s_t in enumerate(scores):
            _flash_update_t(s_t, v_t_ref[0, h, j], m_scr, acc_scr, pl.ds(h * tq, tq))

    lax.fori_loop(0, jd, lambda j, c: (step(j), c)[1], 0)

    hq = tq // 2
    start = pl.multiple_of(jd * kc, kc)
    tri = causal[0:hq, 0:hq]
    keep_hi = jnp.concatenate([jnp.ones((hq, hq), jnp.bool_), tri], axis=0)
    scores = []
    for h in range(hp):
        qk = slice(h * MLA_QK_PAD, (h + 1) * MLA_QK_PAD)
        scores.append((_dot(keys(h, start, hq), qt_ref[qk, 0:hq]),
                       _dot(keys(h, start, kc), qt_ref[qk, hq:tq])))
    for h, (s_lo, s_hi) in enumerate(scores):
        v_t = v_t_ref[0, h, jd]
        _flash_update_t(jnp.where(tri, s_lo, NEG_INF), v_t[:, 0:hq], m_scr, acc_scr, pl.ds(h * tq, hq))
        _flash_update_t(jnp.where(keep_hi, s_hi, NEG_INF), v_t, m_scr, acc_scr, pl.ds(h * tq + hq, hq))

    for h in range(hp):
        c = pl.ds(h * tq, tq)
        o_t = acc_scr[0:MLA_V_DIM, c] / acc_scr[MLA_V_DIM:MLA_V_DIM + 1, c]
        o_ref[0, :, h * MLA_V_DIM:(h + 1) * MLA_V_DIM] = o_t.T.astype(BF16)


def _mla_attention(q_t, k_nope, k_pe, v_t):
    b, seq, _ = k_pe.shape
    tq = min(MLA_TQ, seq)
    hp = MLA_HEADS_PER_STEP
    nq = seq // tq
    assert seq % tq == 0 and MLA_HEADS % hp == 0
    vrows = MLA_V_DIM + MLA_ONES_ROWS
    return pl.pallas_call(
        functools.partial(_mla_kernel, tq=tq, kc=tq, hp=hp),
        grid=(b, MLA_HEADS // hp, nq),
        in_specs=[pl.BlockSpec((hp * MLA_QK_PAD, tq), lambda bi, h, qi: (h, bi * nq + qi)),
                  pl.BlockSpec((1, seq, hp * MLA_NOPE_DIM), lambda bi, h, qi: (bi, 0, h)),
                  pl.BlockSpec((1, seq, LANES), lambda bi, h, qi: (bi, 0, 0)),
                  pl.BlockSpec((1, hp, nq, vrows, tq), lambda bi, h, qi: (bi, h, 0, 0, 0))],
        out_specs=pl.BlockSpec((1, tq, hp * MLA_V_DIM), lambda bi, h, qi: (bi, qi, h)),
        out_shape=jax.ShapeDtypeStruct((b, seq, MLA_WIDTH), BF16),
        scratch_shapes=[pltpu.VMEM((1, hp * tq), F32), pltpu.VMEM((vrows, hp * tq), F32)],
        compiler_params=pltpu.CompilerParams(dimension_semantics=("arbitrary", "arbitrary", "arbitrary"),
                                             vmem_limit_bytes=VMEM_LIMIT),
        name="mla_attn",
    )(q_t, k_nope, k_pe, v_t)


def _final_kernel(y_ref, o_ref, sg_ref, w_ref, g_ref, out_ref):
    gated = (sg_ref[...].astype(F32) * o_ref[...].astype(F32)).astype(BF16)
    out_ref[...] = _rms(y_ref[...] + _dot(gated, w_ref[...]), g_ref[...])


def _final(y, o, sg, w_out, final_norm, seq):
    t = y.shape[0]
    tm = min(PROJ_ROWS, seq)
    rows = lambda n: pl.BlockSpec((tm, n), lambda i: (i, 0))
    full = lambda a: pl.BlockSpec(a.shape, lambda i: (0,) * a.ndim)
    w = w_out.astype(BF16)
    g = final_norm.reshape(1, -1)
    return pl.pallas_call(
        _final_kernel,
        grid=(t // tm,),
        in_specs=[rows(D_MODEL), rows(MLA_WIDTH), rows(MLA_WIDTH), full(w), full(g)],
        out_specs=rows(D_MODEL),
        out_shape=jax.ShapeDtypeStruct((t, D_MODEL), F32),
        compiler_params=pltpu.CompilerParams(dimension_semantics=("arbitrary",),
                                             vmem_limit_bytes=VMEM_LIMIT),
        name="final",
    )(y, o, sg, w, g)


def _rope_constants():
    half = NSA_HEAD_DIM // 2
    inv_freq = ROPE_THETA ** (-jnp.arange(half, dtype=F32) / half)
    invf = jnp.tile(inv_freq, LANES // half).reshape(1, LANES)
    sgn = jnp.tile(jnp.concatenate([-jnp.ones((half,), F32), jnp.ones((half,), F32)]), LANES // (2 * half))
    return invf, sgn.reshape(1, LANES), inv_freq.reshape(half, 1)


def kernel(x, positions, a_norm, a_w_in, a_pe_k, a_pe_v, a_w_ck1, a_w_ck2, a_w_cv1, a_w_cv2, a_conv_w, a_w_out, c_norm, c_w_in, c_q_norm, c_kv_norm, c_w_uq, c_w_ukv, c_w_out, final_norm):
    b, seq, d = x.shape
    assert d == D_MODEL and NSA_HEAD_DIM == MLA_ROPE_DIM
    assert a_norm.shape[0] == 1 and c_norm.shape[0] == 1
    t = b * seq
    invf, sgn, invf_col = _rope_constants()
    x2d = x.reshape(t, d)
    pos_row = positions.reshape(1, t)

    q_t, kc, vc, ksel, vsl_t, kw, vw_t, gates, sga, mixb = _even_proj(
        x2d, pos_row, a_norm[0], a_w_in[0], a_conv_w[0], invf_col, seq)
    nchunk = seq // CMP_STRIDE
    pos_cmp = jnp.pad(positions[:, CMP_BLOCK - 1::CMP_STRIDE], ((0, 0), (0, 1)))[:, :nchunk, None]
    kcmp, vcmp_t = _compress(kc.reshape(b, seq, -1), vc.reshape(b, seq, -1), pos_cmp, a_pe_k[0], a_pe_v[0],
                             a_w_ck1[0], a_w_ck2[0], a_w_cv1[0], a_w_cv2[0], invf, sgn)
    r3 = lambda a: a.reshape(b, seq, a.shape[-1])
    mixa = _nsa_attention(q_t, kcmp, vcmp_t, r3(ksel), vsl_t, r3(kw), vw_t, r3(gates), r3(sga))

    y, mq_t, mkn, mkpe, mv_t, sg = _odd_proj(x2d, mixa.reshape(t, -1), mixb, pos_row, a_w_out[0], c_norm[0],
                                      c_w_in[0], c_q_norm[0], c_kv_norm[0], c_w_uq[0], c_w_ukv[0], invf_col, seq)
    o = _mla_attention(mq_t, r3(mkn), r3(mkpe), mv_t)
    out = _final(y, o.reshape(t, -1), sg, c_w_out[0], final_norm, seq)
    return out.reshape(b, seq, d)
```

```python
import functools

import jax
import jax.numpy as jnp
import numpy as np
from jax import lax
from jax.experimental import pallas as pl
from jax.experimental.pallas import tpu as pltpu

F32 = jnp.float32
BF16 = jnp.bfloat16

D_MODEL = 1024
ROPE_THETA = 10000.0
RMS_EPS = 1e-6
NEG_INF = -1e30
FORCE_SCORE = 1e4
SEL_PENALTY = NEG_INF

NSA_HEADS = 8
NSA_KV_GROUPS = 2
NSA_HPG = NSA_HEADS // NSA_KV_GROUPS
NSA_HEAD_DIM = 64
NSA_WIDTH = NSA_HEADS * NSA_HEAD_DIM
NSA_KV_WIDTH = NSA_KV_GROUPS * NSA_HEAD_DIM
CMP_BLOCK = 32
CMP_STRIDE = 16
CMP_HIDDEN = 2 * NSA_HEAD_DIM
SLC_BLOCK = 64
SLC_TOPK = 16
SLC_SHIFT = SLC_BLOCK.bit_length() - 1
SLC_MAX_BLOCKS = 64
WINDOW = 512
CONV_WIDTH = D_MODEL - NSA_WIDTH
CONV_K = 3

MLA_HEADS = 8
MLA_NOPE_DIM = 128
MLA_ROPE_DIM = 64
MLA_V_DIM = 128
MLA_Q_RANK = 256
MLA_KV_RANK = 256
MLA_WIDTH = MLA_HEADS * MLA_V_DIM
MLA_QK_PAD = 256

LANES = 128
SUBLANES = 8
VMEM_LIMIT = 56 * 1024 * 1024

PROJ_ROWS = 1024
NSA_TQ = 256
NSA_TILES_PER_STEP = 2
MLA_TQ = 512
MLA_HEADS_PER_STEP = 8
MLA_ONES_ROWS = 16
NSA_ONES_ROWS = 16
NSA_CMP_ROWS = NSA_HEAD_DIM + SLC_MAX_BLOCKS + NSA_ONES_ROWS
LOG2_E = 1.4426950408889634


def _dot(a, b):
    return jnp.dot(a, b, preferred_element_type=F32)


def _dot_nt(a, b):
    return lax.dot_general(a, b, (((1,), (1,)), ((), ())), preferred_element_type=F32)


def _silu(x):
    return x * jax.nn.sigmoid(x)


def _rms(x, g):
    return x * lax.rsqrt(jnp.mean(x * x, axis=-1, keepdims=True) + RMS_EPS) * g


def _rope_tables(pos_col, invf_row, sign_row):
    ang = pos_col.astype(F32) * invf_row
    return jnp.cos(ang), jnp.sin(ang) * sign_row


def _rope_slab(s, cos2, sin2):
    half = NSA_HEAD_DIM // 2
    first_half = (lax.broadcasted_iota(jnp.int32, s.shape, 1) & half) == 0
    swapped = jnp.where(first_half, pltpu.roll(s, LANES - half, 1), pltpu.roll(s, half, 1))
    return s * cos2 + swapped * sin2


def _rope_tables_t(invf_col, pos_row):
    ang = invf_col * pos_row.astype(F32)
    return jnp.cos(ang), jnp.sin(ang)


def _flash_update_t(s_t, v_t, m_ref, acc_ref, cols):
    m_old = m_ref[:, cols]
    m_new = jnp.maximum(m_old, jnp.max(s_t, axis=0, keepdims=True))
    p = jnp.exp2(s_t - m_new)
    rows = v_t.shape[0]
    acc_ref[0:rows, cols] = jnp.exp2(m_old - m_new) * acc_ref[0:rows, cols] + _dot(v_t, p.astype(BF16))
    m_ref[:, cols] = m_new


def _even_proj_kernel(x_ref, posr_ref, g_ref, wqt_ref, wkv_ref, wg_ref, wga_ref, wcb_ref, wcc_ref, wch_ref,
                      wgb_ref, convw_ref, invfc_ref,
                      qt_ref, kc_ref, vc_ref, ksel_ref, vslt_ref, kw_ref, vwt_ref, gates_ref, sga_ref,
                      mixb_ref, carry_ref, *, tiles_per_seq, tm, kc):
    i = pl.program_id(0)
    xn = _rms(x_ref[...], g_ref[...]).astype(BF16)
    cos_t, sin_t = _rope_tables_t(invfc_ref[...], posr_ref[...])
    half = NSA_HEAD_DIM // 2
    lane = lax.broadcasted_iota(jnp.int32, (tm, LANES), 1)
    low = lane < NSA_HEAD_DIM

    def rope_t(x_t):
        out = []
        for r in range(0, x_t.shape[0], 2 * half):
            x1, x2 = x_t[r:r + half], x_t[r + half:r + 2 * half]
            out += [x1 * cos_t - x2 * sin_t, x2 * cos_t + x1 * sin_t]
        return jnp.concatenate(out, axis=0)

    def store_values_t(ref, v):
        v_t = v.T.astype(BF16)
        ones = jnp.ones((NSA_ONES_ROWS, kc), BF16)
        for c in range(tm // kc):
            for g in range(NSA_KV_GROUPS):
                ref[c, g, 0:NSA_HEAD_DIM, :] = v_t[g * NSA_HEAD_DIM:(g + 1) * NSA_HEAD_DIM, c * kc:(c + 1) * kc]
                ref[c, g, NSA_HEAD_DIM:, :] = ones

    scale = NSA_HEAD_DIM ** -0.5 * LOG2_E
    qt_ref[...] = (rope_t(_dot_nt(wqt_ref[...], xn)) * scale).astype(BF16)

    seg = _dot(xn, wkv_ref[...])
    kc_ref[...] = seg[:, 0:LANES]
    vc_ref[...] = seg[:, LANES:2 * LANES]
    ksl = rope_t(seg[:, 2 * LANES:3 * LANES].T).T
    store_values_t(vslt_ref, seg[:, 3 * LANES:4 * LANES])
    kw_ref[...] = rope_t(seg[:, 4 * LANES:5 * LANES].T).T.astype(BF16)
    store_values_t(vwt_ref, seg[:, 5 * LANES:6 * LANES])

    spos = (i % tiles_per_seq) * tm + lax.broadcasted_iota(jnp.int32, (tm, LANES), 0)
    blk = lax.shift_right_logical(spos, SLC_SHIFT)
    ksel_ref[:, 0:LANES] = jnp.where(low, ksl, (lane - NSA_HEAD_DIM == blk).astype(F32)).astype(BF16)
    ksel_ref[:, LANES:2 * LANES] = jnp.where(low, (lane == blk).astype(F32), ksl).astype(BF16)

    gates_ref[...] = jax.nn.sigmoid(_dot(xn, wg_ref[...]))
    sga_ref[...] = _silu(_dot(xn, wga_ref[...])).astype(BF16)

    u = _dot(xn, wcc_ref[...]) * _dot(xn, wch_ref[...])

    @pl.when(i % tiles_per_seq == 0)
    def _():
        carry_ref[...] = jnp.zeros_like(carry_ref)

    row = lax.broadcasted_iota(jnp.int32, (tm, CONV_WIDTH), 0)
    prev1 = carry_ref[SUBLANES - 1:SUBLANES, :]
    prev2 = carry_ref[SUBLANES - 2:SUBLANES - 1, :]
    u1 = jnp.where(row == 0, prev1, pltpu.roll(u, 1, 0))
    u2 = jnp.where(row == 0, prev2, jnp.where(row == 1, prev1, pltpu.roll(u, 2, 0)))
    carry_ref[...] = u[tm - SUBLANES:tm, :]
    w = convw_ref[...]
    y = w[0:1, :] * u2 + w[1:2, :] * u1 + w[2:3, :] * u
    cb = _dot(xn, wcb_ref[...])
    mixb_ref[...] = (_silu(_dot(xn, wgb_ref[...])) * (cb * y)).astype(BF16)


def _even_proj(x2d, pos_row, norm_g, w_in, conv_w, invf_col, seq):
    t = x2d.shape[0]
    tm = min(PROJ_ROWS, seq)
    kc = min(NSA_TQ, seq)
    assert seq % tm == 0 and t % tm == 0 and tm % kc == 0 and conv_w.shape[0] == CONV_K == 3
    sizes = (NSA_WIDTH,) + (NSA_KV_WIDTH,) * 6 + (3 * NSA_HEADS, NSA_WIDTH) + (CONV_WIDTH,) * 4
    offs = np.concatenate([[0], np.cumsum(sizes)])
    col = lambda k: w_in[:, offs[k]:offs[k + 1]]
    wqt = col(0).T.astype(BF16)
    wkv = jnp.concatenate([col(k) for k in range(1, 7)], axis=1).astype(BF16)
    wg = jnp.pad(col(7), ((0, 0), (0, LANES - 3 * NSA_HEADS))).astype(BF16)
    wga, wcb, wcc, wch, wgb = (col(k).astype(BF16) for k in range(8, 13))

    full = lambda a: pl.BlockSpec(a.shape, lambda i: (0,) * a.ndim)
    rows = lambda n: pl.BlockSpec((tm, n), lambda i: (i, 0))
    cols = lambda n: pl.BlockSpec((n, tm), lambda i: (0, i))
    vrows = NSA_HEAD_DIM + NSA_ONES_ROWS
    values_t = jax.ShapeDtypeStruct((t // kc, NSA_KV_GROUPS, vrows, kc), BF16)
    values_t_spec = pl.BlockSpec((tm // kc, NSA_KV_GROUPS, vrows, kc), lambda i: (i, 0, 0, 0))
    out_shapes = [
        jax.ShapeDtypeStruct((NSA_WIDTH, t), BF16),
        jax.ShapeDtypeStruct((t, NSA_KV_WIDTH), F32),
        jax.ShapeDtypeStruct((t, NSA_KV_WIDTH), F32),
        jax.ShapeDtypeStruct((t, 2 * LANES), BF16),
        values_t,
        jax.ShapeDtypeStruct((t, NSA_KV_WIDTH), BF16),
        values_t,
        jax.ShapeDtypeStruct((t, LANES), F32),
        jax.ShapeDtypeStruct((t, NSA_WIDTH), BF16),
        jax.ShapeDtypeStruct((t, CONV_WIDTH), BF16),
    ]
    out_specs = [cols(NSA_WIDTH), rows(NSA_KV_WIDTH), rows(NSA_KV_WIDTH), rows(2 * LANES), values_t_spec,
                 rows(NSA_KV_WIDTH), values_t_spec, rows(LANES), rows(NSA_WIDTH), rows(CONV_WIDTH)]
    ins = [x2d, pos_row, norm_g.reshape(1, -1), wqt, wkv, wg, wga, wcb, wcc, wch, wgb, conv_w, invf_col]
    in_specs = [rows(D_MODEL), cols(1)] + [full(a) for a in ins[2:]]
    return pl.pallas_call(
        functools.partial(_even_proj_kernel, tiles_per_seq=seq // tm, tm=tm, kc=kc),
        grid=(t // tm,),
        in_specs=in_specs,
        out_specs=out_specs,
        out_shape=out_shapes,
        scratch_shapes=[pltpu.VMEM((SUBLANES, CONV_WIDTH), F32)],
        compiler_params=pltpu.CompilerParams(dimension_semantics=("arbitrary",),
                                             vmem_limit_bytes=VMEM_LIMIT),
        name="even_proj",
    )(*ins)


def _compress_kernel(kc_ref, vc_ref, pos_ref, pek_ref, pev_ref, w1k_ref, w2k_ref, w1v_ref, w2vt_ref,
                     invf_ref, sgn_ref, ov_ref, kcmp_ref, vcmpt_ref, *, nchunk):
    half = NSA_KV_GROUPS * CMP_HIDDEN
    rows16 = CMP_BLOCK // 2

    def hidden(raw_ref, pe_ref, w1_ref):
        acc = jnp.zeros((nchunk, 2 * half), F32)
        for l in range(rows16):
            x = raw_ref[0, pl.ds(l, nchunk, stride=CMP_STRIDE), :]
            lhs = jnp.concatenate([(x + pe_ref[l:l + 1, :]).astype(BF16),
                                   (x + pe_ref[rows16 + l:rows16 + l + 1, :]).astype(BF16)], axis=1)
            acc = acc + _dot(lhs, w1_ref[l])
        return _silu(acc[:, 0:half] + pltpu.roll(acc[:, half:], nchunk - 1, 0)).astype(BF16)

    cos2, sin2 = _rope_tables(pos_ref[0], invf_ref[...], sgn_ref[...])
    kcmp = _dot(hidden(kc_ref, pek_ref, w1k_ref), w2k_ref[...])
    kcmp_ref[0] = _rope_slab(kcmp, cos2, sin2).astype(BF16)
    vcmp_t = _dot_nt(w2vt_ref[...], hidden(vc_ref, pev_ref, w1v_ref)).astype(BF16)
    for g in range(NSA_KV_GROUPS):
        vcmpt_ref[0, g, 0:NSA_HEAD_DIM, :] = vcmp_t[g * NSA_HEAD_DIM:(g + 1) * NSA_HEAD_DIM]
        vcmpt_ref[0, g, NSA_HEAD_DIM:NSA_HEAD_DIM + SLC_MAX_BLOCKS, :] = ov_ref[...]
        vcmpt_ref[0, g, NSA_HEAD_DIM + SLC_MAX_BLOCKS:, :] = jnp.ones((NSA_ONES_ROWS, nchunk), BF16)


def _overlap_t(ncp, n_slc):
    cmp_starts = np.arange(ncp) * CMP_STRIDE
    slc_starts = np.arange(SLC_MAX_BLOCKS) * SLC_BLOCK
    ov = ((cmp_starts[None, :] < slc_starts[:, None] + SLC_BLOCK)
          & (cmp_starts[None, :] + CMP_BLOCK > slc_starts[:, None])
          & (np.arange(SLC_MAX_BLOCKS)[:, None] < n_slc))
    return jnp.asarray(ov.astype(np.float32), BF16)


def _compress_weights(pe, w1, w2):
    eye = jnp.eye(NSA_KV_GROUPS, dtype=F32)
    rows16 = CMP_BLOCK // 2
    w1r = w1.reshape(CMP_BLOCK, NSA_HEAD_DIM, CMP_HIDDEN)
    big = lambda part: jnp.einsum('ldn,gh->lgdhn', part, eye).reshape(
        rows16, NSA_KV_WIDTH, NSA_KV_GROUPS * CMP_HIDDEN)
    top, bot = big(w1r[:rows16]), big(w1r[rows16:])
    zero = jnp.zeros_like(top)
    w1big = jnp.concatenate([jnp.concatenate([top, zero], axis=2),
                             jnp.concatenate([zero, bot], axis=2)], axis=1).astype(BF16)
    w2big = jnp.einsum('nd,gh->gnhd', w2, eye).reshape(NSA_KV_GROUPS * CMP_HIDDEN, NSA_KV_WIDTH).astype(BF16)
    pe_rows = jnp.tile(pe, (1, NSA_KV_GROUPS))
    return pe_rows, w1big, w2big


def _compress(kc, vc, pos_cmp, pe_k, pe_v, w_ck1, w_ck2, w_cv1, w_cv2, invf, sgn):
    b, seq, _ = kc.shape
    nchunk = seq // CMP_STRIDE
    pek, w1k, w2k = _compress_weights(pe_k, w_ck1, w_ck2)
    pev, w1v, w2v = _compress_weights(pe_v, w_cv1, w_cv2)
    ins = [kc, vc, pos_cmp,
           pek, pev, w1k, w2k, w1v, w2v.T, invf, sgn, _overlap_t(nchunk, seq // SLC_BLOCK)]
    full = lambda a: pl.BlockSpec(a.shape, lambda i: (0,) * a.ndim)
    per_b = lambda n: pl.BlockSpec((1, nchunk, n), lambda i: (i, 0, 0))
    out_shape = [jax.ShapeDtypeStruct((b, nchunk, NSA_KV_WIDTH), BF16),
                 jax.ShapeDtypeStruct((b, NSA_KV_GROUPS, NSA_CMP_ROWS, nchunk), BF16)]
    return pl.pallas_call(
        functools.partial(_compress_kernel, nchunk=nchunk),
        grid=(b,),
        in_specs=[pl.BlockSpec((1, seq, NSA_KV_WIDTH), lambda i: (i, 0, 0))] * 2 + [per_b(1)]
        + [full(a) for a in ins[3:]],
        out_specs=[per_b(NSA_KV_WIDTH),
                   pl.BlockSpec((1, NSA_KV_GROUPS, NSA_CMP_ROWS, nchunk), lambda i: (i, 0, 0, 0))],
        out_shape=out_shape,
        compiler_params=pltpu.CompilerParams(dimension_semantics=("arbitrary",),
                                             vmem_limit_bytes=VMEM_LIMIT),
        name="compress",
    )(*ins)


def _nsa_kernel(qt_ref, kcmp_ref, vcmp_t_ref, ksel_ref, vsl_t_ref, kw_ref, vw_t_ref, gates_ref, sga_ref,
                out_ref, qz_scr, qa_scr, v_scr, rank_scr, m_scr, acc_scr, mix_scr,
                *, tq, kc, n_cmp, n_top, tiles):
    for sub in range(tiles):
        _nsa_tile(pl.program_id(1) * tiles + sub, pl.ds(sub * tq, tq),
                  qt_ref, kcmp_ref, vcmp_t_ref, ksel_ref, vsl_t_ref, kw_ref, vw_t_ref, gates_ref, sga_ref,
                  out_ref, qz_scr, qa_scr, v_scr, rank_scr, m_scr, acc_scr, mix_scr,
                  tq=tq, kc=kc, n_cmp=n_cmp, n_top=n_top)


def _nsa_tile(tile_idx, tile_rows, qt_ref, kcmp_ref, vcmp_t_ref, ksel_ref, vsl_t_ref, kw_ref, vw_t_ref, gates_ref,
              sga_ref, out_ref, qz_scr, qa_scr, v_scr, rank_scr, m_scr, acc_scr, mix_scr, *, tq, kc, n_cmp, n_top):
    t0 = tile_idx * tq
    n = NSA_HPG * tq
    ncp = kcmp_ref.shape[1]
    hd = NSA_HEAD_DIM
    q_t = qt_ref[:, tile_rows]
    gates_t = gates_ref[0, tile_rows, :].T
    kq = lax.broadcasted_iota(jnp.int32, (kc, tq), 0) - lax.broadcasted_iota(jnp.int32, (kc, tq), 1)
    causal = kq <= 0
    newer = kq > 0

    def add_branch(branch, g, o_t):
        for h in range(NSA_HPG):
            head = NSA_HPG * g + h
            gate = gates_t[head * 3 + branch:head * 3 + branch + 1, :]
            term = gate * o_t[:, h * tq:(h + 1) * tq]
            r = pl.ds(head * hd, hd)
            if branch == 0:
                mix_scr[r, :] = term
            else:
                mix_scr[r, :] += term

    def reset_flash():
        m_scr[...] = jnp.full(m_scr.shape, NEG_INF, F32)
        acc_scr[...] = jnp.zeros(acc_scr.shape, F32)

    def flash_step(q_scr, k_ref, v_t_ref, chunks):
        chains = []
        for j, mask in chunks:
            start = pl.multiple_of(j * kc, kc)
            for g in range(NSA_KV_GROUPS):
                col0 = g * LANES if k_ref is ksel_ref else 0
                k = k_ref[0, pl.ds(start, kc), col0:col0 + LANES]
                for h in range(NSA_HPG):
                    s_t = _dot(k, q_scr[g, :, h * tq:(h + 1) * tq])
                    chains.append((j, mask, g, g * n + h * tq, s_t))
        for j, mask, g, c0, s_t in chains:
            if mask is not None:
                s_t = jnp.where(mask, s_t, NEG_INF)
            _flash_update_t(s_t, v_t_ref[0, j, g], m_scr, acc_scr, pl.ds(c0, tq))

    def finish_flash(branch):
        for g in range(NSA_KV_GROUPS):
            c = pl.ds(g * n, n)
            add_branch(branch, g, acc_scr[0:hd, c] / acc_scr[hd:hd + 1, c])

    zeros_q = jnp.zeros((hd, n), BF16)
    qgs = []
    for g in range(NSA_KV_GROUPS):
        qg = jnp.concatenate([q_t[(NSA_HPG * g + h) * hd:(NSA_HPG * g + h + 1) * hd, :] for h in range(NSA_HPG)],
                             axis=1)
        qgs.append(qg)
        qz_scr[g] = jnp.concatenate([qg, zeros_q] if g == 0 else [zeros_q, qg], axis=0)

    reset_flash()
    cq = (lax.broadcasted_iota(jnp.int32, (ncp, tq), 0) * CMP_STRIDE
          - lax.broadcasted_iota(jnp.int32, (ncp, tq), 1))
    cmp_valid = (cq <= t0 - (CMP_BLOCK - 1)) & (lax.broadcasted_iota(jnp.int32, (ncp, tq), 0) < n_cmp)
    chains = [(g * n + h * tq, _dot(kcmp_ref[0], qz_scr[g, :, h * tq:(h + 1) * tq]))
              for g in range(NSA_KV_GROUPS) for h in range(NSA_HPG)]
    for c0, s_t in chains:
        _flash_update_t(jnp.where(cmp_valid, s_t, NEG_INF), vcmp_t_ref[0, c0 // n], m_scr, acc_scr, pl.ds(c0, tq))
    seen = (t0 + lax.broadcasted_iota(jnp.int32, (1, tq), 1) >= CMP_BLOCK - 1).astype(F32)
    seen4 = jnp.concatenate([seen] * NSA_HPG, axis=1)
    imp_rows = slice(hd, hd + SLC_MAX_BLOCKS)
    den_row = slice(hd + SLC_MAX_BLOCKS, hd + SLC_MAX_BLOCKS + 1)

    for g in range(NSA_KV_GROUPS):
        qg = qgs[g]
        c = pl.ds(g * n, n)
        inv_l = seen4 / acc_scr[den_row, c]
        add_branch(0, g, acc_scr[0:hd, c] * inv_l)
        pooled = acc_scr[imp_rows, c] * inv_l
        imp_t = functools.reduce(jnp.add, [pooled[:, h * tq:(h + 1) * tq] for h in range(NSA_HPG)])
        nb = lax.broadcasted_iota(jnp.int32, (SLC_MAX_BLOCKS, tq), 0)
        tqv = t0 + lax.broadcasted_iota(jnp.int32, (SLC_MAX_BLOCKS, tq), 1)
        cur = lax.shift_right_logical(tqv, SLC_SHIFT)
        forced = (nb == 0) | (nb == cur) | (nb == cur - 1)
        v_scr[...] = jnp.where(nb * SLC_BLOCK <= tqv, jnp.where(forced, FORCE_SCORE, imp_t), -1.0)
        groups = SLC_MAX_BLOCKS // SUBLANES
        sub = lax.broadcasted_iota(jnp.int32, (SUBLANES, tq), 0)
        vch = [v_scr[r * SUBLANES:(r + 1) * SUBLANES, :] for r in range(groups)]
        rank_scr[...] = jnp.zeros(rank_scr.shape, jnp.int32)
        for mg in range(groups):
            @pl.when(mg * SUBLANES * SLC_BLOCK < t0 + tq)
            def _():
                count = [jnp.zeros((SUBLANES, tq), jnp.int32) for _ in range(groups)]
                for mblk in range(mg * SUBLANES, (mg + 1) * SUBLANES):
                    vm = jnp.broadcast_to(v_scr[mblk:mblk + 1, :], (SUBLANES, tq))
                    for r in range(groups):
                        if r > mg:
                            before = vm >= vch[r]
                        elif r < mg:
                            before = vm > vch[r]
                        else:
                            before = (vm > vch[r]) | ((vm == vch[r]) & (sub > mblk - r * SUBLANES))
                        count[r] = count[r] + jnp.where(before, 1, 0)
                for r in range(groups):
                    rank_scr[r * SUBLANES:(r + 1) * SUBLANES, :] += count[r]
        pen = jnp.where(rank_scr[...] < n_top, 0.0, SEL_PENALTY)
        pen4 = jnp.concatenate([pen.astype(BF16)] * NSA_HPG, axis=1)
        qa_scr[g] = jnp.concatenate([qg, pen4] if g == 0 else [pen4, qg], axis=0)

    jd = t0 // kc

    reset_flash()
    sel = functools.partial(flash_step, qa_scr, ksel_ref, vsl_t_ref)
    lax.fori_loop(0, jd // 2, lambda i, cr: (sel([(2 * i, None), (2 * i + 1, None)]), cr)[1], 0)

    @pl.when(jd % 2 == 0)
    def _():
        sel([(jd, causal)])

    @pl.when(jd % 2 == 1)
    def _():
        sel([(jd - 1, None), (jd, causal)])

    finish_flash(1)

    reset_flash()
    win = functools.partial(flash_step, qz_scr, kw_ref, vw_t_ref)
    assert WINDOW == 2 * kc

    @pl.when(jd == 0)
    def _():
        win([(jd, causal)])

    @pl.when(jd == 1)
    def _():
        win([(jd - 1, None), (jd, causal)])

    @pl.when(jd >= 2)
    def _():
        win([(jd - 2, newer), (jd - 1, None), (jd, causal)])

    finish_flash(2)

    out_ref[0, tile_rows, :] = (sga_ref[0, tile_rows, :].astype(F32) * mix_scr[...].T).astype(BF16)


def _nsa_attention(q_t, kcmp, vcmp_t, ksel, vsl_t, kw, vw_t, gates, sga):
    b, seq, _ = ksel.shape
    tq = min(NSA_TQ, seq)
    nq = seq // tq
    n_slc = seq // SLC_BLOCK
    assert n_slc <= SLC_MAX_BLOCKS and seq % tq == 0 and WINDOW % tq == 0 and tq & (tq - 1) == 0
    ncp = kcmp.shape[1]
    n_cmp = (seq - CMP_BLOCK) // CMP_STRIDE + 1
    vsl_t = vsl_t.reshape((b, nq) + vsl_t.shape[1:])
    vw_t = vw_t.reshape((b, nq) + vw_t.shape[1:])
    n = NSA_HPG * tq
    tiles = NSA_TILES_PER_STEP if nq % NSA_TILES_PER_STEP == 0 else 1
    tile = lambda w: pl.BlockSpec((1, tiles * tq, w), lambda bi, qi: (bi, qi, 0))
    per_b = lambda a: pl.BlockSpec((1,) + a.shape[1:], lambda bi, qi: (bi,) + (0,) * (a.ndim - 1))
    return pl.pallas_call(
        functools.partial(_nsa_kernel, tq=tq, kc=tq, n_cmp=n_cmp, n_top=min(SLC_TOPK, n_slc), tiles=tiles),
        grid=(b, nq // tiles),
        in_specs=[pl.BlockSpec((NSA_WIDTH, tiles * tq), lambda bi, qi: (0, bi * (nq // tiles) + qi)),
                  per_b(kcmp), per_b(vcmp_t), per_b(ksel), per_b(vsl_t), per_b(kw), per_b(vw_t),
                  tile(LANES), tile(NSA_WIDTH)],
        out_specs=tile(NSA_WIDTH),
        out_shape=jax.ShapeDtypeStruct((b, seq, NSA_WIDTH), BF16),
        scratch_shapes=[pltpu.VMEM((NSA_KV_GROUPS, LANES, n), BF16),
                        pltpu.VMEM((NSA_KV_GROUPS, LANES, n), BF16),
                        pltpu.VMEM((SLC_MAX_BLOCKS, tq), F32),
                        pltpu.VMEM((SLC_MAX_BLOCKS, tq), jnp.int32),
                        pltpu.VMEM((1, NSA_KV_GROUPS * n), F32),
                        pltpu.VMEM((NSA_CMP_ROWS, NSA_KV_GROUPS * n), F32),
                        pltpu.VMEM((NSA_WIDTH, tq), F32)],
        compiler_params=pltpu.CompilerParams(dimension_semantics=("arbitrary", "arbitrary"),
                                             vmem_limit_bytes=VMEM_LIMIT),
        name="nsa_attn",
    )(q_t, kcmp, vcmp_t, ksel, vsl_t, kw, vw_t, gates, sga)


def _odd_proj_kernel(x_ref, mixa_ref, mixb_ref, posr_ref, wout_ref, g_ref, wc_ref, wgate_ref, qn_ref, kvn_ref,
                     wuqt_ref, wuk_ref, wuvt_ref, invfc_ref,
                     y_ref, qt_ref, kn_ref, kpe_ref, vt_ref, sg_ref, *, tm):
    y = (x_ref[...] + _dot(mixa_ref[...], wout_ref[0:NSA_WIDTH, :])
         + _dot(mixb_ref[...], wout_ref[NSA_WIDTH:, :]))
    y_ref[...] = y
    yn = _rms(y, g_ref[...]).astype(BF16)
    cos_t, sin_t = _rope_tables_t(invfc_ref[...], posr_ref[...])
    half = MLA_ROPE_DIM // 2

    def rope_t(x1, x2):
        return x1 * cos_t - x2 * sin_t, x2 * cos_t + x1 * sin_t

    seg = _dot(yn, wc_ref[...])
    kr_t = seg[:, MLA_Q_RANK + MLA_KV_RANK:].T
    kp1, kp2 = rope_t(kr_t[0:half], kr_t[half:2 * half])
    kpe_ref[...] = jnp.concatenate([kp1, kp2, kr_t[2 * half:]], axis=0).T.astype(BF16)
    cq = _rms(seg[:, 0:MLA_Q_RANK], qn_ref[...]).astype(BF16)
    ckv = _rms(seg[:, MLA_Q_RANK:MLA_Q_RANK + MLA_KV_RANK], kvn_ref[...]).astype(BF16)
    scale = (MLA_NOPE_DIM + MLA_ROPE_DIM) ** -0.5 * LOG2_E
    qt = _dot_nt(wuqt_ref[...], cq)
    kn_ref[...] = _dot(ckv, wuk_ref[...]).astype(BF16)
    for h in range(MLA_HEADS):
        c0 = h * MLA_QK_PAD
        r0 = c0 + MLA_NOPE_DIM
        qt_ref[c0:r0, :] = (qt[c0:r0] * scale).astype(BF16)
        q1, q2 = rope_t(qt[r0:r0 + half], qt[r0 + half:r0 + 2 * half])
        qt_ref[r0:r0 + half, :] = (q1 * scale).astype(BF16)
        qt_ref[r0 + half:r0 + 2 * half, :] = (q2 * scale).astype(BF16)
        qt_ref[r0 + 2 * half:c0 + MLA_QK_PAD, :] = jnp.zeros((MLA_QK_PAD - MLA_NOPE_DIM - 2 * half, tm), BF16)
    v_t = _dot_nt(wuvt_ref[...], ckv)
    kc = vt_ref.shape[-1]
    for h in range(MLA_HEADS):
        for c in range(tm // kc):
            vt_ref[0, h, c, 0:MLA_V_DIM, :] = v_t[h * MLA_V_DIM:(h + 1) * MLA_V_DIM, c * kc:(c + 1) * kc].astype(BF16)
            vt_ref[0, h, c, MLA_V_DIM:, :] = jnp.ones((MLA_ONES_ROWS, kc), BF16)
    sg_ref[...] = _silu(_dot(yn, wgate_ref[...])).astype(BF16)


def _odd_proj(x2d, mixa, mixb, pos_row, a_w_out, c_norm, c_w_in, q_norm, kv_norm, w_uq, w_ukv, invf_col, seq):
    t = x2d.shape[0]
    tm = min(PROJ_ROWS, seq)
    r2 = MLA_Q_RANK + MLA_KV_RANK + MLA_ROPE_DIM
    wc = jnp.pad(c_w_in[:, :r2], ((0, 0), (0, LANES - MLA_ROPE_DIM))).astype(BF16)
    wgate = c_w_in[:, r2:].astype(BF16)
    wuqt = jnp.pad(w_uq.reshape(MLA_Q_RANK, MLA_HEADS, MLA_NOPE_DIM + MLA_ROPE_DIM),
                   ((0, 0), (0, 0), (0, MLA_QK_PAD - MLA_NOPE_DIM - MLA_ROPE_DIM))
                   ).reshape(MLA_Q_RANK, MLA_HEADS * MLA_QK_PAD).T.astype(BF16)
    wukv = w_ukv.reshape(MLA_KV_RANK, MLA_HEADS, MLA_NOPE_DIM + MLA_V_DIM)
    wuk = wukv[:, :, :MLA_NOPE_DIM].reshape(MLA_KV_RANK, -1).astype(BF16)
    wuvt = wukv[:, :, MLA_NOPE_DIM:].reshape(MLA_KV_RANK, -1).T.astype(BF16)
    kc = min(MLA_TQ, seq)
    nq = seq // kc
    cps = tm // kc
    assert tm % kc == 0 and seq % tm == 0
    vrows = MLA_V_DIM + MLA_ONES_ROWS
    ins = [x2d, mixa, mixb, pos_row, a_w_out.astype(BF16), c_norm.reshape(1, -1), wc, wgate,
           q_norm.reshape(1, -1), kv_norm.reshape(1, -1), wuqt, wuk, wuvt, invf_col]
    full = lambda a: pl.BlockSpec(a.shape, lambda i: (0,) * a.ndim)
    rows = lambda n: pl.BlockSpec((tm, n), lambda i: (i, 0))
    cols = lambda n: pl.BlockSpec((n, tm), lambda i: (0, i))
    out_shapes = [
        jax.ShapeDtypeStruct((t, D_MODEL), F32),
        jax.ShapeDtypeStruct((MLA_HEADS * MLA_QK_PAD, t), BF16),
        jax.ShapeDtypeStruct((t, MLA_HEADS * MLA_NOPE_DIM), BF16),
        jax.ShapeDtypeStruct((t, LANES), BF16),
        jax.ShapeDtypeStruct((t // seq, MLA_HEADS, nq, vrows, kc), BF16),
        jax.ShapeDtypeStruct((t, MLA_WIDTH), BF16),
    ]
    steps = seq // tm
    vt_spec = pl.BlockSpec((1, MLA_HEADS, cps, vrows, kc), lambda i: (i // steps, 0, i % steps, 0, 0))
    out_specs = [rows(D_MODEL), cols(MLA_HEADS * MLA_QK_PAD), rows(MLA_HEADS * MLA_NOPE_DIM), rows(LANES), vt_spec,
                 rows(MLA_WIDTH)]
    return pl.pallas_call(
        functools.partial(_odd_proj_kernel, tm=tm),
        grid=(t // tm,),
        in_specs=[rows(D_MODEL), rows(NSA_WIDTH), rows(CONV_WIDTH), cols(1)] + [full(a) for a in ins[4:]],
        out_specs=out_specs,
        out_shape=out_shapes,
        compiler_params=pltpu.CompilerParams(dimension_semantics=("arbitrary",),
                                             vmem_limit_bytes=VMEM_LIMIT),
        name="odd_proj",
    )(*ins)


def _mla_kernel(qt_ref, kn_ref, kpe_ref, v_t_ref, o_ref, m_scr, acc_scr, *, tq, kc, hp):
    jd = pl.program_id(2)
    m_scr[...] = jnp.full(m_scr.shape, NEG_INF, F32)
    acc_scr[...] = jnp.zeros(acc_scr.shape, F32)
    causal = (lax.broadcasted_iota(jnp.int32, (kc, tq), 0) <= lax.broadcasted_iota(jnp.int32, (kc, tq), 1))

    def keys(h, start, size):
        return jnp.concatenate([kn_ref[0, pl.ds(start, size), h * MLA_NOPE_DIM:(h + 1) * MLA_NOPE_DIM],
                                kpe_ref[0, pl.ds(start, size), :]], axis=1)

    def step(j):
        start = pl.multiple_of(j * kc, kc)
        scores = []
        for h in range(hp):
            qk = slice(h * MLA_QK_PAD, (h + 1) * MLA_QK_PAD)
            scores.append(_dot(keys(h, start, kc), qt_ref[qk, :]))
        for h, s_t in enumerate(scores):
            _flash_update_t(s_t, v_t_ref[0, h, j], m_scr, acc_scr, pl.ds(h * tq, tq))

    lax.fori_loop(0, jd, lambda j, c: (step(j), c)[1], 0)

    hq = tq // 2
    start = pl.multiple_of(jd * kc, kc)
    tri = causal[0:hq, 0:hq]
    keep_hi = jnp.concatenate([jnp.ones((hq, hq), jnp.bool_), tri], axis=0)
    scores = []
    for h in range(hp):
        qk = slice(h * MLA_QK_PAD, (h + 1) * MLA_QK_PAD)
        scores.append((_dot(keys(h, start, hq), qt_ref[qk, 0:hq]),
                       _dot(keys(h, start, kc), qt_ref[qk, hq:tq])))
    for h, (s_lo, s_hi) in enumerate(scores):
        v_t = v_t_ref[0, h, jd]
        _flash_update_t(jnp.where(tri, s_lo, NEG_INF), v_t[:, 0:hq], m_scr, acc_scr, pl.ds(h * tq, hq))
        _flash_update_t(jnp.where(keep_hi, s_hi, NEG_INF), v_t, m_scr, acc_scr, pl.ds(h * tq + hq, hq))

    for h in range(hp):
        c = pl.ds(h * tq, tq)
        o_t = acc_scr[0:MLA_V_DIM, c] / acc_scr[MLA_V_DIM:MLA_V_DIM + 1, c]
        o_ref[0, :, h * MLA_V_DIM:(h + 1) * MLA_V_DIM] = o_t.T.astype(BF16)


def _mla_attention(q_t, k_nope, k_pe, v_t):
    b, seq, _ = k_pe.shape
    tq = min(MLA_TQ, seq)
    hp = MLA_HEADS_PER_STEP
    nq = seq // tq
    assert seq % tq == 0 and MLA_HEADS % hp == 0
    vrows = MLA_V_DIM + MLA_ONES_ROWS
    return pl.pallas_call(
        functools.partial(_mla_kernel, tq=tq, kc=tq, hp=hp),
        grid=(b, MLA_HEADS // hp, nq),
        in_specs=[pl.BlockSpec((hp * MLA_QK_PAD, tq), lambda bi, h, qi: (h, bi * nq + qi)),
                  pl.BlockSpec((1, seq, hp * MLA_NOPE_DIM), lambda bi, h, qi: (bi, 0, h)),
                  pl.BlockSpec((1, seq, LANES), lambda bi, h, qi: (bi, 0, 0)),
                  pl.BlockSpec((1, hp, nq, vrows, tq), lambda bi, h, qi: (bi, h, 0, 0, 0))],
        out_specs=pl.BlockSpec((1, tq, hp * MLA_V_DIM), lambda bi, h, qi: (bi, qi, h)),
        out_shape=jax.ShapeDtypeStruct((b, seq, MLA_WIDTH), BF16),
        scratch_shapes=[pltpu.VMEM((1, hp * tq), F32), pltpu.VMEM((vrows, hp * tq), F32)],
        compiler_params=pltpu.CompilerParams(dimension_semantics=("arbitrary", "arbitrary", "arbitrary"),
                                             vmem_limit_bytes=VMEM_LIMIT),
        name="mla_attn",
    )(q_t, k_nope, k_pe, v_t)


def _final_kernel(y_ref, o_ref, sg_ref, w_ref, g_ref, out_ref):
    gated = (sg_ref[...].astype(F32) * o_ref[...].astype(F32)).astype(BF16)
    out_ref[...] = _rms(y_ref[...] + _dot(gated, w_ref[...]), g_ref[...])


def _final(y, o, sg, w_out, final_norm, seq):
    t = y.shape[0]
    tm = min(PROJ_ROWS, seq)
    rows = lambda n: pl.BlockSpec((tm, n), lambda i: (i, 0))
    full = lambda a: pl.BlockSpec(a.shape, lambda i: (0,) * a.ndim)
    w = w_out.astype(BF16)
    g = final_norm.reshape(1, -1)
    return pl.pallas_call(
        _final_kernel,
        grid=(t // tm,),
        in_specs=[rows(D_MODEL), rows(MLA_WIDTH), rows(MLA_WIDTH), full(w), full(g)],
        out_specs=rows(D_MODEL),
        out_shape=jax.ShapeDtypeStruct((t, D_MODEL), F32),
        compiler_params=pltpu.CompilerParams(dimension_semantics=("arbitrary",),
                                             vmem_limit_bytes=VMEM_LIMIT),
        name="final",
    )(y, o, sg, w, g)


def _rope_constants():
    half = NSA_HEAD_DIM // 2
    inv_freq = ROPE_THETA ** (-jnp.arange(half, dtype=F32) / half)
    invf = jnp.tile(inv_freq, LANES // half).reshape(1, LANES)
    sgn = jnp.tile(jnp.concatenate([-jnp.ones((half,), F32), jnp.ones((half,), F32)]), LANES // (2 * half))
    return invf, sgn.reshape(1, LANES), inv_freq.reshape(half, 1)


def kernel(x, positions, a_norm, a_w_in, a_pe_k, a_pe_v, a_w_ck1, a_w_ck2, a_w_cv1, a_w_cv2, a_conv_w, a_w_out, c_norm, c_w_in, c_q_norm, c_kv_norm, c_w_uq, c_w_ukv, c_w_out, final_norm):
    b, seq, d = x.shape
    assert d == D_MODEL and NSA_HEAD_DIM == MLA_ROPE_DIM
    assert a_norm.shape[0] == 1 and c_norm.shape[0] == 1
    t = b * seq
    invf, sgn, invf_col = _rope_constants()
    x2d = x.reshape(t, d)
    pos_row = positions.reshape(1, t)

    q_t, kc, vc, ksel, vsl_t, kw, vw_t, gates, sga, mixb = _even_proj(
        x2d, pos_row, a_norm[0], a_w_in[0], a_conv_w[0], invf_col, seq)
    nchunk = seq // CMP_STRIDE
    pos_cmp = jnp.pad(positions[:, CMP_BLOCK - 1::CMP_STRIDE], ((0, 0), (0, 1)))[:, :nchunk, None]
    kcmp, vcmp_t = _compress(kc.reshape(b, seq, -1), vc.reshape(b, seq, -1), pos_cmp, a_pe_k[0], a_pe_v[0],
                             a_w_ck1[0], a_w_ck2[0], a_w_cv1[0], a_w_cv2[0], invf, sgn)
    r3 = lambda a: a.reshape(b, seq, a.shape[-1])
    mixa = _nsa_attention(q_t, kcmp, vcmp_t, r3(ksel), vsl_t, r3(kw), vw_t, r3(gates), r3(sga))

    y, mq_t, mkn, mkpe, mv_t, sg = _odd_proj(x2d, mixa.reshape(t, -1), mixb, pos_row, a_w_out[0], c_norm[0],
                                      c_w_in[0], c_q_norm[0], c_kv_norm[0], c_w_uq[0], c_w_ukv[0], invf_col, seq)
    o = _mla_attention(mq_t, r3(mkn), r3(mkpe), mv_t)
    out = _final(y, o.reshape(t, -1), sg, c_w_out[0], final_norm, seq)
    return out.reshape(b, seq, d)
```

```python
import functools

import jax
import jax.numpy as jnp
import numpy as np
from jax import lax
from jax.experimental import pallas as pl
from jax.experimental.pallas import tpu as pltpu

F32 = jnp.float32
BF16 = jnp.bfloat16

D_MODEL = 1024
ROPE_THETA = 10000.0
RMS_EPS = 1e-6
NEG_INF = -1e30
FORCE_SCORE = 1e4
SEL_PENALTY = NEG_INF

NSA_HEADS = 8
NSA_KV_GROUPS = 2
NSA_HPG = NSA_HEADS // NSA_KV_GROUPS
NSA_HEAD_DIM = 64
NSA_WIDTH = NSA_HEADS * NSA_HEAD_DIM
NSA_KV_WIDTH = NSA_KV_GROUPS * NSA_HEAD_DIM
CMP_BLOCK = 32
CMP_STRIDE = 16
CMP_HIDDEN = 2 * NSA_HEAD_DIM
SLC_BLOCK = 64
SLC_TOPK = 16
SLC_SHIFT = SLC_BLOCK.bit_length() - 1
SLC_MAX_BLOCKS = 64
WINDOW = 512
CONV_WIDTH = D_MODEL - NSA_WIDTH
CONV_K = 3

MLA_HEADS = 8
MLA_NOPE_DIM = 128
MLA_ROPE_DIM = 64
MLA_V_DIM = 128
MLA_Q_RANK = 256
MLA_KV_RANK = 256
MLA_WIDTH = MLA_HEADS * MLA_V_DIM
MLA_QK_PAD = 256

LANES = 128
SUBLANES = 8
VMEM_LIMIT = 56 * 1024 * 1024

PROJ_ROWS = 1024
NSA_TQ = 256
NSA_TILES_PER_STEP = 1
MLA_TQ = 512
MLA_HEADS_PER_STEP = 4
MLA_ONES_ROWS = 16
NSA_ONES_ROWS = 16
NSA_CMP_ROWS = NSA_HEAD_DIM + SLC_MAX_BLOCKS + NSA_ONES_ROWS
LOG2_E = 1.4426950408889634
MAX_STALE_EXCESS = 64.0


def _dot(a, b):
    return jnp.dot(a, b, preferred_element_type=F32)


def _dot_nt(a, b):
    return lax.dot_general(a, b, (((1,), (1,)), ((), ())), preferred_element_type=F32)


def _silu(x):
    return x * jax.nn.sigmoid(x)


def _rms(x, g):
    return x * lax.rsqrt(jnp.mean(x * x, axis=-1, keepdims=True) + RMS_EPS) * g


def _rope_tables(pos_col, invf_row, sign_row):
    ang = pos_col.astype(F32) * invf_row
    return jnp.cos(ang), jnp.sin(ang) * sign_row


def _rope_slab(s, cos2, sin2):
    half = NSA_HEAD_DIM // 2
    first_half = (lax.broadcasted_iota(jnp.int32, s.shape, 1) & half) == 0
    swapped = jnp.where(first_half, pltpu.roll(s, LANES - half, 1), pltpu.roll(s, half, 1))
    return s * cos2 + swapped * sin2


def _rope_tables_t(invf_col, pos_row):
    ang = invf_col * pos_row.astype(F32)
    return jnp.cos(ang), jnp.sin(ang)


def _flash_update_stale(s_t, v_t, m_ref, acc_ref, exc_ref, cols):
    m_old = m_ref[:, cols]
    p = jnp.exp2(s_t - m_old)
    cmax = jnp.max(s_t, axis=0, keepdims=True)
    m_new = jnp.maximum(m_old, cmax)
    rows = v_t.shape[0]
    acc_ref[0:rows, cols] = jnp.exp2(m_old - m_new) * (acc_ref[0:rows, cols] + _dot(v_t, p.astype(BF16)))
    m_ref[:, cols] = m_new
    exc_ref[:, cols] = jnp.maximum(exc_ref[:, cols], cmax - m_old)


def _flash_update_t(s_t, v_t, m_ref, acc_ref, cols):
    m_old = m_ref[:, cols]
    m_new = jnp.maximum(m_old, jnp.max(s_t, axis=0, keepdims=True))
    p = jnp.exp2(s_t - m_new)
    rows = v_t.shape[0]
    acc_ref[0:rows, cols] = jnp.exp2(m_old - m_new) * acc_ref[0:rows, cols] + _dot(v_t, p.astype(BF16))
    m_ref[:, cols] = m_new


def _even_proj_kernel(x_ref, posr_ref, g_ref, wqt_ref, wkv_ref, wg_ref, wga_ref, wcb_ref, wcc_ref, wch_ref,
                      wgb_ref, convw_ref, invfc_ref,
                      qt_ref, kc_ref, vc_ref, ksel_ref, vslt_ref, kw_ref, vwt_ref, gates_ref, sga_ref,
                      mixb_ref, carry_ref, *, tiles_per_seq, tm, kc):
    i = pl.program_id(0)
    xn = _rms(x_ref[...], g_ref[...]).astype(BF16)
    cos_t, sin_t = _rope_tables_t(invfc_ref[...], posr_ref[...])
    half = NSA_HEAD_DIM // 2
    lane = lax.broadcasted_iota(jnp.int32, (tm, LANES), 1)
    low = lane < NSA_HEAD_DIM

    def rope_t(x_t):
        out = []
        for r in range(0, x_t.shape[0], 2 * half):
            x1, x2 = x_t[r:r + half], x_t[r + half:r + 2 * half]
            out += [x1 * cos_t - x2 * sin_t, x2 * cos_t + x1 * sin_t]
        return jnp.concatenate(out, axis=0)

    def store_values_t(ref, v):
        v_t = v.T.astype(BF16)
        ones = jnp.ones((NSA_ONES_ROWS, kc), BF16)
        for c in range(tm // kc):
            for g in range(NSA_KV_GROUPS):
                ref[c, g, 0:NSA_HEAD_DIM, :] = v_t[g * NSA_HEAD_DIM:(g + 1) * NSA_HEAD_DIM, c * kc:(c + 1) * kc]
                ref[c, g, NSA_HEAD_DIM:, :] = ones

    scale = NSA_HEAD_DIM ** -0.5 * LOG2_E
    qt_ref[...] = (rope_t(_dot_nt(wqt_ref[...], xn)) * scale).astype(BF16)

    seg = _dot(xn, wkv_ref[...])
    kc_ref[...] = seg[:, 0:LANES]
    vc_ref[...] = seg[:, LANES:2 * LANES]
    ksl = rope_t(seg[:, 2 * LANES:3 * LANES].T).T
    store_values_t(vslt_ref, seg[:, 3 * LANES:4 * LANES])
    kw_ref[...] = rope_t(seg[:, 4 * LANES:5 * LANES].T).T.astype(BF16)
    store_values_t(vwt_ref, seg[:, 5 * LANES:6 * LANES])

    spos = (i % tiles_per_seq) * tm + lax.broadcasted_iota(jnp.int32, (tm, LANES), 0)
    blk = lax.shift_right_logical(spos, SLC_SHIFT)
    ksel_ref[:, 0:LANES] = jnp.where(low, ksl, (lane - NSA_HEAD_DIM == blk).astype(F32)).astype(BF16)
    ksel_ref[:, LANES:2 * LANES] = jnp.where(low, (lane == blk).astype(F32), ksl).astype(BF16)

    gates_ref[...] = jax.nn.sigmoid(_dot(xn, wg_ref[...]))
    sga_ref[...] = _silu(_dot(xn, wga_ref[...])).astype(BF16)

    u = _dot(xn, wcc_ref[...]) * _dot(xn, wch_ref[...])

    @pl.when(i % tiles_per_seq == 0)
    def _():
        carry_ref[...] = jnp.zeros_like(carry_ref)

    row = lax.broadcasted_iota(jnp.int32, (tm, CONV_WIDTH), 0)
    prev1 = carry_ref[SUBLANES - 1:SUBLANES, :]
    prev2 = carry_ref[SUBLANES - 2:SUBLANES - 1, :]
    u1 = jnp.where(row == 0, prev1, pltpu.roll(u, 1, 0))
    u2 = jnp.where(row == 0, prev2, jnp.where(row == 1, prev1, pltpu.roll(u, 2, 0)))
    carry_ref[...] = u[tm - SUBLANES:tm, :]
    w = convw_ref[...]
    y = w[0:1, :] * u2 + w[1:2, :] * u1 + w[2:3, :] * u
    cb = _dot(xn, wcb_ref[...])
    mixb_ref[...] = (_silu(_dot(xn, wgb_ref[...])) * (cb * y)).astype(BF16)


def _even_proj(x2d, pos_row, norm_g, w_in, conv_w, invf_col, seq):
    t = x2d.shape[0]
    tm = min(PROJ_ROWS, seq)
    kc = min(NSA_TQ, seq)
    assert seq % tm == 0 and t % tm == 0 and tm % kc == 0 and conv_w.shape[0] == CONV_K == 3
    sizes = (NSA_WIDTH,) + (NSA_KV_WIDTH,) * 6 + (3 * NSA_HEADS, NSA_WIDTH) + (CONV_WIDTH,) * 4
    offs = np.concatenate([[0], np.cumsum(sizes)])
    col = lambda k: w_in[:, offs[k]:offs[k + 1]]
    wqt = col(0).T.astype(BF16)
    wkv = jnp.concatenate([col(k) for k in range(1, 7)], axis=1).astype(BF16)
    wg = jnp.pad(col(7), ((0, 0), (0, LANES - 3 * NSA_HEADS))).astype(BF16)
    wga, wcb, wcc, wch, wgb = (col(k).astype(BF16) for k in range(8, 13))

    full = lambda a: pl.BlockSpec(a.shape, lambda i: (0,) * a.ndim)
    rows = lambda n: pl.BlockSpec((tm, n), lambda i: (i, 0))
    cols = lambda n: pl.BlockSpec((n, tm), lambda i: (0, i))
    vrows = NSA_HEAD_DIM + NSA_ONES_ROWS
    values_t = jax.ShapeDtypeStruct((t // kc, NSA_KV_GROUPS, vrows, kc), BF16)
    values_t_spec = pl.BlockSpec((tm // kc, NSA_KV_GROUPS, vrows, kc), lambda i: (i, 0, 0, 0))
    out_shapes = [
        jax.ShapeDtypeStruct((NSA_WIDTH, t), BF16),
        jax.ShapeDtypeStruct((t, NSA_KV_WIDTH), F32),
        jax.ShapeDtypeStruct((t, NSA_KV_WIDTH), F32),
        jax.ShapeDtypeStruct((t, 2 * LANES), BF16),
        values_t,
        jax.ShapeDtypeStruct((t, NSA_KV_WIDTH), BF16),
        values_t,
        jax.ShapeDtypeStruct((t, LANES), F32),
        jax.ShapeDtypeStruct((t, NSA_WIDTH), BF16),
        jax.ShapeDtypeStruct((t, CONV_WIDTH), BF16),
    ]
    out_specs = [cols(NSA_WIDTH), rows(NSA_KV_WIDTH), rows(NSA_KV_WIDTH), rows(2 * LANES), values_t_spec,
                 rows(NSA_KV_WIDTH), values_t_spec, rows(LANES), rows(NSA_WIDTH), rows(CONV_WIDTH)]
    ins = [x2d, pos_row, norm_g.reshape(1, -1), wqt, wkv, wg, wga, wcb, wcc, wch, wgb, conv_w, invf_col]
    in_specs = [rows(D_MODEL), cols(1)] + [full(a) for a in ins[2:]]
    return pl.pallas_call(
        functools.partial(_even_proj_kernel, tiles_per_seq=seq // tm, tm=tm, kc=kc),
        grid=(t // tm,),
        in_specs=in_specs,
        out_specs=out_specs,
        out_shape=out_shapes,
        scratch_shapes=[pltpu.VMEM((SUBLANES, CONV_WIDTH), F32)],
        compiler_params=pltpu.CompilerParams(dimension_semantics=("arbitrary",),
                                             vmem_limit_bytes=VMEM_LIMIT),
        name="even_proj",
    )(*ins)


def _compress_kernel(kc_ref, vc_ref, pos_ref, pek_ref, pev_ref, w1k_ref, w2k_ref, w1v_ref, w2vt_ref,
                     invf_ref, sgn_ref, ov_ref, kcmp_ref, vcmpt_ref, *, nchunk):
    half = NSA_KV_GROUPS * CMP_HIDDEN
    rows16 = CMP_BLOCK // 2

    def hidden(raw_ref, pe_ref, w1_ref):
        acc = jnp.zeros((nchunk, 2 * half), F32)
        for l in range(rows16):
            x = raw_ref[0, pl.ds(l, nchunk, stride=CMP_STRIDE), :]
            lhs = jnp.concatenate([(x + pe_ref[l:l + 1, :]).astype(BF16),
                                   (x + pe_ref[rows16 + l:rows16 + l + 1, :]).astype(BF16)], axis=1)
            acc = acc + _dot(lhs, w1_ref[l])
        return _silu(acc[:, 0:half] + pltpu.roll(acc[:, half:], nchunk - 1, 0)).astype(BF16)

    cos2, sin2 = _rope_tables(pos_ref[0], invf_ref[...], sgn_ref[...])
    kcmp = _dot(hidden(kc_ref, pek_ref, w1k_ref), w2k_ref[...])
    kcmp_ref[0] = _rope_slab(kcmp, cos2, sin2).astype(BF16)
    vcmp_t = _dot_nt(w2vt_ref[...], hidden(vc_ref, pev_ref, w1v_ref)).astype(BF16)
    for g in range(NSA_KV_GROUPS):
        vcmpt_ref[0, g, 0:NSA_HEAD_DIM, :] = vcmp_t[g * NSA_HEAD_DIM:(g + 1) * NSA_HEAD_DIM]
        vcmpt_ref[0, g, NSA_HEAD_DIM:NSA_HEAD_DIM + SLC_MAX_BLOCKS, :] = ov_ref[...]
        vcmpt_ref[0, g, NSA_HEAD_DIM + SLC_MAX_BLOCKS:, :] = jnp.ones((NSA_ONES_ROWS, nchunk), BF16)


def _overlap_t(ncp, n_slc):
    cmp_starts = np.arange(ncp) * CMP_STRIDE
    slc_starts = np.arange(SLC_MAX_BLOCKS) * SLC_BLOCK
    ov = ((cmp_starts[None, :] < slc_starts[:, None] + SLC_BLOCK)
          & (cmp_starts[None, :] + CMP_BLOCK > slc_starts[:, None])
          & (np.arange(SLC_MAX_BLOCKS)[:, None] < n_slc))
    return jnp.asarray(ov.astype(np.float32), BF16)


def _compress_weights(pe, w1, w2):
    eye = jnp.eye(NSA_KV_GROUPS, dtype=F32)
    rows16 = CMP_BLOCK // 2
    w1r = w1.reshape(CMP_BLOCK, NSA_HEAD_DIM, CMP_HIDDEN)
    big = lambda part: jnp.einsum('ldn,gh->lgdhn', part, eye).reshape(
        rows16, NSA_KV_WIDTH, NSA_KV_GROUPS * CMP_HIDDEN)
    top, bot = big(w1r[:rows16]), big(w1r[rows16:])
    zero = jnp.zeros_like(top)
    w1big = jnp.concatenate([jnp.concatenate([top, zero], axis=2),
                             jnp.concatenate([zero, bot], axis=2)], axis=1).astype(BF16)
    w2big = jnp.einsum('nd,gh->gnhd', w2, eye).reshape(NSA_KV_GROUPS * CMP_HIDDEN, NSA_KV_WIDTH).astype(BF16)
    pe_rows = jnp.tile(pe, (1, NSA_KV_GROUPS))
    return pe_rows, w1big, w2big


def _compress(kc, vc, pos_cmp, pe_k, pe_v, w_ck1, w_ck2, w_cv1, w_cv2, invf, sgn):
    b, seq, _ = kc.shape
    nchunk = seq // CMP_STRIDE
    pek, w1k, w2k = _compress_weights(pe_k, w_ck1, w_ck2)
    pev, w1v, w2v = _compress_weights(pe_v, w_cv1, w_cv2)
    ins = [kc, vc, pos_cmp,
           pek, pev, w1k, w2k, w1v, w2v.T, invf, sgn, _overlap_t(nchunk, seq // SLC_BLOCK)]
    full = lambda a: pl.BlockSpec(a.shape, lambda i: (0,) * a.ndim)
    per_b = lambda n: pl.BlockSpec((1, nchunk, n), lambda i: (i, 0, 0))
    out_shape = [jax.ShapeDtypeStruct((b, nchunk, NSA_KV_WIDTH), BF16),
                 jax.ShapeDtypeStruct((b, NSA_KV_GROUPS, NSA_CMP_ROWS, nchunk), BF16)]
    return pl.pallas_call(
        functools.partial(_compress_kernel, nchunk=nchunk),
        grid=(b,),
        in_specs=[pl.BlockSpec((1, seq, NSA_KV_WIDTH), lambda i: (i, 0, 0))] * 2 + [per_b(1)]
        + [full(a) for a in ins[3:]],
        out_specs=[per_b(NSA_KV_WIDTH),
                   pl.BlockSpec((1, NSA_KV_GROUPS, NSA_CMP_ROWS, nchunk), lambda i: (i, 0, 0, 0))],
        out_shape=out_shape,
        compiler_params=pltpu.CompilerParams(dimension_semantics=("arbitrary",),
                                             vmem_limit_bytes=VMEM_LIMIT),
        name="compress",
    )(*ins)


def _nsa_kernel(qt_ref, kcmp_ref, vcmp_t_ref, ksel_ref, vsl_t_ref, kw_ref, vw_t_ref, gates_ref, sga_ref,
                out_ref, qz_scr, qa_scr, v_scr, rank_scr, m_scr, acc_scr, mix_scr, mix2_scr, exc_scr,
                *, tq, kc, n_cmp, n_top, tiles):
    for sub in range(tiles):
        _nsa_tile(pl.program_id(1) * tiles + sub, pl.ds(sub * tq, tq),
                  qt_ref, kcmp_ref, vcmp_t_ref, ksel_ref, vsl_t_ref, kw_ref, vw_t_ref, gates_ref, sga_ref,
                  out_ref, qz_scr, qa_scr, v_scr, rank_scr, m_scr, acc_scr, mix_scr, mix2_scr, exc_scr,
                  tq=tq, kc=kc, n_cmp=n_cmp, n_top=n_top)


def _nsa_tile(tile_idx, tile_rows, qt_ref, kcmp_ref, vcmp_t_ref, ksel_ref, vsl_t_ref, kw_ref, vw_t_ref, gates_ref,
              sga_ref, out_ref, qz_scr, qa_scr, v_scr, rank_scr, m_scr, acc_scr, mix_scr, mix2_scr, exc_scr,
              *, tq, kc, n_cmp, n_top):
    t0 = tile_idx * tq
    n = NSA_HPG * tq
    ncp = kcmp_ref.shape[1]
    hd = NSA_HEAD_DIM
    q_t = qt_ref[:, tile_rows]
    gates_t = gates_ref[0, tile_rows, :].T
    kq = lax.broadcasted_iota(jnp.int32, (kc, tq), 0) - lax.broadcasted_iota(jnp.int32, (kc, tq), 1)
    causal = kq <= 0
    newer = kq > 0

    def add_branch(branch, g, o_t):
        for h in range(NSA_HPG):
            head = NSA_HPG * g + h
            gate = gates_t[head * 3 + branch:head * 3 + branch + 1, :]
            term = gate * o_t[:, h * tq:(h + 1) * tq]
            r = pl.ds(head * hd, hd)
            if branch == 0:
                mix_scr[r, :] = term
            elif branch == 1:
                mix2_scr[r, :] = term
            else:
                mix2_scr[r, :] += term

    def reset_flash():
        m_scr[...] = jnp.full(m_scr.shape, NEG_INF, F32)
        acc_scr[...] = jnp.zeros(acc_scr.shape, F32)

    def flash_step(q_scr, k_ref, v_t_ref, chunks):
        chains = []
        for j, mask, single_pass in chunks:
            start = pl.multiple_of(j * kc, kc)
            for g in range(NSA_KV_GROUPS):
                col0 = g * LANES if k_ref is ksel_ref else 0
                k = k_ref[0, pl.ds(start, kc), col0:col0 + LANES]
                for h in range(NSA_HPG):
                    s_t = _dot(k, q_scr[g, :, h * tq:(h + 1) * tq])
                    chains.append((j, mask, single_pass, g, g * n + h * tq, s_t))
        for j, mask, single_pass, g, c0, s_t in chains:
            if mask is not None:
                s_t = jnp.where(mask, s_t, NEG_INF)
            if single_pass:
                _flash_update_stale(s_t, v_t_ref[0, j, g], m_scr, acc_scr, exc_scr, pl.ds(c0, tq))
            else:
                _flash_update_t(s_t, v_t_ref[0, j, g], m_scr, acc_scr, pl.ds(c0, tq))

    def finish_flash(branch):
        for g in range(NSA_KV_GROUPS):
            c = pl.ds(g * n, n)
            add_branch(branch, g, acc_scr[0:hd, c] / acc_scr[hd:hd + 1, c])

    zeros_q = jnp.zeros((hd, n), BF16)
    qgs = []
    for g in range(NSA_KV_GROUPS):
        qg = jnp.concatenate([q_t[(NSA_HPG * g + h) * hd:(NSA_HPG * g + h + 1) * hd, :] for h in range(NSA_HPG)],
                             axis=1)
        qgs.append(qg)
        qz_scr[g] = jnp.concatenate([qg, zeros_q] if g == 0 else [zeros_q, qg], axis=0)

    reset_flash()
    cq = (lax.broadcasted_iota(jnp.int32, (ncp, tq), 0) * CMP_STRIDE
          - lax.broadcasted_iota(jnp.int32, (ncp, tq), 1))
    cmp_valid = (cq <= t0 - (CMP_BLOCK - 1)) & (lax.broadcasted_iota(jnp.int32, (ncp, tq), 0) < n_cmp)
    chains = [(g * n + h * tq, _dot(kcmp_ref[0], qz_scr[g, :, h * tq:(h + 1) * tq]))
              for g in range(NSA_KV_GROUPS) for h in range(NSA_HPG)]
    for c0, s_t in chains:
        _flash_update_t(jnp.where(cmp_valid, s_t, NEG_INF), vcmp_t_ref[0, c0 // n], m_scr, acc_scr, pl.ds(c0, tq))
    seen = (t0 + lax.broadcasted_iota(jnp.int32, (1, tq), 1) >= CMP_BLOCK - 1).astype(F32)
    seen4 = jnp.concatenate([seen] * NSA_HPG, axis=1)
    imp_rows = slice(hd, hd + SLC_MAX_BLOCKS)
    den_row = slice(hd + SLC_MAX_BLOCKS, hd + SLC_MAX_BLOCKS + 1)

    for g in range(NSA_KV_GROUPS):
        qg = qgs[g]
        c = pl.ds(g * n, n)
        inv_l = seen4 / acc_scr[den_row, c]
        add_branch(0, g, acc_scr[0:hd, c] * inv_l)
        pooled = acc_scr[imp_rows, c] * inv_l
        imp_t = functools.reduce(jnp.add, [pooled[:, h * tq:(h + 1) * tq] for h in range(NSA_HPG)])
        nb = lax.broadcasted_iota(jnp.int32, (SLC_MAX_BLOCKS, tq), 0)
        tqv = t0 + lax.broadcasted_iota(jnp.int32, (SLC_MAX_BLOCKS, tq), 1)
        cur = lax.shift_right_logical(tqv, SLC_SHIFT)
        forced = (nb == 0) | (nb == cur) | (nb == cur - 1)
        v_scr[...] = jnp.where(nb * SLC_BLOCK <= tqv, jnp.where(forced, FORCE_SCORE, imp_t), -1.0)
        groups = SLC_MAX_BLOCKS // SUBLANES
        sub = lax.broadcasted_iota(jnp.int32, (SUBLANES, tq), 0)
        vch = [v_scr[r * SUBLANES:(r + 1) * SUBLANES, :] for r in range(groups)]
        rank_scr[...] = jnp.zeros(rank_scr.shape, jnp.int32)
        for mg in range(groups):
            @pl.when(mg * SUBLANES * SLC_BLOCK < t0 + tq)
            def _():
                count = [jnp.zeros((SUBLANES, tq), jnp.int32) for _ in range(groups)]
                for mblk in range(mg * SUBLANES, (mg + 1) * SUBLANES):
                    vm = jnp.broadcast_to(v_scr[mblk:mblk + 1, :], (SUBLANES, tq))
                    for r in range(groups):
                        if r > mg:
                            before = vm >= vch[r]
                        elif r < mg:
                            before = vm > vch[r]
                        else:
                            before = (vm > vch[r]) | ((vm == vch[r]) & (sub > mblk - r * SUBLANES))
                        count[r] = count[r] + jnp.where(before, 1, 0)
                for r in range(groups):
                    rank_scr[r * SUBLANES:(r + 1) * SUBLANES, :] += count[r]
        pen = jnp.where(rank_scr[...] < n_top, 0.0, SEL_PENALTY)
        pen4 = jnp.concatenate([pen.astype(BF16)] * NSA_HPG, axis=1)
        qa_scr[g] = jnp.concatenate([qg, pen4] if g == 0 else [pen4, qg], axis=0)

    jd = t0 // kc

    def attend(single_pass):
        reset_flash()
        sel = functools.partial(flash_step, qa_scr, ksel_ref, vsl_t_ref)
        sel([(jd, causal, False)])
        lax.fori_loop(0, jd // 2, lambda i, cr: (
            sel([(2 * i, None, single_pass), (2 * i + 1, None, single_pass)]), cr)[1], 0)

        @pl.when(jd % 2 == 1)
        def _():
            sel([(jd - 1, None, single_pass)])

        finish_flash(1)

        reset_flash()
        win = functools.partial(flash_step, qz_scr, kw_ref, vw_t_ref)
        assert WINDOW == 2 * kc
        win([(jd, causal, False)])

        @pl.when(jd == 1)
        def _():
            win([(jd - 1, None, single_pass)])

        @pl.when(jd >= 2)
        def _():
            win([(jd - 1, None, single_pass), (jd - 2, newer, single_pass)])

        finish_flash(2)

    exc_scr[...] = jnp.full(exc_scr.shape, NEG_INF, F32)
    attend(single_pass=True)

    @pl.when(jnp.max(exc_scr[...]) > MAX_STALE_EXCESS)
    def _():
        attend(single_pass=False)

    out_ref[0, tile_rows, :] = (sga_ref[0, tile_rows, :].astype(F32) * (mix_scr[...] + mix2_scr[...]).T).astype(BF16)


def _nsa_attention(q_t, kcmp, vcmp_t, ksel, vsl_t, kw, vw_t, gates, sga):
    b, seq, _ = ksel.shape
    tq = min(NSA_TQ, seq)
    nq = seq // tq
    n_slc = seq // SLC_BLOCK
    assert n_slc <= SLC_MAX_BLOCKS and seq % tq == 0 and WINDOW % tq == 0 and tq & (tq - 1) == 0
    ncp = kcmp.shape[1]
    n_cmp = (seq - CMP_BLOCK) // CMP_STRIDE + 1
    vsl_t = vsl_t.reshape((b, nq) + vsl_t.shape[1:])
    vw_t = vw_t.reshape((b, nq) + vw_t.shape[1:])
    n = NSA_HPG * tq
    tiles = NSA_TILES_PER_STEP if nq % NSA_TILES_PER_STEP == 0 else 1
    tile = lambda w: pl.BlockSpec((1, tiles * tq, w), lambda bi, qi: (bi, qi, 0))
    per_b = lambda a: pl.BlockSpec((1,) + a.shape[1:], lambda bi, qi: (bi,) + (0,) * (a.ndim - 1))
    return pl.pallas_call(
        functools.partial(_nsa_kernel, tq=tq, kc=tq, n_cmp=n_cmp, n_top=min(SLC_TOPK, n_slc), tiles=tiles),
        grid=(b, nq // tiles),
        in_specs=[pl.BlockSpec((NSA_WIDTH, tiles * tq), lambda bi, qi: (0, bi * (nq // tiles) + qi)),
                  per_b(kcmp), per_b(vcmp_t), per_b(ksel), per_b(vsl_t), per_b(kw), per_b(vw_t),
                  tile(LANES), tile(NSA_WIDTH)],
        out_specs=tile(NSA_WIDTH),
        out_shape=jax.ShapeDtypeStruct((b, seq, NSA_WIDTH), BF16),
        scratch_shapes=[pltpu.VMEM((NSA_KV_GROUPS, LANES, n), BF16),
                        pltpu.VMEM((NSA_KV_GROUPS, LANES, n), BF16),
                        pltpu.VMEM((SLC_MAX_BLOCKS, tq), F32),
                        pltpu.VMEM((SLC_MAX_BLOCKS, tq), jnp.int32),
                        pltpu.VMEM((1, NSA_KV_GROUPS * n), F32),
                        pltpu.VMEM((NSA_CMP_ROWS, NSA_KV_GROUPS * n), F32),
                        pltpu.VMEM((NSA_WIDTH, tq), F32),
                        pltpu.VMEM((NSA_WIDTH, tq), F32),
                        pltpu.VMEM((1, NSA_KV_GROUPS * n), F32)],
        compiler_params=pltpu.CompilerParams(dimension_semantics=("arbitrary", "arbitrary"),
                                             vmem_limit_bytes=VMEM_LIMIT),
        name="nsa_attn",
    )(q_t, kcmp, vcmp_t, ksel, vsl_t, kw, vw_t, gates, sga)


def _odd_proj_kernel(x_ref, mixa_ref, mixb_ref, posr_ref, wout_ref, g_ref, wc_ref, wgate_ref, qn_ref, kvn_ref,
                     wuqt_ref, wuk_ref, wuvt_ref, invfc_ref,
                     y_ref, qt_ref, kn_ref, kpe_ref, vt_ref, sg_ref, *, tm):
    y = (x_ref[...] + _dot(mixa_ref[...], wout_ref[0:NSA_WIDTH, :])
         + _dot(mixb_ref[...], wout_ref[NSA_WIDTH:, :]))
    y_ref[...] = y
    yn = _rms(y, g_ref[...]).astype(BF16)
    cos_t, sin_t = _rope_tables_t(invfc_ref[...], posr_ref[...])
    half = MLA_ROPE_DIM // 2

    def rope_t(x1, x2):
        return x1 * cos_t - x2 * sin_t, x2 * cos_t + x1 * sin_t

    seg = _dot(yn, wc_ref[...])
    kr_t = seg[:, MLA_Q_RANK + MLA_KV_RANK:].T
    kp1, kp2 = rope_t(kr_t[0:half], kr_t[half:2 * half])
    kpe_ref[...] = jnp.concatenate([kp1, kp2, kr_t[2 * half:]], axis=0).T.astype(BF16)
    cq = _rms(seg[:, 0:MLA_Q_RANK], qn_ref[...]).astype(BF16)
    ckv = _rms(seg[:, MLA_Q_RANK:MLA_Q_RANK + MLA_KV_RANK], kvn_ref[...]).astype(BF16)
    scale = (MLA_NOPE_DIM + MLA_ROPE_DIM) ** -0.5 * LOG2_E
    qt = _dot_nt(wuqt_ref[...], cq)
    kn_ref[...] = _dot(ckv, wuk_ref[...]).astype(BF16)
    for h in range(MLA_HEADS):
        c0 = h * MLA_QK_PAD
        r0 = c0 + MLA_NOPE_DIM
        qt_ref[c0:r0, :] = (qt[c0:r0] * scale).astype(BF16)
        q1, q2 = rope_t(qt[r0:r0 + half], qt[r0 + half:r0 + 2 * half])
        qt_ref[r0:r0 + half, :] = (q1 * scale).astype(BF16)
        qt_ref[r0 + half:r0 + 2 * half, :] = (q2 * scale).astype(BF16)
        qt_ref[r0 + 2 * half:c0 + MLA_QK_PAD, :] = jnp.zeros((MLA_QK_PAD - MLA_NOPE_DIM - 2 * half, tm), BF16)
    v_t = _dot_nt(wuvt_ref[...], ckv)
    kc = vt_ref.shape[-1]
    for h in range(MLA_HEADS):
        for c in range(tm // kc):
            vt_ref[0, h, c, 0:MLA_V_DIM, :] = v_t[h * MLA_V_DIM:(h + 1) * MLA_V_DIM, c * kc:(c + 1) * kc].astype(BF16)
            vt_ref[0, h, c, MLA_V_DIM:, :] = jnp.ones((MLA_ONES_ROWS, kc), BF16)
    sg_ref[...] = _silu(_dot(yn, wgate_ref[...])).astype(BF16)


def _odd_proj(x2d, mixa, mixb, pos_row, a_w_out, c_norm, c_w_in, q_norm, kv_norm, w_uq, w_ukv, invf_col, seq):
    t = x2d.shape[0]
    tm = min(PROJ_ROWS, seq)
    r2 = MLA_Q_RANK + MLA_KV_RANK + MLA_ROPE_DIM
    wc = jnp.pad(c_w_in[:, :r2], ((0, 0), (0, LANES - MLA_ROPE_DIM))).astype(BF16)
    wgate = c_w_in[:, r2:].astype(BF16)
    wuqt = jnp.pad(w_uq.reshape(MLA_Q_RANK, MLA_HEADS, MLA_NOPE_DIM + MLA_ROPE_DIM),
                   ((0, 0), (0, 0), (0, MLA_QK_PAD - MLA_NOPE_DIM - MLA_ROPE_DIM))
                   ).reshape(MLA_Q_RANK, MLA_HEADS * MLA_QK_PAD).T.astype(BF16)
    wukv = w_ukv.reshape(MLA_KV_RANK, MLA_HEADS, MLA_NOPE_DIM + MLA_V_DIM)
    wuk = wukv[:, :, :MLA_NOPE_DIM].reshape(MLA_KV_RANK, -1).astype(BF16)
    wuvt = wukv[:, :, MLA_NOPE_DIM:].reshape(MLA_KV_RANK, -1).T.astype(BF16)
    kc = min(MLA_TQ, seq)
    nq = seq // kc
    cps = tm // kc
    assert tm % kc == 0 and seq % tm == 0
    vrows = MLA_V_DIM + MLA_ONES_ROWS
    ins = [x2d, mixa, mixb, pos_row, a_w_out.astype(BF16), c_norm.reshape(1, -1), wc, wgate,
           q_norm.reshape(1, -1), kv_norm.reshape(1, -1), wuqt, wuk, wuvt, invf_col]
    full = lambda a: pl.BlockSpec(a.shape, lambda i: (0,) * a.ndim)
    rows = lambda n: pl.BlockSpec((tm, n), lambda i: (i, 0))
    cols = lambda n: pl.BlockSpec((n, tm), lambda i: (0, i))
    out_shapes = [
        jax.ShapeDtypeStruct((t, D_MODEL), F32),
        jax.ShapeDtypeStruct((MLA_HEADS * MLA_QK_PAD, t), BF16),
        jax.ShapeDtypeStruct((t, MLA_HEADS * MLA_NOPE_DIM), BF16),
        jax.ShapeDtypeStruct((t, LANES), BF16),
        jax.ShapeDtypeStruct((t // seq, MLA_HEADS, nq, vrows, kc), BF16),
        jax.ShapeDtypeStruct((t, MLA_WIDTH), BF16),
    ]
    steps = seq // tm
    vt_spec = pl.BlockSpec((1, MLA_HEADS, cps, vrows, kc), lambda i: (i // steps, 0, i % steps, 0, 0))
    out_specs = [rows(D_MODEL), cols(MLA_HEADS * MLA_QK_PAD), rows(MLA_HEADS * MLA_NOPE_DIM), rows(LANES), vt_spec,
                 rows(MLA_WIDTH)]
    return pl.pallas_call(
        functools.partial(_odd_proj_kernel, tm=tm),
        grid=(t // tm,),
        in_specs=[rows(D_MODEL), rows(NSA_WIDTH), rows(CONV_WIDTH), cols(1)] + [full(a) for a in ins[4:]],
        out_specs=out_specs,
        out_shape=out_shapes,
        compiler_params=pltpu.CompilerParams(dimension_semantics=("arbitrary",),
                                             vmem_limit_bytes=VMEM_LIMIT),
        name="odd_proj",
    )(*ins)


def _mla_kernel(qt_ref, kn_ref, kpe_ref, v_t_ref, o_ref, m_scr, acc_scr, exc_scr, *, tq, kc, hp):
    jd = pl.program_id(2)
    causal = (lax.broadcasted_iota(jnp.int32, (kc, tq), 0) <= lax.broadcasted_iota(jnp.int32, (kc, tq), 1))
    hq = tq // 2
    tri = causal[0:hq, 0:hq]
    keep_hi = jnp.concatenate([jnp.ones((hq, hq), jnp.bool_), tri], axis=0)

    def keys(h, start, size):
        return jnp.concatenate([kn_ref[0, pl.ds(start, size), h * MLA_NOPE_DIM:(h + 1) * MLA_NOPE_DIM],
                                kpe_ref[0, pl.ds(start, size), :]], axis=1)

    def step(j, single_pass):
        start = pl.multiple_of(j * kc, kc)
        scores = []
        for h in range(hp):
            qk = slice(h * MLA_QK_PAD, (h + 1) * MLA_QK_PAD)
            scores.append(_dot(keys(h, start, kc), qt_ref[qk, :]))
        for h, s_t in enumerate(scores):
            if single_pass:
                _flash_update_stale(s_t, v_t_ref[0, h, j], m_scr, acc_scr, exc_scr, pl.ds(h * tq, tq))
            else:
                _flash_update_t(s_t, v_t_ref[0, h, j], m_scr, acc_scr, pl.ds(h * tq, tq))

    def attend(single_pass):
        m_scr[...] = jnp.full(m_scr.shape, NEG_INF, F32)
        acc_scr[...] = jnp.zeros(acc_scr.shape, F32)
        start = pl.multiple_of(jd * kc, kc)
        scores = []
        for h in range(hp):
            qk = slice(h * MLA_QK_PAD, (h + 1) * MLA_QK_PAD)
            scores.append((_dot(keys(h, start, hq), qt_ref[qk, 0:hq]),
                           _dot(keys(h, start, kc), qt_ref[qk, hq:tq])))
        for h, (s_lo, s_hi) in enumerate(scores):
            v_t = v_t_ref[0, h, jd]
            _flash_update_t(jnp.where(tri, s_lo, NEG_INF), v_t[:, 0:hq], m_scr, acc_scr, pl.ds(h * tq, hq))
            _flash_update_t(jnp.where(keep_hi, s_hi, NEG_INF), v_t, m_scr, acc_scr, pl.ds(h * tq + hq, hq))
        lax.fori_loop(0, jd, lambda j, c: (step(j, single_pass), c)[1], 0)
        for h in range(hp):
            c = pl.ds(h * tq, tq)
            o_t = acc_scr[0:MLA_V_DIM, c] / acc_scr[MLA_V_DIM:MLA_V_DIM + 1, c]
            o_ref[0, :, h * MLA_V_DIM:(h + 1) * MLA_V_DIM] = o_t.T.astype(BF16)

    exc_scr[...] = jnp.full(exc_scr.shape, NEG_INF, F32)
    attend(single_pass=True)

    @pl.when(jnp.max(exc_scr[...]) > MAX_STALE_EXCESS)
    def _():
        attend(single_pass=False)


def _mla_attention(q_t, k_nope, k_pe, v_t):
    b, seq, _ = k_pe.shape
    tq = min(MLA_TQ, seq)
    hp = MLA_HEADS_PER_STEP
    nq = seq // tq
    assert seq % tq == 0 and MLA_HEADS % hp == 0
    vrows = MLA_V_DIM + MLA_ONES_ROWS
    return pl.pallas_call(
        functools.partial(_mla_kernel, tq=tq, kc=tq, hp=hp),
        grid=(b, MLA_HEADS // hp, nq),
        in_specs=[pl.BlockSpec((hp * MLA_QK_PAD, tq), lambda bi, h, qi: (h, bi * nq + qi)),
                  pl.BlockSpec((1, seq, hp * MLA_NOPE_DIM), lambda bi, h, qi: (bi, 0, h)),
                  pl.BlockSpec((1, seq, LANES), lambda bi, h, qi: (bi, 0, 0)),
                  pl.BlockSpec((1, hp, nq, vrows, tq), lambda bi, h, qi: (bi, h, 0, 0, 0))],
        out_specs=pl.BlockSpec((1, tq, hp * MLA_V_DIM), lambda bi, h, qi: (bi, qi, h)),
        out_shape=jax.ShapeDtypeStruct((b, seq, MLA_WIDTH), BF16),
        scratch_shapes=[pltpu.VMEM((1, hp * tq), F32), pltpu.VMEM((vrows, hp * tq), F32),
                        pltpu.VMEM((1, hp * tq), F32)],
        compiler_params=pltpu.CompilerParams(dimension_semantics=("arbitrary", "arbitrary", "arbitrary"),
                                             vmem_limit_bytes=VMEM_LIMIT),
        name="mla_attn",
    )(q_t, k_nope, k_pe, v_t)


def _final_kernel(y_ref, o_ref, sg_ref, w_ref, g_ref, out_ref):
    gated = (sg_ref[...].astype(F32) * o_ref[...].astype(F32)).astype(BF16)
    out_ref[...] = _rms(y_ref[...] + _dot(gated, w_ref[...]), g_ref[...])


def _final(y, o, sg, w_out, final_norm, seq):
    t = y.shape[0]
    tm = min(PROJ_ROWS, seq)
    rows = lambda n: pl.BlockSpec((tm, n), lambda i: (i, 0))
    full = lambda a: pl.BlockSpec(a.shape, lambda i: (0,) * a.ndim)
    w = w_out.astype(BF16)
    g = final_norm.reshape(1, -1)
    return pl.pallas_call(
        _final_kernel,
        grid=(t // tm,),
        in_specs=[rows(D_MODEL), rows(MLA_WIDTH), rows(MLA_WIDTH), full(w), full(g)],
        out_specs=rows(D_MODEL),
        out_shape=jax.ShapeDtypeStruct((t, D_MODEL), F32),
        compiler_params=pltpu.CompilerParams(dimension_semantics=("arbitrary",),
                                             vmem_limit_bytes=VMEM_LIMIT),
        name="final",
    )(y, o, sg, w, g)


def _rope_constants():
    half = NSA_HEAD_DIM // 2
    inv_freq = ROPE_THETA ** (-jnp.arange(half, dtype=F32) / half)
    invf = jnp.tile(inv_freq, LANES // half).reshape(1, LANES)
    sgn = jnp.tile(jnp.concatenate([-jnp.ones((half,), F32), jnp.ones((half,), F32)]), LANES // (2 * half))
    return invf, sgn.reshape(1, LANES), inv_freq.reshape(half, 1)


def kernel(x, positions, a_norm, a_w_in, a_pe_k, a_pe_v, a_w_ck1, a_w_ck2, a_w_cv1, a_w_cv2, a_conv_w, a_w_out, c_norm, c_w_in, c_q_norm, c_kv_norm, c_w_uq, c_w_ukv, c_w_out, final_norm):
    b, seq, d = x.shape
    assert d == D_MODEL and NSA_HEAD_DIM == MLA_ROPE_DIM
    assert a_norm.shape[0] == 1 and c_norm.shape[0] == 1
    t = b * seq
    invf, sgn, invf_col = _rope_constants()
    x2d = x.reshape(t, d)
    pos_row = positions.reshape(1, t)

    q_t, kc, vc, ksel, vsl_t, kw, vw_t, gates, sga, mixb = _even_proj(
        x2d, pos_row, a_norm[0], a_w_in[0], a_conv_w[0], invf_col, seq)
    nchunk = seq // CMP_STRIDE
    pos_cmp = jnp.pad(positions[:, CMP_BLOCK - 1::CMP_STRIDE], ((0, 0), (0, 1)))[:, :nchunk, None]
    kcmp, vcmp_t = _compress(kc.reshape(b, seq, -1), vc.reshape(b, seq, -1), pos_cmp, a_pe_k[0], a_pe_v[0],
                             a_w_ck1[0], a_w_ck2[0], a_w_cv1[0], a_w_cv2[0], invf, sgn)
    r3 = lambda a: a.reshape(b, seq, a.shape[-1])
    mixa = _nsa_attention(q_t, kcmp, vcmp_t, r3(ksel), vsl_t, r3(kw), vw_t, r3(gates), r3(sga))

    y, mq_t, mkn, mkpe, mv_t, sg = _odd_proj(x2d, mixa.reshape(t, -1), mixb, pos_row, a_w_out[0], c_norm[0],
                                      c_w_in[0], c_q_norm[0], c_kv_norm[0], c_w_uq[0], c_w_ukv[0], invf_col, seq)
    o = _mla_attention(mq_t, r3(mkn), r3(mkpe), mv_t)
    out = _final(y, o.reshape(t, -1), sg, c_w_out[0], final_norm, seq)
    return out.reshape(b, seq, d)
```

```python
import functools

import jax
import jax.numpy as jnp
import numpy as np
from jax import lax
from jax.experimental import pallas as pl
from jax.experimental.pallas import tpu as pltpu

F32 = jnp.float32
BF16 = jnp.bfloat16

D_MODEL = 1024
ROPE_THETA = 10000.0
RMS_EPS = 1e-6
NEG_INF = -1e30
FORCE_SCORE = 1e4
SEL_PENALTY = NEG_INF

NSA_HEADS = 8
NSA_KV_GROUPS = 2
NSA_HPG = NSA_HEADS // NSA_KV_GROUPS
NSA_HEAD_DIM = 64
NSA_WIDTH = NSA_HEADS * NSA_HEAD_DIM
NSA_KV_WIDTH = NSA_KV_GROUPS * NSA_HEAD_DIM
CMP_BLOCK = 32
CMP_STRIDE = 16
CMP_HIDDEN = 2 * NSA_HEAD_DIM
SLC_BLOCK = 64
SLC_TOPK = 16
SLC_SHIFT = SLC_BLOCK.bit_length() - 1
SLC_MAX_BLOCKS = 64
WINDOW = 512
CONV_WIDTH = D_MODEL - NSA_WIDTH
CONV_K = 3

MLA_HEADS = 8
MLA_NOPE_DIM = 128
MLA_ROPE_DIM = 64
MLA_V_DIM = 128
MLA_Q_RANK = 256
MLA_KV_RANK = 256
MLA_WIDTH = MLA_HEADS * MLA_V_DIM
MLA_QK_PAD = 256

LANES = 128
SUBLANES = 8
VMEM_LIMIT = 56 * 1024 * 1024

PROJ_ROWS = 1024
NSA_TQ = 256
NSA_TILES_PER_STEP = 1
MLA_TQ = 512
MLA_HEADS_PER_STEP = 4
MLA_ONES_ROWS = 16
NSA_ONES_ROWS = 16
NSA_CMP_ROWS = NSA_HEAD_DIM + SLC_MAX_BLOCKS + NSA_ONES_ROWS
LOG2_E = 1.4426950408889634
MAX_STALE_EXCESS = 64.0


def _dot(a, b):
    return jnp.dot(a, b, preferred_element_type=F32)


def _dot_nt(a, b):
    return lax.dot_general(a, b, (((1,), (1,)), ((), ())), preferred_element_type=F32)


def _silu(x):
    return x * jax.nn.sigmoid(x)


def _rms(x, g):
    return x * lax.rsqrt(jnp.mean(x * x, axis=-1, keepdims=True) + RMS_EPS) * g


def _rope_tables(pos_col, invf_row, sign_row):
    ang = pos_col.astype(F32) * invf_row
    return jnp.cos(ang), jnp.sin(ang) * sign_row


def _rope_slab(s, cos2, sin2):
    half = NSA_HEAD_DIM // 2
    first_half = (lax.broadcasted_iota(jnp.int32, s.shape, 1) & half) == 0
    swapped = jnp.where(first_half, pltpu.roll(s, LANES - half, 1), pltpu.roll(s, half, 1))
    return s * cos2 + swapped * sin2


def _rope_tables_t(invf_col, pos_row):
    ang = invf_col * pos_row.astype(F32)
    return jnp.cos(ang), jnp.sin(ang)


def _flash_update_stale(s_t, v_t, m_ref, acc_ref, exc_ref, cols):
    m_old = m_ref[:, cols]
    p = jnp.exp2(s_t - m_old)
    cmax = jnp.max(s_t, axis=0, keepdims=True)
    m_new = jnp.maximum(m_old, cmax)
    rows = v_t.shape[0]
    acc_ref[0:rows, cols] = jnp.exp2(m_old - m_new) * (acc_ref[0:rows, cols] + _dot(v_t, p.astype(BF16)))
    m_ref[:, cols] = m_new
    exc_ref[:, cols] = jnp.maximum(exc_ref[:, cols], cmax - m_old)


def _flash_update_t(s_t, v_t, m_ref, acc_ref, cols):
    m_old = m_ref[:, cols]
    m_new = jnp.maximum(m_old, jnp.max(s_t, axis=0, keepdims=True))
    p = jnp.exp2(s_t - m_new)
    rows = v_t.shape[0]
    acc_ref[0:rows, cols] = jnp.exp2(m_old - m_new) * acc_ref[0:rows, cols] + _dot(v_t, p.astype(BF16))
    m_ref[:, cols] = m_new


def _even_proj_kernel(x_ref, posr_ref, g_ref, wqt_ref, wkv_ref, wg_ref, wga_ref, wcb_ref, wcc_ref, wch_ref,
                      wgb_ref, convw_ref, invfc_ref,
                      qt_ref, kc_ref, vc_ref, ksel_ref, vslt_ref, kw_ref, vwt_ref, gates_ref, sga_ref,
                      mixb_ref, carry_ref, *, tiles_per_seq, tm, kc):
    i = pl.program_id(0)
    xn = _rms(x_ref[...], g_ref[...]).astype(BF16)
    cos_t, sin_t = _rope_tables_t(invfc_ref[...], posr_ref[...])
    half = NSA_HEAD_DIM // 2
    lane = lax.broadcasted_iota(jnp.int32, (tm, LANES), 1)
    low = lane < NSA_HEAD_DIM

    def rope_t(x_t):
        out = []
        for r in range(0, x_t.shape[0], 2 * half):
            x1, x2 = x_t[r:r + half], x_t[r + half:r + 2 * half]
            out += [x1 * cos_t - x2 * sin_t, x2 * cos_t + x1 * sin_t]
        return jnp.concatenate(out, axis=0)

    def store_values_t(ref, v):
        v_t = v.T.astype(BF16)
        ones = jnp.ones((NSA_ONES_ROWS, kc), BF16)
        for c in range(tm // kc):
            for g in range(NSA_KV_GROUPS):
                ref[c, g, 0:NSA_HEAD_DIM, :] = v_t[g * NSA_HEAD_DIM:(g + 1) * NSA_HEAD_DIM, c * kc:(c + 1) * kc]
                ref[c, g, NSA_HEAD_DIM:, :] = ones

    scale = NSA_HEAD_DIM ** -0.5 * LOG2_E
    qt_ref[...] = (rope_t(_dot_nt(wqt_ref[...], xn)) * scale).astype(BF16)

    seg = _dot(xn, wkv_ref[...])
    kc_ref[...] = seg[:, 0:LANES]
    vc_ref[...] = seg[:, LANES:2 * LANES]
    ksl = rope_t(seg[:, 2 * LANES:3 * LANES].T).T
    store_values_t(vslt_ref, seg[:, 3 * LANES:4 * LANES])
    kw_ref[...] = rope_t(seg[:, 4 * LANES:5 * LANES].T).T.astype(BF16)
    store_values_t(vwt_ref, seg[:, 5 * LANES:6 * LANES])

    spos = (i % tiles_per_seq) * tm + lax.broadcasted_iota(jnp.int32, (tm, LANES), 0)
    blk = lax.shift_right_logical(spos, SLC_SHIFT)
    ksel_ref[:, 0:LANES] = jnp.where(low, ksl, (lane - NSA_HEAD_DIM == blk).astype(F32)).astype(BF16)
    ksel_ref[:, LANES:2 * LANES] = jnp.where(low, (lane == blk).astype(F32), ksl).astype(BF16)

    gates_ref[...] = jax.nn.sigmoid(_dot(xn, wg_ref[...]))
    sga_ref[...] = _silu(_dot(xn, wga_ref[...])).astype(BF16)

    u = _dot(xn, wcc_ref[...]) * _dot(xn, wch_ref[...])

    @pl.when(i % tiles_per_seq == 0)
    def _():
        carry_ref[...] = jnp.zeros_like(carry_ref)

    row = lax.broadcasted_iota(jnp.int32, (tm, CONV_WIDTH), 0)
    prev1 = carry_ref[SUBLANES - 1:SUBLANES, :]
    prev2 = carry_ref[SUBLANES - 2:SUBLANES - 1, :]
    u1 = jnp.where(row == 0, prev1, pltpu.roll(u, 1, 0))
    u2 = jnp.where(row == 0, prev2, jnp.where(row == 1, prev1, pltpu.roll(u, 2, 0)))
    carry_ref[...] = u[tm - SUBLANES:tm, :]
    w = convw_ref[...]
    y = w[0:1, :] * u2 + w[1:2, :] * u1 + w[2:3, :] * u
    cb = _dot(xn, wcb_ref[...])
    mixb_ref[...] = (_silu(_dot(xn, wgb_ref[...])) * (cb * y)).astype(BF16)


def _even_proj(x2d, pos_row, norm_g, w_in, conv_w, invf_col, seq):
    t = x2d.shape[0]
    tm = min(PROJ_ROWS, seq)
    kc = min(NSA_TQ, seq)
    assert seq % tm == 0 and t % tm == 0 and tm % kc == 0 and conv_w.shape[0] == CONV_K == 3
    sizes = (NSA_WIDTH,) + (NSA_KV_WIDTH,) * 6 + (3 * NSA_HEADS, NSA_WIDTH) + (CONV_WIDTH,) * 4
    offs = np.concatenate([[0], np.cumsum(sizes)])
    col = lambda k: w_in[:, offs[k]:offs[k + 1]]
    wqt = col(0).T.astype(BF16)
    wkv = jnp.concatenate([col(k) for k in range(1, 7)], axis=1).astype(BF16)
    wg = jnp.pad(col(7), ((0, 0), (0, LANES - 3 * NSA_HEADS))).astype(BF16)
    wga, wcb, wcc, wch, wgb = (col(k).astype(BF16) for k in range(8, 13))

    full = lambda a: pl.BlockSpec(a.shape, lambda i: (0,) * a.ndim)
    rows = lambda n: pl.BlockSpec((tm, n), lambda i: (i, 0))
    cols = lambda n: pl.BlockSpec((n, tm), lambda i: (0, i))
    vrows = NSA_HEAD_DIM + NSA_ONES_ROWS
    values_t = jax.ShapeDtypeStruct((t // kc, NSA_KV_GROUPS, vrows, kc), BF16)
    values_t_spec = pl.BlockSpec((tm // kc, NSA_KV_GROUPS, vrows, kc), lambda i: (i, 0, 0, 0))
    out_shapes = [
        jax.ShapeDtypeStruct((NSA_WIDTH, t), BF16),
        jax.ShapeDtypeStruct((t, NSA_KV_WIDTH), F32),
        jax.ShapeDtypeStruct((t, NSA_KV_WIDTH), F32),
        jax.ShapeDtypeStruct((t, 2 * LANES), BF16),
        values_t,
        jax.ShapeDtypeStruct((t, NSA_KV_WIDTH), BF16),
        values_t,
        jax.ShapeDtypeStruct((t, LANES), F32),
        jax.ShapeDtypeStruct((t, NSA_WIDTH), BF16),
        jax.ShapeDtypeStruct((t, CONV_WIDTH), BF16),
    ]
    out_specs = [cols(NSA_WIDTH), rows(NSA_KV_WIDTH), rows(NSA_KV_WIDTH), rows(2 * LANES), values_t_spec,
                 rows(NSA_KV_WIDTH), values_t_spec, rows(LANES), rows(NSA_WIDTH), rows(CONV_WIDTH)]
    ins = [x2d, pos_row, norm_g.reshape(1, -1), wqt, wkv, wg, wga, wcb, wcc, wch, wgb, conv_w, invf_col]
    in_specs = [rows(D_MODEL), cols(1)] + [full(a) for a in ins[2:]]
    return pl.pallas_call(
        functools.partial(_even_proj_kernel, tiles_per_seq=seq // tm, tm=tm, kc=kc),
        grid=(t // tm,),
        in_specs=in_specs,
        out_specs=out_specs,
        out_shape=out_shapes,
        scratch_shapes=[pltpu.VMEM((SUBLANES, CONV_WIDTH), F32)],
        compiler_params=pltpu.CompilerParams(dimension_semantics=("arbitrary",),
                                             vmem_limit_bytes=VMEM_LIMIT),
        name="even_proj",
    )(*ins)


def _compress_kernel(kc_ref, vc_ref, pos_ref, pek_ref, pev_ref, w1k_ref, w2k_ref, w1v_ref, w2vt_ref,
                     invf_ref, sgn_ref, ov_ref, kcmp_ref, vcmpt_ref, *, nchunk):
    half = NSA_KV_GROUPS * CMP_HIDDEN
    rows16 = CMP_BLOCK // 2

    def hidden(raw_ref, pe_ref, w1_ref):
        acc = jnp.zeros((nchunk, 2 * half), F32)
        for l in range(rows16):
            x = raw_ref[0, pl.ds(l, nchunk, stride=CMP_STRIDE), :]
            lhs = jnp.concatenate([(x + pe_ref[l:l + 1, :]).astype(BF16),
                                   (x + pe_ref[rows16 + l:rows16 + l + 1, :]).astype(BF16)], axis=1)
            acc = acc + _dot(lhs, w1_ref[l])
        return _silu(acc[:, 0:half] + pltpu.roll(acc[:, half:], nchunk - 1, 0)).astype(BF16)

    cos2, sin2 = _rope_tables(pos_ref[0], invf_ref[...], sgn_ref[...])
    kcmp = _dot(hidden(kc_ref, pek_ref, w1k_ref), w2k_ref[...])
    kcmp_ref[0] = _rope_slab(kcmp, cos2, sin2).astype(BF16)
    vcmp_t = _dot_nt(w2vt_ref[...], hidden(vc_ref, pev_ref, w1v_ref)).astype(BF16)
    for g in range(NSA_KV_GROUPS):
        vcmpt_ref[0, g, 0:NSA_HEAD_DIM, :] = vcmp_t[g * NSA_HEAD_DIM:(g + 1) * NSA_HEAD_DIM]
        vcmpt_ref[0, g, NSA_HEAD_DIM:NSA_HEAD_DIM + SLC_MAX_BLOCKS, :] = ov_ref[...]
        vcmpt_ref[0, g, NSA_HEAD_DIM + SLC_MAX_BLOCKS:, :] = jnp.ones((NSA_ONES_ROWS, nchunk), BF16)


def _overlap_t(ncp, n_slc):
    cmp_starts = np.arange(ncp) * CMP_STRIDE
    slc_starts = np.arange(SLC_MAX_BLOCKS) * SLC_BLOCK
    ov = ((cmp_starts[None, :] < slc_starts[:, None] + SLC_BLOCK)
          & (cmp_starts[None, :] + CMP_BLOCK > slc_starts[:, None])
          & (np.arange(SLC_MAX_BLOCKS)[:, None] < n_slc))
    return jnp.asarray(ov.astype(np.float32), BF16)


def _compress_weights(pe, w1, w2):
    eye = jnp.eye(NSA_KV_GROUPS, dtype=F32)
    rows16 = CMP_BLOCK // 2
    w1r = w1.reshape(CMP_BLOCK, NSA_HEAD_DIM, CMP_HIDDEN)
    big = lambda part: jnp.einsum('ldn,gh->lgdhn', part, eye).reshape(
        rows16, NSA_KV_WIDTH, NSA_KV_GROUPS * CMP_HIDDEN)
    top, bot = big(w1r[:rows16]), big(w1r[rows16:])
    zero = jnp.zeros_like(top)
    w1big = jnp.concatenate([jnp.concatenate([top, zero], axis=2),
                             jnp.concatenate([zero, bot], axis=2)], axis=1).astype(BF16)
    w2big = jnp.einsum('nd,gh->gnhd', w2, eye).reshape(NSA_KV_GROUPS * CMP_HIDDEN, NSA_KV_WIDTH).astype(BF16)
    pe_rows = jnp.tile(pe, (1, NSA_KV_GROUPS))
    return pe_rows, w1big, w2big


def _compress(kc, vc, pos_cmp, pe_k, pe_v, w_ck1, w_ck2, w_cv1, w_cv2, invf, sgn):
    b, seq, _ = kc.shape
    nchunk = seq // CMP_STRIDE
    pek, w1k, w2k = _compress_weights(pe_k, w_ck1, w_ck2)
    pev, w1v, w2v = _compress_weights(pe_v, w_cv1, w_cv2)
    ins = [kc, vc, pos_cmp,
           pek, pev, w1k, w2k, w1v, w2v.T, invf, sgn, _overlap_t(nchunk, seq // SLC_BLOCK)]
    full = lambda a: pl.BlockSpec(a.shape, lambda i: (0,) * a.ndim)
    per_b = lambda n: pl.BlockSpec((1, nchunk, n), lambda i: (i, 0, 0))
    out_shape = [jax.ShapeDtypeStruct((b, nchunk, NSA_KV_WIDTH), BF16),
                 jax.ShapeDtypeStruct((b, NSA_KV_GROUPS, NSA_CMP_ROWS, nchunk), BF16)]
    return pl.pallas_call(
        functools.partial(_compress_kernel, nchunk=nchunk),
        grid=(b,),
        in_specs=[pl.BlockSpec((1, seq, NSA_KV_WIDTH), lambda i: (i, 0, 0))] * 2 + [per_b(1)]
        + [full(a) for a in ins[3:]],
        out_specs=[per_b(NSA_KV_WIDTH),
                   pl.BlockSpec((1, NSA_KV_GROUPS, NSA_CMP_ROWS, nchunk), lambda i: (i, 0, 0, 0))],
        out_shape=out_shape,
        compiler_params=pltpu.CompilerParams(dimension_semantics=("arbitrary",),
                                             vmem_limit_bytes=VMEM_LIMIT),
        name="compress",
    )(*ins)


def _nsa_kernel(qt_ref, kcmp_ref, vcmp_t_ref, ksel_ref, vsl_t_ref, kw_ref, vw_t_ref, gates_ref, sga_ref,
                out_ref, qz_scr, qa_scr, v_scr, rank_scr, m_scr, acc_scr, mix_scr, mix2_scr, exc_scr,
                *, tq, kc, n_cmp, n_top, tiles):
    for sub in range(tiles):
        _nsa_tile(pl.program_id(1) * tiles + sub, pl.ds(sub * tq, tq),
                  qt_ref, kcmp_ref, vcmp_t_ref, ksel_ref, vsl_t_ref, kw_ref, vw_t_ref, gates_ref, sga_ref,
                  out_ref, qz_scr, qa_scr, v_scr, rank_scr, m_scr, acc_scr, mix_scr, mix2_scr, exc_scr,
                  tq=tq, kc=kc, n_cmp=n_cmp, n_top=n_top)


def _nsa_tile(tile_idx, tile_rows, qt_ref, kcmp_ref, vcmp_t_ref, ksel_ref, vsl_t_ref, kw_ref, vw_t_ref, gates_ref,
              sga_ref, out_ref, qz_scr, qa_scr, v_scr, rank_scr, m_scr, acc_scr, mix_scr, mix2_scr, exc_scr,
              *, tq, kc, n_cmp, n_top):
    t0 = tile_idx * tq
    n = NSA_HPG * tq
    ncp = kcmp_ref.shape[1]
    hd = NSA_HEAD_DIM
    q_t = qt_ref[:, tile_rows]
    gates_t = gates_ref[0, tile_rows, :].T
    kq = lax.broadcasted_iota(jnp.int32, (kc, tq), 0) - lax.broadcasted_iota(jnp.int32, (kc, tq), 1)
    causal = kq <= 0
    newer = kq > 0

    def add_branch(branch, g, o_t):
        for h in range(NSA_HPG):
            head = NSA_HPG * g + h
            gate = gates_t[head * 3 + branch:head * 3 + branch + 1, :]
            term = gate * o_t[:, h * tq:(h + 1) * tq]
            r = pl.ds(head * hd, hd)
            if branch == 0:
                mix_scr[r, :] = term
            elif branch == 1:
                mix2_scr[r, :] = term
            else:
                mix2_scr[r, :] += term

    def reset_flash():
        m_scr[...] = jnp.full(m_scr.shape, NEG_INF, F32)
        acc_scr[...] = jnp.zeros(acc_scr.shape, F32)

    def flash_step(q_scr, k_ref, v_t_ref, chunks):
        chains = []
        for j, mask, single_pass in chunks:
            start = pl.multiple_of(j * kc, kc)
            for g in range(NSA_KV_GROUPS):
                col0 = g * LANES if k_ref is ksel_ref else 0
                k = k_ref[0, pl.ds(start, kc), col0:col0 + LANES]
                for h in range(NSA_HPG):
                    s_t = _dot(k, q_scr[g, :, h * tq:(h + 1) * tq])
                    chains.append((j, mask, single_pass, g, g * n + h * tq, s_t))
        for j, mask, single_pass, g, c0, s_t in chains:
            if mask is not None:
                s_t = jnp.where(mask, s_t, NEG_INF)
            if single_pass:
                _flash_update_stale(s_t, v_t_ref[0, j, g], m_scr, acc_scr, exc_scr, pl.ds(c0, tq))
            else:
                _flash_update_t(s_t, v_t_ref[0, j, g], m_scr, acc_scr, pl.ds(c0, tq))

    def finish_flash(branch):
        for g in range(NSA_KV_GROUPS):
            c = pl.ds(g * n, n)
            add_branch(branch, g, acc_scr[0:hd, c] / acc_scr[hd:hd + 1, c])

    zeros_q = jnp.zeros((hd, n), BF16)
    qgs = []
    for g in range(NSA_KV_GROUPS):
        qg = jnp.concatenate([q_t[(NSA_HPG * g + h) * hd:(NSA_HPG * g + h + 1) * hd, :] for h in range(NSA_HPG)],
                             axis=1)
        qgs.append(qg)
        qz_scr[g] = jnp.concatenate([qg, zeros_q] if g == 0 else [zeros_q, qg], axis=0)

    reset_flash()
    cq = (lax.broadcasted_iota(jnp.int32, (ncp, tq), 0) * CMP_STRIDE
          - lax.broadcasted_iota(jnp.int32, (ncp, tq), 1))
    cmp_valid = (cq <= t0 - (CMP_BLOCK - 1)) & (lax.broadcasted_iota(jnp.int32, (ncp, tq), 0) < n_cmp)
    chains = [(g * n + h * tq, _dot(kcmp_ref[0], qz_scr[g, :, h * tq:(h + 1) * tq]))
              for g in range(NSA_KV_GROUPS) for h in range(NSA_HPG)]
    for c0, s_t in chains:
        _flash_update_t(jnp.where(cmp_valid, s_t, NEG_INF), vcmp_t_ref[0, c0 // n], m_scr, acc_scr, pl.ds(c0, tq))
    seen = (t0 + lax.broadcasted_iota(jnp.int32, (1, tq), 1) >= CMP_BLOCK - 1).astype(F32)
    seen4 = jnp.concatenate([seen] * NSA_HPG, axis=1)
    imp_rows = slice(hd, hd + SLC_MAX_BLOCKS)
    den_row = slice(hd + SLC_MAX_BLOCKS, hd + SLC_MAX_BLOCKS + 1)

    for g in range(NSA_KV_GROUPS):
        qg = qgs[g]
        c = pl.ds(g * n, n)
        inv_l = seen4 / acc_scr[den_row, c]
        add_branch(0, g, acc_scr[0:hd, c] * inv_l)
        pooled = acc_scr[imp_rows, c] * inv_l
        imp_t = functools.reduce(jnp.add, [pooled[:, h * tq:(h + 1) * tq] for h in range(NSA_HPG)])
        nb = lax.broadcasted_iota(jnp.int32, (SLC_MAX_BLOCKS, tq), 0)
        tqv = t0 + lax.broadcasted_iota(jnp.int32, (SLC_MAX_BLOCKS, tq), 1)
        cur = lax.shift_right_logical(tqv, SLC_SHIFT)
        forced = (nb == 0) | (nb == cur) | (nb == cur - 1)
        v_scr[...] = jnp.where(nb * SLC_BLOCK <= tqv, jnp.where(forced, FORCE_SCORE, imp_t), -1.0)
        groups = SLC_MAX_BLOCKS // SUBLANES
        sub = lax.broadcasted_iota(jnp.int32, (SUBLANES, tq), 0)
        vch = [v_scr[r * SUBLANES:(r + 1) * SUBLANES, :] for r in range(groups)]
        rank_scr[...] = jnp.zeros(rank_scr.shape, jnp.int32)
        for mg in range(groups):
            @pl.when(mg * SUBLANES * SLC_BLOCK < t0 + tq)
            def _():
                count = [jnp.zeros((SUBLANES, tq), jnp.int32) for _ in range(groups)]
                for mblk in range(mg * SUBLANES, (mg + 1) * SUBLANES):
                    vm = jnp.broadcast_to(v_scr[mblk:mblk + 1, :], (SUBLANES, tq))
                    for r in range(groups):
                        if r > mg:
                            before = vm >= vch[r]
                        elif r < mg:
                            before = vm > vch[r]
                        else:
                            before = (vm > vch[r]) | ((vm == vch[r]) & (sub > mblk - r * SUBLANES))
                        count[r] = count[r] + jnp.where(before, 1, 0)
                for r in range(groups):
                    rank_scr[r * SUBLANES:(r + 1) * SUBLANES, :] += count[r]
        pen = jnp.where(rank_scr[...] < n_top, 0.0, SEL_PENALTY)
        pen4 = jnp.concatenate([pen.astype(BF16)] * NSA_HPG, axis=1)
        qa_scr[g] = jnp.concatenate([qg, pen4] if g == 0 else [pen4, qg], axis=0)

    jd = t0 // kc

    def attend(single_pass):
        reset_flash()
        sel = functools.partial(flash_step, qa_scr, ksel_ref, vsl_t_ref)
        sel([(jd, causal, False)])
        lax.fori_loop(0, jd // 2, lambda i, cr: (
            sel([(2 * i, None, single_pass), (2 * i + 1, None, single_pass)]), cr)[1], 0)

        @pl.when(jd % 2 == 1)
        def _():
            sel([(jd - 1, None, single_pass)])

        finish_flash(1)

        reset_flash()
        win = functools.partial(flash_step, qz_scr, kw_ref, vw_t_ref)
        assert WINDOW == 2 * kc
        win([(jd, causal, False)])

        @pl.when(jd == 1)
        def _():
            win([(jd - 1, None, single_pass)])

        @pl.when(jd >= 2)
        def _():
            win([(jd - 1, None, single_pass), (jd - 2, newer, single_pass)])

        finish_flash(2)

    exc_scr[...] = jnp.full(exc_scr.shape, NEG_INF, F32)
    attend(single_pass=True)

    @pl.when(jnp.max(exc_scr[...]) > MAX_STALE_EXCESS)
    def _():
        attend(single_pass=False)

    out_ref[0, tile_rows, :] = (sga_ref[0, tile_rows, :].astype(F32) * (mix_scr[...] + mix2_scr[...]).T).astype(BF16)


def _nsa_attention(q_t, kcmp, vcmp_t, ksel, vsl_t, kw, vw_t, gates, sga):
    b, seq, _ = ksel.shape
    tq = min(NSA_TQ, seq)
    nq = seq // tq
    n_slc = seq // SLC_BLOCK
    assert n_slc <= SLC_MAX_BLOCKS and seq % tq == 0 and WINDOW % tq == 0 and tq & (tq - 1) == 0
    ncp = kcmp.shape[1]
    n_cmp = (seq - CMP_BLOCK) // CMP_STRIDE + 1
    vsl_t = vsl_t.reshape((b, nq) + vsl_t.shape[1:])
    vw_t = vw_t.reshape((b, nq) + vw_t.shape[1:])
    n = NSA_HPG * tq
    tiles = NSA_TILES_PER_STEP if nq % NSA_TILES_PER_STEP == 0 else 1
    tile = lambda w: pl.BlockSpec((1, tiles * tq, w), lambda bi, qi: (bi, qi, 0))
    per_b = lambda a: pl.BlockSpec((1,) + a.shape[1:], lambda bi, qi: (bi,) + (0,) * (a.ndim - 1))
    return pl.pallas_call(
        functools.partial(_nsa_kernel, tq=tq, kc=tq, n_cmp=n_cmp, n_top=min(SLC_TOPK, n_slc), tiles=tiles),
        grid=(b, nq // tiles),
        in_specs=[pl.BlockSpec((NSA_WIDTH, tiles * tq), lambda bi, qi: (0, bi * (nq // tiles) + qi)),
                  per_b(kcmp), per_b(vcmp_t), per_b(ksel), per_b(vsl_t), per_b(kw), per_b(vw_t),
                  tile(LANES), tile(NSA_WIDTH)],
        out_specs=tile(NSA_WIDTH),
        out_shape=jax.ShapeDtypeStruct((b, seq, NSA_WIDTH), BF16),
        scratch_shapes=[pltpu.VMEM((NSA_KV_GROUPS, LANES, n), BF16),
                        pltpu.VMEM((NSA_KV_GROUPS, LANES, n), BF16),
                        pltpu.VMEM((SLC_MAX_BLOCKS, tq), F32),
                        pltpu.VMEM((SLC_MAX_BLOCKS, tq), jnp.int32),
                        pltpu.VMEM((1, NSA_KV_GROUPS * n), F32),
                        pltpu.VMEM((NSA_CMP_ROWS, NSA_KV_GROUPS * n), F32),
                        pltpu.VMEM((NSA_WIDTH, tq), F32),
                        pltpu.VMEM((NSA_WIDTH, tq), F32),
                        pltpu.VMEM((1, NSA_KV_GROUPS * n), F32)],
        compiler_params=pltpu.CompilerParams(dimension_semantics=("arbitrary", "arbitrary"),
                                             vmem_limit_bytes=VMEM_LIMIT),
        name="nsa_attn",
    )(q_t, kcmp, vcmp_t, ksel, vsl_t, kw, vw_t, gates, sga)


def _odd_proj_kernel(x_ref, mixa_ref, mixb_ref, posr_ref, wout_ref, g_ref, wc_ref, wgate_ref, qn_ref, kvn_ref,
                     wuqt_ref, wuk_ref, wuvt_ref, invfc_ref,
                     y_ref, qt_ref, kn_ref, kpe_ref, vt_ref, sg_ref, *, tm):
    y = (x_ref[...] + _dot(mixa_ref[...], wout_ref[0:NSA_WIDTH, :])
         + _dot(mixb_ref[...], wout_ref[NSA_WIDTH:, :]))
    y_ref[...] = y
    yn = _rms(y, g_ref[...]).astype(BF16)
    cos_t, sin_t = _rope_tables_t(invfc_ref[...], posr_ref[...])
    half = MLA_ROPE_DIM // 2

    def rope_t(x1, x2):
        return x1 * cos_t - x2 * sin_t, x2 * cos_t + x1 * sin_t

    seg = _dot(yn, wc_ref[...])
    kr_t = seg[:, MLA_Q_RANK + MLA_KV_RANK:].T
    kp1, kp2 = rope_t(kr_t[0:half], kr_t[half:2 * half])
    kpe_ref[...] = jnp.concatenate([kp1, kp2, kr_t[2 * half:]], axis=0).T.astype(BF16)
    cq = _rms(seg[:, 0:MLA_Q_RANK], qn_ref[...]).astype(BF16)
    ckv = _rms(seg[:, MLA_Q_RANK:MLA_Q_RANK + MLA_KV_RANK], kvn_ref[...]).astype(BF16)
    scale = (MLA_NOPE_DIM + MLA_ROPE_DIM) ** -0.5 * LOG2_E
    qt = _dot_nt(wuqt_ref[...], cq)
    kn_ref[...] = _dot(ckv, wuk_ref[...]).astype(BF16)
    for h in range(MLA_HEADS):
        c0 = h * MLA_QK_PAD
        r0 = c0 + MLA_NOPE_DIM
        qt_ref[c0:r0, :] = (qt[c0:r0] * scale).astype(BF16)
        q1, q2 = rope_t(qt[r0:r0 + half], qt[r0 + half:r0 + 2 * half])
        qt_ref[r0:r0 + half, :] = (q1 * scale).astype(BF16)
        qt_ref[r0 + half:r0 + 2 * half, :] = (q2 * scale).astype(BF16)
        qt_ref[r0 + 2 * half:c0 + MLA_QK_PAD, :] = jnp.zeros((MLA_QK_PAD - MLA_NOPE_DIM - 2 * half, tm), BF16)
    v_t = _dot_nt(wuvt_ref[...], ckv)
    kc = vt_ref.shape[-1]
    for h in range(MLA_HEADS):
        for c in range(tm // kc):
            vt_ref[0, h, c, 0:MLA_V_DIM, :] = v_t[h * MLA_V_DIM:(h + 1) * MLA_V_DIM, c * kc:(c + 1) * kc].astype(BF16)
            vt_ref[0, h, c, MLA_V_DIM:, :] = jnp.ones((MLA_ONES_ROWS, kc), BF16)
    sg_ref[...] = _silu(_dot(yn, wgate_ref[...])).astype(BF16)


def _odd_proj(x2d, mixa, mixb, pos_row, a_w_out, c_norm, c_w_in, q_norm, kv_norm, w_uq, w_ukv, invf_col, seq):
    t = x2d.shape[0]
    tm = min(PROJ_ROWS, seq)
    r2 = MLA_Q_RANK + MLA_KV_RANK + MLA_ROPE_DIM
    wc = jnp.pad(c_w_in[:, :r2], ((0, 0), (0, LANES - MLA_ROPE_DIM))).astype(BF16)
    wgate = c_w_in[:, r2:].astype(BF16)
    wuqt = jnp.pad(w_uq.reshape(MLA_Q_RANK, MLA_HEADS, MLA_NOPE_DIM + MLA_ROPE_DIM),
                   ((0, 0), (0, 0), (0, MLA_QK_PAD - MLA_NOPE_DIM - MLA_ROPE_DIM))
                   ).reshape(MLA_Q_RANK, MLA_HEADS * MLA_QK_PAD).T.astype(BF16)
    wukv = w_ukv.reshape(MLA_KV_RANK, MLA_HEADS, MLA_NOPE_DIM + MLA_V_DIM)
    wuk = wukv[:, :, :MLA_NOPE_DIM].reshape(MLA_KV_RANK, -1).astype(BF16)
    wuvt = wukv[:, :, MLA_NOPE_DIM:].reshape(MLA_KV_RANK, -1).T.astype(BF16)
    kc = min(MLA_TQ, seq)
    nq = seq // kc
    cps = tm // kc
    assert tm % kc == 0 and seq % tm == 0
    vrows = MLA_V_DIM + MLA_ONES_ROWS
    ins = [x2d, mixa, mixb, pos_row, a_w_out.astype(BF16), c_norm.reshape(1, -1), wc, wgate,
           q_norm.reshape(1, -1), kv_norm.reshape(1, -1), wuqt, wuk, wuvt, invf_col]
    full = lambda a: pl.BlockSpec(a.shape, lambda i: (0,) * a.ndim)
    rows = lambda n: pl.BlockSpec((tm, n), lambda i: (i, 0))
    cols = lambda n: pl.BlockSpec((n, tm), lambda i: (0, i))
    out_shapes = [
        jax.ShapeDtypeStruct((t, D_MODEL), F32),
        jax.ShapeDtypeStruct((MLA_HEADS * MLA_QK_PAD, t), BF16),
        jax.ShapeDtypeStruct((t, MLA_HEADS * MLA_NOPE_DIM), BF16),
        jax.ShapeDtypeStruct((t, LANES), BF16),
        jax.ShapeDtypeStruct((t // seq, MLA_HEADS, nq, vrows, kc), BF16),
        jax.ShapeDtypeStruct((t, MLA_WIDTH), BF16),
    ]
    steps = seq // tm
    vt_spec = pl.BlockSpec((1, MLA_HEADS, cps, vrows, kc), lambda i: (i // steps, 0, i % steps, 0, 0))
    out_specs = [rows(D_MODEL), cols(MLA_HEADS * MLA_QK_PAD), rows(MLA_HEADS * MLA_NOPE_DIM), rows(LANES), vt_spec,
                 rows(MLA_WIDTH)]
    return pl.pallas_call(
        functools.partial(_odd_proj_kernel, tm=tm),
        grid=(t // tm,),
        in_specs=[rows(D_MODEL), rows(NSA_WIDTH), rows(CONV_WIDTH), cols(1)] + [full(a) for a in ins[4:]],
        out_specs=out_specs,
        out_shape=out_shapes,
        compiler_params=pltpu.CompilerParams(dimension_semantics=("arbitrary",),
                                             vmem_limit_bytes=VMEM_LIMIT),
        name="odd_proj",
    )(*ins)


def _mla_kernel(qt_ref, kn_ref, kpe_ref, v_t_ref, o_ref, m_scr, acc_scr, exc_scr, *, tq, kc, hp):
    jd = pl.program_id(2)
    causal = (lax.broadcasted_iota(jnp.int32, (kc, tq), 0) <= lax.broadcasted_iota(jnp.int32, (kc, tq), 1))
    hq = tq // 2
    tri = causal[0:hq, 0:hq]
    keep_hi = jnp.concatenate([jnp.ones((hq, hq), jnp.bool_), tri], axis=0)

    def keys(h, start, size):
        return jnp.concatenate([kn_ref[0, pl.ds(start, size), h * MLA_NOPE_DIM:(h + 1) * MLA_NOPE_DIM],
                                kpe_ref[0, pl.ds(start, size), :]], axis=1)

    def step(js, single_pass):
        scores = []
        for j in js:
            start = pl.multiple_of(j * kc, kc)
            for h in range(hp):
                qk = slice(h * MLA_QK_PAD, (h + 1) * MLA_QK_PAD)
                scores.append((j, h, _dot(keys(h, start, kc), qt_ref[qk, :])))
        for j, h, s_t in scores:
            if single_pass:
                _flash_update_stale(s_t, v_t_ref[0, h, j], m_scr, acc_scr, exc_scr, pl.ds(h * tq, tq))
            else:
                _flash_update_t(s_t, v_t_ref[0, h, j], m_scr, acc_scr, pl.ds(h * tq, tq))

    def attend(single_pass):
        m_scr[...] = jnp.full(m_scr.shape, NEG_INF, F32)
        acc_scr[...] = jnp.zeros(acc_scr.shape, F32)
        start = pl.multiple_of(jd * kc, kc)
        scores = []
        for h in range(hp):
            qk = slice(h * MLA_QK_PAD, (h + 1) * MLA_QK_PAD)
            scores.append((_dot(keys(h, start, hq), qt_ref[qk, 0:hq]),
                           _dot(keys(h, start, kc), qt_ref[qk, hq:tq])))
        for h, (s_lo, s_hi) in enumerate(scores):
            v_t = v_t_ref[0, h, jd]
            _flash_update_t(jnp.where(tri, s_lo, NEG_INF), v_t[:, 0:hq], m_scr, acc_scr, pl.ds(h * tq, hq))
            _flash_update_t(jnp.where(keep_hi, s_hi, NEG_INF), v_t, m_scr, acc_scr, pl.ds(h * tq + hq, hq))
        lax.fori_loop(0, jd // 2, lambda i, c: (step([2 * i, 2 * i + 1], single_pass), c)[1], 0)

        @pl.when(jd % 2 == 1)
        def _():
            step([jd - 1], single_pass)

        for h in range(hp):
            c = pl.ds(h * tq, tq)
            o_t = acc_scr[0:MLA_V_DIM, c] / acc_scr[MLA_V_DIM:MLA_V_DIM + 1, c]
            o_ref[0, :, h * MLA_V_DIM:(h + 1) * MLA_V_DIM] = o_t.T.astype(BF16)

    exc_scr[...] = jnp.full(exc_scr.shape, NEG_INF, F32)
    attend(single_pass=True)

    @pl.when(jnp.max(exc_scr[...]) > MAX_STALE_EXCESS)
    def _():
        attend(single_pass=False)


def _mla_attention(q_t, k_nope, k_pe, v_t):
    b, seq, _ = k_pe.shape
    tq = min(MLA_TQ, seq)
    hp = MLA_HEADS_PER_STEP
    nq = seq // tq
    assert seq % tq == 0 and MLA_HEADS % hp == 0
    vrows = MLA_V_DIM + MLA_ONES_ROWS
    return pl.pallas_call(
        functools.partial(_mla_kernel, tq=tq, kc=tq, hp=hp),
        grid=(b, MLA_HEADS // hp, nq),
        in_specs=[pl.BlockSpec((hp * MLA_QK_PAD, tq), lambda bi, h, qi: (h, bi * nq + qi)),
                  pl.BlockSpec((1, seq, hp * MLA_NOPE_DIM), lambda bi, h, qi: (bi, 0, h)),
                  pl.BlockSpec((1, seq, LANES), lambda bi, h, qi: (bi, 0, 0)),
                  pl.BlockSpec((1, hp, nq, vrows, tq), lambda bi, h, qi: (bi, h, 0, 0, 0))],
        out_specs=pl.BlockSpec((1, tq, hp * MLA_V_DIM), lambda bi, h, qi: (bi, qi, h)),
        out_shape=jax.ShapeDtypeStruct((b, seq, MLA_WIDTH), BF16),
        scratch_shapes=[pltpu.VMEM((1, hp * tq), F32), pltpu.VMEM((vrows, hp * tq), F32),
                        pltpu.VMEM((1, hp * tq), F32)],
        compiler_params=pltpu.CompilerParams(dimension_semantics=("arbitrary", "arbitrary", "arbitrary"),
                                             vmem_limit_bytes=VMEM_LIMIT),
        name="mla_attn",
    )(q_t, k_nope, k_pe, v_t)


def _final_kernel(y_ref, o_ref, sg_ref, w_ref, g_ref, out_ref):
    gated = (sg_ref[...].astype(F32) * o_ref[...].astype(F32)).astype(BF16)
    out_ref[...] = _rms(y_ref[...] + _dot(gated, w_ref[...]), g_ref[...])


def _final(y, o, sg, w_out, final_norm, seq):
    t = y.shape[0]
    tm = min(PROJ_ROWS, seq)
    rows = lambda n: pl.BlockSpec((tm, n), lambda i: (i, 0))
    full = lambda a: pl.BlockSpec(a.shape, lambda i: (0,) * a.ndim)
    w = w_out.astype(BF16)
    g = final_norm.reshape(1, -1)
    return pl.pallas_call(
        _final_kernel,
        grid=(t // tm,),
        in_specs=[rows(D_MODEL), rows(MLA_WIDTH), rows(MLA_WIDTH), full(w), full(g)],
        out_specs=rows(D_MODEL),
        out_shape=jax.ShapeDtypeStruct((t, D_MODEL), F32),
        compiler_params=pltpu.CompilerParams(dimension_semantics=("arbitrary",),
                                             vmem_limit_bytes=VMEM_LIMIT),
        name="final",
    )(y, o, sg, w, g)


def _rope_constants():
    half = NSA_HEAD_DIM // 2
    inv_freq = ROPE_THETA ** (-jnp.arange(half, dtype=F32) / half)
    invf = jnp.tile(inv_freq, LANES // half).reshape(1, LANES)
    sgn = jnp.tile(jnp.concatenate([-jnp.ones((half,), F32), jnp.ones((half,), F32)]), LANES // (2 * half))
    return invf, sgn.reshape(1, LANES), inv_freq.reshape(half, 1)


def kernel(x, positions, a_norm, a_w_in, a_pe_k, a_pe_v, a_w_ck1, a_w_ck2, a_w_cv1, a_w_cv2, a_conv_w, a_w_out, c_norm, c_w_in, c_q_norm, c_kv_norm, c_w_uq, c_w_ukv, c_w_out, final_norm):
    b, seq, d = x.shape
    assert d == D_MODEL and NSA_HEAD_DIM == MLA_ROPE_DIM
    assert a_norm.shape[0] == 1 and c_norm.shape[0] == 1
    t = b * seq
    invf, sgn, invf_col = _rope_constants()
    x2d = x.reshape(t, d)
    pos_row = positions.reshape(1, t)

    q_t, kc, vc, ksel, vsl_t, kw, vw_t, gates, sga, mixb = _even_proj(
        x2d, pos_row, a_norm[0], a_w_in[0], a_conv_w[0], invf_col, seq)
    nchunk = seq // CMP_STRIDE
    pos_cmp = jnp.pad(positions[:, CMP_BLOCK - 1::CMP_STRIDE], ((0, 0), (0, 1)))[:, :nchunk, None]
    kcmp, vcmp_t = _compress(kc.reshape(b, seq, -1), vc.reshape(b, seq, -1), pos_cmp, a_pe_k[0], a_pe_v[0],
                             a_w_ck1[0], a_w_ck2[0], a_w_cv1[0], a_w_cv2[0], invf, sgn)
    r3 = lambda a: a.reshape(b, seq, a.shape[-1])
    mixa = _nsa_attention(q_t, kcmp, vcmp_t, r3(ksel), vsl_t, r3(kw), vw_t, r3(gates), r3(sga))

    y, mq_t, mkn, mkpe, mv_t, sg = _odd_proj(x2d, mixa.reshape(t, -1), mixb, pos_row, a_w_out[0], c_norm[0],
                                      c_w_in[0], c_q_norm[0], c_kv_norm[0], c_w_uq[0], c_w_ukv[0], invf_col, seq)
    o = _mla_attention(mq_t, r3(mkn), r3(mkpe), mv_t)
    out = _final(y, o.reshape(t, -1), sg, c_w_out[0], final_norm, seq)
    return out.reshape(b, seq, d)
```

```python
import functools

import jax
import jax.numpy as jnp
import numpy as np
from jax import lax
from jax.experimental import pallas as pl
from jax.experimental.pallas import tpu as pltpu

F32 = jnp.float32
BF16 = jnp.bfloat16

D_MODEL = 1024
ROPE_THETA = 10000.0
RMS_EPS = 1e-6
NEG_INF = -1e30
FORCE_SCORE = 1e4
SEL_PENALTY = NEG_INF

NSA_HEADS = 8
NSA_KV_GROUPS = 2
NSA_HPG = NSA_HEADS // NSA_KV_GROUPS
NSA_HEAD_DIM = 64
NSA_WIDTH = NSA_HEADS * NSA_HEAD_DIM
NSA_KV_WIDTH = NSA_KV_GROUPS * NSA_HEAD_DIM
CMP_BLOCK = 32
CMP_STRIDE = 16
CMP_HIDDEN = 2 * NSA_HEAD_DIM
SLC_BLOCK = 64
SLC_TOPK = 16
SLC_SHIFT = SLC_BLOCK.bit_length() - 1
SLC_MAX_BLOCKS = 64
WINDOW = 512
CONV_WIDTH = D_MODEL - NSA_WIDTH
CONV_K = 3

MLA_HEADS = 8
MLA_NOPE_DIM = 128
MLA_ROPE_DIM = 64
MLA_V_DIM = 128
MLA_Q_RANK = 256
MLA_KV_RANK = 256
MLA_WIDTH = MLA_HEADS * MLA_V_DIM
MLA_QK_PAD = 256

LANES = 128
SUBLANES = 8
VMEM_LIMIT = 56 * 1024 * 1024

PROJ_ROWS = 1024
NSA_TQ = 256
NSA_TILES_PER_STEP = 1
MLA_TQ = 512
MLA_HEADS_PER_STEP = 4
MLA_ONES_ROWS = 16
NSA_ONES_ROWS = 16
NSA_CMP_ROWS = NSA_HEAD_DIM + SLC_MAX_BLOCKS + NSA_ONES_ROWS
LOG2_E = 1.4426950408889634
MAX_STALE_EXCESS = 64.0


def _dot(a, b):
    return jnp.dot(a, b, preferred_element_type=F32)


def _dot_nt(a, b):
    return lax.dot_general(a, b, (((1,), (1,)), ((), ())), preferred_element_type=F32)


def _silu(x):
    return x * jax.nn.sigmoid(x)


def _rms(x, g):
    return x * lax.rsqrt(jnp.mean(x * x, axis=-1, keepdims=True) + RMS_EPS) * g


def _rope_tables(pos_col, invf_row, sign_row):
    ang = pos_col.astype(F32) * invf_row
    return jnp.cos(ang), jnp.sin(ang) * sign_row


def _rope_slab(s, cos2, sin2):
    half = NSA_HEAD_DIM // 2
    first_half = (lax.broadcasted_iota(jnp.int32, s.shape, 1) & half) == 0
    swapped = jnp.where(first_half, pltpu.roll(s, LANES - half, 1), pltpu.roll(s, half, 1))
    return s * cos2 + swapped * sin2


def _rope_tables_t(invf_col, pos_row):
    ang = invf_col * pos_row.astype(F32)
    return jnp.cos(ang), jnp.sin(ang)


def _flash_update_stale(s_t, v_t, m_ref, acc_ref, exc_ref, cols):
    m_old = m_ref[:, cols]
    p = jnp.exp2(s_t - m_old)
    cmax = jnp.max(s_t, axis=0, keepdims=True)
    m_new = jnp.maximum(m_old, cmax)
    rows = v_t.shape[0]
    acc_ref[0:rows, cols] = jnp.exp2(m_old - m_new) * (acc_ref[0:rows, cols] + _dot(v_t, p.astype(BF16)))
    m_ref[:, cols] = m_new
    exc_ref[:, cols] = jnp.maximum(exc_ref[:, cols], cmax - m_old)


def _flash_update_t(s_t, v_t, m_ref, acc_ref, cols):
    m_old = m_ref[:, cols]
    m_new = jnp.maximum(m_old, jnp.max(s_t, axis=0, keepdims=True))
    p = jnp.exp2(s_t - m_new)
    rows = v_t.shape[0]
    acc_ref[0:rows, cols] = jnp.exp2(m_old - m_new) * acc_ref[0:rows, cols] + _dot(v_t, p.astype(BF16))
    m_ref[:, cols] = m_new


def _even_proj_kernel(x_ref, posr_ref, g_ref, wqt_ref, wkv_ref, wg_ref, wga_ref, wcb_ref, wcc_ref, wch_ref,
                      wgb_ref, convw_ref, invfc_ref,
                      qt_ref, kc_ref, vc_ref, ksel_ref, vslt_ref, kw_ref, vwt_ref, gates_ref, sga_ref,
                      mixb_ref, carry_ref, *, tiles_per_seq, tm, kc):
    i = pl.program_id(0)
    xn = _rms(x_ref[...], g_ref[...]).astype(BF16)
    cos_t, sin_t = _rope_tables_t(invfc_ref[...], posr_ref[...])
    half = NSA_HEAD_DIM // 2
    lane = lax.broadcasted_iota(jnp.int32, (tm, LANES), 1)
    low = lane < NSA_HEAD_DIM

    def rope_t(x_t):
        out = []
        for r in range(0, x_t.shape[0], 2 * half):
            x1, x2 = x_t[r:r + half], x_t[r + half:r + 2 * half]
            out += [x1 * cos_t - x2 * sin_t, x2 * cos_t + x1 * sin_t]
        return jnp.concatenate(out, axis=0)

    def store_values_t(ref, v):
        v_t = v.T.astype(BF16)
        ones = jnp.ones((NSA_ONES_ROWS, kc), BF16)
        for c in range(tm // kc):
            for g in range(NSA_KV_GROUPS):
                ref[c, g, 0:NSA_HEAD_DIM, :] = v_t[g * NSA_HEAD_DIM:(g + 1) * NSA_HEAD_DIM, c * kc:(c + 1) * kc]
                ref[c, g, NSA_HEAD_DIM:, :] = ones

    scale = NSA_HEAD_DIM ** -0.5 * LOG2_E
    qt_ref[...] = (rope_t(_dot_nt(wqt_ref[...], xn)) * scale).astype(BF16)

    seg = _dot(xn, wkv_ref[...])
    kc_ref[...] = seg[:, 0:LANES]
    vc_ref[...] = seg[:, LANES:2 * LANES]
    ksl = rope_t(seg[:, 2 * LANES:3 * LANES].T).T
    store_values_t(vslt_ref, seg[:, 3 * LANES:4 * LANES])
    kw_ref[...] = rope_t(seg[:, 4 * LANES:5 * LANES].T).T.astype(BF16)
    store_values_t(vwt_ref, seg[:, 5 * LANES:6 * LANES])

    spos = (i % tiles_per_seq) * tm + lax.broadcasted_iota(jnp.int32, (tm, LANES), 0)
    blk = lax.shift_right_logical(spos, SLC_SHIFT)
    ksel_ref[:, 0:LANES] = jnp.where(low, ksl, (lane - NSA_HEAD_DIM == blk).astype(F32)).astype(BF16)
    ksel_ref[:, LANES:2 * LANES] = jnp.where(low, (lane == blk).astype(F32), ksl).astype(BF16)

    gates_ref[...] = jax.nn.sigmoid(_dot(xn, wg_ref[...]))
    sga_ref[...] = _silu(_dot(xn, wga_ref[...])).astype(BF16)

    u = _dot(xn, wcc_ref[...]) * _dot(xn, wch_ref[...])

    @pl.when(i % tiles_per_seq == 0)
    def _():
        carry_ref[...] = jnp.zeros_like(carry_ref)

    row = lax.broadcasted_iota(jnp.int32, (tm, CONV_WIDTH), 0)
    prev1 = carry_ref[SUBLANES - 1:SUBLANES, :]
    prev2 = carry_ref[SUBLANES - 2:SUBLANES - 1, :]
    u1 = jnp.where(row == 0, prev1, pltpu.roll(u, 1, 0))
    u2 = jnp.where(row == 0, prev2, jnp.where(row == 1, prev1, pltpu.roll(u, 2, 0)))
    carry_ref[...] = u[tm - SUBLANES:tm, :]
    w = convw_ref[...]
    y = w[0:1, :] * u2 + w[1:2, :] * u1 + w[2:3, :] * u
    cb = _dot(xn, wcb_ref[...])
    mixb_ref[...] = (_silu(_dot(xn, wgb_ref[...])) * (cb * y)).astype(BF16)


def _even_proj(x2d, pos_row, norm_g, w_in, conv_w, invf_col, seq):
    t = x2d.shape[0]
    tm = min(PROJ_ROWS, seq)
    kc = min(NSA_TQ, seq)
    assert seq % tm == 0 and t % tm == 0 and tm % kc == 0 and conv_w.shape[0] == CONV_K == 3
    sizes = (NSA_WIDTH,) + (NSA_KV_WIDTH,) * 6 + (3 * NSA_HEADS, NSA_WIDTH) + (CONV_WIDTH,) * 4
    offs = np.concatenate([[0], np.cumsum(sizes)])
    col = lambda k: w_in[:, offs[k]:offs[k + 1]]
    wqt = col(0).T.astype(BF16)
    wkv = jnp.concatenate([col(k) for k in range(1, 7)], axis=1).astype(BF16)
    wg = jnp.pad(col(7), ((0, 0), (0, LANES - 3 * NSA_HEADS))).astype(BF16)
    wga, wcb, wcc, wch, wgb = (col(k).astype(BF16) for k in range(8, 13))

    full = lambda a: pl.BlockSpec(a.shape, lambda i: (0,) * a.ndim)
    rows = lambda n: pl.BlockSpec((tm, n), lambda i: (i, 0))
    cols = lambda n: pl.BlockSpec((n, tm), lambda i: (0, i))
    vrows = NSA_HEAD_DIM + NSA_ONES_ROWS
    values_t = jax.ShapeDtypeStruct((t // kc, NSA_KV_GROUPS, vrows, kc), BF16)
    values_t_spec = pl.BlockSpec((tm // kc, NSA_KV_GROUPS, vrows, kc), lambda i: (i, 0, 0, 0))
    out_shapes = [
        jax.ShapeDtypeStruct((NSA_WIDTH, t), BF16),
        jax.ShapeDtypeStruct((t, NSA_KV_WIDTH), F32),
        jax.ShapeDtypeStruct((t, NSA_KV_WIDTH), F32),
        jax.ShapeDtypeStruct((t, 2 * LANES), BF16),
        values_t,
        jax.ShapeDtypeStruct((t, NSA_KV_WIDTH), BF16),
        values_t,
        jax.ShapeDtypeStruct((t, LANES), F32),
        jax.ShapeDtypeStruct((t, NSA_WIDTH), BF16),
        jax.ShapeDtypeStruct((t, CONV_WIDTH), BF16),
    ]
    out_specs = [cols(NSA_WIDTH), rows(NSA_KV_WIDTH), rows(NSA_KV_WIDTH), rows(2 * LANES), values_t_spec,
                 rows(NSA_KV_WIDTH), values_t_spec, rows(LANES), rows(NSA_WIDTH), rows(CONV_WIDTH)]
    ins = [x2d, pos_row, norm_g.reshape(1, -1), wqt, wkv, wg, wga, wcb, wcc, wch, wgb, conv_w, invf_col]
    in_specs = [rows(D_MODEL), cols(1)] + [full(a) for a in ins[2:]]
    return pl.pallas_call(
        functools.partial(_even_proj_kernel, tiles_per_seq=seq // tm, tm=tm, kc=kc),
        grid=(t // tm,),
        in_specs=in_specs,
        out_specs=out_specs,
        out_shape=out_shapes,
        scratch_shapes=[pltpu.VMEM((SUBLANES, CONV_WIDTH), F32)],
        compiler_params=pltpu.CompilerParams(dimension_semantics=("arbitrary",),
                                             vmem_limit_bytes=VMEM_LIMIT),
        name="even_proj",
    )(*ins)


def _compress_kernel(kc_ref, vc_ref, pos_ref, pek_ref, pev_ref, w1k_ref, w2k_ref, w1v_ref, w2vt_ref,
                     invf_ref, sgn_ref, ov_ref, kcmp_ref, vcmpt_ref, *, nchunk):
    half = NSA_KV_GROUPS * CMP_HIDDEN
    rows16 = CMP_BLOCK // 2

    def hidden(raw_ref, pe_ref, w1_ref):
        acc = jnp.zeros((nchunk, 2 * half), F32)
        for l in range(rows16):
            x = raw_ref[0, pl.ds(l, nchunk, stride=CMP_STRIDE), :]
            lhs = jnp.concatenate([(x + pe_ref[l:l + 1, :]).astype(BF16),
                                   (x + pe_ref[rows16 + l:rows16 + l + 1, :]).astype(BF16)], axis=1)
            acc = acc + _dot(lhs, w1_ref[l])
        return _silu(acc[:, 0:half] + pltpu.roll(acc[:, half:], nchunk - 1, 0)).astype(BF16)

    cos2, sin2 = _rope_tables(pos_ref[0], invf_ref[...], sgn_ref[...])
    kcmp = _dot(hidden(kc_ref, pek_ref, w1k_ref), w2k_ref[...])
    kcmp_ref[0] = _rope_slab(kcmp, cos2, sin2).astype(BF16)
    vcmp_t = _dot_nt(w2vt_ref[...], hidden(vc_ref, pev_ref, w1v_ref)).astype(BF16)
    for g in range(NSA_KV_GROUPS):
        vcmpt_ref[0, g, 0:NSA_HEAD_DIM, :] = vcmp_t[g * NSA_HEAD_DIM:(g + 1) * NSA_HEAD_DIM]
        vcmpt_ref[0, g, NSA_HEAD_DIM:NSA_HEAD_DIM + SLC_MAX_BLOCKS, :] = ov_ref[...]
        vcmpt_ref[0, g, NSA_HEAD_DIM + SLC_MAX_BLOCKS:, :] = jnp.ones((NSA_ONES_ROWS, nchunk), BF16)


def _overlap_t(ncp, n_slc):
    cmp_starts = np.arange(ncp) * CMP_STRIDE
    slc_starts = np.arange(SLC_MAX_BLOCKS) * SLC_BLOCK
    ov = ((cmp_starts[None, :] < slc_starts[:, None] + SLC_BLOCK)
          & (cmp_starts[None, :] + CMP_BLOCK > slc_starts[:, None])
          & (np.arange(SLC_MAX_BLOCKS)[:, None] < n_slc))
    return jnp.asarray(ov.astype(np.float32), BF16)


def _compress_weights(pe, w1, w2):
    eye = jnp.eye(NSA_KV_GROUPS, dtype=F32)
    rows16 = CMP_BLOCK // 2
    w1r = w1.reshape(CMP_BLOCK, NSA_HEAD_DIM, CMP_HIDDEN)
    big = lambda part: jnp.einsum('ldn,gh->lgdhn', part, eye).reshape(
        rows16, NSA_KV_WIDTH, NSA_KV_GROUPS * CMP_HIDDEN)
    top, bot = big(w1r[:rows16]), big(w1r[rows16:])
    zero = jnp.zeros_like(top)
    w1big = jnp.concatenate([jnp.concatenate([top, zero], axis=2),
                             jnp.concatenate([zero, bot], axis=2)], axis=1).astype(BF16)
    w2big = jnp.einsum('nd,gh->gnhd', w2, eye).reshape(NSA_KV_GROUPS * CMP_HIDDEN, NSA_KV_WIDTH).astype(BF16)
    pe_rows = jnp.tile(pe, (1, NSA_KV_GROUPS))
    return pe_rows, w1big, w2big


def _compress(kc, vc, pos_cmp, pe_k, pe_v, w_ck1, w_ck2, w_cv1, w_cv2, invf, sgn):
    b, seq, _ = kc.shape
    nchunk = seq // CMP_STRIDE
    pek, w1k, w2k = _compress_weights(pe_k, w_ck1, w_ck2)
    pev, w1v, w2v = _compress_weights(pe_v, w_cv1, w_cv2)
    ins = [kc, vc, pos_cmp,
           pek, pev, w1k, w2k, w1v, w2v.T, invf, sgn, _overlap_t(nchunk, seq // SLC_BLOCK)]
    full = lambda a: pl.BlockSpec(a.shape, lambda i: (0,) * a.ndim)
    per_b = lambda n: pl.BlockSpec((1, nchunk, n), lambda i: (i, 0, 0))
    out_shape = [jax.ShapeDtypeStruct((b, nchunk, NSA_KV_WIDTH), BF16),
                 jax.ShapeDtypeStruct((b, NSA_KV_GROUPS, NSA_CMP_ROWS, nchunk), BF16)]
    return pl.pallas_call(
        functools.partial(_compress_kernel, nchunk=nchunk),
        grid=(b,),
        in_specs=[pl.BlockSpec((1, seq, NSA_KV_WIDTH), lambda i: (i, 0, 0))] * 2 + [per_b(1)]
        + [full(a) for a in ins[3:]],
        out_specs=[per_b(NSA_KV_WIDTH),
                   pl.BlockSpec((1, NSA_KV_GROUPS, NSA_CMP_ROWS, nchunk), lambda i: (i, 0, 0, 0))],
        out_shape=out_shape,
        compiler_params=pltpu.CompilerParams(dimension_semantics=("arbitrary",),
                                             vmem_limit_bytes=VMEM_LIMIT),
        name="compress",
    )(*ins)


def _nsa_kernel(qt_ref, kcmp_ref, vcmp_t_ref, ksel_ref, vsl_t_ref, kw_ref, vw_t_ref, gates_ref, sga_ref,
                out_ref, qz_scr, qa_scr, v_scr, rank_scr, m_scr, acc_scr, mix_scr, mix2_scr, exc_scr,
                *, tq, kc, n_cmp, n_top, tiles):
    for sub in range(tiles):
        _nsa_tile(pl.program_id(1) * tiles + sub, pl.ds(sub * tq, tq),
                  qt_ref, kcmp_ref, vcmp_t_ref, ksel_ref, vsl_t_ref, kw_ref, vw_t_ref, gates_ref, sga_ref,
                  out_ref, qz_scr, qa_scr, v_scr, rank_scr, m_scr, acc_scr, mix_scr, mix2_scr, exc_scr,
                  tq=tq, kc=kc, n_cmp=n_cmp, n_top=n_top)


def _nsa_tile(tile_idx, tile_rows, qt_ref, kcmp_ref, vcmp_t_ref, ksel_ref, vsl_t_ref, kw_ref, vw_t_ref, gates_ref,
              sga_ref, out_ref, qz_scr, qa_scr, v_scr, rank_scr, m_scr, acc_scr, mix_scr, mix2_scr, exc_scr,
              *, tq, kc, n_cmp, n_top):
    t0 = tile_idx * tq
    n = NSA_HPG * tq
    ncp = kcmp_ref.shape[1]
    hd = NSA_HEAD_DIM
    q_t = qt_ref[:, tile_rows]
    gates_t = gates_ref[0, tile_rows, :].T
    kq = lax.broadcasted_iota(jnp.int32, (kc, tq), 0) - lax.broadcasted_iota(jnp.int32, (kc, tq), 1)
    causal = kq <= 0
    newer = kq > 0

    def add_branch(branch, g, o_t):
        for h in range(NSA_HPG):
            head = NSA_HPG * g + h
            gate = gates_t[head * 3 + branch:head * 3 + branch + 1, :]
            term = gate * o_t[:, h * tq:(h + 1) * tq]
            r = pl.ds(head * hd, hd)
            if branch == 0:
                mix_scr[r, :] = term
            elif branch == 1:
                mix2_scr[r, :] = term
            else:
                mix2_scr[r, :] += term

    def reset_flash():
        m_scr[...] = jnp.full(m_scr.shape, NEG_INF, F32)
        acc_scr[...] = jnp.zeros(acc_scr.shape, F32)

    def flash_step(q_scr, k_ref, v_t_ref, chunks):
        chains = []
        for j, mask, single_pass in chunks:
            start = pl.multiple_of(j * kc, kc)
            for g in range(NSA_KV_GROUPS):
                col0 = g * LANES if k_ref is ksel_ref else 0
                k = k_ref[0, pl.ds(start, kc), col0:col0 + LANES]
                for h in range(NSA_HPG):
                    s_t = _dot(k, q_scr[g, :, h * tq:(h + 1) * tq])
                    chains.append((j, mask, single_pass, g, g * n + h * tq, s_t))
        for j, mask, single_pass, g, c0, s_t in chains:
            if mask is not None:
                s_t = jnp.where(mask, s_t, NEG_INF)
            if single_pass:
                _flash_update_stale(s_t, v_t_ref[0, j, g], m_scr, acc_scr, exc_scr, pl.ds(c0, tq))
            else:
                _flash_update_t(s_t, v_t_ref[0, j, g], m_scr, acc_scr, pl.ds(c0, tq))

    def finish_flash(branch):
        for g in range(NSA_KV_GROUPS):
            c = pl.ds(g * n, n)
            add_branch(branch, g, acc_scr[0:hd, c] / acc_scr[hd:hd + 1, c])

    zeros_q = jnp.zeros((hd, n), BF16)
    qgs = []
    for g in range(NSA_KV_GROUPS):
        qg = jnp.concatenate([q_t[(NSA_HPG * g + h) * hd:(NSA_HPG * g + h + 1) * hd, :] for h in range(NSA_HPG)],
                             axis=1)
        qgs.append(qg)
        qz_scr[g] = jnp.concatenate([qg, zeros_q] if g == 0 else [zeros_q, qg], axis=0)

    reset_flash()
    cq = (lax.broadcasted_iota(jnp.int32, (ncp, tq), 0) * CMP_STRIDE
          - lax.broadcasted_iota(jnp.int32, (ncp, tq), 1))
    cmp_valid = (cq <= t0 - (CMP_BLOCK - 1)) & (lax.broadcasted_iota(jnp.int32, (ncp, tq), 0) < n_cmp)
    chains = [(g * n + h * tq, _dot(kcmp_ref[0], qz_scr[g, :, h * tq:(h + 1) * tq]))
              for g in range(NSA_KV_GROUPS) for h in range(NSA_HPG)]
    for c0, s_t in chains:
        _flash_update_t(jnp.where(cmp_valid, s_t, NEG_INF), vcmp_t_ref[0, c0 // n], m_scr, acc_scr, pl.ds(c0, tq))
    seen = (t0 + lax.broadcasted_iota(jnp.int32, (1, tq), 1) >= CMP_BLOCK - 1).astype(F32)
    seen4 = jnp.concatenate([seen] * NSA_HPG, axis=1)
    imp_rows = slice(hd, hd + SLC_MAX_BLOCKS)
    den_row = slice(hd + SLC_MAX_BLOCKS, hd + SLC_MAX_BLOCKS + 1)

    for g in range(NSA_KV_GROUPS):
        qg = qgs[g]
        c = pl.ds(g * n, n)
        inv_l = seen4 / acc_scr[den_row, c]
        add_branch(0, g, acc_scr[0:hd, c] * inv_l)
        pooled = acc_scr[imp_rows, c] * inv_l
        imp_t = functools.reduce(jnp.add, [pooled[:, h * tq:(h + 1) * tq] for h in range(NSA_HPG)])
        nb = lax.broadcasted_iota(jnp.int32, (SLC_MAX_BLOCKS, tq), 0)
        tqv = t0 + lax.broadcasted_iota(jnp.int32, (SLC_MAX_BLOCKS, tq), 1)
        cur = lax.shift_right_logical(tqv, SLC_SHIFT)
        forced = (nb == 0) | (nb == cur) | (nb == cur - 1)
        v_scr[...] = jnp.where(nb * SLC_BLOCK <= tqv, jnp.where(forced, FORCE_SCORE, imp_t), -1.0)
        groups = SLC_MAX_BLOCKS // SUBLANES
        sub = lax.broadcasted_iota(jnp.int32, (SUBLANES, tq), 0)
        vch = [v_scr[r * SUBLANES:(r + 1) * SUBLANES, :] for r in range(groups)]
        rank_scr[...] = jnp.zeros(rank_scr.shape, jnp.int32)
        for mg in range(groups):
            @pl.when(mg * SUBLANES * SLC_BLOCK < t0 + tq)
            def _():
                count = [jnp.zeros((SUBLANES, tq), jnp.int32) for _ in range(groups)]
                for mblk in range(mg * SUBLANES, (mg + 1) * SUBLANES):
                    vm = jnp.broadcast_to(v_scr[mblk:mblk + 1, :], (SUBLANES, tq))
                    for r in range(groups):
                        if r > mg:
                            before = vm >= vch[r]
                        elif r < mg:
                            before = vm > vch[r]
                        else:
                            before = (vm > vch[r]) | ((vm == vch[r]) & (sub > mblk - r * SUBLANES))
                        count[r] = count[r] + jnp.where(before, 1, 0)
                for r in range(groups):
                    rank_scr[r * SUBLANES:(r + 1) * SUBLANES, :] += count[r]
        pen = jnp.where(rank_scr[...] < n_top, 0.0, SEL_PENALTY)
        pen4 = jnp.concatenate([pen.astype(BF16)] * NSA_HPG, axis=1)
        qa_scr[g] = jnp.concatenate([qg, pen4] if g == 0 else [pen4, qg], axis=0)

    jd = t0 // kc

    def attend(single_pass):
        reset_flash()
        sel = functools.partial(flash_step, qa_scr, ksel_ref, vsl_t_ref)

        @pl.when(jd == 0)
        def _():
            sel([(jd, causal, False)])

        @pl.when(jd > 0)
        def _():
            sel([(0, None, False), (jd, causal, single_pass)])

        rest = jd - 1
        lax.fori_loop(0, rest // 2, lambda i, cr: (
            sel([(2 * i + 1, None, single_pass), (2 * i + 2, None, single_pass)]), cr)[1], 0)

        @pl.when((rest > 0) & (rest % 2 == 1))
        def _():
            sel([(jd - 1, None, single_pass)])

        finish_flash(1)

        reset_flash()
        win = functools.partial(flash_step, qz_scr, kw_ref, vw_t_ref)
        assert WINDOW == 2 * kc

        @pl.when(jd == 0)
        def _():
            win([(jd, causal, False)])

        @pl.when(jd == 1)
        def _():
            win([(jd - 1, None, False), (jd, causal, single_pass)])

        @pl.when(jd >= 2)
        def _():
            win([(jd - 1, None, False), (jd, causal, single_pass), (jd - 2, newer, single_pass)])

        finish_flash(2)

    exc_scr[...] = jnp.full(exc_scr.shape, NEG_INF, F32)
    attend(single_pass=True)

    @pl.when(jnp.max(exc_scr[...]) > MAX_STALE_EXCESS)
    def _():
        attend(single_pass=False)

    out_ref[0, tile_rows, :] = (sga_ref[0, tile_rows, :].astype(F32) * (mix_scr[...] + mix2_scr[...]).T).astype(BF16)


def _nsa_attention(q_t, kcmp, vcmp_t, ksel, vsl_t, kw, vw_t, gates, sga):
    b, seq, _ = ksel.shape
    tq = min(NSA_TQ, seq)
    nq = seq // tq
    n_slc = seq // SLC_BLOCK
    assert n_slc <= SLC_MAX_BLOCKS and seq % tq == 0 and WINDOW % tq == 0 and tq & (tq - 1) == 0
    ncp = kcmp.shape[1]
    n_cmp = (seq - CMP_BLOCK) // CMP_STRIDE + 1
    vsl_t = vsl_t.reshape((b, nq) + vsl_t.shape[1:])
    vw_t = vw_t.reshape((b, nq) + vw_t.shape[1:])
    n = NSA_HPG * tq
    tiles = NSA_TILES_PER_STEP if nq % NSA_TILES_PER_STEP == 0 else 1
    tile = lambda w: pl.BlockSpec((1, tiles * tq, w), lambda bi, qi: (bi, qi, 0))
    per_b = lambda a: pl.BlockSpec((1,) + a.shape[1:], lambda bi, qi: (bi,) + (0,) * (a.ndim - 1))
    return pl.pallas_call(
        functools.partial(_nsa_kernel, tq=tq, kc=tq, n_cmp=n_cmp, n_top=min(SLC_TOPK, n_slc), tiles=tiles),
        grid=(b, nq // tiles),
        in_specs=[pl.BlockSpec((NSA_WIDTH, tiles * tq), lambda bi, qi: (0, bi * (nq // tiles) + qi)),
                  per_b(kcmp), per_b(vcmp_t), per_b(ksel), per_b(vsl_t), per_b(kw), per_b(vw_t),
                  tile(LANES), tile(NSA_WIDTH)],
        out_specs=tile(NSA_WIDTH),
        out_shape=jax.ShapeDtypeStruct((b, seq, NSA_WIDTH), BF16),
        scratch_shapes=[pltpu.VMEM((NSA_KV_GROUPS, LANES, n), BF16),
                        pltpu.VMEM((NSA_KV_GROUPS, LANES, n), BF16),
                        pltpu.VMEM((SLC_MAX_BLOCKS, tq), F32),
                        pltpu.VMEM((SLC_MAX_BLOCKS, tq), jnp.int32),
                        pltpu.VMEM((1, NSA_KV_GROUPS * n), F32),
                        pltpu.VMEM((NSA_CMP_ROWS, NSA_KV_GROUPS * n), F32),
                        pltpu.VMEM((NSA_WIDTH, tq), F32),
                        pltpu.VMEM((NSA_WIDTH, tq), F32),
                        pltpu.VMEM((1, NSA_KV_GROUPS * n), F32)],
        compiler_params=pltpu.CompilerParams(dimension_semantics=("arbitrary", "arbitrary"),
                                             vmem_limit_bytes=VMEM_LIMIT),
        name="nsa_attn",
    )(q_t, kcmp, vcmp_t, ksel, vsl_t, kw, vw_t, gates, sga)


def _odd_proj_kernel(x_ref, mixa_ref, mixb_ref, posr_ref, wout_ref, g_ref, wc_ref, wgate_ref, qn_ref, kvn_ref,
                     wuqt_ref, wuk_ref, wuvt_ref, invfc_ref,
                     y_ref, qt_ref, kn_ref, kpe_ref, vt_ref, sg_ref, *, tm):
    y = (x_ref[...] + _dot(mixa_ref[...], wout_ref[0:NSA_WIDTH, :])
         + _dot(mixb_ref[...], wout_ref[NSA_WIDTH:, :]))
    y_ref[...] = y
    yn = _rms(y, g_ref[...]).astype(BF16)
    cos_t, sin_t = _rope_tables_t(invfc_ref[...], posr_ref[...])
    half = MLA_ROPE_DIM // 2

    def rope_t(x1, x2):
        return x1 * cos_t - x2 * sin_t, x2 * cos_t + x1 * sin_t

    seg = _dot(yn, wc_ref[...])
    kr_t = seg[:, MLA_Q_RANK + MLA_KV_RANK:].T
    kp1, kp2 = rope_t(kr_t[0:half], kr_t[half:2 * half])
    kpe_ref[...] = jnp.concatenate([kp1, kp2, kr_t[2 * half:]], axis=0).T.astype(BF16)
    cq = _rms(seg[:, 0:MLA_Q_RANK], qn_ref[...]).astype(BF16)
    ckv = _rms(seg[:, MLA_Q_RANK:MLA_Q_RANK + MLA_KV_RANK], kvn_ref[...]).astype(BF16)
    scale = (MLA_NOPE_DIM + MLA_ROPE_DIM) ** -0.5 * LOG2_E
    qt = _dot_nt(wuqt_ref[...], cq)
    kn_ref[...] = _dot(ckv, wuk_ref[...]).astype(BF16)
    for h in range(MLA_HEADS):
        c0 = h * MLA_QK_PAD
        r0 = c0 + MLA_NOPE_DIM
        qt_ref[c0:r0, :] = (qt[c0:r0] * scale).astype(BF16)
        q1, q2 = rope_t(qt[r0:r0 + half], qt[r0 + half:r0 + 2 * half])
        qt_ref[r0:r0 + half, :] = (q1 * scale).astype(BF16)
        qt_ref[r0 + half:r0 + 2 * half, :] = (q2 * scale).astype(BF16)
        qt_ref[r0 + 2 * half:c0 + MLA_QK_PAD, :] = jnp.zeros((MLA_QK_PAD - MLA_NOPE_DIM - 2 * half, tm), BF16)
    v_t = _dot_nt(wuvt_ref[...], ckv)
    kc = vt_ref.shape[-1]
    for h in range(MLA_HEADS):
        for c in range(tm // kc):
            vt_ref[0, h, c, 0:MLA_V_DIM, :] = v_t[h * MLA_V_DIM:(h + 1) * MLA_V_DIM, c * kc:(c + 1) * kc].astype(BF16)
            vt_ref[0, h, c, MLA_V_DIM:, :] = jnp.ones((MLA_ONES_ROWS, kc), BF16)
    sg_ref[...] = _silu(_dot(yn, wgate_ref[...])).astype(BF16)


def _odd_proj(x2d, mixa, mixb, pos_row, a_w_out, c_norm, c_w_in, q_norm, kv_norm, w_uq, w_ukv, invf_col, seq):
    t = x2d.shape[0]
    tm = min(PROJ_ROWS, seq)
    r2 = MLA_Q_RANK + MLA_KV_RANK + MLA_ROPE_DIM
    wc = jnp.pad(c_w_in[:, :r2], ((0, 0), (0, LANES - MLA_ROPE_DIM))).astype(BF16)
    wgate = c_w_in[:, r2:].astype(BF16)
    wuqt = jnp.pad(w_uq.reshape(MLA_Q_RANK, MLA_HEADS, MLA_NOPE_DIM + MLA_ROPE_DIM),
                   ((0, 0), (0, 0), (0, MLA_QK_PAD - MLA_NOPE_DIM - MLA_ROPE_DIM))
                   ).reshape(MLA_Q_RANK, MLA_HEADS * MLA_QK_PAD).T.astype(BF16)
    wukv = w_ukv.reshape(MLA_KV_RANK, MLA_HEADS, MLA_NOPE_DIM + MLA_V_DIM)
    wuk = wukv[:, :, :MLA_NOPE_DIM].reshape(MLA_KV_RANK, -1).astype(BF16)
    wuvt = wukv[:, :, MLA_NOPE_DIM:].reshape(MLA_KV_RANK, -1).T.astype(BF16)
    kc = min(MLA_TQ, seq)
    nq = seq // kc
    cps = tm // kc
    assert tm % kc == 0 and seq % tm == 0
    vrows = MLA_V_DIM + MLA_ONES_ROWS
    ins = [x2d, mixa, mixb, pos_row, a_w_out.astype(BF16), c_norm.reshape(1, -1), wc, wgate,
           q_norm.reshape(1, -1), kv_norm.reshape(1, -1), wuqt, wuk, wuvt, invf_col]
    full = lambda a: pl.BlockSpec(a.shape, lambda i: (0,) * a.ndim)
    rows = lambda n: pl.BlockSpec((tm, n), lambda i: (i, 0))
    cols = lambda n: pl.BlockSpec((n, tm), lambda i: (0, i))
    out_shapes = [
        jax.ShapeDtypeStruct((t, D_MODEL), F32),
        jax.ShapeDtypeStruct((MLA_HEADS * MLA_QK_PAD, t), BF16),
        jax.ShapeDtypeStruct((t, MLA_HEADS * MLA_NOPE_DIM), BF16),
        jax.ShapeDtypeStruct((t, LANES), BF16),
        jax.ShapeDtypeStruct((t // seq, MLA_HEADS, nq, vrows, kc), BF16),
        jax.ShapeDtypeStruct((t, MLA_WIDTH), BF16),
    ]
    steps = seq // tm
    vt_spec = pl.BlockSpec((1, MLA_HEADS, cps, vrows, kc), lambda i: (i // steps, 0, i % steps, 0, 0))
    out_specs = [rows(D_MODEL), cols(MLA_HEADS * MLA_QK_PAD), rows(MLA_HEADS * MLA_NOPE_DIM), rows(LANES), vt_spec,
                 rows(MLA_WIDTH)]
    return pl.pallas_call(
        functools.partial(_odd_proj_kernel, tm=tm),
        grid=(t // tm,),
        in_specs=[rows(D_MODEL), rows(NSA_WIDTH), rows(CONV_WIDTH), cols(1)] + [full(a) for a in ins[4:]],
        out_specs=out_specs,
        out_shape=out_shapes,
        compiler_params=pltpu.CompilerParams(dimension_semantics=("arbitrary",),
                                             vmem_limit_bytes=VMEM_LIMIT),
        name="odd_proj",
    )(*ins)


def _mla_kernel(qt_ref, kn_ref, kpe_ref, v_t_ref, o_ref, m_scr, acc_scr, exc_scr, *, tq, kc, hp):
    jd = pl.program_id(2)
    causal = (lax.broadcasted_iota(jnp.int32, (kc, tq), 0) <= lax.broadcasted_iota(jnp.int32, (kc, tq), 1))
    hq = tq // 2
    tri = causal[0:hq, 0:hq]
    keep_hi = jnp.concatenate([jnp.ones((hq, hq), jnp.bool_), tri], axis=0)

    def keys(h, start, size):
        return jnp.concatenate([kn_ref[0, pl.ds(start, size), h * MLA_NOPE_DIM:(h + 1) * MLA_NOPE_DIM],
                                kpe_ref[0, pl.ds(start, size), :]], axis=1)

    def step(js, single_pass):
        scores = []
        for j in js:
            start = pl.multiple_of(j * kc, kc)
            for h in range(hp):
                qk = slice(h * MLA_QK_PAD, (h + 1) * MLA_QK_PAD)
                scores.append((j, h, _dot(keys(h, start, kc), qt_ref[qk, :])))
        for j, h, s_t in scores:
            if single_pass:
                _flash_update_stale(s_t, v_t_ref[0, h, j], m_scr, acc_scr, exc_scr, pl.ds(h * tq, tq))
            else:
                _flash_update_t(s_t, v_t_ref[0, h, j], m_scr, acc_scr, pl.ds(h * tq, tq))

    def diagonal(single_pass):
        start = pl.multiple_of(jd * kc, kc)
        scores = []
        for h in range(hp):
            qk = slice(h * MLA_QK_PAD, (h + 1) * MLA_QK_PAD)
            scores.append((_dot(keys(h, start, hq), qt_ref[qk, 0:hq]),
                           _dot(keys(h, start, kc), qt_ref[qk, hq:tq])))
        for h, (s_lo, s_hi) in enumerate(scores):
            v_t = v_t_ref[0, h, jd]
            parts = ((jnp.where(tri, s_lo, NEG_INF), v_t[:, 0:hq], pl.ds(h * tq, hq)),
                     (jnp.where(keep_hi, s_hi, NEG_INF), v_t, pl.ds(h * tq + hq, hq)))
            for s_t, vv, cols in parts:
                if single_pass:
                    _flash_update_stale(s_t, vv, m_scr, acc_scr, exc_scr, cols)
                else:
                    _flash_update_t(s_t, vv, m_scr, acc_scr, cols)

    def attend(single_pass):
        m_scr[...] = jnp.full(m_scr.shape, NEG_INF, F32)
        acc_scr[...] = jnp.zeros(acc_scr.shape, F32)

        @pl.when(jd == 0)
        def _():
            diagonal(False)

        @pl.when(jd > 0)
        def _():
            step([0], False)
            diagonal(single_pass)

        rest = jd - 1
        lax.fori_loop(0, rest // 2, lambda i, c: (step([2 * i + 1, 2 * i + 2], single_pass), c)[1], 0)

        @pl.when((rest > 0) & (rest % 2 == 1))
        def _():
            step([jd - 1], single_pass)

        for h in range(hp):
            c = pl.ds(h * tq, tq)
            o_t = acc_scr[0:MLA_V_DIM, c] / acc_scr[MLA_V_DIM:MLA_V_DIM + 1, c]
            o_ref[0, :, h * MLA_V_DIM:(h + 1) * MLA_V_DIM] = o_t.T.astype(BF16)

    exc_scr[...] = jnp.full(exc_scr.shape, NEG_INF, F32)
    attend(single_pass=True)

    @pl.when(jnp.max(exc_scr[...]) > MAX_STALE_EXCESS)
    def _():
        attend(single_pass=False)


def _mla_attention(q_t, k_nope, k_pe, v_t):
    b, seq, _ = k_pe.shape
    tq = min(MLA_TQ, seq)
    hp = MLA_HEADS_PER_STEP
    nq = seq // tq
    assert seq % tq == 0 and MLA_HEADS % hp == 0
    vrows = MLA_V_DIM + MLA_ONES_ROWS
    return pl.pallas_call(
        functools.partial(_mla_kernel, tq=tq, kc=tq, hp=hp),
        grid=(b, MLA_HEADS // hp, nq),
        in_specs=[pl.BlockSpec((hp * MLA_QK_PAD, tq), lambda bi, h, qi: (h, bi * nq + qi)),
                  pl.BlockSpec((1, seq, hp * MLA_NOPE_DIM), lambda bi, h, qi: (bi, 0, h)),
                  pl.BlockSpec((1, seq, LANES), lambda bi, h, qi: (bi, 0, 0)),
                  pl.BlockSpec((1, hp, nq, vrows, tq), lambda bi, h, qi: (bi, h, 0, 0, 0))],
        out_specs=pl.BlockSpec((1, tq, hp * MLA_V_DIM), lambda bi, h, qi: (bi, qi, h)),
        out_shape=jax.ShapeDtypeStruct((b, seq, MLA_WIDTH), BF16),
        scratch_shapes=[pltpu.VMEM((1, hp * tq), F32), pltpu.VMEM((vrows, hp * tq), F32),
                        pltpu.VMEM((1, hp * tq), F32)],
        compiler_params=pltpu.CompilerParams(dimension_semantics=("arbitrary", "arbitrary", "arbitrary"),
                                             vmem_limit_bytes=VMEM_LIMIT),
        name="mla_attn",
    )(q_t, k_nope, k_pe, v_t)


def _final_kernel(y_ref, o_ref, sg_ref, w_ref, g_ref, out_ref):
    gated = (sg_ref[...].astype(F32) * o_ref[...].astype(F32)).astype(BF16)
    out_ref[...] = _rms(y_ref[...] + _dot(gated, w_ref[...]), g_ref[...])


def _final(y, o, sg, w_out, final_norm, seq):
    t = y.shape[0]
    tm = min(PROJ_ROWS, seq)
    rows = lambda n: pl.BlockSpec((tm, n), lambda i: (i, 0))
    full = lambda a: pl.BlockSpec(a.shape, lambda i: (0,) * a.ndim)
    w = w_out.astype(BF16)
    g = final_norm.reshape(1, -1)
    return pl.pallas_call(
        _final_kernel,
        grid=(t // tm,),
        in_specs=[rows(D_MODEL), rows(MLA_WIDTH), rows(MLA_WIDTH), full(w), full(g)],
        out_specs=rows(D_MODEL),
        out_shape=jax.ShapeDtypeStruct((t, D_MODEL), F32),
        compiler_params=pltpu.CompilerParams(dimension_semantics=("arbitrary",),
                                             vmem_limit_bytes=VMEM_LIMIT),
        name="final",
    )(y, o, sg, w, g)


def _rope_constants():
    half = NSA_HEAD_DIM // 2
    inv_freq = ROPE_THETA ** (-jnp.arange(half, dtype=F32) / half)
    invf = jnp.tile(inv_freq, LANES // half).reshape(1, LANES)
    sgn = jnp.tile(jnp.concatenate([-jnp.ones((half,), F32), jnp.ones((half,), F32)]), LANES // (2 * half))
    return invf, sgn.reshape(1, LANES), inv_freq.reshape(half, 1)


def kernel(x, positions, a_norm, a_w_in, a_pe_k, a_pe_v, a_w_ck1, a_w_ck2, a_w_cv1, a_w_cv2, a_conv_w, a_w_out, c_norm, c_w_in, c_q_norm, c_kv_norm, c_w_uq, c_w_ukv, c_w_out, final_norm):
    b, seq, d = x.shape
    assert d == D_MODEL and NSA_HEAD_DIM == MLA_ROPE_DIM
    assert a_norm.shape[0] == 1 and c_norm.shape[0] == 1
    t = b * seq
    invf, sgn, invf_col = _rope_constants()
    x2d = x.reshape(t, d)
    pos_row = positions.reshape(1, t)

    q_t, kc, vc, ksel, vsl_t, kw, vw_t, gates, sga, mixb = _even_proj(
        x2d, pos_row, a_norm[0], a_w_in[0], a_conv_w[0], invf_col, seq)
    nchunk = seq // CMP_STRIDE
    pos_cmp = jnp.pad(positions[:, CMP_BLOCK - 1::CMP_STRIDE], ((0, 0), (0, 1)))[:, :nchunk, None]
    kcmp, vcmp_t = _compress(kc.reshape(b, seq, -1), vc.reshape(b, seq, -1), pos_cmp, a_pe_k[0], a_pe_v[0],
                             a_w_ck1[0], a_w_ck2[0], a_w_cv1[0], a_w_cv2[0], invf, sgn)
    r3 = lambda a: a.reshape(b, seq, a.shape[-1])
    mixa = _nsa_attention(q_t, kcmp, vcmp_t, r3(ksel), vsl_t, r3(kw), vw_t, r3(gates), r3(sga))

    y, mq_t, mkn, mkpe, mv_t, sg = _odd_proj(x2d, mixa.reshape(t, -1), mixb, pos_row, a_w_out[0], c_norm[0],
                                      c_w_in[0], c_q_norm[0], c_kv_norm[0], c_w_uq[0], c_w_ukv[0], invf_col, seq)
    o = _mla_attention(mq_t, r3(mkn), r3(mkpe), mv_t)
    out = _final(y, o.reshape(t, -1), sg, c_w_out[0], final_norm, seq)
    return out.reshape(b, seq, d)
```

```python
import functools

import jax
import jax.numpy as jnp
import numpy as np
from jax import lax
from jax.experimental import pallas as pl
from jax.experimental.pallas import tpu as pltpu

F32 = jnp.float32
BF16 = jnp.bfloat16

D_MODEL = 1024
ROPE_THETA = 10000.0
RMS_EPS = 1e-6
NEG_INF = -1e30
FORCE_SCORE = 1e4
SEL_PENALTY = NEG_INF

NSA_HEADS = 8
NSA_KV_GROUPS = 2
NSA_HPG = NSA_HEADS // NSA_KV_GROUPS
NSA_HEAD_DIM = 64
NSA_WIDTH = NSA_HEADS * NSA_HEAD_DIM
NSA_KV_WIDTH = NSA_KV_GROUPS * NSA_HEAD_DIM
CMP_BLOCK = 32
CMP_STRIDE = 16
CMP_HIDDEN = 2 * NSA_HEAD_DIM
SLC_BLOCK = 64
SLC_TOPK = 16
SLC_SHIFT = SLC_BLOCK.bit_length() - 1
SLC_MAX_BLOCKS = 64
WINDOW = 512
CONV_WIDTH = D_MODEL - NSA_WIDTH
CONV_K = 3

MLA_HEADS = 8
MLA_NOPE_DIM = 128
MLA_ROPE_DIM = 64
MLA_V_DIM = 128
MLA_Q_RANK = 256
MLA_KV_RANK = 256
MLA_WIDTH = MLA_HEADS * MLA_V_DIM
MLA_QK_PAD = 256

LANES = 128
SUBLANES = 8
VMEM_LIMIT = 56 * 1024 * 1024

PROJ_ROWS = 1024
NSA_TQ = 256
NSA_TILES_PER_STEP = 1
MLA_TQ = 512
MLA_HEADS_PER_STEP = 4
MLA_ONES_ROWS = 16
NSA_ONES_ROWS = 16
NSA_CMP_ROWS = NSA_HEAD_DIM + SLC_MAX_BLOCKS + NSA_ONES_ROWS
LOG2_E = 1.4426950408889634
MAX_STALE_EXCESS = 64.0


def _dot(a, b):
    return jnp.dot(a, b, preferred_element_type=F32)


def _dot_nt(a, b):
    return lax.dot_general(a, b, (((1,), (1,)), ((), ())), preferred_element_type=F32)


def _silu(x):
    return x * jax.nn.sigmoid(x)


def _rms(x, g):
    return x * lax.rsqrt(jnp.mean(x * x, axis=-1, keepdims=True) + RMS_EPS) * g


def _rope_tables(pos_col, invf_row, sign_row):
    ang = pos_col.astype(F32) * invf_row
    return jnp.cos(ang), jnp.sin(ang) * sign_row


def _rope_slab(s, cos2, sin2):
    half = NSA_HEAD_DIM // 2
    first_half = (lax.broadcasted_iota(jnp.int32, s.shape, 1) & half) == 0
    swapped = jnp.where(first_half, pltpu.roll(s, LANES - half, 1), pltpu.roll(s, half, 1))
    return s * cos2 + swapped * sin2


def _rope_tables_t(invf_col, pos_row):
    ang = invf_col * pos_row.astype(F32)
    return jnp.cos(ang), jnp.sin(ang)


def _flash_update_stale(s_t, v_t, m_ref, acc_ref, exc_ref, cols):
    m_old = m_ref[:, cols]
    p = jnp.exp2(s_t - m_old)
    cmax = jnp.max(s_t, axis=0, keepdims=True)
    m_new = jnp.maximum(m_old, cmax)
    rows = v_t.shape[0]
    acc_ref[0:rows, cols] = jnp.exp2(m_old - m_new) * (acc_ref[0:rows, cols] + _dot(v_t, p.astype(BF16)))
    m_ref[:, cols] = m_new
    exc_ref[:, cols] = jnp.maximum(exc_ref[:, cols], cmax - m_old)


def _flash_update_t(s_t, v_t, m_ref, acc_ref, cols):
    m_old = m_ref[:, cols]
    m_new = jnp.maximum(m_old, jnp.max(s_t, axis=0, keepdims=True))
    p = jnp.exp2(s_t - m_new)
    rows = v_t.shape[0]
    acc_ref[0:rows, cols] = jnp.exp2(m_old - m_new) * acc_ref[0:rows, cols] + _dot(v_t, p.astype(BF16))
    m_ref[:, cols] = m_new


def _even_proj_kernel(x_ref, posr_ref, g_ref, wqt_ref, wkv_ref, wg_ref, wga_ref, wcb_ref, wcc_ref, wch_ref,
                      wgb_ref, convw_ref, invfc_ref,
                      qt_ref, kc_ref, vc_ref, ksel_ref, vslt_ref, kw_ref, vwt_ref, gates_ref, sga_ref,
                      mixb_ref, carry_ref, *, tiles_per_seq, tm, kc):
    i = pl.program_id(0)
    xn = _rms(x_ref[...], g_ref[...]).astype(BF16)
    cos_t, sin_t = _rope_tables_t(invfc_ref[...], posr_ref[...])
    half = NSA_HEAD_DIM // 2
    lane = lax.broadcasted_iota(jnp.int32, (tm, LANES), 1)
    low = lane < NSA_HEAD_DIM

    def rope_t(x_t):
        out = []
        for r in range(0, x_t.shape[0], 2 * half):
            x1, x2 = x_t[r:r + half], x_t[r + half:r + 2 * half]
            out += [x1 * cos_t - x2 * sin_t, x2 * cos_t + x1 * sin_t]
        return jnp.concatenate(out, axis=0)

    def store_values_t(ref, v):
        v_t = v.T.astype(BF16)
        ones = jnp.ones((NSA_ONES_ROWS, kc), BF16)
        for c in range(tm // kc):
            for g in range(NSA_KV_GROUPS):
                ref[c, g, 0:NSA_HEAD_DIM, :] = v_t[g * NSA_HEAD_DIM:(g + 1) * NSA_HEAD_DIM, c * kc:(c + 1) * kc]
                ref[c, g, NSA_HEAD_DIM:, :] = ones

    scale = NSA_HEAD_DIM ** -0.5 * LOG2_E
    qt_ref[...] = (rope_t(_dot_nt(wqt_ref[...], xn)) * scale).astype(BF16)

    seg = _dot(xn, wkv_ref[...])
    kc_ref[...] = seg[:, 0:LANES]
    vc_ref[...] = seg[:, LANES:2 * LANES]
    ksl = rope_t(seg[:, 2 * LANES:3 * LANES].T).T
    store_values_t(vslt_ref, seg[:, 3 * LANES:4 * LANES])
    kw_ref[...] = rope_t(seg[:, 4 * LANES:5 * LANES].T).T.astype(BF16)
    store_values_t(vwt_ref, seg[:, 5 * LANES:6 * LANES])

    spos = (i % tiles_per_seq) * tm + lax.broadcasted_iota(jnp.int32, (tm, LANES), 0)
    blk = lax.shift_right_logical(spos, SLC_SHIFT)
    ksel_ref[:, 0:LANES] = jnp.where(low, ksl, (lane - NSA_HEAD_DIM == blk).astype(F32)).astype(BF16)
    ksel_ref[:, LANES:2 * LANES] = jnp.where(low, (lane == blk).astype(F32), ksl).astype(BF16)

    gates_ref[...] = jax.nn.sigmoid(_dot(xn, wg_ref[...]))
    sga_ref[...] = _silu(_dot(xn, wga_ref[...])).astype(BF16)

    u = _dot(xn, wcc_ref[...]) * _dot(xn, wch_ref[...])

    @pl.when(i % tiles_per_seq == 0)
    def _():
        carry_ref[...] = jnp.zeros_like(carry_ref)

    row = lax.broadcasted_iota(jnp.int32, (tm, CONV_WIDTH), 0)
    prev1 = carry_ref[SUBLANES - 1:SUBLANES, :]
    prev2 = carry_ref[SUBLANES - 2:SUBLANES - 1, :]
    u1 = jnp.where(row == 0, prev1, pltpu.roll(u, 1, 0))
    u2 = jnp.where(row == 0, prev2, jnp.where(row == 1, prev1, pltpu.roll(u, 2, 0)))
    carry_ref[...] = u[tm - SUBLANES:tm, :]
    w = convw_ref[...]
    y = w[0:1, :] * u2 + w[1:2, :] * u1 + w[2:3, :] * u
    cb = _dot(xn, wcb_ref[...])
    mixb_ref[...] = (_silu(_dot(xn, wgb_ref[...])) * (cb * y)).astype(BF16)


def _even_proj(x2d, pos_row, norm_g, w_in, conv_w, invf_col, seq):
    t = x2d.shape[0]
    tm = min(PROJ_ROWS, seq)
    kc = min(NSA_TQ, seq)
    assert seq % tm == 0 and t % tm == 0 and tm % kc == 0 and conv_w.shape[0] == CONV_K == 3
    sizes = (NSA_WIDTH,) + (NSA_KV_WIDTH,) * 6 + (3 * NSA_HEADS, NSA_WIDTH) + (CONV_WIDTH,) * 4
    offs = np.concatenate([[0], np.cumsum(sizes)])
    col = lambda k: w_in[:, offs[k]:offs[k + 1]]
    wqt = col(0).T.astype(BF16)
    wkv = jnp.concatenate([col(k) for k in range(1, 7)], axis=1).astype(BF16)
    wg = jnp.pad(col(7), ((0, 0), (0, LANES - 3 * NSA_HEADS))).astype(BF16)
    wga, wcb, wcc, wch, wgb = (col(k).astype(BF16) for k in range(8, 13))

    full = lambda a: pl.BlockSpec(a.shape, lambda i: (0,) * a.ndim)
    rows = lambda n: pl.BlockSpec((tm, n), lambda i: (i, 0))
    cols = lambda n: pl.BlockSpec((n, tm), lambda i: (0, i))
    vrows = NSA_HEAD_DIM + NSA_ONES_ROWS
    values_t = jax.ShapeDtypeStruct((t // kc, NSA_KV_GROUPS, vrows, kc), BF16)
    values_t_spec = pl.BlockSpec((tm // kc, NSA_KV_GROUPS, vrows, kc), lambda i: (i, 0, 0, 0))
    out_shapes = [
        jax.ShapeDtypeStruct((NSA_WIDTH, t), BF16),
        jax.ShapeDtypeStruct((t, NSA_KV_WIDTH), F32),
        jax.ShapeDtypeStruct((t, NSA_KV_WIDTH), F32),
        jax.ShapeDtypeStruct((t, 2 * LANES), BF16),
        values_t,
        jax.ShapeDtypeStruct((t, NSA_KV_WIDTH), BF16),
        values_t,
        jax.ShapeDtypeStruct((t, LANES), F32),
        jax.ShapeDtypeStruct((t, NSA_WIDTH), BF16),
        jax.ShapeDtypeStruct((t, CONV_WIDTH), BF16),
    ]
    out_specs = [cols(NSA_WIDTH), rows(NSA_KV_WIDTH), rows(NSA_KV_WIDTH), rows(2 * LANES), values_t_spec,
                 rows(NSA_KV_WIDTH), values_t_spec, rows(LANES), rows(NSA_WIDTH), rows(CONV_WIDTH)]
    ins = [x2d, pos_row, norm_g.reshape(1, -1), wqt, wkv, wg, wga, wcb, wcc, wch, wgb, conv_w, invf_col]
    in_specs = [rows(D_MODEL), cols(1)] + [full(a) for a in ins[2:]]
    return pl.pallas_call(
        functools.partial(_even_proj_kernel, tiles_per_seq=seq // tm, tm=tm, kc=kc),
        grid=(t // tm,),
        in_specs=in_specs,
        out_specs=out_specs,
        out_shape=out_shapes,
        scratch_shapes=[pltpu.VMEM((SUBLANES, CONV_WIDTH), F32)],
        compiler_params=pltpu.CompilerParams(dimension_semantics=("arbitrary",),
                                             vmem_limit_bytes=VMEM_LIMIT),
        name="even_proj",
    )(*ins)


def _compress_kernel(kc_ref, vc_ref, pos_ref, pek_ref, pev_ref, w1k_ref, w2k_ref, w1v_ref, w2vt_ref,
                     invf_ref, sgn_ref, ov_ref, kcmp_ref, vcmpt_ref, *, nchunk):
    half = NSA_KV_GROUPS * CMP_HIDDEN
    rows16 = CMP_BLOCK // 2

    def hidden(raw_ref, pe_ref, w1_ref):
        acc = jnp.zeros((nchunk, 2 * half), F32)
        for l in range(rows16):
            x = raw_ref[0, pl.ds(l, nchunk, stride=CMP_STRIDE), :]
            lhs = jnp.concatenate([(x + pe_ref[l:l + 1, :]).astype(BF16),
                                   (x + pe_ref[rows16 + l:rows16 + l + 1, :]).astype(BF16)], axis=1)
            acc = acc + _dot(lhs, w1_ref[l])
        return _silu(acc[:, 0:half] + pltpu.roll(acc[:, half:], nchunk - 1, 0)).astype(BF16)

    cos2, sin2 = _rope_tables(pos_ref[0], invf_ref[...], sgn_ref[...])
    kcmp = _dot(hidden(kc_ref, pek_ref, w1k_ref), w2k_ref[...])
    kcmp_ref[0] = _rope_slab(kcmp, cos2, sin2).astype(BF16)
    vcmp_t = _dot_nt(w2vt_ref[...], hidden(vc_ref, pev_ref, w1v_ref)).astype(BF16)
    for g in range(NSA_KV_GROUPS):
        vcmpt_ref[0, g, 0:NSA_HEAD_DIM, :] = vcmp_t[g * NSA_HEAD_DIM:(g + 1) * NSA_HEAD_DIM]
        vcmpt_ref[0, g, NSA_HEAD_DIM:NSA_HEAD_DIM + SLC_MAX_BLOCKS, :] = ov_ref[...]
        vcmpt_ref[0, g, NSA_HEAD_DIM + SLC_MAX_BLOCKS:, :] = jnp.ones((NSA_ONES_ROWS, nchunk), BF16)


def _overlap_t(ncp, n_slc):
    cmp_starts = np.arange(ncp) * CMP_STRIDE
    slc_starts = np.arange(SLC_MAX_BLOCKS) * SLC_BLOCK
    ov = ((cmp_starts[None, :] < slc_starts[:, None] + SLC_BLOCK)
          & (cmp_starts[None, :] + CMP_BLOCK > slc_starts[:, None])
          & (np.arange(SLC_MAX_BLOCKS)[:, None] < n_slc))
    return jnp.asarray(ov.astype(np.float32), BF16)


def _compress_weights(pe, w1, w2):
    eye = jnp.eye(NSA_KV_GROUPS, dtype=F32)
    rows16 = CMP_BLOCK // 2
    w1r = w1.reshape(CMP_BLOCK, NSA_HEAD_DIM, CMP_HIDDEN)
    big = lambda part: jnp.einsum('ldn,gh->lgdhn', part, eye).reshape(
        rows16, NSA_KV_WIDTH, NSA_KV_GROUPS * CMP_HIDDEN)
    top, bot = big(w1r[:rows16]), big(w1r[rows16:])
    zero = jnp.zeros_like(top)
    w1big = jnp.concatenate([jnp.concatenate([top, zero], axis=2),
                             jnp.concatenate([zero, bot], axis=2)], axis=1).astype(BF16)
    w2big = jnp.einsum('nd,gh->gnhd', w2, eye).reshape(NSA_KV_GROUPS * CMP_HIDDEN, NSA_KV_WIDTH).astype(BF16)
    pe_rows = jnp.tile(pe, (1, NSA_KV_GROUPS))
    return pe_rows, w1big, w2big


def _compress(kc, vc, pos_cmp, pe_k, pe_v, w_ck1, w_ck2, w_cv1, w_cv2, invf, sgn):
    b, seq, _ = kc.shape
    nchunk = seq // CMP_STRIDE
    pek, w1k, w2k = _compress_weights(pe_k, w_ck1, w_ck2)
    pev, w1v, w2v = _compress_weights(pe_v, w_cv1, w_cv2)
    ins = [kc, vc, pos_cmp,
           pek, pev, w1k, w2k, w1v, w2v.T, invf, sgn, _overlap_t(nchunk, seq // SLC_BLOCK)]
    full = lambda a: pl.BlockSpec(a.shape, lambda i: (0,) * a.ndim)
    per_b = lambda n: pl.BlockSpec((1, nchunk, n), lambda i: (i, 0, 0))
    out_shape = [jax.ShapeDtypeStruct((b, nchunk, NSA_KV_WIDTH), BF16),
                 jax.ShapeDtypeStruct((b, NSA_KV_GROUPS, NSA_CMP_ROWS, nchunk), BF16)]
    return pl.pallas_call(
        functools.partial(_compress_kernel, nchunk=nchunk),
        grid=(b,),
        in_specs=[pl.BlockSpec((1, seq, NSA_KV_WIDTH), lambda i: (i, 0, 0))] * 2 + [per_b(1)]
        + [full(a) for a in ins[3:]],
        out_specs=[per_b(NSA_KV_WIDTH),
                   pl.BlockSpec((1, NSA_KV_GROUPS, NSA_CMP_ROWS, nchunk), lambda i: (i, 0, 0, 0))],
        out_shape=out_shape,
        compiler_params=pltpu.CompilerParams(dimension_semantics=("arbitrary",),
                                             vmem_limit_bytes=VMEM_LIMIT),
        name="compress",
    )(*ins)


def _nsa_kernel(qt_ref, kcmp_ref, vcmp_t_ref, ksel_ref, vsl_t_ref, kw_ref, vw_t_ref, gates_ref, sga_ref,
                out_ref, qz_scr, qa_scr, v_scr, rank_scr, m_scr, acc_scr, mix_scr, mix2_scr, exc_scr,
                *, tq, kc, n_cmp, n_top, tiles):
    for sub in range(tiles):
        _nsa_tile(pl.program_id(1) * tiles + sub, pl.ds(sub * tq, tq),
                  qt_ref, kcmp_ref, vcmp_t_ref, ksel_ref, vsl_t_ref, kw_ref, vw_t_ref, gates_ref, sga_ref,
                  out_ref, qz_scr, qa_scr, v_scr, rank_scr, m_scr, acc_scr, mix_scr, mix2_scr, exc_scr,
                  tq=tq, kc=kc, n_cmp=n_cmp, n_top=n_top)


def _nsa_tile(tile_idx, tile_rows, qt_ref, kcmp_ref, vcmp_t_ref, ksel_ref, vsl_t_ref, kw_ref, vw_t_ref, gates_ref,
              sga_ref, out_ref, qz_scr, qa_scr, v_scr, rank_scr, m_scr, acc_scr, mix_scr, mix2_scr, exc_scr,
              *, tq, kc, n_cmp, n_top):
    t0 = tile_idx * tq
    n = NSA_HPG * tq
    ncp = kcmp_ref.shape[1]
    hd = NSA_HEAD_DIM
    q_t = qt_ref[:, tile_rows]
    gates_t = gates_ref[0, tile_rows, :].T
    kq = lax.broadcasted_iota(jnp.int32, (kc, tq), 0) - lax.broadcasted_iota(jnp.int32, (kc, tq), 1)
    causal = kq <= 0
    newer = kq > 0

    def add_branch(branch, g, o_t):
        for h in range(NSA_HPG):
            head = NSA_HPG * g + h
            gate = gates_t[head * 3 + branch:head * 3 + branch + 1, :]
            term = gate * o_t[:, h * tq:(h + 1) * tq]
            r = pl.ds(head * hd, hd)
            if branch == 0:
                mix_scr[r, :] = term
            elif branch == 1:
                mix2_scr[r, :] = term
            else:
                mix2_scr[r, :] += term

    def reset_flash():
        m_scr[...] = jnp.full(m_scr.shape, NEG_INF, F32)
        acc_scr[...] = jnp.zeros(acc_scr.shape, F32)

    def flash_step(q_scr, k_ref, v_t_ref, chunks):
        chains = []
        for j, mask, single_pass in chunks:
            start = pl.multiple_of(j * kc, kc)
            for g in range(NSA_KV_GROUPS):
                col0 = g * LANES if k_ref is ksel_ref else 0
                k = k_ref[0, pl.ds(start, kc), col0:col0 + LANES]
                for h in range(NSA_HPG):
                    s_t = _dot(k, q_scr[g, :, h * tq:(h + 1) * tq])
                    chains.append((j, mask, single_pass, g, g * n + h * tq, s_t))
        for j, mask, single_pass, g, c0, s_t in chains:
            if mask is not None:
                s_t = jnp.where(mask, s_t, NEG_INF)
            if single_pass:
                _flash_update_stale(s_t, v_t_ref[0, j, g], m_scr, acc_scr, exc_scr, pl.ds(c0, tq))
            else:
                _flash_update_t(s_t, v_t_ref[0, j, g], m_scr, acc_scr, pl.ds(c0, tq))

    def finish_flash(branch):
        for g in range(NSA_KV_GROUPS):
            c = pl.ds(g * n, n)
            add_branch(branch, g, acc_scr[0:hd, c] / acc_scr[hd:hd + 1, c])

    zeros_q = jnp.zeros((hd, n), BF16)
    qgs = []
    for g in range(NSA_KV_GROUPS):
        qg = jnp.concatenate([q_t[(NSA_HPG * g + h) * hd:(NSA_HPG * g + h + 1) * hd, :] for h in range(NSA_HPG)],
                             axis=1)
        qgs.append(qg)
        qz_scr[g] = jnp.concatenate([qg, zeros_q] if g == 0 else [zeros_q, qg], axis=0)

    reset_flash()
    cq = (lax.broadcasted_iota(jnp.int32, (ncp, tq), 0) * CMP_STRIDE
          - lax.broadcasted_iota(jnp.int32, (ncp, tq), 1))
    cmp_valid = (cq <= t0 - (CMP_BLOCK - 1)) & (lax.broadcasted_iota(jnp.int32, (ncp, tq), 0) < n_cmp)
    chains = [(g * n + h * tq, _dot(kcmp_ref[0], qz_scr[g, :, h * tq:(h + 1) * tq]))
              for g in range(NSA_KV_GROUPS) for h in range(NSA_HPG)]
    for c0, s_t in chains:
        _flash_update_t(jnp.where(cmp_valid, s_t, NEG_INF), vcmp_t_ref[0, c0 // n], m_scr, acc_scr, pl.ds(c0, tq))
    seen = (t0 + lax.broadcasted_iota(jnp.int32, (1, tq), 1) >= CMP_BLOCK - 1).astype(F32)
    seen4 = jnp.concatenate([seen] * NSA_HPG, axis=1)
    imp_rows = slice(hd, hd + SLC_MAX_BLOCKS)
    den_row = slice(hd + SLC_MAX_BLOCKS, hd + SLC_MAX_BLOCKS + 1)

    for g in range(NSA_KV_GROUPS):
        qg = qgs[g]
        c = pl.ds(g * n, n)
        inv_l = seen4 / acc_scr[den_row, c]
        add_branch(0, g, acc_scr[0:hd, c] * inv_l)
        pooled = acc_scr[imp_rows, c] * inv_l
        imp_t = functools.reduce(jnp.add, [pooled[:, h * tq:(h + 1) * tq] for h in range(NSA_HPG)])
        nb = lax.broadcasted_iota(jnp.int32, (SLC_MAX_BLOCKS, tq), 0)
        tqv = t0 + lax.broadcasted_iota(jnp.int32, (SLC_MAX_BLOCKS, tq), 1)
        cur = lax.shift_right_logical(tqv, SLC_SHIFT)
        forced = (nb == 0) | (nb == cur) | (nb == cur - 1)
        v_scr[...] = jnp.where(nb * SLC_BLOCK <= tqv, jnp.where(forced, FORCE_SCORE, imp_t), -1.0)
        groups = SLC_MAX_BLOCKS // SUBLANES
        sub = lax.broadcasted_iota(jnp.int32, (SUBLANES, tq), 0)
        vch = [v_scr[r * SUBLANES:(r + 1) * SUBLANES, :] for r in range(groups)]
        rank_scr[...] = jnp.zeros(rank_scr.shape, jnp.int32)
        for mg in range(groups):
            @pl.when(mg * SUBLANES * SLC_BLOCK < t0 + tq)
            def _():
                count = [jnp.zeros((SUBLANES, tq), jnp.int32) for _ in range(groups)]
                for mblk in range(mg * SUBLANES, (mg + 1) * SUBLANES):
                    vm = jnp.broadcast_to(v_scr[mblk:mblk + 1, :], (SUBLANES, tq))
                    for r in range(groups):
                        if r > mg:
                            before = vm >= vch[r]
                        elif r < mg:
                            before = vm > vch[r]
                        else:
                            before = (vm > vch[r]) | ((vm == vch[r]) & (sub > mblk - r * SUBLANES))
                        count[r] = count[r] + jnp.where(before, 1, 0)
                for r in range(groups):
                    rank_scr[r * SUBLANES:(r + 1) * SUBLANES, :] += count[r]
        pen = jnp.where(rank_scr[...] < n_top, 0.0, SEL_PENALTY)
        pen4 = jnp.concatenate([pen.astype(BF16)] * NSA_HPG, axis=1)
        qa_scr[g] = jnp.concatenate([qg, pen4] if g == 0 else [pen4, qg], axis=0)

    jd = t0 // kc

    def attend(single_pass):
        reset_flash()
        sel = functools.partial(flash_step, qa_scr, ksel_ref, vsl_t_ref)
        sel([(jd, causal, False)])
        lax.fori_loop(0, jd // 3, lambda i, cr: (
            sel([(3 * i, None, single_pass), (3 * i + 1, None, single_pass), (3 * i + 2, None, single_pass)]),
            cr)[1], 0)

        @pl.when(jd % 3 == 1)
        def _():
            sel([(jd - 1, None, single_pass)])

        @pl.when(jd % 3 == 2)
        def _():
            sel([(jd - 2, None, single_pass), (jd - 1, None, single_pass)])

        finish_flash(1)

        reset_flash()
        win = functools.partial(flash_step, qz_scr, kw_ref, vw_t_ref)
        assert WINDOW == 2 * kc
        win([(jd, causal, False)])

        @pl.when(jd == 1)
        def _():
            win([(jd - 1, None, single_pass)])

        @pl.when(jd >= 2)
        def _():
            win([(jd - 1, None, single_pass), (jd - 2, newer, single_pass)])

        finish_flash(2)

    exc_scr[...] = jnp.full(exc_scr.shape, NEG_INF, F32)
    attend(single_pass=True)

    @pl.when(jnp.max(exc_scr[...]) > MAX_STALE_EXCESS)
    def _():
        attend(single_pass=False)

    out_ref[0, tile_rows, :] = (sga_ref[0, tile_rows, :].astype(F32) * (mix_scr[...] + mix2_scr[...]).T).astype(BF16)


def _nsa_attention(q_t, kcmp, vcmp_t, ksel, vsl_t, kw, vw_t, gates, sga):
    b, seq, _ = ksel.shape
    tq = min(NSA_TQ, seq)
    nq = seq // tq
    n_slc = seq // SLC_BLOCK
    assert n_slc <= SLC_MAX_BLOCKS and seq % tq == 0 and WINDOW % tq == 0 and tq & (tq - 1) == 0
    ncp = kcmp.shape[1]
    n_cmp = (seq - CMP_BLOCK) // CMP_STRIDE + 1
    vsl_t = vsl_t.reshape((b, nq) + vsl_t.shape[1:])
    vw_t = vw_t.reshape((b, nq) + vw_t.shape[1:])
    n = NSA_HPG * tq
    tiles = NSA_TILES_PER_STEP if nq % NSA_TILES_PER_STEP == 0 else 1
    tile = lambda w: pl.BlockSpec((1, tiles * tq, w), lambda bi, qi: (bi, qi, 0))
    per_b = lambda a: pl.BlockSpec((1,) + a.shape[1:], lambda bi, qi: (bi,) + (0,) * (a.ndim - 1))
    return pl.pallas_call(
        functools.partial(_nsa_kernel, tq=tq, kc=tq, n_cmp=n_cmp, n_top=min(SLC_TOPK, n_slc), tiles=tiles),
        grid=(b, nq // tiles),
        in_specs=[pl.BlockSpec((NSA_WIDTH, tiles * tq), lambda bi, qi: (0, bi * (nq // tiles) + qi)),
                  per_b(kcmp), per_b(vcmp_t), per_b(ksel), per_b(vsl_t), per_b(kw), per_b(vw_t),
                  tile(LANES), tile(NSA_WIDTH)],
        out_specs=tile(NSA_WIDTH),
        out_shape=jax.ShapeDtypeStruct((b, seq, NSA_WIDTH), BF16),
        scratch_shapes=[pltpu.VMEM((NSA_KV_GROUPS, LANES, n), BF16),
                        pltpu.VMEM((NSA_KV_GROUPS, LANES, n), BF16),
                        pltpu.VMEM((SLC_MAX_BLOCKS, tq), F32),
                        pltpu.VMEM((SLC_MAX_BLOCKS, tq), jnp.int32),
                        pltpu.VMEM((1, NSA_KV_GROUPS * n), F32),
                        pltpu.VMEM((NSA_CMP_ROWS, NSA_KV_GROUPS * n), F32),
                        pltpu.VMEM((NSA_WIDTH, tq), F32),
                        pltpu.VMEM((NSA_WIDTH, tq), F32),
                        pltpu.VMEM((1, NSA_KV_GROUPS * n), F32)],
        compiler_params=pltpu.CompilerParams(dimension_semantics=("arbitrary", "arbitrary"),
                                             vmem_limit_bytes=VMEM_LIMIT),
        name="nsa_attn",
    )(q_t, kcmp, vcmp_t, ksel, vsl_t, kw, vw_t, gates, sga)


def _odd_proj_kernel(x_ref, mixa_ref, mixb_ref, posr_ref, wout_ref, g_ref, wc_ref, wgate_ref, qn_ref, kvn_ref,
                     wuqt_ref, wuk_ref, wuvt_ref, invfc_ref,
                     y_ref, qt_ref, kn_ref, kpe_ref, vt_ref, sg_ref, *, tm):
    y = (x_ref[...] + _dot(mixa_ref[...], wout_ref[0:NSA_WIDTH, :])
         + _dot(mixb_ref[...], wout_ref[NSA_WIDTH:, :]))
    y_ref[...] = y
    yn = _rms(y, g_ref[...]).astype(BF16)
    cos_t, sin_t = _rope_tables_t(invfc_ref[...], posr_ref[...])
    half = MLA_ROPE_DIM // 2

    def rope_t(x1, x2):
        return x1 * cos_t - x2 * sin_t, x2 * cos_t + x1 * sin_t

    seg = _dot(yn, wc_ref[...])
    kr_t = seg[:, MLA_Q_RANK + MLA_KV_RANK:].T
    kp1, kp2 = rope_t(kr_t[0:half], kr_t[half:2 * half])
    kpe_ref[...] = jnp.concatenate([kp1, kp2, kr_t[2 * half:]], axis=0).T.astype(BF16)
    cq = _rms(seg[:, 0:MLA_Q_RANK], qn_ref[...]).astype(BF16)
    ckv = _rms(seg[:, MLA_Q_RANK:MLA_Q_RANK + MLA_KV_RANK], kvn_ref[...]).astype(BF16)
    scale = (MLA_NOPE_DIM + MLA_ROPE_DIM) ** -0.5 * LOG2_E
    qt = _dot_nt(wuqt_ref[...], cq)
    kn_ref[...] = _dot(ckv, wuk_ref[...]).astype(BF16)
    for h in range(MLA_HEADS):
        c0 = h * MLA_QK_PAD
        r0 = c0 + MLA_NOPE_DIM
        qt_ref[c0:r0, :] = (qt[c0:r0] * scale).astype(BF16)
        q1, q2 = rope_t(qt[r0:r0 + half], qt[r0 + half:r0 + 2 * half])
        qt_ref[r0:r0 + half, :] = (q1 * scale).astype(BF16)
        qt_ref[r0 + half:r0 + 2 * half, :] = (q2 * scale).astype(BF16)
        qt_ref[r0 + 2 * half:c0 + MLA_QK_PAD, :] = jnp.zeros((MLA_QK_PAD - MLA_NOPE_DIM - 2 * half, tm), BF16)
    v_t = _dot_nt(wuvt_ref[...], ckv)
    kc = vt_ref.shape[-1]
    for h in range(MLA_HEADS):
        for c in range(tm // kc):
            vt_ref[0, h, c, 0:MLA_V_DIM, :] = v_t[h * MLA_V_DIM:(h + 1) * MLA_V_DIM, c * kc:(c + 1) * kc].astype(BF16)
            vt_ref[0, h, c, MLA_V_DIM:, :] = jnp.ones((MLA_ONES_ROWS, kc), BF16)
    sg_ref[...] = _silu(_dot(yn, wgate_ref[...])).astype(BF16)


def _odd_proj(x2d, mixa, mixb, pos_row, a_w_out, c_norm, c_w_in, q_norm, kv_norm, w_uq, w_ukv, invf_col, seq):
    t = x2d.shape[0]
    tm = min(PROJ_ROWS, seq)
    r2 = MLA_Q_RANK + MLA_KV_RANK + MLA_ROPE_DIM
    wc = jnp.pad(c_w_in[:, :r2], ((0, 0), (0, LANES - MLA_ROPE_DIM))).astype(BF16)
    wgate = c_w_in[:, r2:].astype(BF16)
    wuqt = jnp.pad(w_uq.reshape(MLA_Q_RANK, MLA_HEADS, MLA_NOPE_DIM + MLA_ROPE_DIM),
                   ((0, 0), (0, 0), (0, MLA_QK_PAD - MLA_NOPE_DIM - MLA_ROPE_DIM))
                   ).reshape(MLA_Q_RANK, MLA_HEADS * MLA_QK_PAD).T.astype(BF16)
    wukv = w_ukv.reshape(MLA_KV_RANK, MLA_HEADS, MLA_NOPE_DIM + MLA_V_DIM)
    wuk = wukv[:, :, :MLA_NOPE_DIM].reshape(MLA_KV_RANK, -1).astype(BF16)
    wuvt = wukv[:, :, MLA_NOPE_DIM:].reshape(MLA_KV_RANK, -1).T.astype(BF16)
    kc = min(MLA_TQ, seq)
    nq = seq // kc
    cps = tm // kc
    assert tm % kc == 0 and seq % tm == 0
    vrows = MLA_V_DIM + MLA_ONES_ROWS
    ins = [x2d, mixa, mixb, pos_row, a_w_out.astype(BF16), c_norm.reshape(1, -1), wc, wgate,
           q_norm.reshape(1, -1), kv_norm.reshape(1, -1), wuqt, wuk, wuvt, invf_col]
    full = lambda a: pl.BlockSpec(a.shape, lambda i: (0,) * a.ndim)
    rows = lambda n: pl.BlockSpec((tm, n), lambda i: (i, 0))
    cols = lambda n: pl.BlockSpec((n, tm), lambda i: (0, i))
    out_shapes = [
        jax.ShapeDtypeStruct((t, D_MODEL), F32),
        jax.ShapeDtypeStruct((MLA_HEADS * MLA_QK_PAD, t), BF16),
        jax.ShapeDtypeStruct((t, MLA_HEADS * MLA_NOPE_DIM), BF16),
        jax.ShapeDtypeStruct((t, LANES), BF16),
        jax.ShapeDtypeStruct((t // seq, MLA_HEADS, nq, vrows, kc), BF16),
        jax.ShapeDtypeStruct((t, MLA_WIDTH), BF16),
    ]
    steps = seq // tm
    vt_spec = pl.BlockSpec((1, MLA_HEADS, cps, vrows, kc), lambda i: (i // steps, 0, i % steps, 0, 0))
    out_specs = [rows(D_MODEL), cols(MLA_HEADS * MLA_QK_PAD), rows(MLA_HEADS * MLA_NOPE_DIM), rows(LANES), vt_spec,
                 rows(MLA_WIDTH)]
    return pl.pallas_call(
        functools.partial(_odd_proj_kernel, tm=tm),
        grid=(t // tm,),
        in_specs=[rows(D_MODEL), rows(NSA_WIDTH), rows(CONV_WIDTH), cols(1)] + [full(a) for a in ins[4:]],
        out_specs=out_specs,
        out_shape=out_shapes,
        compiler_params=pltpu.CompilerParams(dimension_semantics=("arbitrary",),
                                             vmem_limit_bytes=VMEM_LIMIT),
        name="odd_proj",
    )(*ins)


def _mla_kernel(qt_ref, kn_ref, kpe_ref, v_t_ref, o_ref, m_scr, acc_scr, exc_scr, *, tq, kc, hp):
    jd = pl.program_id(2)
    causal = (lax.broadcasted_iota(jnp.int32, (kc, tq), 0) <= lax.broadcasted_iota(jnp.int32, (kc, tq), 1))
    hq = tq // 2
    tri = causal[0:hq, 0:hq]
    keep_hi = jnp.concatenate([jnp.ones((hq, hq), jnp.bool_), tri], axis=0)

    def keys(h, start, size):
        return jnp.concatenate([kn_ref[0, pl.ds(start, size), h * MLA_NOPE_DIM:(h + 1) * MLA_NOPE_DIM],
                                kpe_ref[0, pl.ds(start, size), :]], axis=1)

    def step(js, single_pass):
        scores = []
        for j in js:
            start = pl.multiple_of(j * kc, kc)
            for h in range(hp):
                qk = slice(h * MLA_QK_PAD, (h + 1) * MLA_QK_PAD)
                scores.append((j, h, _dot(keys(h, start, kc), qt_ref[qk, :])))
        for j, h, s_t in scores:
            if single_pass:
                _flash_update_stale(s_t, v_t_ref[0, h, j], m_scr, acc_scr, exc_scr, pl.ds(h * tq, tq))
            else:
                _flash_update_t(s_t, v_t_ref[0, h, j], m_scr, acc_scr, pl.ds(h * tq, tq))

    def attend(single_pass):
        m_scr[...] = jnp.full(m_scr.shape, NEG_INF, F32)
        acc_scr[...] = jnp.zeros(acc_scr.shape, F32)
        start = pl.multiple_of(jd * kc, kc)
        scores = []
        for h in range(hp):
            qk = slice(h * MLA_QK_PAD, (h + 1) * MLA_QK_PAD)
            scores.append((_dot(keys(h, start, hq), qt_ref[qk, 0:hq]),
                           _dot(keys(h, start, kc), qt_ref[qk, hq:tq])))
        for h, (s_lo, s_hi) in enumerate(scores):
            v_t = v_t_ref[0, h, jd]
            _flash_update_t(jnp.where(tri, s_lo, NEG_INF), v_t[:, 0:hq], m_scr, acc_scr, pl.ds(h * tq, hq))
            _flash_update_t(jnp.where(keep_hi, s_hi, NEG_INF), v_t, m_scr, acc_scr, pl.ds(h * tq + hq, hq))
        lax.fori_loop(0, jd // 2, lambda i, c: (step([2 * i, 2 * i + 1], single_pass), c)[1], 0)

        @pl.when(jd % 2 == 1)
        def _():
            step([jd - 1], single_pass)

        for h in range(hp):
            c = pl.ds(h * tq, tq)
            o_t = acc_scr[0:MLA_V_DIM, c] / acc_scr[MLA_V_DIM:MLA_V_DIM + 1, c]
            o_ref[0, :, h * MLA_V_DIM:(h + 1) * MLA_V_DIM] = o_t.T.astype(BF16)

    exc_scr[...] = jnp.full(exc_scr.shape, NEG_INF, F32)
    attend(single_pass=True)

    @pl.when(jnp.max(exc_scr[...]) > MAX_STALE_EXCESS)
    def _():
        attend(single_pass=False)


def _mla_attention(q_t, k_nope, k_pe, v_t):
    b, seq, _ = k_pe.shape
    tq = min(MLA_TQ, seq)
    hp = MLA_HEADS_PER_STEP
    nq = seq // tq
    assert seq % tq == 0 and MLA_HEADS % hp == 0
    vrows = MLA_V_DIM + MLA_ONES_ROWS
    return pl.pallas_call(
        functools.partial(_mla_kernel, tq=tq, kc=tq, hp=hp),
        grid=(b, MLA_HEADS // hp, nq),
        in_specs=[pl.BlockSpec((hp * MLA_QK_PAD, tq), lambda bi, h, qi: (h, bi * nq + qi)),
                  pl.BlockSpec((1, seq, hp * MLA_NOPE_DIM), lambda bi, h, qi: (bi, 0, h)),
                  pl.BlockSpec((1, seq, LANES), lambda bi, h, qi: (bi, 0, 0)),
                  pl.BlockSpec((1, hp, nq, vrows, tq), lambda bi, h, qi: (bi, h, 0, 0, 0))],
        out_specs=pl.BlockSpec((1, tq, hp * MLA_V_DIM), lambda bi, h, qi: (bi, qi, h)),
        out_shape=jax.ShapeDtypeStruct((b, seq, MLA_WIDTH), BF16),
        scratch_shapes=[pltpu.VMEM((1, hp * tq), F32), pltpu.VMEM((vrows, hp * tq), F32),
                        pltpu.VMEM((1, hp * tq), F32)],
        compiler_params=pltpu.CompilerParams(dimension_semantics=("arbitrary", "arbitrary", "arbitrary"),
                                             vmem_limit_bytes=VMEM_LIMIT),
        name="mla_attn",
    )(q_t, k_nope, k_pe, v_t)


def _final_kernel(y_ref, o_ref, sg_ref, w_ref, g_ref, out_ref):
    gated = (sg_ref[...].astype(F32) * o_ref[...].astype(F32)).astype(BF16)
    out_ref[...] = _rms(y_ref[...] + _dot(gated, w_ref[...]), g_ref[...])


def _final(y, o, sg, w_out, final_norm, seq):
    t = y.shape[0]
    tm = min(PROJ_ROWS, seq)
    rows = lambda n: pl.BlockSpec((tm, n), lambda i: (i, 0))
    full = lambda a: pl.BlockSpec(a.shape, lambda i: (0,) * a.ndim)
    w = w_out.astype(BF16)
    g = final_norm.reshape(1, -1)
    return pl.pallas_call(
        _final_kernel,
        grid=(t // tm,),
        in_specs=[rows(D_MODEL), rows(MLA_WIDTH), rows(MLA_WIDTH), full(w), full(g)],
        out_specs=rows(D_MODEL),
        out_shape=jax.ShapeDtypeStruct((t, D_MODEL), F32),
        compiler_params=pltpu.CompilerParams(dimension_semantics=("arbitrary",),
                                             vmem_limit_bytes=VMEM_LIMIT),
        name="final",
    )(y, o, sg, w, g)


def _rope_constants():
    half = NSA_HEAD_DIM // 2
    inv_freq = ROPE_THETA ** (-jnp.arange(half, dtype=F32) / half)
    invf = jnp.tile(inv_freq, LANES // half).reshape(1, LANES)
    sgn = jnp.tile(jnp.concatenate([-jnp.ones((half,), F32), jnp.ones((half,), F32)]), LANES // (2 * half))
    return invf, sgn.reshape(1, LANES), inv_freq.reshape(half, 1)


def kernel(x, positions, a_norm, a_w_in, a_pe_k, a_pe_v, a_w_ck1, a_w_ck2, a_w_cv1, a_w_cv2, a_conv_w, a_w_out, c_norm, c_w_in, c_q_norm, c_kv_norm, c_w_uq, c_w_ukv, c_w_out, final_norm):
    b, seq, d = x.shape
    assert d == D_MODEL and NSA_HEAD_DIM == MLA_ROPE_DIM
    assert a_norm.shape[0] == 1 and c_norm.shape[0] == 1
    t = b * seq
    invf, sgn, invf_col = _rope_constants()
    x2d = x.reshape(t, d)
    pos_row = positions.reshape(1, t)

    q_t, kc, vc, ksel, vsl_t, kw, vw_t, gates, sga, mixb = _even_proj(
        x2d, pos_row, a_norm[0], a_w_in[0], a_conv_w[0], invf_col, seq)
    nchunk = seq // CMP_STRIDE
    pos_cmp = jnp.pad(positions[:, CMP_BLOCK - 1::CMP_STRIDE], ((0, 0), (0, 1)))[:, :nchunk, None]
    kcmp, vcmp_t = _compress(kc.reshape(b, seq, -1), vc.reshape(b, seq, -1), pos_cmp, a_pe_k[0], a_pe_v[0],
                             a_w_ck1[0], a_w_ck2[0], a_w_cv1[0], a_w_cv2[0], invf, sgn)
    r3 = lambda a: a.reshape(b, seq, a.shape[-1])
    mixa = _nsa_attention(q_t, kcmp, vcmp_t, r3(ksel), vsl_t, r3(kw), vw_t, r3(gates), r3(sga))

    y, mq_t, mkn, mkpe, mv_t, sg = _odd_proj(x2d, mixa.reshape(t, -1), mixb, pos_row, a_w_out[0], c_norm[0],
                                      c_w_in[0], c_q_norm[0], c_kv_norm[0], c_w_uq[0], c_w_ukv[0], invf_col, seq)
    o = _mla_attention(mq_t, r3(mkn), r3(mkpe), mv_t)
    out = _final(y, o.reshape(t, -1), sg, c_w_out[0], final_norm, seq)
    return out.reshape(b, seq, d)
```

```python
import functools

import jax
import jax.numpy as jnp
import numpy as np
from jax import lax
from jax.experimental import pallas as pl
from jax.experimental.pallas import tpu as pltpu

F32 = jnp.float32
BF16 = jnp.bfloat16

D_MODEL = 1024
ROPE_THETA = 10000.0
RMS_EPS = 1e-6
NEG_INF = -1e30
FORCE_SCORE = 1e4
SEL_PENALTY = NEG_INF

NSA_HEADS = 8
NSA_KV_GROUPS = 2
NSA_HPG = NSA_HEADS // NSA_KV_GROUPS
NSA_HEAD_DIM = 64
NSA_WIDTH = NSA_HEADS * NSA_HEAD_DIM
NSA_KV_WIDTH = NSA_KV_GROUPS * NSA_HEAD_DIM
CMP_BLOCK = 32
CMP_STRIDE = 16
CMP_HIDDEN = 2 * NSA_HEAD_DIM
SLC_BLOCK = 64
SLC_TOPK = 16
SLC_SHIFT = SLC_BLOCK.bit_length() - 1
SLC_MAX_BLOCKS = 64
WINDOW = 512
CONV_WIDTH = D_MODEL - NSA_WIDTH
CONV_K = 3

MLA_HEADS = 8
MLA_NOPE_DIM = 128
MLA_ROPE_DIM = 64
MLA_V_DIM = 128
MLA_Q_RANK = 256
MLA_KV_RANK = 256
MLA_WIDTH = MLA_HEADS * MLA_V_DIM
MLA_QK_PAD = 256

LANES = 128
SUBLANES = 8
VMEM_LIMIT = 56 * 1024 * 1024

PROJ_ROWS = 1024
NSA_TQ = 256
NSA_TILES_PER_STEP = 1
MLA_TQ = 512
MLA_HEADS_PER_STEP = 4
MLA_ONES_ROWS = 16
NSA_ONES_ROWS = 16
NSA_CMP_ROWS = NSA_HEAD_DIM + SLC_MAX_BLOCKS + NSA_ONES_ROWS
LOG2_E = 1.4426950408889634
MAX_STALE_EXCESS = 64.0


def _dot(a, b):
    return jnp.dot(a, b, preferred_element_type=F32)


def _dot_nt(a, b):
    return lax.dot_general(a, b, (((1,), (1,)), ((), ())), preferred_element_type=F32)


def _silu(x):
    return x * jax.nn.sigmoid(x)


def _rms(x, g):
    return x * lax.rsqrt(jnp.mean(x * x, axis=-1, keepdims=True) + RMS_EPS) * g


def _rope_tables(pos_col, invf_row, sign_row):
    ang = pos_col.astype(F32) * invf_row
    return jnp.cos(ang), jnp.sin(ang) * sign_row


def _rope_slab(s, cos2, sin2):
    half = NSA_HEAD_DIM // 2
    first_half = (lax.broadcasted_iota(jnp.int32, s.shape, 1) & half) == 0
    swapped = jnp.where(first_half, pltpu.roll(s, LANES - half, 1), pltpu.roll(s, half, 1))
    return s * cos2 + swapped * sin2


def _rope_tables_t(invf_col, pos_row):
    ang = invf_col * pos_row.astype(F32)
    return jnp.cos(ang), jnp.sin(ang)


def _flash_update_stale(s_t, v_t, m_ref, acc_ref, exc_ref, cols):
    m_old = m_ref[:, cols]
    p = jnp.exp2(s_t - m_old)
    cmax = jnp.max(s_t, axis=0, keepdims=True)
    m_new = jnp.maximum(m_old, cmax)
    rows = v_t.shape[0]
    acc_ref[0:rows, cols] = jnp.exp2(m_old - m_new) * (acc_ref[0:rows, cols] + _dot(v_t, p.astype(BF16)))
    m_ref[:, cols] = m_new
    exc_ref[:, cols] = jnp.maximum(exc_ref[:, cols], cmax - m_old)


def _flash_update_t(s_t, v_t, m_ref, acc_ref, cols):
    m_old = m_ref[:, cols]
    m_new = jnp.maximum(m_old, jnp.max(s_t, axis=0, keepdims=True))
    p = jnp.exp2(s_t - m_new)
    rows = v_t.shape[0]
    acc_ref[0:rows, cols] = jnp.exp2(m_old - m_new) * acc_ref[0:rows, cols] + _dot(v_t, p.astype(BF16))
    m_ref[:, cols] = m_new


def _even_proj_kernel(x_ref, posr_ref, g_ref, wqt_ref, wkv_ref, wg_ref, wga_ref, wcb_ref, wcc_ref, wch_ref,
                      wgb_ref, convw_ref, invfc_ref,
                      qt_ref, kc_ref, vc_ref, ksel_ref, vslt_ref, kw_ref, vwt_ref, gates_ref, sga_ref,
                      mixb_ref, carry_ref, *, tiles_per_seq, tm, kc):
    i = pl.program_id(0)
    xn = _rms(x_ref[...], g_ref[...]).astype(BF16)
    cos_t, sin_t = _rope_tables_t(invfc_ref[...], posr_ref[...])
    half = NSA_HEAD_DIM // 2
    lane = lax.broadcasted_iota(jnp.int32, (tm, LANES), 1)
    low = lane < NSA_HEAD_DIM

    def rope_t(x_t):
        out = []
        for r in range(0, x_t.shape[0], 2 * half):
            x1, x2 = x_t[r:r + half], x_t[r + half:r + 2 * half]
            out += [x1 * cos_t - x2 * sin_t, x2 * cos_t + x1 * sin_t]
        return jnp.concatenate(out, axis=0)

    def store_values_t(ref, v):
        v_t = v.T.astype(BF16)
        ones = jnp.ones((NSA_ONES_ROWS, kc), BF16)
        for c in range(tm // kc):
            for g in range(NSA_KV_GROUPS):
                ref[c, g, 0:NSA_HEAD_DIM, :] = v_t[g * NSA_HEAD_DIM:(g + 1) * NSA_HEAD_DIM, c * kc:(c + 1) * kc]
                ref[c, g, NSA_HEAD_DIM:, :] = ones

    scale = NSA_HEAD_DIM ** -0.5 * LOG2_E
    qt_ref[...] = (rope_t(_dot_nt(wqt_ref[...], xn)) * scale).astype(BF16)

    seg = _dot(xn, wkv_ref[...])
    kc_ref[...] = seg[:, 0:LANES]
    vc_ref[...] = seg[:, LANES:2 * LANES]
    ksl = rope_t(seg[:, 2 * LANES:3 * LANES].T).T
    store_values_t(vslt_ref, seg[:, 3 * LANES:4 * LANES])
    kw_ref[...] = rope_t(seg[:, 4 * LANES:5 * LANES].T).T.astype(BF16)
    store_values_t(vwt_ref, seg[:, 5 * LANES:6 * LANES])

    spos = (i % tiles_per_seq) * tm + lax.broadcasted_iota(jnp.int32, (tm, LANES), 0)
    blk = lax.shift_right_logical(spos, SLC_SHIFT)
    ksel_ref[:, 0:LANES] = jnp.where(low, ksl, (lane - NSA_HEAD_DIM == blk).astype(F32)).astype(BF16)
    ksel_ref[:, LANES:2 * LANES] = jnp.where(low, (lane == blk).astype(F32), ksl).astype(BF16)

    gates_ref[...] = jax.nn.sigmoid(_dot(xn, wg_ref[...]))
    sga_ref[...] = _silu(_dot(xn, wga_ref[...])).astype(BF16)

    u = _dot(xn, wcc_ref[...]) * _dot(xn, wch_ref[...])

    @pl.when(i % tiles_per_seq == 0)
    def _():
        carry_ref[...] = jnp.zeros_like(carry_ref)

    row = lax.broadcasted_iota(jnp.int32, (tm, CONV_WIDTH), 0)
    prev1 = carry_ref[SUBLANES - 1:SUBLANES, :]
    prev2 = carry_ref[SUBLANES - 2:SUBLANES - 1, :]
    u1 = jnp.where(row == 0, prev1, pltpu.roll(u, 1, 0))
    u2 = jnp.where(row == 0, prev2, jnp.where(row == 1, prev1, pltpu.roll(u, 2, 0)))
    carry_ref[...] = u[tm - SUBLANES:tm, :]
    w = convw_ref[...]
    y = w[0:1, :] * u2 + w[1:2, :] * u1 + w[2:3, :] * u
    cb = _dot(xn, wcb_ref[...])
    mixb_ref[...] = (_silu(_dot(xn, wgb_ref[...])) * (cb * y)).astype(BF16)


def _even_proj(x2d, pos_row, norm_g, w_in, conv_w, invf_col, seq):
    t = x2d.shape[0]
    tm = min(PROJ_ROWS, seq)
    kc = min(NSA_TQ, seq)
    assert seq % tm == 0 and t % tm == 0 and tm % kc == 0 and conv_w.shape[0] == CONV_K == 3
    sizes = (NSA_WIDTH,) + (NSA_KV_WIDTH,) * 6 + (3 * NSA_HEADS, NSA_WIDTH) + (CONV_WIDTH,) * 4
    offs = np.concatenate([[0], np.cumsum(sizes)])
    col = lambda k: w_in[:, offs[k]:offs[k + 1]]
    wqt = col(0).T.astype(BF16)
    wkv = jnp.concatenate([col(k) for k in range(1, 7)], axis=1).astype(BF16)
    wg = jnp.pad(col(7), ((0, 0), (0, LANES - 3 * NSA_HEADS))).astype(BF16)
    wga, wcb, wcc, wch, wgb = (col(k).astype(BF16) for k in range(8, 13))

    full = lambda a: pl.BlockSpec(a.shape, lambda i: (0,) * a.ndim)
    rows = lambda n: pl.BlockSpec((tm, n), lambda i: (i, 0))
    cols = lambda n: pl.BlockSpec((n, tm), lambda i: (0, i))
    vrows = NSA_HEAD_DIM + NSA_ONES_ROWS
    values_t = jax.ShapeDtypeStruct((t // kc, NSA_KV_GROUPS, vrows, kc), BF16)
    values_t_spec = pl.BlockSpec((tm // kc, NSA_KV_GROUPS, vrows, kc), lambda i: (i, 0, 0, 0))
    out_shapes = [
        jax.ShapeDtypeStruct((NSA_WIDTH, t), BF16),
        jax.ShapeDtypeStruct((t, NSA_KV_WIDTH), F32),
        jax.ShapeDtypeStruct((t, NSA_KV_WIDTH), F32),
        jax.ShapeDtypeStruct((t, 2 * LANES), BF16),
        values_t,
        jax.ShapeDtypeStruct((t, NSA_KV_WIDTH), BF16),
        values_t,
        jax.ShapeDtypeStruct((t, LANES), F32),
        jax.ShapeDtypeStruct((t, NSA_WIDTH), BF16),
        jax.ShapeDtypeStruct((t, CONV_WIDTH), BF16),
    ]
    out_specs = [cols(NSA_WIDTH), rows(NSA_KV_WIDTH), rows(NSA_KV_WIDTH), rows(2 * LANES), values_t_spec,
                 rows(NSA_KV_WIDTH), values_t_spec, rows(LANES), rows(NSA_WIDTH), rows(CONV_WIDTH)]
    ins = [x2d, pos_row, norm_g.reshape(1, -1), wqt, wkv, wg, wga, wcb, wcc, wch, wgb, conv_w, invf_col]
    in_specs = [rows(D_MODEL), cols(1)] + [full(a) for a in ins[2:]]
    return pl.pallas_call(
        functools.partial(_even_proj_kernel, tiles_per_seq=seq // tm, tm=tm, kc=kc),
        grid=(t // tm,),
        in_specs=in_specs,
        out_specs=out_specs,
        out_shape=out_shapes,
        scratch_shapes=[pltpu.VMEM((SUBLANES, CONV_WIDTH), F32)],
        compiler_params=pltpu.CompilerParams(dimension_semantics=("arbitrary",),
                                             vmem_limit_bytes=VMEM_LIMIT),
        name="even_proj",
    )(*ins)


def _compress_kernel(kc_ref, vc_ref, pos_ref, pek_ref, pev_ref, w1k_ref, w2k_ref, w1v_ref, w2vt_ref,
                     invf_ref, sgn_ref, ov_ref, kcmp_ref, vcmpt_ref, *, nchunk):
    half = NSA_KV_GROUPS * CMP_HIDDEN
    rows16 = CMP_BLOCK // 2

    def hidden(raw_ref, pe_ref, w1_ref):
        acc = jnp.zeros((nchunk, 2 * half), F32)
        for l in range(rows16):
            x = raw_ref[0, pl.ds(l, nchunk, stride=CMP_STRIDE), :]
            lhs = jnp.concatenate([(x + pe_ref[l:l + 1, :]).astype(BF16),
                                   (x + pe_ref[rows16 + l:rows16 + l + 1, :]).astype(BF16)], axis=1)
            acc = acc + _dot(lhs, w1_ref[l])
        return _silu(acc[:, 0:half] + pltpu.roll(acc[:, half:], nchunk - 1, 0)).astype(BF16)

    cos2, sin2 = _rope_tables(pos_ref[0], invf_ref[...], sgn_ref[...])
    kcmp = _dot(hidden(kc_ref, pek_ref, w1k_ref), w2k_ref[...])
    kcmp_ref[0] = _rope_slab(kcmp, cos2, sin2).astype(BF16)
    vcmp_t = _dot_nt(w2vt_ref[...], hidden(vc_ref, pev_ref, w1v_ref)).astype(BF16)
    for g in range(NSA_KV_GROUPS):
        vcmpt_ref[0, g, 0:NSA_HEAD_DIM, :] = vcmp_t[g * NSA_HEAD_DIM:(g + 1) * NSA_HEAD_DIM]
        vcmpt_ref[0, g, NSA_HEAD_DIM:NSA_HEAD_DIM + SLC_MAX_BLOCKS, :] = ov_ref[...]
        vcmpt_ref[0, g, NSA_HEAD_DIM + SLC_MAX_BLOCKS:, :] = jnp.ones((NSA_ONES_ROWS, nchunk), BF16)


def _overlap_t(ncp, n_slc):
    cmp_starts = np.arange(ncp) * CMP_STRIDE
    slc_starts = np.arange(SLC_MAX_BLOCKS) * SLC_BLOCK
    ov = ((cmp_starts[None, :] < slc_starts[:, None] + SLC_BLOCK)
          & (cmp_starts[None, :] + CMP_BLOCK > slc_starts[:, None])
          & (np.arange(SLC_MAX_BLOCKS)[:, None] < n_slc))
    return jnp.asarray(ov.astype(np.float32), BF16)


def _compress_weights(pe, w1, w2):
    eye = jnp.eye(NSA_KV_GROUPS, dtype=F32)
    rows16 = CMP_BLOCK // 2
    w1r = w1.reshape(CMP_BLOCK, NSA_HEAD_DIM, CMP_HIDDEN)
    big = lambda part: jnp.einsum('ldn,gh->lgdhn', part, eye).reshape(
        rows16, NSA_KV_WIDTH, NSA_KV_GROUPS * CMP_HIDDEN)
    top, bot = big(w1r[:rows16]), big(w1r[rows16:])
    zero = jnp.zeros_like(top)
    w1big = jnp.concatenate([jnp.concatenate([top, zero], axis=2),
                             jnp.concatenate([zero, bot], axis=2)], axis=1).astype(BF16)
    w2big = jnp.einsum('nd,gh->gnhd', w2, eye).reshape(NSA_KV_GROUPS * CMP_HIDDEN, NSA_KV_WIDTH).astype(BF16)
    pe_rows = jnp.tile(pe, (1, NSA_KV_GROUPS))
    return pe_rows, w1big, w2big


def _compress(kc, vc, pos_cmp, pe_k, pe_v, w_ck1, w_ck2, w_cv1, w_cv2, invf, sgn):
    b, seq, _ = kc.shape
    nchunk = seq // CMP_STRIDE
    pek, w1k, w2k = _compress_weights(pe_k, w_ck1, w_ck2)
    pev, w1v, w2v = _compress_weights(pe_v, w_cv1, w_cv2)
    ins = [kc, vc, pos_cmp,
           pek, pev, w1k, w2k, w1v, w2v.T, invf, sgn, _overlap_t(nchunk, seq // SLC_BLOCK)]
    full = lambda a: pl.BlockSpec(a.shape, lambda i: (0,) * a.ndim)
    per_b = lambda n: pl.BlockSpec((1, nchunk, n), lambda i: (i, 0, 0))
    out_shape = [jax.ShapeDtypeStruct((b, nchunk, NSA_KV_WIDTH), BF16),
                 jax.ShapeDtypeStruct((b, NSA_KV_GROUPS, NSA_CMP_ROWS, nchunk), BF16)]
    return pl.pallas_call(
        functools.partial(_compress_kernel, nchunk=nchunk),
        grid=(b,),
        in_specs=[pl.BlockSpec((1, seq, NSA_KV_WIDTH), lambda i: (i, 0, 0))] * 2 + [per_b(1)]
        + [full(a) for a in ins[3:]],
        out_specs=[per_b(NSA_KV_WIDTH),
                   pl.BlockSpec((1, NSA_KV_GROUPS, NSA_CMP_ROWS, nchunk), lambda i: (i, 0, 0, 0))],
        out_shape=out_shape,
        compiler_params=pltpu.CompilerParams(dimension_semantics=("arbitrary",),
                                             vmem_limit_bytes=VMEM_LIMIT),
        name="compress",
    )(*ins)


def _nsa_kernel(qt_ref, kcmp_ref, vcmp_t_ref, ksel_ref, vsl_t_ref, kw_ref, vw_t_ref, gates_ref, sga_ref,
                out_ref, qz_scr, qa_scr, v_scr, rank_scr, m_scr, acc_scr, mix_scr, mix2_scr, exc_scr,
                *, tq, kc, n_cmp, n_top, tiles):
    for sub in range(tiles):
        _nsa_tile(pl.program_id(1) * tiles + sub, pl.ds(sub * tq, tq),
                  qt_ref, kcmp_ref, vcmp_t_ref, ksel_ref, vsl_t_ref, kw_ref, vw_t_ref, gates_ref, sga_ref,
                  out_ref, qz_scr, qa_scr, v_scr, rank_scr, m_scr, acc_scr, mix_scr, mix2_scr, exc_scr,
                  tq=tq, kc=kc, n_cmp=n_cmp, n_top=n_top)


def _nsa_tile(tile_idx, tile_rows, qt_ref, kcmp_ref, vcmp_t_ref, ksel_ref, vsl_t_ref, kw_ref, vw_t_ref, gates_ref,
              sga_ref, out_ref, qz_scr, qa_scr, v_scr, rank_scr, m_scr, acc_scr, mix_scr, mix2_scr, exc_scr,
              *, tq, kc, n_cmp, n_top):
    t0 = tile_idx * tq
    n = NSA_HPG * tq
    ncp = kcmp_ref.shape[1]
    hd = NSA_HEAD_DIM
    q_t = qt_ref[:, tile_rows]
    gates_t = gates_ref[0, tile_rows, :].T
    kq = lax.broadcasted_iota(jnp.int32, (kc, tq), 0) - lax.broadcasted_iota(jnp.int32, (kc, tq), 1)
    causal = kq <= 0
    newer = kq > 0

    def add_branch(branch, g, o_t):
        for h in range(NSA_HPG):
            head = NSA_HPG * g + h
            gate = gates_t[head * 3 + branch:head * 3 + branch + 1, :]
            term = gate * o_t[:, h * tq:(h + 1) * tq]
            r = pl.ds(head * hd, hd)
            if branch == 0:
                mix_scr[r, :] = term
            elif branch == 1:
                mix2_scr[r, :] = term
            else:
                mix2_scr[r, :] += term

    def reset_flash():
        m_scr[...] = jnp.full(m_scr.shape, NEG_INF, F32)
        acc_scr[...] = jnp.zeros(acc_scr.shape, F32)

    def flash_step(q_scr, k_ref, v_t_ref, chunks):
        chains = []
        for j, mask, single_pass in chunks:
            start = pl.multiple_of(j * kc, kc)
            for g in range(NSA_KV_GROUPS):
                col0 = g * LANES if k_ref is ksel_ref else 0
                k = k_ref[0, pl.ds(start, kc), col0:col0 + LANES]
                for h in range(NSA_HPG):
                    s_t = _dot(k, q_scr[g, :, h * tq:(h + 1) * tq])
                    chains.append((j, mask, single_pass, g, g * n + h * tq, s_t))
        for j, mask, single_pass, g, c0, s_t in chains:
            if mask is not None:
                s_t = jnp.where(mask, s_t, NEG_INF)
            if single_pass:
                _flash_update_stale(s_t, v_t_ref[0, j, g], m_scr, acc_scr, exc_scr, pl.ds(c0, tq))
            else:
                _flash_update_t(s_t, v_t_ref[0, j, g], m_scr, acc_scr, pl.ds(c0, tq))

    def finish_flash(branch):
        for g in range(NSA_KV_GROUPS):
            c = pl.ds(g * n, n)
            add_branch(branch, g, acc_scr[0:hd, c] / acc_scr[hd:hd + 1, c])

    zeros_q = jnp.zeros((hd, n), BF16)
    qgs = []
    for g in range(NSA_KV_GROUPS):
        qg = jnp.concatenate([q_t[(NSA_HPG * g + h) * hd:(NSA_HPG * g + h + 1) * hd, :] for h in range(NSA_HPG)],
                             axis=1)
        qgs.append(qg)
        qz_scr[g] = jnp.concatenate([qg, zeros_q] if g == 0 else [zeros_q, qg], axis=0)

    reset_flash()
    cq = (lax.broadcasted_iota(jnp.int32, (ncp, tq), 0) * CMP_STRIDE
          - lax.broadcasted_iota(jnp.int32, (ncp, tq), 1))
    cmp_valid = (cq <= t0 - (CMP_BLOCK - 1)) & (lax.broadcasted_iota(jnp.int32, (ncp, tq), 0) < n_cmp)
    chains = [(g * n + h * tq, _dot(kcmp_ref[0], qz_scr[g, :, h * tq:(h + 1) * tq]))
              for g in range(NSA_KV_GROUPS) for h in range(NSA_HPG)]
    for c0, s_t in chains:
        _flash_update_t(jnp.where(cmp_valid, s_t, NEG_INF), vcmp_t_ref[0, c0 // n], m_scr, acc_scr, pl.ds(c0, tq))
    seen = (t0 + lax.broadcasted_iota(jnp.int32, (1, tq), 1) >= CMP_BLOCK - 1).astype(F32)
    seen4 = jnp.concatenate([seen] * NSA_HPG, axis=1)
    imp_rows = slice(hd, hd + SLC_MAX_BLOCKS)
    den_row = slice(hd + SLC_MAX_BLOCKS, hd + SLC_MAX_BLOCKS + 1)

    for g in range(NSA_KV_GROUPS):
        qg = qgs[g]
        c = pl.ds(g * n, n)
        inv_l = seen4 / acc_scr[den_row, c]
        add_branch(0, g, acc_scr[0:hd, c] * inv_l)
        pooled = acc_scr[imp_rows, c] * inv_l
        imp_t = functools.reduce(jnp.add, [pooled[:, h * tq:(h + 1) * tq] for h in range(NSA_HPG)])
        nb = lax.broadcasted_iota(jnp.int32, (SLC_MAX_BLOCKS, tq), 0)
        tqv = t0 + lax.broadcasted_iota(jnp.int32, (SLC_MAX_BLOCKS, tq), 1)
        cur = lax.shift_right_logical(tqv, SLC_SHIFT)
        forced = (nb == 0) | (nb == cur) | (nb == cur - 1)
        v_scr[...] = jnp.where(nb * SLC_BLOCK <= tqv, jnp.where(forced, FORCE_SCORE, imp_t), -1.0)
        groups = SLC_MAX_BLOCKS // SUBLANES
        sub = lax.broadcasted_iota(jnp.int32, (SUBLANES, tq), 0)
        vch = [v_scr[r * SUBLANES:(r + 1) * SUBLANES, :] for r in range(groups)]
        rank_scr[...] = jnp.zeros(rank_scr.shape, jnp.int32)
        for mg in range(groups):
            @pl.when(mg * SUBLANES * SLC_BLOCK < t0 + tq)
            def _():
                count = [jnp.zeros((SUBLANES, tq), jnp.int32) for _ in range(groups)]
                for mblk in range(mg * SUBLANES, (mg + 1) * SUBLANES):
                    vm = jnp.broadcast_to(v_scr[mblk:mblk + 1, :], (SUBLANES, tq))
                    for r in range(groups):
                        if r > mg:
                            before = vm >= vch[r]
                        elif r < mg:
                            before = vm > vch[r]
                        else:
                            before = (vm > vch[r]) | ((vm == vch[r]) & (sub > mblk - r * SUBLANES))
                        count[r] = count[r] + jnp.where(before, 1, 0)
                for r in range(groups):
                    rank_scr[r * SUBLANES:(r + 1) * SUBLANES, :] += count[r]
        pen = jnp.where(rank_scr[...] < n_top, 0.0, SEL_PENALTY)
        pen4 = jnp.concatenate([pen.astype(BF16)] * NSA_HPG, axis=1)
        qa_scr[g] = jnp.concatenate([qg, pen4] if g == 0 else [pen4, qg], axis=0)

    jd = t0 // kc

    def attend(single_pass):
        reset_flash()
        sel = functools.partial(flash_step, qa_scr, ksel_ref, vsl_t_ref)
        sel([(jd, causal, False)])
        lax.fori_loop(0, jd // 3, lambda i, cr: (
            sel([(3 * i, None, single_pass), (3 * i + 1, None, single_pass), (3 * i + 2, None, single_pass)]),
            cr)[1], 0)

        @pl.when(jd % 3 == 1)
        def _():
            sel([(jd - 1, None, single_pass)])

        @pl.when(jd % 3 == 2)
        def _():
            sel([(jd - 2, None, single_pass), (jd - 1, None, single_pass)])

        finish_flash(1)

        reset_flash()
        win = functools.partial(flash_step, qz_scr, kw_ref, vw_t_ref)
        assert WINDOW == 2 * kc
        win([(jd, causal, False)])

        @pl.when(jd == 1)
        def _():
            win([(jd - 1, None, single_pass)])

        @pl.when(jd >= 2)
        def _():
            win([(jd - 1, None, single_pass), (jd - 2, newer, single_pass)])

        finish_flash(2)

    exc_scr[...] = jnp.full(exc_scr.shape, NEG_INF, F32)
    attend(single_pass=True)

    @pl.when(jnp.max(exc_scr[...]) > MAX_STALE_EXCESS)
    def _():
        attend(single_pass=False)

    out_ref[0, tile_rows, :] = (sga_ref[0, tile_rows, :].astype(F32) * (mix_scr[...] + mix2_scr[...]).T).astype(BF16)


def _nsa_attention(q_t, kcmp, vcmp_t, ksel, vsl_t, kw, vw_t, gates, sga):
    b, seq, _ = ksel.shape
    tq = min(NSA_TQ, seq)
    nq = seq // tq
    n_slc = seq // SLC_BLOCK
    assert n_slc <= SLC_MAX_BLOCKS and seq % tq == 0 and WINDOW % tq == 0 and tq & (tq - 1) == 0
    ncp = kcmp.shape[1]
    n_cmp = (seq - CMP_BLOCK) // CMP_STRIDE + 1
    vsl_t = vsl_t.reshape((b, nq) + vsl_t.shape[1:])
    vw_t = vw_t.reshape((b, nq) + vw_t.shape[1:])
    n = NSA_HPG * tq
    tiles = NSA_TILES_PER_STEP if nq % NSA_TILES_PER_STEP == 0 else 1
    tile = lambda w: pl.BlockSpec((1, tiles * tq, w), lambda bi, qi: (bi, qi, 0))
    per_b = lambda a: pl.BlockSpec((1,) + a.shape[1:], lambda bi, qi: (bi,) + (0,) * (a.ndim - 1))
    return pl.pallas_call(
        functools.partial(_nsa_kernel, tq=tq, kc=tq, n_cmp=n_cmp, n_top=min(SLC_TOPK, n_slc), tiles=tiles),
        grid=(b, nq // tiles),
        in_specs=[pl.BlockSpec((NSA_WIDTH, tiles * tq), lambda bi, qi: (0, bi * (nq // tiles) + qi)),
                  per_b(kcmp), per_b(vcmp_t), per_b(ksel), per_b(vsl_t), per_b(kw), per_b(vw_t),
                  tile(LANES), tile(NSA_WIDTH)],
        out_specs=tile(NSA_WIDTH),
        out_shape=jax.ShapeDtypeStruct((b, seq, NSA_WIDTH), BF16),
        scratch_shapes=[pltpu.VMEM((NSA_KV_GROUPS, LANES, n), BF16),
                        pltpu.VMEM((NSA_KV_GROUPS, LANES, n), BF16),
                        pltpu.VMEM((SLC_MAX_BLOCKS, tq), F32),
                        pltpu.VMEM((SLC_MAX_BLOCKS, tq), jnp.int32),
                        pltpu.VMEM((1, NSA_KV_GROUPS * n), F32),
                        pltpu.VMEM((NSA_CMP_ROWS, NSA_KV_GROUPS * n), F32),
                        pltpu.VMEM((NSA_WIDTH, tq), F32),
                        pltpu.VMEM((NSA_WIDTH, tq), F32),
                        pltpu.VMEM((1, NSA_KV_GROUPS * n), F32)],
        compiler_params=pltpu.CompilerParams(dimension_semantics=("arbitrary", "arbitrary"),
                                             vmem_limit_bytes=VMEM_LIMIT),
        name="nsa_attn",
    )(q_t, kcmp, vcmp_t, ksel, vsl_t, kw, vw_t, gates, sga)


def _odd_proj_kernel(x_ref, mixa_ref, mixb_ref, posr_ref, wout_ref, g_ref, wc_ref, wgate_ref, qn_ref, kvn_ref,
                     wuqt_ref, wuk_ref, wuvt_ref, invfc_ref,
                     y_ref, qt_ref, kn_ref, kpe_ref, vt_ref, sg_ref, *, tm):
    y = (x_ref[...] + _dot(mixa_ref[...], wout_ref[0:NSA_WIDTH, :])
         + _dot(mixb_ref[...], wout_ref[NSA_WIDTH:, :]))
    y_ref[...] = y
    yn = _rms(y, g_ref[...]).astype(BF16)
    cos_t, sin_t = _rope_tables_t(invfc_ref[...], posr_ref[...])
    half = MLA_ROPE_DIM // 2

    def rope_t(x1, x2):
        return x1 * cos_t - x2 * sin_t, x2 * cos_t + x1 * sin_t

    seg = _dot(yn, wc_ref[...])
    kr_t = seg[:, MLA_Q_RANK + MLA_KV_RANK:].T
    kp1, kp2 = rope_t(kr_t[0:half], kr_t[half:2 * half])
    kpe_ref[...] = jnp.concatenate([kp1, kp2, kr_t[2 * half:]], axis=0).T.astype(BF16)
    cq = _rms(seg[:, 0:MLA_Q_RANK], qn_ref[...]).astype(BF16)
    ckv = _rms(seg[:, MLA_Q_RANK:MLA_Q_RANK + MLA_KV_RANK], kvn_ref[...]).astype(BF16)
    scale = (MLA_NOPE_DIM + MLA_ROPE_DIM) ** -0.5 * LOG2_E
    qt = _dot_nt(wuqt_ref[...], cq)
    kn_ref[...] = _dot(ckv, wuk_ref[...]).astype(BF16)
    for h in range(MLA_HEADS):
        c0 = h * MLA_QK_PAD
        r0 = c0 + MLA_NOPE_DIM
        qt_ref[c0:r0, :] = (qt[c0:r0] * scale).astype(BF16)
        q1, q2 = rope_t(qt[r0:r0 + half], qt[r0 + half:r0 + 2 * half])
        qt_ref[r0:r0 + half, :] = (q1 * scale).astype(BF16)
        qt_ref[r0 + half:r0 + 2 * half, :] = (q2 * scale).astype(BF16)
        qt_ref[r0 + 2 * half:c0 + MLA_QK_PAD, :] = jnp.zeros((MLA_QK_PAD - MLA_NOPE_DIM - 2 * half, tm), BF16)
    v_t = _dot_nt(wuvt_ref[...], ckv)
    kc = vt_ref.shape[-1]
    for h in range(MLA_HEADS):
        for c in range(tm // kc):
            vt_ref[0, h, c, 0:MLA_V_DIM, :] = v_t[h * MLA_V_DIM:(h + 1) * MLA_V_DIM, c * kc:(c + 1) * kc].astype(BF16)
            vt_ref[0, h, c, MLA_V_DIM:, :] = jnp.ones((MLA_ONES_ROWS, kc), BF16)
    sg_ref[...] = _silu(_dot(yn, wgate_ref[...])).astype(BF16)


def _odd_proj(x2d, mixa, mixb, pos_row, a_w_out, c_norm, c_w_in, q_norm, kv_norm, w_uq, w_ukv, invf_col, seq):
    t = x2d.shape[0]
    tm = min(PROJ_ROWS, seq)
    r2 = MLA_Q_RANK + MLA_KV_RANK + MLA_ROPE_DIM
    wc = jnp.pad(c_w_in[:, :r2], ((0, 0), (0, LANES - MLA_ROPE_DIM))).astype(BF16)
    wgate = c_w_in[:, r2:].astype(BF16)
    wuqt = jnp.pad(w_uq.reshape(MLA_Q_RANK, MLA_HEADS, MLA_NOPE_DIM + MLA_ROPE_DIM),
                   ((0, 0), (0, 0), (0, MLA_QK_PAD - MLA_NOPE_DIM - MLA_ROPE_DIM))
                   ).reshape(MLA_Q_RANK, MLA_HEADS * MLA_QK_PAD).T.astype(BF16)
    wukv = w_ukv.reshape(MLA_KV_RANK, MLA_HEADS, MLA_NOPE_DIM + MLA_V_DIM)
    wuk = wukv[:, :, :MLA_NOPE_DIM].reshape(MLA_KV_RANK, -1).astype(BF16)
    wuvt = wukv[:, :, MLA_NOPE_DIM:].reshape(MLA_KV_RANK, -1).T.astype(BF16)
    kc = min(MLA_TQ, seq)
    nq = seq // kc
    cps = tm // kc
    assert tm % kc == 0 and seq % tm == 0
    vrows = MLA_V_DIM + MLA_ONES_ROWS
    ins = [x2d, mixa, mixb, pos_row, a_w_out.astype(BF16), c_norm.reshape(1, -1), wc, wgate,
           q_norm.reshape(1, -1), kv_norm.reshape(1, -1), wuqt, wuk, wuvt, invf_col]
    full = lambda a: pl.BlockSpec(a.shape, lambda i: (0,) * a.ndim)
    rows = lambda n: pl.BlockSpec((tm, n), lambda i: (i, 0))
    cols = lambda n: pl.BlockSpec((n, tm), lambda i: (0, i))
    out_shapes = [
        jax.ShapeDtypeStruct((t, D_MODEL), F32),
        jax.ShapeDtypeStruct((MLA_HEADS * MLA_QK_PAD, t), BF16),
        jax.ShapeDtypeStruct((t, MLA_HEADS * MLA_NOPE_DIM), BF16),
        jax.ShapeDtypeStruct((t, LANES), BF16),
        jax.ShapeDtypeStruct((t // seq, MLA_HEADS, nq, vrows, kc), BF16),
        jax.ShapeDtypeStruct((t, MLA_WIDTH), BF16),
    ]
    steps = seq // tm
    vt_spec = pl.BlockSpec((1, MLA_HEADS, cps, vrows, kc), lambda i: (i // steps, 0, i % steps, 0, 0))
    out_specs = [rows(D_MODEL), cols(MLA_HEADS * MLA_QK_PAD), rows(MLA_HEADS * MLA_NOPE_DIM), rows(LANES), vt_spec,
                 rows(MLA_WIDTH)]
    return pl.pallas_call(
        functools.partial(_odd_proj_kernel, tm=tm),
        grid=(t // tm,),
        in_specs=[rows(D_MODEL), rows(NSA_WIDTH), rows(CONV_WIDTH), cols(1)] + [full(a) for a in ins[4:]],
        out_specs=out_specs,
        out_shape=out_shapes,
        compiler_params=pltpu.CompilerParams(dimension_semantics=("arbitrary",),
                                             vmem_limit_bytes=VMEM_LIMIT),
        name="odd_proj",
    )(*ins)


def _mla_kernel(qt_ref, kn_ref, kpe_ref, v_t_ref, o_ref, m_scr, acc_scr, exc_scr, *, tq, kc, hp):
    jd = pl.program_id(2)
    causal = (lax.broadcasted_iota(jnp.int32, (kc, tq), 0) <= lax.broadcasted_iota(jnp.int32, (kc, tq), 1))
    hq = tq // 2
    tri = causal[0:hq, 0:hq]
    keep_hi = jnp.concatenate([jnp.ones((hq, hq), jnp.bool_), tri], axis=0)

    def keys(h, start, size):
        return jnp.concatenate([kn_ref[0, pl.ds(start, size), h * MLA_NOPE_DIM:(h + 1) * MLA_NOPE_DIM],
                                kpe_ref[0, pl.ds(start, size), :]], axis=1)

    def step(js, single_pass):
        scores = []
        for j in js:
            start = pl.multiple_of(j * kc, kc)
            for h in range(hp):
                qk = slice(h * MLA_QK_PAD, (h + 1) * MLA_QK_PAD)
                scores.append((j, h, _dot(keys(h, start, kc), qt_ref[qk, :])))
        for j, h, s_t in scores:
            if single_pass:
                _flash_update_stale(s_t, v_t_ref[0, h, j], m_scr, acc_scr, exc_scr, pl.ds(h * tq, tq))
            else:
                _flash_update_t(s_t, v_t_ref[0, h, j], m_scr, acc_scr, pl.ds(h * tq, tq))

    def attend(single_pass):
        m_scr[...] = jnp.full(m_scr.shape, NEG_INF, F32)
        acc_scr[...] = jnp.zeros(acc_scr.shape, F32)
        start = pl.multiple_of(jd * kc, kc)
        scores = []
        for h in range(hp):
            qk = slice(h * MLA_QK_PAD, (h + 1) * MLA_QK_PAD)
            scores.append((_dot(keys(h, start, hq), qt_ref[qk, 0:hq]),
                           _dot(keys(h, start, kc), qt_ref[qk, hq:tq])))
        for h, (s_lo, s_hi) in enumerate(scores):
            v_t = v_t_ref[0, h, jd]
            _flash_update_t(jnp.where(tri, s_lo, NEG_INF), v_t[:, 0:hq], m_scr, acc_scr, pl.ds(h * tq, hq))
            _flash_update_t(jnp.where(keep_hi, s_hi, NEG_INF), v_t, m_scr, acc_scr, pl.ds(h * tq + hq, hq))
        lax.fori_loop(0, jd // 3, lambda i, c: (step([3 * i, 3 * i + 1, 3 * i + 2], single_pass), c)[1], 0)

        @pl.when(jd % 3 == 1)
        def _():
            step([jd - 1], single_pass)

        @pl.when(jd % 3 == 2)
        def _():
            step([jd - 2, jd - 1], single_pass)

        for h in range(hp):
            c = pl.ds(h * tq, tq)
            o_t = acc_scr[0:MLA_V_DIM, c] / acc_scr[MLA_V_DIM:MLA_V_DIM + 1, c]
            o_ref[0, :, h * MLA_V_DIM:(h + 1) * MLA_V_DIM] = o_t.T.astype(BF16)

    exc_scr[...] = jnp.full(exc_scr.shape, NEG_INF, F32)
    attend(single_pass=True)

    @pl.when(jnp.max(exc_scr[...]) > MAX_STALE_EXCESS)
    def _():
        attend(single_pass=False)


def _mla_attention(q_t, k_nope, k_pe, v_t):
    b, seq, _ = k_pe.shape
    tq = min(MLA_TQ, seq)
    hp = MLA_HEADS_PER_STEP
    nq = seq // tq
    assert seq % tq == 0 and MLA_HEADS % hp == 0
    vrows = MLA_V_DIM + MLA_ONES_ROWS
    return pl.pallas_call(
        functools.partial(_mla_kernel, tq=tq, kc=tq, hp=hp),
        grid=(b, MLA_HEADS // hp, nq),
        in_specs=[pl.BlockSpec((hp * MLA_QK_PAD, tq), lambda bi, h, qi: (h, bi * nq + qi)),
                  pl.BlockSpec((1, seq, hp * MLA_NOPE_DIM), lambda bi, h, qi: (bi, 0, h)),
                  pl.BlockSpec((1, seq, LANES), lambda bi, h, qi: (bi, 0, 0)),
                  pl.BlockSpec((1, hp, nq, vrows, tq), lambda bi, h, qi: (bi, h, 0, 0, 0))],
        out_specs=pl.BlockSpec((1, tq, hp * MLA_V_DIM), lambda bi, h, qi: (bi, qi, h)),
        out_shape=jax.ShapeDtypeStruct((b, seq, MLA_WIDTH), BF16),
        scratch_shapes=[pltpu.VMEM((1, hp * tq), F32), pltpu.VMEM((vrows, hp * tq), F32),
                        pltpu.VMEM((1, hp * tq), F32)],
        compiler_params=pltpu.CompilerParams(dimension_semantics=("arbitrary", "arbitrary", "arbitrary"),
                                             vmem_limit_bytes=VMEM_LIMIT),
        name="mla_attn",
    )(q_t, k_nope, k_pe, v_t)


def _final_kernel(y_ref, o_ref, sg_ref, w_ref, g_ref, out_ref):
    gated = (sg_ref[...].astype(F32) * o_ref[...].astype(F32)).astype(BF16)
    out_ref[...] = _rms(y_ref[...] + _dot(gated, w_ref[...]), g_ref[...])


def _final(y, o, sg, w_out, final_norm, seq):
    t = y.shape[0]
    tm = min(PROJ_ROWS, seq)
    rows = lambda n: pl.BlockSpec((tm, n), lambda i: (i, 0))
    full = lambda a: pl.BlockSpec(a.shape, lambda i: (0,) * a.ndim)
    w = w_out.astype(BF16)
    g = final_norm.reshape(1, -1)
    return pl.pallas_call(
        _final_kernel,
        grid=(t // tm,),
        in_specs=[rows(D_MODEL), rows(MLA_WIDTH), rows(MLA_WIDTH), full(w), full(g)],
        out_specs=rows(D_MODEL),
        out_shape=jax.ShapeDtypeStruct((t, D_MODEL), F32),
        compiler_params=pltpu.CompilerParams(dimension_semantics=("arbitrary",),
                                             vmem_limit_bytes=VMEM_LIMIT),
        name="final",
    )(y, o, sg, w, g)


def _rope_constants():
    half = NSA_HEAD_DIM // 2
    inv_freq = ROPE_THETA ** (-jnp.arange(half, dtype=F32) / half)
    invf = jnp.tile(inv_freq, LANES // half).reshape(1, LANES)
    sgn = jnp.tile(jnp.concatenate([-jnp.ones((half,), F32), jnp.ones((half,), F32)]), LANES // (2 * half))
    return invf, sgn.reshape(1, LANES), inv_freq.reshape(half, 1)


def kernel(x, positions, a_norm, a_w_in, a_pe_k, a_pe_v, a_w_ck1, a_w_ck2, a_w_cv1, a_w_cv2, a_conv_w, a_w_out, c_norm, c_w_in, c_q_norm, c_kv_norm, c_w_uq, c_w_ukv, c_w_out, final_norm):
    b, seq, d = x.shape
    assert d == D_MODEL and NSA_HEAD_DIM == MLA_ROPE_DIM
    assert a_norm.shape[0] == 1 and c_norm.shape[0] == 1
    t = b * seq
    invf, sgn, invf_col = _rope_constants()
    x2d = x.reshape(t, d)
    pos_row = positions.reshape(1, t)

    q_t, kc, vc, ksel, vsl_t, kw, vw_t, gates, sga, mixb = _even_proj(
        x2d, pos_row, a_norm[0], a_w_in[0], a_conv_w[0], invf_col, seq)
    nchunk = seq // CMP_STRIDE
    pos_cmp = jnp.pad(positions[:, CMP_BLOCK - 1::CMP_STRIDE], ((0, 0), (0, 1)))[:, :nchunk, None]
    kcmp, vcmp_t = _compress(kc.reshape(b, seq, -1), vc.reshape(b, seq, -1), pos_cmp, a_pe_k[0], a_pe_v[0],
                             a_w_ck1[0], a_w_ck2[0], a_w_cv1[0], a_w_cv2[0], invf, sgn)
    r3 = lambda a: a.reshape(b, seq, a.shape[-1])
    mixa = _nsa_attention(q_t, kcmp, vcmp_t, r3(ksel), vsl_t, r3(kw), vw_t, r3(gates), r3(sga))

    y, mq_t, mkn, mkpe, mv_t, sg = _odd_proj(x2d, mixa.reshape(t, -1), mixb, pos_row, a_w_out[0], c_norm[0],
                                      c_w_in[0], c_q_norm[0], c_kv_norm[0], c_w_uq[0], c_w_ukv[0], invf_col, seq)
    o = _mla_attention(mq_t, r3(mkn), r3(mkpe), mv_t)
    out = _final(y, o.reshape(t, -1), sg, c_w_out[0], final_norm, seq)
    return out.reshape(b, seq, d)
```
